```python
import math
import jax
import jax.numpy as jnp
from jax import lax
import numpy as np

D_MODEL = 1024
BATCH = 8
SEQ = 2048
DEPTH = 2
DEC_BATCH = 128
DEC_SEQ = 1
PAST_LEN = 16384
PAGE_SIZE = 128

S5_WIDTH = D_MODEL // 2
S5_GROUP = 16
S5_GROUPS = S5_WIDTH // S5_GROUP
S5_STATE = 64
RET_HEADS = 4
RET_DK = 128
RET_DV = 256
RET_QK = RET_HEADS * RET_DK
RET_V = RET_HEADS * RET_DV
RET_CHUNK = 128
ROPE_BASE = 10000.0
CONV_WIDTH = D_MODEL // 2
CONV_K = 31
D_FF = 2816
N_EXPERTS = 8
TOP_K = 2
D_FF_EXPERT = 3584
N_DENSE = (DEPTH + 1) // 2
N_MOE = DEPTH // 2
N_BRANCH = 3
EPS = 1e-6
OFF_S5 = 0
OFF_Q = OFF_S5 + S5_WIDTH
OFF_K = OFF_Q + RET_QK
OFF_V = OFF_K + RET_QK
OFF_G = OFF_V + RET_V
OFF_CONV = OFF_G + RET_V
OFF_GATE = OFF_CONV + 2 * CONV_WIDTH
N_IN = OFF_GATE + N_BRANCH * D_MODEL

kernel_name = "hybrid_s5_retention_conformer_moe_step"


def rmsnorm(x, g):
    xf = x.astype(jnp.float32)
    y = xf * lax.rsqrt(jnp.mean(xf * xf, axis=-1, keepdims=True) + EPS)
    return (y * g.astype(jnp.float32)).astype(x.dtype)


def s5_discretize(lam_re, lam_im, log_dt, b_re, b_im):
    dt = jnp.exp(log_dt)[:, None]
    mag = jnp.exp(lam_re * dt)
    ang = lam_im * dt
    lbar_re = mag * jnp.cos(ang)
    lbar_im = mag * jnp.sin(ang)
    den = lam_re * lam_re + lam_im * lam_im
    nr = lbar_re - 1.0
    f_re = (nr * lam_re + lbar_im * lam_im) / den
    f_im = (lbar_im * lam_re - nr * lam_im) / den
    bbar_re = f_re[..., None] * b_re - f_im[..., None] * b_im
    bbar_im = f_re[..., None] * b_im + f_im[..., None] * b_re
    return lbar_re, lbar_im, bbar_re, bbar_im


def _complex_affine_combine(e1, e2):
    a1r, a1i, b1r, b1i = e1
    a2r, a2i, b2r, b2i = e2
    return (a2r * a1r - a2i * a1i,
            a2r * a1i + a2i * a1r,
            a2r * b1r - a2i * b1i + b2r,
            a2r * b1i + a2i * b1r + b2i)


def s5_branch(u, h0_re, h0_im, lam_re, lam_im, log_dt, b_re, b_im, c_re, c_im, d_skip, glu_w, glu_b):
    f32 = jnp.float32
    bsz, seq_len, _ = u.shape
    uf = u.astype(f32)
    lbar_re, lbar_im, bbar_re, bbar_im = s5_discretize(
        lam_re.astype(f32), lam_im.astype(f32), log_dt.astype(f32), b_re.astype(f32), b_im.astype(f32))
    ug = uf.reshape(bsz, seq_len, S5_GROUPS, S5_GROUP)
    bu_re = jnp.einsum("blgp,gnp->blgn", ug, bbar_re)
    bu_im = jnp.einsum("blgp,gnp->blgn", ug, bbar_im)
    a_re = jnp.broadcast_to(lbar_re, bu_re.shape)
    a_im = jnp.broadcast_to(lbar_im, bu_im.shape)
    acc_re, acc_im, hs_re, hs_im = lax.associative_scan(
        _complex_affine_combine, (a_re, a_im, bu_re, bu_im), axis=1)
    h0r = h0_re.astype(f32)[:, None]
    h0i = h0_im.astype(f32)[:, None]
    h_re = acc_re * h0r - acc_im * h0i + hs_re
    h_im = acc_re * h0i + acc_im * h0r + hs_im
    y = (jnp.einsum("blgn,gpn->blgp", h_re, c_re.astype(f32))
         - jnp.einsum("blgn,gpn->blgp", h_im, c_im.astype(f32))).reshape(bsz, seq_len, S5_WIDTH)
    y = y + d_skip.astype(f32) * uf
    z = jax.nn.gelu(y)
    out = z * jax.nn.sigmoid(z @ glu_w.astype(f32) + glu_b.astype(f32))
    return out, h_re[:, -1], h_im[:, -1]


def rotary(x, pos):
    half = RET_DK // 2
    freqs = ROPE_BASE ** (-jnp.arange(half, dtype=jnp.float32) / half)
    ang = pos[:, None] * freqs[None, :]
    cos = jnp.cos(ang)[None, :, None, :]
    sin = jnp.sin(ang)[None, :, None, :]
    x1 = x[..., :half]
    x2 = x[..., half:]
    return jnp.concatenate([x1 * cos - x2 * sin, x2 * cos + x1 * sin], axis=-1)


def retention_chunk(state, q, k, v, log_gamma):
    c = q.shape[2]
    idx = jnp.arange(c, dtype=jnp.float32)
    diff = idx[:, None] - idx[None, :]
    decay = jnp.where(diff >= 0, jnp.exp(jnp.maximum(diff, 0.0)[None] * log_gamma[:, None, None]), 0.0)
    inner = jnp.einsum("bhid,bhjd->bhij", q, k) * decay
    cross_decay = jnp.exp((idx + 1.0)[None, :] * log_gamma[:, None])[None, :, :, None]
    out = jnp.einsum("bhij,bhjv->bhiv", inner, v) + jnp.einsum("bhid,bhdv->bhiv", q, state) * cross_decay
    k_decay = jnp.exp((c - 1.0 - idx)[None, :] * log_gamma[:, None])[None, :, :, None]
    new_state = (jnp.exp(c * log_gamma)[None, :, None, None] * state
                 + jnp.einsum("bhjd,bhjv->bhdv", k * k_decay, v))
    return new_state, out


def retention_branch(q, k, v, g, state0, pos_offset):
    f32 = jnp.float32
    bsz, seq_len, _ = q.shape
    pos = jnp.arange(seq_len, dtype=f32) + pos_offset
    qh = rotary(q.astype(f32).reshape(bsz, seq_len, RET_HEADS, RET_DK), pos)
    kh = rotary(k.astype(f32).reshape(bsz, seq_len, RET_HEADS, RET_DK), pos) * (RET_DK ** -0.5)
    vh = v.astype(f32).reshape(bsz, seq_len, RET_HEADS, RET_DV)
    log_gamma = jnp.log(1.0 - 2.0 ** (-5.0 - jnp.arange(RET_HEADS, dtype=f32)))
    chunk = RET_CHUNK if seq_len % RET_CHUNK == 0 else seq_len
    n_chunks = seq_len // chunk

    def to_chunks(t):
        return t.transpose(0, 2, 1, 3).reshape(bsz, RET_HEADS, n_chunks, chunk, t.shape[-1]).transpose(2, 0, 1, 3, 4)

    def step(state, qkv):
        qc, kc, vc = qkv
        return retention_chunk(state, qc, kc, vc, log_gamma)

    new_state, o = lax.scan(step, state0.astype(f32), (to_chunks(qh), to_chunks(kh), to_chunks(vh)))
    o = o.transpose(1, 2, 0, 3, 4).reshape(bsz, RET_HEADS, seq_len, RET_DV).transpose(0, 2, 1, 3)
    mu = jnp.mean(o, axis=-1, keepdims=True)
    var = jnp.mean(jnp.square(o - mu), axis=-1, keepdims=True)
    o = ((o - mu) * lax.rsqrt(var + EPS)).reshape(bsz, seq_len, RET_V)
    return jax.nn.silu(g.astype(f32)) * o, new_state


def conv_branch(a, buf, pw_b, dw_w, dw_b, ln_g, ln_b):
    f32 = jnp.float32
    a = a + pw_b
    glu = a[..., :CONV_WIDTH] * jax.nn.sigmoid(a[..., CONV_WIDTH:])
    xc = jnp.concatenate([buf.astype(glu.dtype), glu], axis=1)
    y = lax.conv_general_dilated(
        xc, dw_w[:, None, :].astype(xc.dtype), window_strides=(1,), padding="VALID",
        dimension_numbers=("NWC", "WIO", "NWC"), feature_group_count=CONV_WIDTH) + dw_b
    yf = y.astype(f32)
    mu = jnp.mean(yf, axis=-1, keepdims=True)
    var = jnp.mean(jnp.square(yf - mu), axis=-1, keepdims=True)
    yn = (yf - mu) * lax.rsqrt(var + EPS) * ln_g.astype(f32) + ln_b.astype(f32)
    return jax.nn.silu(yn), xc[:, -(CONV_K - 1):]


def mixer_block(h, s5_h0_re, s5_h0_im, ret_s0, conv_buf, pos_offset,
                norm_g, w_in, lam_re, lam_im, log_dt, b_re, b_im, c_re, c_im, d_skip, glu_w, glu_b, s5_proj,
                ret_proj, pw_b, dw_w, dw_b, ln_g, ln_b, conv_proj, w_out):
    f32 = jnp.float32
    u = rmsnorm(h, norm_g)
    z = u @ w_in
    s5_out, s5_re, s5_im = s5_branch(z[..., OFF_S5:OFF_Q], s5_h0_re, s5_h0_im, lam_re, lam_im, log_dt,
                                     b_re, b_im, c_re, c_im, d_skip, glu_w, glu_b)
    ret_out, ret_state = retention_branch(z[..., OFF_Q:OFF_K], z[..., OFF_K:OFF_V], z[..., OFF_V:OFF_G],
                                          z[..., OFF_G:OFF_CONV], ret_s0, pos_offset)
    conv_out, conv_state = conv_branch(z[..., OFF_CONV:OFF_GATE], conv_buf, pw_b, dw_w, dw_b, ln_g, ln_b)
    gates = jax.nn.sigmoid(z[..., OFF_GATE:].astype(f32))
    merged = (gates[..., :D_MODEL] * (s5_out @ s5_proj.astype(f32))
              + gates[..., D_MODEL:2 * D_MODEL] * (ret_out @ ret_proj.astype(f32))
              + gates[..., 2 * D_MODEL:] * (conv_out @ conv_proj.astype(f32)))
    delta = (merged @ w_out.astype(f32)).astype(h.dtype)
    return delta, (s5_re, s5_im, ret_state, conv_state)


def swiglu(x, w_gate, w_up, w_down):
    return (jax.nn.silu(x @ w_gate) * (x @ w_up)) @ w_down


def moe_swiglu(x, router, w_gate, w_up, w_down):
    f32 = jnp.float32
    logits = (x @ router).astype(f32)
    top_v, top_i = lax.top_k(logits, TOP_K)
    top_w = jax.nn.softmax(top_v, axis=-1)
    combine = jnp.sum(jax.nn.one_hot(top_i, N_EXPERTS, dtype=f32) * top_w[..., None], axis=-2)
    out = jnp.zeros(x.shape, f32)
    for e in range(N_EXPERTS):
        out = out + combine[..., e:e + 1] * swiglu(x, w_gate[e], w_up[e], w_down[e]).astype(f32)
    return out.astype(x.dtype)


def setup_inputs(seed: int = 0) -> dict:
    key = jax.random.key(seed)
    keys = iter(jax.random.split(key, 48))
    f32 = jnp.float32

    def nrm(shape, scale):
        return scale * jax.random.normal(next(keys), shape, f32)

    lam_im_base = jnp.pi * jnp.arange(S5_STATE, dtype=f32)
    return {
        "x_prompt": nrm((BATCH, SEQ, D_MODEL), 1.0),
        "x_sample": nrm((DEC_BATCH, DEC_SEQ, D_MODEL), 1.0),
        "state_s5_re": nrm((DEPTH, DEC_BATCH, S5_GROUPS, S5_STATE), 0.1),
        "state_s5_im": nrm((DEPTH, DEC_BATCH, S5_GROUPS, S5_STATE), 0.1),
        "state_ret": nrm((DEPTH, DEC_BATCH, RET_HEADS, RET_DK, RET_DV), 1.0),
        "state_conv": nrm((DEPTH, DEC_BATCH, CONV_K - 1, CONV_WIDTH), 0.5),
        "norm_mix_g": 1.0 + nrm((DEPTH, D_MODEL), 0.02),
        "w_in": nrm((DEPTH, D_MODEL, N_IN), D_MODEL ** -0.5),
        "s5_lambda_re": -0.5 + nrm((DEPTH, S5_GROUPS, S5_STATE), 0.01),
        "s5_lambda_im": lam_im_base + nrm((DEPTH, S5_GROUPS, S5_STATE), 0.01),
        "s5_log_dt": jax.random.uniform(next(keys), (DEPTH, S5_GROUPS), f32, math.log(1e-3), math.log(1e-1)),
        "s5_b_re": nrm((DEPTH, S5_GROUPS, S5_STATE, S5_GROUP), (2.0 * S5_GROUP) ** -0.5),
        "s5_b_im": nrm((DEPTH, S5_GROUPS, S5_STATE, S5_GROUP), (2.0 * S5_GROUP) ** -0.5),
        "s5_c_re": nrm((DEPTH, S5_GROUPS, S5_GROUP, S5_STATE), (2.0 * S5_STATE) ** -0.5),
        "s5_c_im": nrm((DEPTH, S5_GROUPS, S5_GROUP, S5_STATE), (2.0 * S5_STATE) ** -0.5),
        "s5_d": nrm((DEPTH, S5_WIDTH), 1.0),
        "s5_glu_w": nrm((DEPTH, S5_WIDTH, S5_WIDTH), S5_WIDTH ** -0.5),
        "s5_glu_b": nrm((DEPTH, S5_WIDTH), 0.01),
        "s5_proj": nrm((DEPTH, S5_WIDTH, D_MODEL), S5_WIDTH ** -0.5),
        "ret_proj": nrm((DEPTH, RET_V, D_MODEL), RET_V ** -0.5),
        "conv_pw_b": nrm((DEPTH, 2 * CONV_WIDTH), 0.01),
        "conv_dw_w": nrm((DEPTH, CONV_K, CONV_WIDTH), CONV_K ** -0.5),
        "conv_dw_b": nrm((DEPTH, CONV_WIDTH), 0.01),
        "conv_ln_g": 1.0 + nrm((DEPTH, CONV_WIDTH), 0.02),
        "conv_ln_b": nrm((DEPTH, CONV_WIDTH), 0.01),
        "conv_proj": nrm((DEPTH, CONV_WIDTH, D_MODEL), CONV_WIDTH ** -0.5),
        "w_out": nrm((DEPTH, D_MODEL, D_MODEL), D_MODEL ** -0.5),
        "norm_ffn_g": 1.0 + nrm((DEPTH, D_MODEL), 0.02),
        "ffn_w_gate": nrm((N_DENSE, D_MODEL, D_FF), D_MODEL ** -0.5),
        "ffn_w_up": nrm((N_DENSE, D_MODEL, D_FF), D_MODEL ** -0.5),
        "ffn_w_down": nrm((N_DENSE, D_FF, D_MODEL), D_FF ** -0.5),
        "moe_router": nrm((N_MOE, D_MODEL, N_EXPERTS), D_MODEL ** -0.5),
        "moe_w_gate": nrm((N_MOE, N_EXPERTS, D_MODEL, D_FF_EXPERT), D_MODEL ** -0.5),
        "moe_w_up": nrm((N_MOE, N_EXPERTS, D_MODEL, D_FF_EXPERT), D_MODEL ** -0.5),
        "moe_w_down": nrm((N_MOE, N_EXPERTS, D_FF_EXPERT, D_MODEL), D_FF_EXPERT ** -0.5),
        "norm_final_g": 1.0 + nrm((D_MODEL,), 0.02),
    }


def reference(x_prompt, x_sample, state_s5_re, state_s5_im, state_ret, state_conv,
              norm_mix_g, w_in, s5_lambda_re, s5_lambda_im, s5_log_dt, s5_b_re, s5_b_im, s5_c_re, s5_c_im,
              s5_d, s5_glu_w, s5_glu_b, s5_proj, ret_proj, conv_pw_b, conv_dw_w, conv_dw_b, conv_ln_g, conv_ln_b,
              conv_proj, w_out, norm_ffn_g, ffn_w_gate, ffn_w_up, ffn_w_down,
              moe_router, moe_w_gate, moe_w_up, moe_w_down, norm_final_g):
    f32 = jnp.float32
    bp = x_prompt.shape[0]
    zero_s5 = jnp.zeros((bp, S5_GROUPS, S5_STATE), f32)
    zero_ret = jnp.zeros((bp, RET_HEADS, RET_DK, RET_DV), f32)
    zero_conv = jnp.zeros((bp, CONV_K - 1, CONV_WIDTH), x_prompt.dtype)
    hp = x_prompt
    hs = x_sample
    p_re_l, p_im_l, p_ret_l, p_conv_l = [], [], [], []
    s_re_l, s_im_l, s_ret_l, s_conv_l = [], [], [], []
    for l in range(DEPTH):
        layer_params = (norm_mix_g[l], w_in[l], s5_lambda_re[l], s5_lambda_im[l], s5_log_dt[l],
                        s5_b_re[l], s5_b_im[l], s5_c_re[l], s5_c_im[l], s5_d[l], s5_glu_w[l], s5_glu_b[l],
                        s5_proj[l], ret_proj[l], conv_pw_b[l], conv_dw_w[l], conv_dw_b[l], conv_ln_g[l],
                        conv_ln_b[l], conv_proj[l], w_out[l])
        dp, (p_re, p_im, p_ret, p_conv) = mixer_block(hp, zero_s5, zero_s5, zero_ret, zero_conv, 0, *layer_params)
        ds, (s_re, s_im, s_ret, s_conv) = mixer_block(hs, state_s5_re[l], state_s5_im[l], state_ret[l],
                                                      state_conv[l], PAST_LEN, *layer_params)
        hp = hp + dp
        hs = hs + ds
        up = rmsnorm(hp, norm_ffn_g[l])
        us = rmsnorm(hs, norm_ffn_g[l])
        j = l // 2
        if l % 2 == 0:
            hp = hp + swiglu(up, ffn_w_gate[j], ffn_w_up[j], ffn_w_down[j])
            hs = hs + swiglu(us, ffn_w_gate[j], ffn_w_up[j], ffn_w_down[j])
        else:
            hp = hp + moe_swiglu(up, moe_router[j], moe_w_gate[j], moe_w_up[j], moe_w_down[j])
            hs = hs + moe_swiglu(us, moe_router[j], moe_w_gate[j], moe_w_up[j], moe_w_down[j])
        p_re_l.append(p_re)
        p_im_l.append(p_im)
        p_ret_l.append(p_ret)
        p_conv_l.append(p_conv)
        s_re_l.append(s_re)
        s_im_l.append(s_im)
        s_ret_l.append(s_ret)
        s_conv_l.append(s_conv)
    y_prompt = rmsnorm(hp, norm_final_g)
    y_sample = rmsnorm(hs, norm_final_g)
    s5_re_prompt = jnp.stack(p_re_l)
    s5_im_prompt = jnp.stack(p_im_l)
    ret_prompt = jnp.stack(p_ret_l)
    conv_prompt = jnp.stack(p_conv_l)
    s5_re_sample = jnp.stack(s_re_l)
    s5_im_sample = jnp.stack(s_im_l)
    ret_sample = jnp.stack(s_ret_l)
    conv_sample = jnp.stack(s_conv_l)
    return (y_prompt, y_sample, s5_re_prompt, s5_im_prompt, ret_prompt, conv_prompt,
            s5_re_sample, s5_im_sample, ret_sample, conv_sample)
```

```python
import functools
import math

import jax
import jax.numpy as jnp
from jax import lax
from jax.experimental import pallas as pl
from jax.experimental.pallas import tpu as pltpu

F32 = jnp.float32
BF16 = jnp.bfloat16

D_MODEL = 1024
S5_WIDTH = 512
S5_GROUP = 16
S5_GROUPS = 32
S5_STATE = 64
S5_LANES = S5_GROUPS * S5_STATE
RET_HEADS = 4
RET_DK = 128
RET_DV = 256
RET_QK = RET_HEADS * RET_DK
RET_V = RET_HEADS * RET_DV
RET_CHUNK = 128
ROPE_BASE = 10000.0
CONV_WIDTH = 512
CONV_K = 31
CONV_HIST = 32
N_EXPERTS = 8
TOP_K = 2
N_BRANCH = 3
EPS = 1e-6
N_MAIN = 2 * RET_QK + 2 * RET_V + 2 * CONV_WIDTH + N_BRANCH * D_MODEL
COL_Q = 0
COL_K = RET_QK
COL_V = 2 * RET_QK
COL_G = COL_V + RET_V
COL_CONV = COL_G + RET_V
COL_GATE = COL_CONV + 2 * CONV_WIDTH

VMEM_LIMIT = 48 * 1024 * 1024


def _params(sem):
    return pltpu.CompilerParams(dimension_semantics=sem, vmem_limit_bytes=VMEM_LIMIT)


def _silu(x):
    return x * jax.nn.sigmoid(x)


def _bdot(a, b):
    return jnp.dot(a.astype(BF16), b.astype(BF16), preferred_element_type=F32)


def _norm_proj_kernel(x_ref, g_ref, w_ref, o_ref, u_scr):
    @pl.when(pl.program_id(2) == 0)
    def _():
        x = x_ref[...]
        ms = jnp.mean(x * x, axis=-1, keepdims=True)
        u_scr[...] = (x * lax.rsqrt(ms + EPS) * g_ref[...]).astype(BF16)

    o_ref[...] = jnp.dot(u_scr[...], w_ref[...], preferred_element_type=F32)


def _norm_proj(x, g, w, *, tt, tn, time_major):
    bsz, seq, d = x.shape
    n = w.shape[1]
    grid = (bsz, seq // tt, n // tn)
    if time_major:
        assert tn == n
        out_shape = jax.ShapeDtypeStruct((seq, bsz * n), F32)
        out_spec = pl.BlockSpec((tt, tn), lambda b, i, j: (i, b))
    else:
        out_shape = jax.ShapeDtypeStruct((bsz, seq, n), F32)
        out_spec = pl.BlockSpec((None, tt, tn), lambda b, i, j: (b, i, j))
    return pl.pallas_call(
        _norm_proj_kernel,
        grid=grid,
        in_specs=[
            pl.BlockSpec((None, tt, d), lambda b, i, j: (b, i, 0)),
            pl.BlockSpec((1, d), lambda b, i, j: (0, 0)),
            pl.BlockSpec((d, tn), lambda b, i, j: (0, j)),
        ],
        out_specs=out_spec,
        out_shape=out_shape,
        scratch_shapes=[pltpu.VMEM((tt, d), BF16)],
        compiler_params=_params(("parallel", "parallel", "arbitrary")),
        name="norm_proj",
    )(x, g.reshape(1, d), w)


def _s5_disc_kernel(lre_ref, lim_ref, ldt_ref, bre_ref, bim_ref, are_ref, aim_ref, ore_ref, oim_ref):
    lam_re = lre_ref[...]
    lam_im = lim_ref[...]
    dt = jnp.exp(ldt_ref[...])
    mag = jnp.exp(lam_re * dt)
    ang = lam_im * dt
    lbar_re = mag * jnp.cos(ang)
    lbar_im = mag * jnp.sin(ang)
    den = lam_re * lam_re + lam_im * lam_im
    nr = lbar_re - 1.0
    f_re = (nr * lam_re + lbar_im * lam_im) / den
    f_im = (lbar_im * lam_re - nr * lam_im) / den
    b_re = bre_ref[...]
    b_im = bim_ref[...]
    are_ref[...] = lbar_re
    aim_ref[...] = lbar_im
    ore_ref[...] = f_re * b_re - f_im * b_im
    oim_ref[...] = f_re * b_im + f_im * b_re


def _s5_discretize(lam_re, lam_im, log_dt, b_re, b_im):
    g, n = lam_re.shape
    p = b_re.shape[-1]
    rows = g * n
    col = lambda a: a.reshape(rows, 1)
    ldt = jnp.broadcast_to(log_dt[:, None], (g, n))
    outs = pl.pallas_call(
        _s5_disc_kernel,
        out_shape=[jax.ShapeDtypeStruct((rows, 1), F32), jax.ShapeDtypeStruct((rows, 1), F32),
                   jax.ShapeDtypeStruct((rows, p), F32), jax.ShapeDtypeStruct((rows, p), F32)],
        name="s5_discretize",
    )(col(lam_re), col(lam_im), col(ldt), b_re.reshape(rows, p), b_im.reshape(rows, p))
    a_re, a_im, bb_re, bb_im = outs
    return a_re.reshape(g, n), a_im.reshape(g, n), bb_re.reshape(g, n, p), bb_im.reshape(g, n, p)


S5_KCH = 128
S5_NCHUNK = S5_WIDTH // S5_KCH
S5_GPC = S5_KCH // S5_GROUP
S5_SPC = S5_GPC * S5_STATE


def _s5_block_mats(bbar_re, bbar_im, c_re, c_im):
    eye = jnp.eye(S5_GPC, dtype=F32)

    def in_blocks(bb):
        t = bb.reshape(S5_NCHUNK, S5_GPC, S5_STATE, S5_GROUP)
        m = jnp.einsum("cgnp,gh->cgphn", t, eye)
        return m.reshape(S5_NCHUNK, S5_KCH, S5_SPC)

    def out_blocks(cc):
        t = cc.reshape(S5_NCHUNK, S5_GPC, S5_GROUP, S5_STATE)
        m = jnp.einsum("cgpn,gh->cgnhp", t, eye)
        return m.reshape(S5_NCHUNK, S5_SPC, S5_KCH)

    bmat = jnp.concatenate([in_blocks(bbar_re), in_blocks(bbar_im)], axis=-1).astype(BF16)
    cmat = jnp.stack([out_blocks(c_re), -out_blocks(c_im)], axis=1).astype(BF16)
    return bmat, cmat


def _gelu_tanh(x):
    return 0.5 * x * (1.0 + jnp.tanh(math.sqrt(2.0 / math.pi) * (x + 0.044715 * (x * x * x))))


def _s5_kernel(u_ref, h0_ref, are_ref, aim_ref, bmat_ref, cmat_ref, d_ref, gw_ref, gb_ref,
               o_ref, hout_ref, hs_scr, h_scr, *, nb, tt, lane_chunk):
    @pl.when(pl.program_id(0) == 0)
    def _():
        h_scr[...] = h0_ref[...]

    u = u_ref[...]
    ub = u.astype(BF16)
    for c in range(S5_NCHUNK):
        bu = jnp.dot(ub[:, c * S5_KCH:(c + 1) * S5_KCH], bmat_ref[c], preferred_element_type=F32)
        hs_scr[:, c * S5_SPC:(c + 1) * S5_SPC] = bu[:, :S5_SPC]
        hs_scr[:, S5_LANES + c * S5_SPC:S5_LANES + (c + 1) * S5_SPC] = bu[:, S5_SPC:]

    for lc in range(S5_LANES // lane_chunk):
        re_sl = slice(lc * lane_chunk, (lc + 1) * lane_chunk)
        im_sl = slice(S5_LANES + lc * lane_chunk, S5_LANES + (lc + 1) * lane_chunk)
        a_re = jnp.broadcast_to(are_ref[:, re_sl], (nb, lane_chunk))
        a_im = jnp.broadcast_to(aim_ref[:, re_sl], (nb, lane_chunk))

        def step(t, carry):
            h_re, h_im = carry
            r0 = t * nb if isinstance(t, int) else pl.multiple_of(t * nb, nb)
            n_re = a_re * h_re - a_im * h_im + hs_scr[pl.ds(r0, nb), re_sl]
            n_im = a_re * h_im + a_im * h_re + hs_scr[pl.ds(r0, nb), im_sl]
            hs_scr[pl.ds(r0, nb), re_sl] = n_re
            hs_scr[pl.ds(r0, nb), im_sl] = n_im
            return n_re, n_im

        carry = (h_scr[:, re_sl], h_scr[:, im_sl])
        if tt == 1:
            carry = step(0, carry)
        else:
            carry = lax.fori_loop(0, tt, step, carry, unroll=4)
        h_scr[:, re_sl] = carry[0]
        h_scr[:, im_sl] = carry[1]

    hout_ref[...] = h_scr[...]

    ys = []
    for c in range(S5_NCHUNK):
        h_re = hs_scr[:, c * S5_SPC:(c + 1) * S5_SPC].astype(BF16)
        h_im = hs_scr[:, S5_LANES + c * S5_SPC:S5_LANES + (c + 1) * S5_SPC].astype(BF16)
        ys.append(jnp.dot(h_re, cmat_ref[c, 0], preferred_element_type=F32)
                  + jnp.dot(h_im, cmat_ref[c, 1], preferred_element_type=F32))
    y = jnp.concatenate(ys, axis=-1) + d_ref[...] * u
    z = _gelu_tanh(y)
    gate = jnp.dot(z.astype(BF16), gw_ref[...], preferred_element_type=F32) + gb_ref[...]
    o_ref[...] = z * jax.nn.sigmoid(gate)


def _s5_branch(u_tm, h0, a_re, a_im, bmat, cmat, d_skip, glu_w, glu_b, *, nb, tt):
    rows = u_tm.shape[0]
    rblk = tt * nb
    lane_chunk = 1024 if nb <= 8 else 512
    const = lambda shape: pl.BlockSpec(shape, lambda i: (0,) * len(shape))
    return pl.pallas_call(
        functools.partial(_s5_kernel, nb=nb, tt=tt, lane_chunk=lane_chunk),
        grid=(rows // rblk,),
        in_specs=[
            pl.BlockSpec((rblk, S5_WIDTH), lambda i: (i, 0)),
            const((nb, 2 * S5_LANES)),
            const((1, S5_LANES)),
            const((1, S5_LANES)),
            const(bmat.shape),
            const(cmat.shape),
            const((1, S5_WIDTH)),
            const((S5_WIDTH, S5_WIDTH)),
            const((1, S5_WIDTH)),
        ],
        out_specs=[pl.BlockSpec((rblk, S5_WIDTH), lambda i: (i, 0)), const((nb, 2 * S5_LANES))],
        out_shape=[jax.ShapeDtypeStruct((rows, S5_WIDTH), F32), jax.ShapeDtypeStruct((nb, 2 * S5_LANES), F32)],
        scratch_shapes=[pltpu.VMEM((rblk, 2 * S5_LANES), F32), pltpu.VMEM((nb, 2 * S5_LANES), F32)],
        compiler_params=_params(("arbitrary",)),
        name="s5_branch",
    )(u_tm, h0, a_re.reshape(1, S5_LANES), a_im.reshape(1, S5_LANES), bmat, cmat,
      d_skip.reshape(1, S5_WIDTH), glu_w, glu_b.reshape(1, S5_WIDTH))


def _rope_tables(pos):
    half = RET_DK // 2
    freqs = ROPE_BASE ** (-jnp.arange(half, dtype=F32) / half)
    ang = pos[:, None] * freqs[None, :]
    cos = jnp.cos(ang)
    sin = jnp.sin(ang)
    return jnp.concatenate([cos, cos], axis=-1), jnp.concatenate([-sin, sin], axis=-1)


def _rope(x, cos, sin):
    return x * cos + pltpu.roll(x, RET_DK // 2, 1) * sin


def _group_norm(o):
    mu = jnp.mean(o, axis=-1, keepdims=True)
    d = o - mu
    var = jnp.mean(d * d, axis=-1, keepdims=True)
    return d * lax.rsqrt(var + EPS)


def _retention_tables(chunk):
    log_gamma = jnp.log(1.0 - 2.0 ** (-5.0 - jnp.arange(RET_HEADS, dtype=F32)))
    idx = jnp.arange(chunk, dtype=F32)
    diff = idx[:, None] - idx[None, :]
    decay = jnp.where(diff >= 0, jnp.exp(jnp.maximum(diff, 0.0)[None] * log_gamma[:, None, None]), 0.0)
    cross = jnp.exp((idx + 1.0)[None, :] * log_gamma[:, None])[:, :, None]
    kdec = jnp.exp((chunk - 1.0 - idx)[None, :] * log_gamma[:, None])[:, :, None]
    full = jnp.exp(chunk * log_gamma)
    return decay, cross, kdec, full


def _retention_kernel(q_ref, k_ref, v_ref, g_ref, cos_ref, sin_ref, s0_ref, decay_ref, cross_ref, kdec_ref,
                      full_ref, o_ref, sout_ref, s_scr, *, n_chunks):
    @pl.when(pl.program_id(1) == 0)
    def _():
        s_scr[...] = s0_ref[...]

    for c in range(n_chunks):
        rows = slice(c * RET_CHUNK, (c + 1) * RET_CHUNK)
        cos = cos_ref[rows, :]
        sin = sin_ref[rows, :]
        for h in range(RET_HEADS):
            qk_cols = slice(h * RET_DK, (h + 1) * RET_DK)
            v_cols = slice(h * RET_DV, (h + 1) * RET_DV)
            qh = _rope(q_ref[rows, qk_cols], cos, sin)
            kh = _rope(k_ref[rows, qk_cols], cos, sin) * (RET_DK ** -0.5)
            vb = v_ref[rows, v_cols].astype(BF16)
            qb = qh.astype(BF16)
            state = s_scr[h]
            inner = lax.dot_general(qb, kh.astype(BF16), (((1,), (1,)), ((), ())),
                                    preferred_element_type=F32) * decay_ref[h]
            out = (jnp.dot(inner.astype(BF16), vb, preferred_element_type=F32)
                   + jnp.dot(qb, state.astype(BF16), preferred_element_type=F32) * cross_ref[h])
            kd = (kh * kdec_ref[h]).astype(BF16)
            s_scr[h] = full_ref[h] * state + jnp.dot(kd.T, vb, preferred_element_type=F32)
            o_ref[rows, v_cols] = _silu(g_ref[rows, v_cols]) * _group_norm(out)

    sout_ref[...] = s_scr[...]


def _retention_prompt(z_main, state0, pos_offset, *, tt):
    bsz, seq, _ = z_main.shape
    assert seq % RET_CHUNK == 0 and tt % RET_CHUNK == 0
    cos, sin = _rope_tables(jnp.arange(seq, dtype=F32) + pos_offset)
    decay, cross, kdec, full = _retention_tables(RET_CHUNK)
    full = jnp.broadcast_to(full[:, None, None], (RET_HEADS, 1, RET_DV))
    const = lambda shape: pl.BlockSpec(shape, lambda b, i: (0,) * len(shape))
    return pl.pallas_call(
        functools.partial(_retention_kernel, n_chunks=tt // RET_CHUNK),
        grid=(bsz, seq // tt),
        in_specs=[
            pl.BlockSpec((None, tt, RET_QK), lambda b, i: (b, i, COL_Q // RET_QK)),
            pl.BlockSpec((None, tt, RET_QK), lambda b, i: (b, i, COL_K // RET_QK)),
            pl.BlockSpec((None, tt, RET_V), lambda b, i: (b, i, COL_V // RET_V)),
            pl.BlockSpec((None, tt, RET_V), lambda b, i: (b, i, COL_G // RET_V)),
            pl.BlockSpec((tt, RET_DK), lambda b, i: (i, 0)),
            pl.BlockSpec((tt, RET_DK), lambda b, i: (i, 0)),
            pl.BlockSpec((None, RET_HEADS, RET_DK, RET_DV), lambda b, i: (b, 0, 0, 0)),
            const((RET_HEADS, RET_CHUNK, RET_CHUNK)),
            const((RET_HEADS, RET_CHUNK, 1)),
            const((RET_HEADS, RET_CHUNK, 1)),
            const((RET_HEADS, 1, RET_DV)),
        ],
        out_specs=[
            pl.BlockSpec((None, tt, RET_V), lambda b, i: (b, i, 0)),
            pl.BlockSpec((None, RET_HEADS, RET_DK, RET_DV), lambda b, i: (b, 0, 0, 0)),
        ],
        out_shape=[jax.ShapeDtypeStruct((bsz, seq, RET_V), F32),
                   jax.ShapeDtypeStruct((bsz, RET_HEADS, RET_DK, RET_DV), F32)],
        scratch_shapes=[pltpu.VMEM((RET_HEADS, RET_DK, RET_DV), F32)],
        compiler_params=_params(("parallel", "arbitrary")),
        name="retention_prompt",
    )(z_main, z_main, z_main, z_main, cos, sin, state0, decay, cross, kdec, full)


def _retention_step_kernel(q_ref, k_ref, v_ref, g_ref, cos_ref, sin_ref, s_ref, gam_ref, o_ref, sout_ref,
                           o_scr, *, bb):
    cos = cos_ref[...]
    sin = sin_ref[...]
    for h in range(RET_HEADS):
        qk_cols = slice(h * RET_DK, (h + 1) * RET_DK)
        v_cols = slice(h * RET_DV, (h + 1) * RET_DV)
        qh = _rope(q_ref[:, qk_cols], cos, sin)
        kh = _rope(k_ref[:, qk_cols], cos, sin) * (RET_DK ** -0.5)
        qk = jnp.sum(qh * kh, axis=-1, keepdims=True)
        q_t = qh.T
        k_t = kh.T
        gamma = gam_ref[h]
        for b in range(bb):
            state = s_ref[b, h]
            vrow = v_ref[b:b + 1, v_cols]
            qs = jnp.sum(q_t[:, b:b + 1] * state, axis=0, keepdims=True)
            o_scr[b:b + 1, v_cols] = qk[b:b + 1, :] * vrow + qs * gamma
            sout_ref[b, h] = gamma * state + k_t[:, b:b + 1] * vrow
    for h in range(RET_HEADS):
        v_cols = slice(h * RET_DV, (h + 1) * RET_DV)
        o_ref[:, v_cols] = _silu(g_ref[:, v_cols]) * _group_norm(o_scr[:, v_cols])


def _retention_step(z_main, state, pos, *, bb):
    n = z_main.shape[0]
    cos, sin = _rope_tables(jnp.full((1,), pos, F32))
    log_gamma = jnp.log(1.0 - 2.0 ** (-5.0 - jnp.arange(RET_HEADS, dtype=F32)))
    gam = jnp.broadcast_to(jnp.exp(log_gamma)[:, None, None], (RET_HEADS, 1, RET_DV))
    const = lambda shape: pl.BlockSpec(shape, lambda i: (0,) * len(shape))
    return pl.pallas_call(
        functools.partial(_retention_step_kernel, bb=bb),
        grid=(n // bb,),
        in_specs=[
            pl.BlockSpec((bb, RET_QK), lambda i: (i, COL_Q // RET_QK)),
            pl.BlockSpec((bb, RET_QK), lambda i: (i, COL_K // RET_QK)),
            pl.BlockSpec((bb, RET_V), lambda i: (i, COL_V // RET_V)),
            pl.BlockSpec((bb, RET_V), lambda i: (i, COL_G // RET_V)),
            const((1, RET_DK)),
            const((1, RET_DK)),
            pl.BlockSpec((bb, RET_HEADS, RET_DK, RET_DV), lambda i: (i, 0, 0, 0)),
            const((RET_HEADS, 1, RET_DV)),
        ],
        out_specs=[
            pl.BlockSpec((bb, RET_V), lambda i: (i, 0)),
            pl.BlockSpec((bb, RET_HEADS, RET_DK, RET_DV), lambda i: (i, 0, 0, 0)),
        ],
        out_shape=[jax.ShapeDtypeStruct((n, RET_V), F32),
                   jax.ShapeDtypeStruct((n, RET_HEADS, RET_DK, RET_DV), F32)],
        scratch_shapes=[pltpu.VMEM((bb, RET_V), F32)],
        compiler_params=_params(("parallel",)),
        name="retention_step",
    )(z_main, z_main, z_main, z_main, cos, sin, state, gam)


def _layer_norm(y, g, b):
    mu = jnp.mean(y, axis=-1, keepdims=True)
    d = y - mu
    var = jnp.mean(d * d, axis=-1, keepdims=True)
    return d * lax.rsqrt(var + EPS) * g + b


CONV_RB = 32


def _conv_kernel(a_ref, buf_ref, pwb_ref, dww_ref, dwb_ref, lng_ref, lnb_ref, o_ref, hist_ref, x_scr, *, tt):
    i = pl.program_id(1)

    @pl.when(i == 0)
    def _():
        x_scr[0:CONV_HIST, :] = buf_ref[...]

    @pl.when(i > 0)
    def _():
        x_scr[0:CONV_HIST, :] = x_scr[tt:tt + CONV_HIST, :]

    a = a_ref[...] + pwb_ref[...]
    x_scr[CONV_HIST:CONV_HIST + tt, :] = a[:, :CONV_WIDTH] * jax.nn.sigmoid(a[:, CONV_WIDTH:])
    hist_ref[...] = x_scr[tt:tt + CONV_HIST, :]

    off = CONV_HIST - (CONV_K - 1)
    for r in range(tt // CONV_RB):
        acc = jnp.broadcast_to(dwb_ref[...], (CONV_RB, CONV_WIDTH))
        for k in range(CONV_K):
            acc = acc + dww_ref[k:k + 1, :] * x_scr[r * CONV_RB + off + k:r * CONV_RB + off + k + CONV_RB, :]
        yn = _layer_norm(acc, lng_ref[...], lnb_ref[...])
        o_ref[r * CONV_RB:(r + 1) * CONV_RB, :] = _silu(yn)


def _conv_prompt(z_main, buf, pw_b, dw_w, dw_b, ln_g, ln_b, *, tt):
    bsz, seq, _ = z_main.shape
    assert seq >= CONV_HIST and tt >= CONV_HIST
    buf32 = jnp.pad(buf, ((0, 0), (CONV_HIST - (CONV_K - 1), 0), (0, 0)))
    const = lambda shape: pl.BlockSpec(shape, lambda b, i: (0,) * len(shape))
    out, hist = pl.pallas_call(
        functools.partial(_conv_kernel, tt=tt),
        grid=(bsz, seq // tt),
        in_specs=[
            pl.BlockSpec((None, tt, 2 * CONV_WIDTH), lambda b, i: (b, i, COL_CONV // (2 * CONV_WIDTH))),
            pl.BlockSpec((None, CONV_HIST, CONV_WIDTH), lambda b, i: (b, 0, 0)),
            const((1, 2 * CONV_WIDTH)),
            const((CONV_K, CONV_WIDTH)),
            const((1, CONV_WIDTH)),
            const((1, CONV_WIDTH)),
            const((1, CONV_WIDTH)),
        ],
        out_specs=[
            pl.BlockSpec((None, tt, CONV_WIDTH), lambda b, i: (b, i, 0)),
            pl.BlockSpec((None, CONV_HIST, CONV_WIDTH), lambda b, i: (b, 0, 0)),
        ],
        out_shape=[jax.ShapeDtypeStruct((bsz, seq, CONV_WIDTH), F32),
                   jax.ShapeDtypeStruct((bsz, CONV_HIST, CONV_WIDTH), F32)],
        scratch_shapes=[pltpu.VMEM((CONV_HIST + tt, CONV_WIDTH), F32)],
        compiler_params=_params(("parallel", "arbitrary")),
        name="conv_prompt",
    )(z_main, buf32, pw_b.reshape(1, -1), dw_w, dw_b.reshape(1, -1), ln_g.reshape(1, -1), ln_b.reshape(1, -1))
    return out, hist[:, CONV_HIST - (CONV_K - 1):, :]


def _conv_step_kernel(a_ref, buf_ref, pwb_ref, dww_ref, dwb_ref, lng_ref, lnb_ref, o_ref, hist_ref):
    hist_len = CONV_K - 1
    a = a_ref[...] + pwb_ref[...]
    glu = a[:, :CONV_WIDTH] * jax.nn.sigmoid(a[:, CONV_WIDTH:])
    acc = dwb_ref[...] + dww_ref[hist_len:hist_len + 1, :] * glu
    for k in range(hist_len):
        acc = acc + dww_ref[k:k + 1, :] * buf_ref[:, k * CONV_WIDTH:(k + 1) * CONV_WIDTH]
    o_ref[...] = _silu(_layer_norm(acc, lng_ref[...], lnb_ref[...]))
    hist_ref[:, :(hist_len - 1) * CONV_WIDTH] = buf_ref[:, CONV_WIDTH:]
    hist_ref[:, (hist_len - 1) * CONV_WIDTH:] = glu


def _conv_step(z_main, buf, pw_b, dw_w, dw_b, ln_g, ln_b, *, bb):
    n = z_main.shape[0]
    hist_len = CONV_K - 1
    const = lambda shape: pl.BlockSpec(shape, lambda i: (0,) * len(shape))
    out, hist = pl.pallas_call(
        _conv_step_kernel,
        grid=(n // bb,),
        in_specs=[
            pl.BlockSpec((bb, 2 * CONV_WIDTH), lambda i: (i, COL_CONV // (2 * CONV_WIDTH))),
            pl.BlockSpec((bb, hist_len * CONV_WIDTH), lambda i: (i, 0)),
            const((1, 2 * CONV_WIDTH)),
            const((CONV_K, CONV_WIDTH)),
            const((1, CONV_WIDTH)),
            const((1, CONV_WIDTH)),
            const((1, CONV_WIDTH)),
        ],
        out_specs=[
            pl.BlockSpec((bb, CONV_WIDTH), lambda i: (i, 0)),
            pl.BlockSpec((bb, hist_len * CONV_WIDTH), lambda i: (i, 0)),
        ],
        out_shape=[jax.ShapeDtypeStruct((n, CONV_WIDTH), F32),
                   jax.ShapeDtypeStruct((n, hist_len * CONV_WIDTH), F32)],
        compiler_params=_params(("parallel",)),
        name="conv_step",
    )(z_main, buf.reshape(n, hist_len * CONV_WIDTH), pw_b.reshape(1, -1), dw_w, dw_b.reshape(1, -1),
      ln_g.reshape(1, -1), ln_b.reshape(1, -1))
    return out, hist.reshape(n, hist_len, CONV_WIDTH)


def _merge_kernel(x_ref, s5_ref, ret_ref, conv_ref, g1_ref, g2_ref, g3_ref, ps5_ref, pret_ref, pconv_ref, wout_ref,
                  g_ref, h_ref, u_ref):
    merged = (jax.nn.sigmoid(g1_ref[...]) * _bdot(s5_ref[...], ps5_ref[...])
              + jax.nn.sigmoid(g2_ref[...]) * _bdot(ret_ref[...], pret_ref[...])
              + jax.nn.sigmoid(g3_ref[...]) * _bdot(conv_ref[...], pconv_ref[...]))
    h = x_ref[...] + _bdot(merged, wout_ref[...])
    h_ref[...] = h
    ms = jnp.mean(h * h, axis=-1, keepdims=True)
    u_ref[...] = h * lax.rsqrt(ms + EPS) * g_ref[...]


def _merge(x, s5_tm, ret_out, conv_out, z_main, s5_proj, ret_proj, conv_proj, w_out, norm_g, *, tm):
    bsz, seq, d = x.shape
    const = lambda shape: pl.BlockSpec(shape, lambda b, i: (0,) * len(shape))
    tok = lambda w, col=0: pl.BlockSpec((None, tm, w), lambda b, i: (b, i, col))
    return pl.pallas_call(
        _merge_kernel,
        grid=(bsz, seq // tm),
        in_specs=[
            tok(d),
            pl.BlockSpec((tm, S5_WIDTH), lambda b, i: (i, b)),
            tok(RET_V),
            tok(CONV_WIDTH),
            tok(d, COL_GATE // d),
            tok(d, COL_GATE // d + 1),
            tok(d, COL_GATE // d + 2),
            const((S5_WIDTH, d)),
            const((RET_V, d)),
            const((CONV_WIDTH, d)),
            const((d, d)),
            const((1, d)),
        ],
        out_specs=[tok(d), tok(d)],
        out_shape=[jax.ShapeDtypeStruct((bsz, seq, d), F32), jax.ShapeDtypeStruct((bsz, seq, d), F32)],
        compiler_params=_params(("parallel", "parallel")),
        name="merge",
    )(x, s5_tm, ret_out, conv_out, z_main, z_main, z_main, s5_proj, ret_proj, conv_proj, w_out,
      norm_g.reshape(1, d))


def _ffn_kernel(h_ref, u_ref, wg_ref, wu_ref, wd_ref, o_ref):
    f = pl.program_id(1)
    ub = u_ref[...].astype(BF16)
    gate = jnp.dot(ub, wg_ref[...], preferred_element_type=F32)
    up = jnp.dot(ub, wu_ref[...], preferred_element_type=F32)
    part = _bdot(_silu(gate) * up, wd_ref[...])

    @pl.when(f == 0)
    def _():
        o_ref[...] = h_ref[...] + part

    @pl.when(f > 0)
    def _():
        o_ref[...] = o_ref[...] + part


def _ffn(h, u, w_gate, w_up, w_down, *, tm, tf):
    rows, d = h.shape
    dff = w_gate.shape[1]
    tok = pl.BlockSpec((tm, d), lambda i, f: (i, 0))
    return pl.pallas_call(
        _ffn_kernel,
        grid=(rows // tm, dff // tf),
        in_specs=[tok, tok,
                  pl.BlockSpec((d, tf), lambda i, f: (0, f)),
                  pl.BlockSpec((d, tf), lambda i, f: (0, f)),
                  pl.BlockSpec((tf, d), lambda i, f: (f, 0))],
        out_specs=tok,
        out_shape=jax.ShapeDtypeStruct((rows, d), F32),
        compiler_params=_params(("parallel", "arbitrary")),
        name="ffn_dense",
    )(h, u, w_gate, w_up, w_down)


def _split_bf16(x):
    hi = x.astype(BF16)
    return hi, (x - hi.astype(F32)).astype(BF16)


def _router_kernel(u_ref, rt_ref, tri_ref, idx_ref, wts_ref, rank_ref, cnt_ref, cnt_scr):
    @pl.when(pl.program_id(0) == 0)
    def _():
        cnt_scr[...] = jnp.zeros_like(cnt_scr)

    u_hi, u_lo = _split_bf16(u_ref[...])
    r_hi, r_lo = _split_bf16(rt_ref[...])
    dn = (((1,), (1,)), ((), ()))
    logits = (lax.dot_general(r_hi, u_hi, dn, preferred_element_type=F32)
              + lax.dot_general(r_lo, u_hi, dn, preferred_element_type=F32)
              + lax.dot_general(r_hi, u_lo, dn, preferred_element_type=F32))
    eidx = lax.broadcasted_iota(jnp.int32, logits.shape, 0)
    m1 = jnp.max(logits, axis=0, keepdims=True)
    i1 = jnp.min(jnp.where(logits == m1, eidx, N_EXPERTS), axis=0, keepdims=True)
    rest = jnp.where(eidx == i1, -jnp.inf, logits)
    m2 = jnp.max(rest, axis=0, keepdims=True)
    i2 = jnp.min(jnp.where(rest == m2, eidx, N_EXPERTS), axis=0, keepdims=True)
    e2 = jnp.exp(m2 - m1)
    w1 = 1.0 / (1.0 + e2)
    idx_ref[...] = jnp.concatenate([i1, i2], axis=0)
    wts_ref[...] = jnp.concatenate([w1, e2 * w1], axis=0)

    hit1 = eidx == i1
    hit2 = eidx == i2
    hits = jnp.where(hit1 | hit2, 1.0, 0.0)
    before = jnp.dot(hits.astype(BF16), tri_ref[...], preferred_element_type=F32) + cnt_scr[...]
    rank_ref[...] = jnp.concatenate(
        [jnp.sum(jnp.where(hit1, before, 0.0), axis=0, keepdims=True),
         jnp.sum(jnp.where(hit2, before, 0.0), axis=0, keepdims=True)], axis=0).astype(jnp.int32)
    cnt_scr[...] = cnt_scr[...] + jnp.sum(hits, axis=1, keepdims=True)
    cnt_ref[...] = cnt_scr[...]


def _router(u, router, *, tm):
    rows, d = u.shape
    tri = (jnp.arange(tm)[:, None] < jnp.arange(tm)[None, :]).astype(BF16)
    const = lambda shape: pl.BlockSpec(shape, lambda i: (0,) * len(shape))
    lane = pl.BlockSpec((TOP_K, tm), lambda i: (0, i))
    return pl.pallas_call(
        _router_kernel,
        grid=(rows // tm,),
        in_specs=[pl.BlockSpec((tm, d), lambda i: (i, 0)), const((N_EXPERTS, d)), const((tm, tm))],
        out_specs=[lane, lane, lane, const((N_EXPERTS, 1))],
        out_shape=[jax.ShapeDtypeStruct((TOP_K, rows), jnp.int32), jax.ShapeDtypeStruct((TOP_K, rows), F32),
                   jax.ShapeDtypeStruct((TOP_K, rows), jnp.int32), jax.ShapeDtypeStruct((N_EXPERTS, 1), F32)],
        scratch_shapes=[pltpu.VMEM((N_EXPERTS, 1), F32)],
        compiler_params=_params(("arbitrary",)),
        name="moe_router",
    )(u, router.T, tri)


def _dispatch_kernel(pos_ref, src_ref, init_ref, xs_ref, sem, *, tb):
    del init_ref
    base = pl.program_id(0) * tb

    def copy(r, s):
        return pltpu.make_async_copy(src_ref.at[pl.ds(base + r, 1)], xs_ref.at[pl.ds(pos_ref[s, r], 1)], sem)

    def start(r, carry):
        for s in range(TOP_K):
            copy(r, s).start()
        return carry

    def wait(r, carry):
        for s in range(TOP_K):
            copy(r, s).wait()
        return carry

    lax.fori_loop(0, tb, start, 0)
    lax.fori_loop(0, tb, wait, 0)


def _dispatch(u, pos, xs, *, tb):
    rows, d = u.shape
    return pl.pallas_call(
        functools.partial(_dispatch_kernel, tb=tb),
        grid=(rows // tb,),
        in_specs=[
            pl.BlockSpec((TOP_K, tb), lambda i: (0, i), memory_space=pltpu.SMEM),
            pl.BlockSpec(memory_space=pl.ANY),
            pl.BlockSpec(memory_space=pl.ANY),
        ],
        out_specs=pl.BlockSpec(memory_space=pl.ANY),
        out_shape=jax.ShapeDtypeStruct(xs.shape, xs.dtype),
        scratch_shapes=[pltpu.SemaphoreType.DMA(())],
        input_output_aliases={2: 0},
        compiler_params=_params(("arbitrary",)),
        name="moe_dispatch",
    )(pos, u, xs)


def _experts_kernel(te_ref, nu_ref, x_ref, wg_ref, wu_ref, wd_ref, o_ref, acc_scr):
    del te_ref
    i = pl.program_id(0)
    f = pl.program_id(1)

    @pl.when(i < nu_ref[0])
    def _():
        xb = x_ref[...].astype(BF16)
        gate = jnp.dot(xb, wg_ref[...], preferred_element_type=F32)
        up = jnp.dot(xb, wu_ref[...], preferred_element_type=F32)
        part = _bdot(_silu(gate) * up, wd_ref[...])

        @pl.when(f == 0)
        def _():
            acc_scr[...] = part

        @pl.when(f > 0)
        def _():
            acc_scr[...] = acc_scr[...] + part

        @pl.when(f == pl.num_programs(1) - 1)
        def _():
            o_ref[...] = acc_scr[...]

    @pl.when(i >= nu_ref[0])
    def _():
        o_ref[...] = jnp.zeros_like(o_ref)


def _experts(xs, tile_expert, n_used, w_gate, w_up, w_down, *, tm, tf):
    rows, d = xs.shape
    dff = w_gate.shape[-1]
    n_f = dff // tf
    last_f = n_f - 1

    def row_map(i, f, te, nu):
        return (jnp.minimum(i, nu[0] - 1), 0)

    def fsel(i, f, nu):
        return jnp.where(i < nu[0], f, last_f)

    grid_spec = pltpu.PrefetchScalarGridSpec(
        num_scalar_prefetch=2,
        grid=(rows // tm, n_f),
        in_specs=[
            pl.BlockSpec((tm, d), row_map),
            pl.BlockSpec((None, d, tf), lambda i, f, te, nu: (te[i], 0, fsel(i, f, nu))),
            pl.BlockSpec((None, d, tf), lambda i, f, te, nu: (te[i], 0, fsel(i, f, nu))),
            pl.BlockSpec((None, tf, d), lambda i, f, te, nu: (te[i], fsel(i, f, nu), 0)),
        ],
        out_specs=pl.BlockSpec((tm, d), lambda i, f, te, nu: (i, 0)),
        scratch_shapes=[pltpu.VMEM((tm, d), F32)],
    )
    return pl.pallas_call(
        _experts_kernel,
        grid_spec=grid_spec,
        out_shape=jax.ShapeDtypeStruct((rows, d), F32),
        compiler_params=_params(("arbitrary", "arbitrary")),
        name="moe_experts",
    )(tile_expert, n_used, xs, w_gate, w_up, w_down)


def _combine_kernel(pos_ref, h_ref, wts_ref, ys_ref, o_ref, y_scr, sem, *, tb):
    def copy(r, s):
        return pltpu.make_async_copy(ys_ref.at[pl.ds(pos_ref[s, r], 1)], y_scr.at[s, pl.ds(r, 1)], sem)

    def start(r, carry):
        for s in range(TOP_K):
            copy(r, s).start()
        return carry

    def wait(r, carry):
        for s in range(TOP_K):
            copy(r, s).wait()
        return carry

    lax.fori_loop(0, tb, start, 0)
    lax.fori_loop(0, tb, wait, 0)
    o_ref[...] = h_ref[...] + wts_ref[:, 0:1] * y_scr[0] + wts_ref[:, 1:2] * y_scr[1]


def _combine(h, pos, wts_t, ys, *, tb):
    rows, d = h.shape
    return pl.pallas_call(
        functools.partial(_combine_kernel, tb=tb),
        grid=(rows // tb,),
        in_specs=[
            pl.BlockSpec((TOP_K, tb), lambda i: (0, i), memory_space=pltpu.SMEM),
            pl.BlockSpec((tb, d), lambda i: (i, 0)),
            pl.BlockSpec((tb, TOP_K), lambda i: (i, 0)),
            pl.BlockSpec(memory_space=pl.ANY),
        ],
        out_specs=pl.BlockSpec((tb, d), lambda i: (i, 0)),
        out_shape=jax.ShapeDtypeStruct((rows, d), F32),
        scratch_shapes=[pltpu.VMEM((TOP_K, tb, d), F32), pltpu.SemaphoreType.DMA(())],
        compiler_params=_params(("arbitrary",)),
        name="moe_combine",
    )(pos, h, wts_t, ys)


FFN_TF = 1408
MOE_TM = 512
MOE_TF = 1792


def _moe(h_list, u_list, router, w_gate, w_up, w_down):
    d = u_list[0].shape[1]
    routes = []
    for u in u_list:
        tm = min(512, u.shape[0])
        routes.append(_router(u, router, tm=tm))
    counts = [r[3][:, 0].astype(jnp.int32) for r in routes]
    total = sum(counts)
    padded = ((total + MOE_TM - 1) // MOE_TM) * MOE_TM
    ends = jnp.cumsum(padded)
    starts = ends - padded
    n_rows = sum(u.shape[0] for u in u_list) * TOP_K
    n_tiles = n_rows // MOE_TM + N_EXPERTS
    n_used = (ends[-1] // MOE_TM).astype(jnp.int32)
    tile_start = jnp.arange(n_tiles, dtype=jnp.int32) * MOE_TM
    tile_expert = jnp.sum((tile_start[:, None] >= ends[None, :]).astype(jnp.int32), axis=1)
    last_expert = jnp.sum((((n_used - 1) * MOE_TM) >= ends).astype(jnp.int32))
    tile_expert = jnp.where(jnp.arange(n_tiles) < n_used, tile_expert, last_expert).astype(jnp.int32)

    xs = jnp.zeros((n_tiles * MOE_TM, d), F32)
    poss = []
    seen = jnp.zeros((N_EXPERTS,), jnp.int32)
    for u, (idx, _, rank, _), cnt in zip(u_list, routes, counts):
        pos = (starts + seen)[idx] + rank
        poss.append(pos)
        seen = seen + cnt
        xs = _dispatch(u, pos, xs, tb=min(256, u.shape[0]))
    ys = _experts(xs, tile_expert, n_used.reshape(1), w_gate, w_up, w_down, tm=MOE_TM, tf=MOE_TF)
    outs = []
    for h, pos, (_, wts, _, _) in zip(h_list, poss, routes):
        outs.append(_combine(h, pos, wts.T, ys, tb=min(256, h.shape[0])))
    return outs


def _rmsnorm_kernel(x_ref, g_ref, o_ref):
    x = x_ref[...]
    ms = jnp.mean(x * x, axis=-1, keepdims=True)
    o_ref[...] = x * lax.rsqrt(ms + EPS) * g_ref[...]


def _rmsnorm(x, g, *, tm):
    rows, d = x.shape
    return pl.pallas_call(
        _rmsnorm_kernel,
        grid=(rows // tm,),
        in_specs=[pl.BlockSpec((tm, d), lambda i: (i, 0)), pl.BlockSpec((1, d), lambda i: (0, 0))],
        out_specs=pl.BlockSpec((tm, d), lambda i: (i, 0)),
        out_shape=jax.ShapeDtypeStruct((rows, d), F32),
        compiler_params=_params(("parallel",)),
        name="final_norm",
    )(x, g.reshape(1, d))


def _pack_s5_state(re, im):
    n = re.shape[0]
    return jnp.concatenate([re.reshape(n, S5_LANES), im.reshape(n, S5_LANES)], axis=-1)


def _unpack_s5_state(h):
    n = h.shape[0]
    return (h[:, :S5_LANES].reshape(n, S5_GROUPS, S5_STATE), h[:, S5_LANES:].reshape(n, S5_GROUPS, S5_STATE))


def _mixer(x, s5_h0, ret_s0, conv_buf, pos_offset, p, *, single_step):
    bsz, seq, d = x.shape
    tt = min(512, seq)
    z_main = _norm_proj(x, p["norm_g"], p["w_main"], tt=tt, tn=N_MAIN // 4, time_major=False)
    z_s5 = _norm_proj(x, p["norm_g"], p["w_s5"], tt=tt, tn=S5_WIDTH, time_major=True)

    if single_step:
        nb, s5_tt = seq, 1
    else:
        nb, s5_tt = bsz, 64
    s5_out, s5_state = _s5_branch(z_s5.reshape(-1, S5_WIDTH), s5_h0, p["a_re"], p["a_im"], p["bmat"], p["cmat"],
                                  p["d_skip"], p["glu_w"], p["glu_b"], nb=nb, tt=s5_tt)
    s5_out = s5_out.reshape(z_s5.shape)

    if single_step:
        zm = z_main.reshape(seq, N_MAIN)
        ret_out, ret_state = _retention_step(zm, ret_s0, pos_offset, bb=8)
        conv_out, conv_state = _conv_step(zm, conv_buf, p["pw_b"], p["dw_w"], p["dw_b"], p["ln_g"], p["ln_b"], bb=32)
        ret_out = ret_out.reshape(bsz, seq, RET_V)
        conv_out = conv_out.reshape(bsz, seq, CONV_WIDTH)
    else:
        ret_out, ret_state = _retention_prompt(z_main, ret_s0, pos_offset, tt=256)
        conv_out, conv_state = _conv_prompt(z_main, conv_buf, p["pw_b"], p["dw_w"], p["dw_b"], p["ln_g"],
                                            p["ln_b"], tt=256)

    h, u = _merge(x, s5_out, ret_out, conv_out, z_main, p["s5_proj"], p["ret_proj"], p["conv_proj"], p["w_out"],
                  p["norm_ffn_g"], tm=min(512, seq))
    return h, u, (s5_state, ret_state, conv_state)


def kernel(x_prompt, x_sample, state_s5_re, state_s5_im, state_ret, state_conv, norm_mix_g, w_in, s5_lambda_re, s5_lambda_im, s5_log_dt, s5_b_re, s5_b_im, s5_c_re, s5_c_im, s5_d, s5_glu_w, s5_glu_b, s5_proj, ret_proj, conv_pw_b, conv_dw_w, conv_dw_b, conv_ln_g, conv_ln_b, conv_proj, w_out, norm_ffn_g, ffn_w_gate, ffn_w_up, ffn_w_down, moe_router, moe_w_gate, moe_w_up, moe_w_down, norm_final_g):
    depth = w_in.shape[0]
    bp, seq, d = x_prompt.shape
    ns = x_sample.shape[0]
    past_len = 16384
    bf = lambda a: a.astype(BF16)

    hp = x_prompt
    hs = x_sample.reshape(1, ns, d)
    zero_s5 = jnp.zeros((bp, 2 * S5_LANES), F32)
    zero_ret = jnp.zeros((bp, RET_HEADS, RET_DK, RET_DV), F32)
    zero_conv = jnp.zeros((bp, CONV_K - 1, CONV_WIDTH), F32)

    p_states, s_states = [], []
    for l in range(depth):
        a_re, a_im, bb_re, bb_im = _s5_discretize(s5_lambda_re[l], s5_lambda_im[l], s5_log_dt[l],
                                                  s5_b_re[l], s5_b_im[l])
        bmat, cmat = _s5_block_mats(bb_re, bb_im, s5_c_re[l], s5_c_im[l])
        w_in_b = bf(w_in[l])
        p = dict(norm_g=norm_mix_g[l], w_s5=w_in_b[:, :S5_WIDTH], w_main=w_in_b[:, S5_WIDTH:],
                 a_re=a_re, a_im=a_im, bmat=bmat, cmat=cmat, d_skip=s5_d[l], glu_w=bf(s5_glu_w[l]),
                 glu_b=s5_glu_b[l], pw_b=conv_pw_b[l], dw_w=conv_dw_w[l], dw_b=conv_dw_b[l], ln_g=conv_ln_g[l],
                 ln_b=conv_ln_b[l], s5_proj=bf(s5_proj[l]), ret_proj=bf(ret_proj[l]), conv_proj=bf(conv_proj[l]),
                 w_out=bf(w_out[l]), norm_ffn_g=norm_ffn_g[l])
        hp, up, st_p = _mixer(hp, zero_s5, zero_ret, zero_conv, 0.0, p, single_step=False)
        hs, us, st_s = _mixer(hs, _pack_s5_state(state_s5_re[l], state_s5_im[l]), state_ret[l], state_conv[l],
                              float(past_len), p, single_step=True)
        p_states.append(st_p)
        s_states.append(st_s)

        hp2, up2 = hp.reshape(bp * seq, d), up.reshape(bp * seq, d)
        hs2, us2 = hs.reshape(ns, d), us.reshape(ns, d)
        j = l // 2
        if l % 2 == 0:
            wg, wu, wd = bf(ffn_w_gate[j]), bf(ffn_w_up[j]), bf(ffn_w_down[j])
            hp2 = _ffn(hp2, up2, wg, wu, wd, tm=512, tf=FFN_TF)
            hs2 = _ffn(hs2, us2, wg, wu, wd, tm=ns, tf=FFN_TF)
        else:
            hp2, hs2 = _moe([hp2, hs2], [up2, us2], moe_router[j], bf(moe_w_gate[j]), bf(moe_w_up[j]),
                            bf(moe_w_down[j]))
        hp = hp2.reshape(bp, seq, d)
        hs = hs2.reshape(1, ns, d)

    y_prompt = _rmsnorm(hp.reshape(bp * seq, d), norm_final_g, tm=512).reshape(bp, seq, d)
    y_sample = _rmsnorm(hs.reshape(ns, d), norm_final_g, tm=ns).reshape(ns, 1, d)

    def stack_states(states):
        s5 = [_unpack_s5_state(s[0]) for s in states]
        return (jnp.stack([a for a, _ in s5]), jnp.stack([b for _, b in s5]),
                jnp.stack([s[1] for s in states]), jnp.stack([s[2] for s in states]))

    p_re, p_im, p_ret, p_conv = stack_states(p_states)
    s_re, s_im, s_ret, s_conv = stack_states(s_states)
    return (y_prompt, y_sample, p_re, p_im, p_ret, p_conv, s_re, s_im, s_ret, s_conv)
```

```python
import functools
import math

import jax
import jax.numpy as jnp
from jax import lax
from jax.experimental import pallas as pl
from jax.experimental.pallas import tpu as pltpu

F32 = jnp.float32
BF16 = jnp.bfloat16

D_MODEL = 1024
S5_WIDTH = 512
S5_GROUP = 16
S5_GROUPS = 32
S5_STATE = 64
S5_LANES = S5_GROUPS * S5_STATE
RET_HEADS = 4
RET_DK = 128
RET_DV = 256
RET_QK = RET_HEADS * RET_DK
RET_V = RET_HEADS * RET_DV
RET_CHUNK = 128
ROPE_BASE = 10000.0
CONV_WIDTH = 512
CONV_K = 31
CONV_HIST = 32
N_EXPERTS = 8
TOP_K = 2
N_BRANCH = 3
EPS = 1e-6
N_MAIN = 2 * RET_QK + 2 * RET_V + 2 * CONV_WIDTH + N_BRANCH * D_MODEL
COL_Q = 0
COL_K = RET_QK
COL_V = 2 * RET_QK
COL_G = COL_V + RET_V
COL_CONV = COL_G + RET_V
COL_GATE = COL_CONV + 2 * CONV_WIDTH

VMEM_LIMIT = 48 * 1024 * 1024


def _params(sem):
    return pltpu.CompilerParams(dimension_semantics=sem, vmem_limit_bytes=VMEM_LIMIT)


def _silu(x):
    return x * jax.nn.sigmoid(x)


def _bdot(a, b):
    return jnp.dot(a.astype(BF16), b.astype(BF16), preferred_element_type=F32)


def _norm_proj_kernel(x_ref, g_ref, w_ref, o_ref, u_scr):
    @pl.when(pl.program_id(2) == 0)
    def _():
        x = x_ref[...]
        ms = jnp.mean(x * x, axis=-1, keepdims=True)
        u_scr[...] = (x * lax.rsqrt(ms + EPS) * g_ref[...]).astype(BF16)

    o_ref[...] = jnp.dot(u_scr[...], w_ref[...], preferred_element_type=F32)


def _norm_proj(x, g, w, *, tt, tn, time_major):
    bsz, seq, d = x.shape
    n = w.shape[1]
    grid = (bsz, seq // tt, n // tn)
    if time_major:
        assert tn == n
        out_shape = jax.ShapeDtypeStruct((seq, bsz * n), F32)
        out_spec = pl.BlockSpec((tt, tn), lambda b, i, j: (i, b))
    else:
        out_shape = jax.ShapeDtypeStruct((bsz, seq, n), F32)
        out_spec = pl.BlockSpec((None, tt, tn), lambda b, i, j: (b, i, j))
    return pl.pallas_call(
        _norm_proj_kernel,
        grid=grid,
        in_specs=[
            pl.BlockSpec((None, tt, d), lambda b, i, j: (b, i, 0)),
            pl.BlockSpec((1, d), lambda b, i, j: (0, 0)),
            pl.BlockSpec((d, tn), lambda b, i, j: (0, j)),
        ],
        out_specs=out_spec,
        out_shape=out_shape,
        scratch_shapes=[pltpu.VMEM((tt, d), BF16)],
        compiler_params=_params(("parallel", "parallel", "arbitrary")),
        name="norm_proj",
    )(x, g.reshape(1, d), w)


def _s5_disc_kernel(lre_ref, lim_ref, ldt_ref, bre_ref, bim_ref, are_ref, aim_ref, ore_ref, oim_ref):
    lam_re = lre_ref[...]
    lam_im = lim_ref[...]
    dt = jnp.exp(ldt_ref[...])
    mag = jnp.exp(lam_re * dt)
    ang = lam_im * dt
    lbar_re = mag * jnp.cos(ang)
    lbar_im = mag * jnp.sin(ang)
    den = lam_re * lam_re + lam_im * lam_im
    nr = lbar_re - 1.0
    f_re = (nr * lam_re + lbar_im * lam_im) / den
    f_im = (lbar_im * lam_re - nr * lam_im) / den
    b_re = bre_ref[...]
    b_im = bim_ref[...]
    are_ref[...] = lbar_re
    aim_ref[...] = lbar_im
    ore_ref[...] = f_re * b_re - f_im * b_im
    oim_ref[...] = f_re * b_im + f_im * b_re


def _s5_discretize(lam_re, lam_im, log_dt, b_re, b_im):
    g, n = lam_re.shape
    p = b_re.shape[-1]
    rows = g * n
    col = lambda a: a.reshape(rows, 1)
    ldt = jnp.broadcast_to(log_dt[:, None], (g, n))
    outs = pl.pallas_call(
        _s5_disc_kernel,
        out_shape=[jax.ShapeDtypeStruct((rows, 1), F32), jax.ShapeDtypeStruct((rows, 1), F32),
                   jax.ShapeDtypeStruct((rows, p), F32), jax.ShapeDtypeStruct((rows, p), F32)],
        name="s5_discretize",
    )(col(lam_re), col(lam_im), col(ldt), b_re.reshape(rows, p), b_im.reshape(rows, p))
    a_re, a_im, bb_re, bb_im = outs
    return a_re.reshape(g, n), a_im.reshape(g, n), bb_re.reshape(g, n, p), bb_im.reshape(g, n, p)


S5_KCH = 128
S5_NCHUNK = S5_WIDTH // S5_KCH
S5_GPC = S5_KCH // S5_GROUP
S5_SPC = S5_GPC * S5_STATE


def _s5_block_mats(bbar_re, bbar_im, c_re, c_im):
    eye = jnp.eye(S5_GPC, dtype=F32)

    def in_blocks(bb):
        t = bb.reshape(S5_NCHUNK, S5_GPC, S5_STATE, S5_GROUP)
        m = jnp.einsum("cgnp,gh->cgphn", t, eye)
        return m.reshape(S5_NCHUNK, S5_KCH, S5_SPC)

    def out_blocks(cc):
        t = cc.reshape(S5_NCHUNK, S5_GPC, S5_GROUP, S5_STATE)
        m = jnp.einsum("cgpn,gh->cgnhp", t, eye)
        return m.reshape(S5_NCHUNK, S5_SPC, S5_KCH)

    bmat = jnp.concatenate([in_blocks(bbar_re), in_blocks(bbar_im)], axis=-1).astype(BF16)
    cmat = jnp.stack([out_blocks(c_re), -out_blocks(c_im)], axis=1).astype(BF16)
    return bmat, cmat


def _gelu_tanh(x):
    return 0.5 * x * (1.0 + jnp.tanh(math.sqrt(2.0 / math.pi) * (x + 0.044715 * (x * x * x))))


def _s5_kernel(u_ref, h0_ref, are_ref, aim_ref, bmat_ref, cmat_ref, d_ref, gw_ref, gb_ref,
               o_ref, hout_ref, hs_scr, h_scr, io_scr, *, nb, tt, lane_chunk):
    @pl.when(pl.program_id(0) == 0)
    def _():
        h_scr[...] = h0_ref[...]

    def seq_rows(b):
        return pl.ds(b, tt, stride=nb) if tt > 1 else pl.ds(b, 1)

    if tt == 1:
        for c in range(S5_NCHUNK):
            io_scr[c] = u_ref[:, c * S5_KCH:(c + 1) * S5_KCH]
    else:
        for b in range(nb):
            for c in range(S5_NCHUNK):
                col = b * S5_WIDTH + c * S5_KCH
                io_scr[c, seq_rows(b), :] = u_ref[:, col:col + S5_KCH]

    u = jnp.concatenate([io_scr[c] for c in range(S5_NCHUNK)], axis=-1)
    for c in range(S5_NCHUNK):
        bu = jnp.dot(io_scr[c].astype(BF16), bmat_ref[c], preferred_element_type=F32)
        hs_scr[:, c * S5_SPC:(c + 1) * S5_SPC] = bu[:, :S5_SPC]
        hs_scr[:, S5_LANES + c * S5_SPC:S5_LANES + (c + 1) * S5_SPC] = bu[:, S5_SPC:]

    for lc in range(S5_LANES // lane_chunk):
        re_sl = slice(lc * lane_chunk, (lc + 1) * lane_chunk)
        im_sl = slice(S5_LANES + lc * lane_chunk, S5_LANES + (lc + 1) * lane_chunk)
        a_re = jnp.broadcast_to(are_ref[:, re_sl], (nb, lane_chunk))
        a_im = jnp.broadcast_to(aim_ref[:, re_sl], (nb, lane_chunk))

        def step(t, carry):
            h_re, h_im = carry
            r0 = t * nb if isinstance(t, int) else pl.multiple_of(t * nb, nb)
            n_re = a_re * h_re - a_im * h_im + hs_scr[pl.ds(r0, nb), re_sl]
            n_im = a_re * h_im + a_im * h_re + hs_scr[pl.ds(r0, nb), im_sl]
            hs_scr[pl.ds(r0, nb), re_sl] = n_re
            hs_scr[pl.ds(r0, nb), im_sl] = n_im
            return n_re, n_im

        carry = (h_scr[:, re_sl], h_scr[:, im_sl])
        if tt == 1:
            carry = step(0, carry)
        else:
            carry = lax.fori_loop(0, tt, step, carry, unroll=4)
        h_scr[:, re_sl] = carry[0]
        h_scr[:, im_sl] = carry[1]

    hout_ref[...] = h_scr[...]

    ys = []
    for c in range(S5_NCHUNK):
        h_re = hs_scr[:, c * S5_SPC:(c + 1) * S5_SPC].astype(BF16)
        h_im = hs_scr[:, S5_LANES + c * S5_SPC:S5_LANES + (c + 1) * S5_SPC].astype(BF16)
        ys.append(jnp.dot(h_re, cmat_ref[c, 0], preferred_element_type=F32)
                  + jnp.dot(h_im, cmat_ref[c, 1], preferred_element_type=F32))
    y = jnp.concatenate(ys, axis=-1) + d_ref[...] * u
    z = _gelu_tanh(y)
    gate = jnp.dot(z.astype(BF16), gw_ref[...], preferred_element_type=F32) + gb_ref[...]
    out = z * jax.nn.sigmoid(gate)
    if tt == 1:
        o_ref[...] = out
    else:
        for c in range(S5_NCHUNK):
            io_scr[c] = out[:, c * S5_KCH:(c + 1) * S5_KCH]
        for b in range(nb):
            for c in range(S5_NCHUNK):
                col = b * S5_WIDTH + c * S5_KCH
                o_ref[:, col:col + S5_KCH] = io_scr[c, seq_rows(b), :]


def _s5_branch(u, h0, a_re, a_im, bmat, cmat, d_skip, glu_w, glu_b, *, nb, tt):
    steps, width = u.shape
    if tt == 1:
        assert steps == nb and width == S5_WIDTH
        blk = (nb, S5_WIDTH)
        grid = (1,)
    else:
        assert width == nb * S5_WIDTH
        blk = (tt, width)
        grid = (steps // tt,)
    rblk = tt * nb
    lane_chunk = 1024 if nb <= 8 else 512
    const = lambda shape: pl.BlockSpec(shape, lambda i: (0,) * len(shape))
    return pl.pallas_call(
        functools.partial(_s5_kernel, nb=nb, tt=tt, lane_chunk=lane_chunk),
        grid=grid,
        in_specs=[
            pl.BlockSpec(blk, lambda i: (i, 0)),
            const((nb, 2 * S5_LANES)),
            const((1, S5_LANES)),
            const((1, S5_LANES)),
            const(bmat.shape),
            const(cmat.shape),
            const((1, S5_WIDTH)),
            const((S5_WIDTH, S5_WIDTH)),
            const((1, S5_WIDTH)),
        ],
        out_specs=[pl.BlockSpec(blk, lambda i: (i, 0)), const((nb, 2 * S5_LANES))],
        out_shape=[jax.ShapeDtypeStruct(u.shape, F32), jax.ShapeDtypeStruct((nb, 2 * S5_LANES), F32)],
        scratch_shapes=[pltpu.VMEM((rblk, 2 * S5_LANES), F32), pltpu.VMEM((nb, 2 * S5_LANES), F32),
                        pltpu.VMEM((S5_NCHUNK, rblk, S5_KCH), F32)],
        compiler_params=_params(("arbitrary",)),
        name="s5_branch",
    )(u, h0, a_re.reshape(1, S5_LANES), a_im.reshape(1, S5_LANES), bmat, cmat,
      d_skip.reshape(1, S5_WIDTH), glu_w, glu_b.reshape(1, S5_WIDTH))


def _rope_tables(pos):
    half = RET_DK // 2
    freqs = ROPE_BASE ** (-jnp.arange(half, dtype=F32) / half)
    ang = pos[:, None] * freqs[None, :]
    cos = jnp.cos(ang)
    sin = jnp.sin(ang)
    return jnp.concatenate([cos, cos], axis=-1), jnp.concatenate([-sin, sin], axis=-1)


def _rope(x, cos, sin):
    return x * cos + pltpu.roll(x, RET_DK // 2, 1) * sin


def _group_norm(o):
    mu = jnp.mean(o, axis=-1, keepdims=True)
    d = o - mu
    var = jnp.mean(d * d, axis=-1, keepdims=True)
    return d * lax.rsqrt(var + EPS)


def _retention_tables(chunk):
    log_gamma = jnp.log(1.0 - 2.0 ** (-5.0 - jnp.arange(RET_HEADS, dtype=F32)))
    idx = jnp.arange(chunk, dtype=F32)
    diff = idx[:, None] - idx[None, :]
    decay = jnp.where(diff >= 0, jnp.exp(jnp.maximum(diff, 0.0)[None] * log_gamma[:, None, None]), 0.0)
    cross = jnp.exp((idx + 1.0)[None, :] * log_gamma[:, None])[:, :, None]
    kdec = jnp.exp((chunk - 1.0 - idx)[None, :] * log_gamma[:, None])[:, :, None]
    full = jnp.exp(chunk * log_gamma)
    return decay, cross, kdec, full


def _retention_kernel(q_ref, k_ref, v_ref, g_ref, cos_ref, sin_ref, s0_ref, decay_ref, cross_ref, kdec_ref,
                      full_ref, o_ref, sout_ref, s_scr, *, n_chunks):
    @pl.when(pl.program_id(1) == 0)
    def _():
        s_scr[...] = s0_ref[...]

    for c in range(n_chunks):
        rows = slice(c * RET_CHUNK, (c + 1) * RET_CHUNK)
        cos = cos_ref[rows, :]
        sin = sin_ref[rows, :]
        for h in range(RET_HEADS):
            qk_cols = slice(h * RET_DK, (h + 1) * RET_DK)
            v_cols = slice(h * RET_DV, (h + 1) * RET_DV)
            qh = _rope(q_ref[rows, qk_cols], cos, sin)
            kh = _rope(k_ref[rows, qk_cols], cos, sin) * (RET_DK ** -0.5)
            vb = v_ref[rows, v_cols].astype(BF16)
            qb = qh.astype(BF16)
            state = s_scr[h]
            inner = lax.dot_general(qb, kh.astype(BF16), (((1,), (1,)), ((), ())),
                                    preferred_element_type=F32) * decay_ref[h]
            out = (jnp.dot(inner.astype(BF16), vb, preferred_element_type=F32)
                   + jnp.dot(qb, state.astype(BF16), preferred_element_type=F32) * cross_ref[h])
            kd = (kh * kdec_ref[h]).astype(BF16)
            s_scr[h] = full_ref[h] * state + jnp.dot(kd.T, vb, preferred_element_type=F32)
            o_ref[rows, v_cols] = _silu(g_ref[rows, v_cols]) * _group_norm(out)

    sout_ref[...] = s_scr[...]


def _retention_prompt(z_main, state0, pos_offset, *, tt):
    bsz, seq, _ = z_main.shape
    assert seq % RET_CHUNK == 0 and tt % RET_CHUNK == 0
    cos, sin = _rope_tables(jnp.arange(seq, dtype=F32) + pos_offset)
    decay, cross, kdec, full = _retention_tables(RET_CHUNK)
    full = jnp.broadcast_to(full[:, None, None], (RET_HEADS, 1, RET_DV))
    const = lambda shape: pl.BlockSpec(shape, lambda b, i: (0,) * len(shape))
    return pl.pallas_call(
        functools.partial(_retention_kernel, n_chunks=tt // RET_CHUNK),
        grid=(bsz, seq // tt),
        in_specs=[
            pl.BlockSpec((None, tt, RET_QK), lambda b, i: (b, i, COL_Q // RET_QK)),
            pl.BlockSpec((None, tt, RET_QK), lambda b, i: (b, i, COL_K // RET_QK)),
            pl.BlockSpec((None, tt, RET_V), lambda b, i: (b, i, COL_V // RET_V)),
            pl.BlockSpec((None, tt, RET_V), lambda b, i: (b, i, COL_G // RET_V)),
            pl.BlockSpec((tt, RET_DK), lambda b, i: (i, 0)),
            pl.BlockSpec((tt, RET_DK), lambda b, i: (i, 0)),
            pl.BlockSpec((None, RET_HEADS, RET_DK, RET_DV), lambda b, i: (b, 0, 0, 0)),
            const((RET_HEADS, RET_CHUNK, RET_CHUNK)),
            const((RET_HEADS, RET_CHUNK, 1)),
            const((RET_HEADS, RET_CHUNK, 1)),
            const((RET_HEADS, 1, RET_DV)),
        ],
        out_specs=[
            pl.BlockSpec((None, tt, RET_V), lambda b, i: (b, i, 0)),
            pl.BlockSpec((None, RET_HEADS, RET_DK, RET_DV), lambda b, i: (b, 0, 0, 0)),
        ],
        out_shape=[jax.ShapeDtypeStruct((bsz, seq, RET_V), F32),
                   jax.ShapeDtypeStruct((bsz, RET_HEADS, RET_DK, RET_DV), F32)],
        scratch_shapes=[pltpu.VMEM((RET_HEADS, RET_DK, RET_DV), F32)],
        compiler_params=_params(("parallel", "arbitrary")),
        name="retention_prompt",
    )(z_main, z_main, z_main, z_main, cos, sin, state0, decay, cross, kdec, full)


def _retention_step_kernel(q_ref, k_ref, v_ref, g_ref, cos_ref, sin_ref, s_ref, gam_ref, *rest, bb, layer):
    if layer == 0:
        o_ref, stack_ref, o_scr = rest
        sout_ref = stack_ref.at[0]
        for other in range(1, stack_ref.shape[0]):
            stack_ref[other] = jnp.zeros(stack_ref.shape[1:], F32)
    else:
        _, o_ref, sout_ref, o_scr = rest
    cos = cos_ref[...]
    sin = sin_ref[...]
    for h in range(RET_HEADS):
        qk_cols = slice(h * RET_DK, (h + 1) * RET_DK)
        v_cols = slice(h * RET_DV, (h + 1) * RET_DV)
        qh = _rope(q_ref[:, qk_cols], cos, sin)
        kh = _rope(k_ref[:, qk_cols], cos, sin) * (RET_DK ** -0.5)
        qk = jnp.sum(qh * kh, axis=-1, keepdims=True)
        q_t = qh.T
        k_t = kh.T
        gamma = gam_ref[h]
        for b in range(bb):
            state = s_ref[b, h]
            vrow = v_ref[b:b + 1, v_cols]
            qs = jnp.sum(q_t[:, b:b + 1] * state, axis=0, keepdims=True)
            o_scr[b:b + 1, v_cols] = qk[b:b + 1, :] * vrow + qs * gamma
            sout_ref[b, h] = gamma * state + k_t[:, b:b + 1] * vrow
    for h in range(RET_HEADS):
        v_cols = slice(h * RET_DV, (h + 1) * RET_DV)
        o_ref[:, v_cols] = _silu(g_ref[:, v_cols]) * _group_norm(o_scr[:, v_cols])


def _retention_step(z_main, states, layer, stack, pos, *, bb):
    n = z_main.shape[0]
    depth = states.shape[0]
    cos, sin = _rope_tables(jnp.full((1,), pos, F32))
    log_gamma = jnp.log(1.0 - 2.0 ** (-5.0 - jnp.arange(RET_HEADS, dtype=F32)))
    gam = jnp.broadcast_to(jnp.exp(log_gamma)[:, None, None], (RET_HEADS, 1, RET_DV))
    const = lambda shape: pl.BlockSpec(shape, lambda i: (0,) * len(shape))
    slab = pl.BlockSpec((None, bb, RET_HEADS, RET_DK, RET_DV), lambda i: (layer, i, 0, 0, 0))
    in_specs = [
        pl.BlockSpec((bb, RET_QK), lambda i: (i, COL_Q // RET_QK)),
        pl.BlockSpec((bb, RET_QK), lambda i: (i, COL_K // RET_QK)),
        pl.BlockSpec((bb, RET_V), lambda i: (i, COL_V // RET_V)),
        pl.BlockSpec((bb, RET_V), lambda i: (i, COL_G // RET_V)),
        const((1, RET_DK)),
        const((1, RET_DK)),
        slab,
        const((RET_HEADS, 1, RET_DV)),
    ]
    args = [z_main, z_main, z_main, z_main, cos, sin, states, gam]
    if layer == 0:
        stack_spec = pl.BlockSpec((depth, bb, RET_HEADS, RET_DK, RET_DV), lambda i: (0, i, 0, 0, 0))
        aliases = {}
    else:
        in_specs.append(pl.BlockSpec(memory_space=pl.ANY))
        args.append(stack)
        stack_spec = slab
        aliases = {len(args) - 1: 1}
    return pl.pallas_call(
        functools.partial(_retention_step_kernel, bb=bb, layer=layer),
        grid=(n // bb,),
        in_specs=in_specs,
        out_specs=[pl.BlockSpec((bb, RET_V), lambda i: (i, 0)), stack_spec],
        out_shape=[jax.ShapeDtypeStruct((n, RET_V), F32), jax.ShapeDtypeStruct(states.shape, F32)],
        scratch_shapes=[pltpu.VMEM((bb, RET_V), F32)],
        input_output_aliases=aliases,
        compiler_params=_params(("parallel",)),
        name="retention_step",
    )(*args)


def _layer_norm(y, g, b):
    mu = jnp.mean(y, axis=-1, keepdims=True)
    d = y - mu
    var = jnp.mean(d * d, axis=-1, keepdims=True)
    return d * lax.rsqrt(var + EPS) * g + b


CONV_RB = 32


def _conv_kernel(a_ref, buf_ref, pwb_ref, dww_ref, dwb_ref, lng_ref, lnb_ref, o_ref, hist_ref, x_scr, *, tt):
    i = pl.program_id(1)

    @pl.when(i == 0)
    def _():
        x_scr[0:CONV_HIST, :] = buf_ref[...]

    @pl.when(i > 0)
    def _():
        x_scr[0:CONV_HIST, :] = x_scr[tt:tt + CONV_HIST, :]

    a = a_ref[...] + pwb_ref[...]
    x_scr[CONV_HIST:CONV_HIST + tt, :] = a[:, :CONV_WIDTH] * jax.nn.sigmoid(a[:, CONV_WIDTH:])
    hist_ref[...] = x_scr[tt:tt + CONV_HIST, :]

    off = CONV_HIST - (CONV_K - 1)
    for r in range(tt // CONV_RB):
        acc = jnp.broadcast_to(dwb_ref[...], (CONV_RB, CONV_WIDTH))
        for k in range(CONV_K):
            acc = acc + dww_ref[k:k + 1, :] * x_scr[r * CONV_RB + off + k:r * CONV_RB + off + k + CONV_RB, :]
        yn = _layer_norm(acc, lng_ref[...], lnb_ref[...])
        o_ref[r * CONV_RB:(r + 1) * CONV_RB, :] = _silu(yn)


def _conv_prompt(z_main, buf, pw_b, dw_w, dw_b, ln_g, ln_b, *, tt):
    bsz, seq, _ = z_main.shape
    assert seq >= CONV_HIST and tt >= CONV_HIST
    buf32 = jnp.pad(buf, ((0, 0), (CONV_HIST - (CONV_K - 1), 0), (0, 0)))
    const = lambda shape: pl.BlockSpec(shape, lambda b, i: (0,) * len(shape))
    out, hist = pl.pallas_call(
        functools.partial(_conv_kernel, tt=tt),
        grid=(bsz, seq // tt),
        in_specs=[
            pl.BlockSpec((None, tt, 2 * CONV_WIDTH), lambda b, i: (b, i, COL_CONV // (2 * CONV_WIDTH))),
            pl.BlockSpec((None, CONV_HIST, CONV_WIDTH), lambda b, i: (b, 0, 0)),
            const((1, 2 * CONV_WIDTH)),
            const((CONV_K, CONV_WIDTH)),
            const((1, CONV_WIDTH)),
            const((1, CONV_WIDTH)),
            const((1, CONV_WIDTH)),
        ],
        out_specs=[
            pl.BlockSpec((None, tt, CONV_WIDTH), lambda b, i: (b, i, 0)),
            pl.BlockSpec((None, CONV_HIST, CONV_WIDTH), lambda b, i: (b, 0, 0)),
        ],
        out_shape=[jax.ShapeDtypeStruct((bsz, seq, CONV_WIDTH), F32),
                   jax.ShapeDtypeStruct((bsz, CONV_HIST, CONV_WIDTH), F32)],
        scratch_shapes=[pltpu.VMEM((CONV_HIST + tt, CONV_WIDTH), F32)],
        compiler_params=_params(("parallel", "arbitrary")),
        name="conv_prompt",
    )(z_main, buf32, pw_b.reshape(1, -1), dw_w, dw_b.reshape(1, -1), ln_g.reshape(1, -1), ln_b.reshape(1, -1))
    return out, hist[:, CONV_HIST - (CONV_K - 1):, :]


def _conv_step_kernel(a_ref, buf_ref, pwb_ref, dww_ref, dwb_ref, lng_ref, lnb_ref, o_ref, hist_ref):
    hist_len = CONV_K - 1
    a = a_ref[...] + pwb_ref[...]
    glu = a[:, :CONV_WIDTH] * jax.nn.sigmoid(a[:, CONV_WIDTH:])
    acc = dwb_ref[...] + dww_ref[hist_len:hist_len + 1, :] * glu
    for k in range(hist_len):
        acc = acc + dww_ref[k:k + 1, :] * buf_ref[:, k * CONV_WIDTH:(k + 1) * CONV_WIDTH]
    o_ref[...] = _silu(_layer_norm(acc, lng_ref[...], lnb_ref[...]))
    hist_ref[:, :(hist_len - 1) * CONV_WIDTH] = buf_ref[:, CONV_WIDTH:]
    hist_ref[:, (hist_len - 1) * CONV_WIDTH:] = glu


def _conv_step(z_main, buf, pw_b, dw_w, dw_b, ln_g, ln_b, *, bb):
    n = z_main.shape[0]
    hist_len = CONV_K - 1
    const = lambda shape: pl.BlockSpec(shape, lambda i: (0,) * len(shape))
    out, hist = pl.pallas_call(
        _conv_step_kernel,
        grid=(n // bb,),
        in_specs=[
            pl.BlockSpec((bb, 2 * CONV_WIDTH), lambda i: (i, COL_CONV // (2 * CONV_WIDTH))),
            pl.BlockSpec((bb, hist_len * CONV_WIDTH), lambda i: (i, 0)),
            const((1, 2 * CONV_WIDTH)),
            const((CONV_K, CONV_WIDTH)),
            const((1, CONV_WIDTH)),
            const((1, CONV_WIDTH)),
            const((1, CONV_WIDTH)),
        ],
        out_specs=[
            pl.BlockSpec((bb, CONV_WIDTH), lambda i: (i, 0)),
            pl.BlockSpec((bb, hist_len * CONV_WIDTH), lambda i: (i, 0)),
        ],
        out_shape=[jax.ShapeDtypeStruct((n, CONV_WIDTH), F32),
                   jax.ShapeDtypeStruct((n, hist_len * CONV_WIDTH), F32)],
        compiler_params=_params(("parallel",)),
        name="conv_step",
    )(z_main, buf.reshape(n, hist_len * CONV_WIDTH), pw_b.reshape(1, -1), dw_w, dw_b.reshape(1, -1),
      ln_g.reshape(1, -1), ln_b.reshape(1, -1))
    return out, hist.reshape(n, hist_len, CONV_WIDTH)


ROW_TILE = 8
LANES = 128
ROW_CHUNKS = D_MODEL // LANES


def _rows_to_tiles(tile_ref, x, rows):
    for c in range(ROW_CHUNKS):
        tile_ref[pl.ds(c, rows, stride=ROW_TILE), :] = x[:, c * LANES:(c + 1) * LANES]


def _tiles_chunk(tile_ref, c, rows):
    return tile_ref[pl.ds(c, rows, stride=ROW_TILE), :]


def _tiles_to_rows(tile_ref, rows):
    return jnp.concatenate([_tiles_chunk(tile_ref, c, rows) for c in range(ROW_CHUNKS)], axis=-1)


def _merge_kernel(x_ref, s5_ref, ret_ref, conv_ref, g1_ref, g2_ref, g3_ref, ps5_ref, pret_ref, pconv_ref, wout_ref,
                  g_ref, h_ref, u_ref, *, tiled_u):
    merged = (jax.nn.sigmoid(g1_ref[...]) * _bdot(s5_ref[...], ps5_ref[...])
              + jax.nn.sigmoid(g2_ref[...]) * _bdot(ret_ref[...], pret_ref[...])
              + jax.nn.sigmoid(g3_ref[...]) * _bdot(conv_ref[...], pconv_ref[...]))
    h = x_ref[...] + _bdot(merged, wout_ref[...])
    h_ref[...] = h
    ms = jnp.mean(h * h, axis=-1, keepdims=True)
    u = h * lax.rsqrt(ms + EPS) * g_ref[...]
    if tiled_u:
        _rows_to_tiles(u_ref, u, u.shape[0])
    else:
        u_ref[...] = u


def _merge(x, s5_tm, ret_out, conv_out, z_main, s5_proj, ret_proj, conv_proj, w_out, norm_g, *, tm, tiled_u):
    bsz, seq, d = x.shape
    const = lambda shape: pl.BlockSpec(shape, lambda b, i: (0,) * len(shape))
    tok = lambda w, col=0: pl.BlockSpec((None, tm, w), lambda b, i: (b, i, col))
    n_i = seq // tm
    if tiled_u:
        u_spec = pl.BlockSpec((tm * ROW_TILE, LANES), lambda b, i: (b * n_i + i, 0))
        u_shape = jax.ShapeDtypeStruct((bsz * seq * ROW_TILE, LANES), F32)
    else:
        u_spec = tok(d)
        u_shape = jax.ShapeDtypeStruct((bsz, seq, d), F32)
    return pl.pallas_call(
        functools.partial(_merge_kernel, tiled_u=tiled_u),
        grid=(bsz, seq // tm),
        in_specs=[
            tok(d),
            pl.BlockSpec((tm, S5_WIDTH), lambda b, i: (i, b)),
            tok(RET_V),
            tok(CONV_WIDTH),
            tok(d, COL_GATE // d),
            tok(d, COL_GATE // d + 1),
            tok(d, COL_GATE // d + 2),
            const((S5_WIDTH, d)),
            const((RET_V, d)),
            const((CONV_WIDTH, d)),
            const((d, d)),
            const((1, d)),
        ],
        out_specs=[tok(d), u_spec],
        out_shape=[jax.ShapeDtypeStruct((bsz, seq, d), F32), u_shape],
        compiler_params=_params(("parallel", "parallel")),
        name="merge",
    )(x, s5_tm, ret_out, conv_out, z_main, z_main, z_main, s5_proj, ret_proj, conv_proj, w_out,
      norm_g.reshape(1, d))


def _ffn_kernel(h_ref, u_ref, wg_ref, wu_ref, wd_ref, o_ref):
    f = pl.program_id(1)
    ub = u_ref[...].astype(BF16)
    gate = jnp.dot(ub, wg_ref[...], preferred_element_type=F32)
    up = jnp.dot(ub, wu_ref[...], preferred_element_type=F32)
    part = _bdot(_silu(gate) * up, wd_ref[...])

    @pl.when(f == 0)
    def _():
        o_ref[...] = h_ref[...] + part

    @pl.when(f > 0)
    def _():
        o_ref[...] = o_ref[...] + part


def _ffn(h, u, w_gate, w_up, w_down, *, tm, tf):
    rows, d = h.shape
    dff = w_gate.shape[1]
    tok = pl.BlockSpec((tm, d), lambda i, f: (i, 0))
    return pl.pallas_call(
        _ffn_kernel,
        grid=(rows // tm, dff // tf),
        in_specs=[tok, tok,
                  pl.BlockSpec((d, tf), lambda i, f: (0, f)),
                  pl.BlockSpec((d, tf), lambda i, f: (0, f)),
                  pl.BlockSpec((tf, d), lambda i, f: (f, 0))],
        out_specs=tok,
        out_shape=jax.ShapeDtypeStruct((rows, d), F32),
        compiler_params=_params(("parallel", "arbitrary")),
        name="ffn_dense",
    )(h, u, w_gate, w_up, w_down)


def _split_bf16(x):
    hi = x.astype(BF16)
    return hi, (x - hi.astype(F32)).astype(BF16)


def _router_kernel(u_ref, rt_ref, tri_ref, idx_ref, wts_ref, rank_ref, cnt_ref, cnt_scr, *, tm):
    @pl.when(pl.program_id(0) == 0)
    def _():
        cnt_scr[...] = jnp.zeros_like(cnt_scr)

    u_hi, u_lo = _split_bf16(_tiles_to_rows(u_ref, tm))
    r_hi, r_lo = _split_bf16(rt_ref[...])
    dn = (((1,), (1,)), ((), ()))
    logits = (lax.dot_general(r_hi, u_hi, dn, preferred_element_type=F32)
              + lax.dot_general(r_lo, u_hi, dn, preferred_element_type=F32)
              + lax.dot_general(r_hi, u_lo, dn, preferred_element_type=F32))
    eidx = lax.broadcasted_iota(jnp.int32, logits.shape, 0)
    m1 = jnp.max(logits, axis=0, keepdims=True)
    i1 = jnp.min(jnp.where(logits == m1, eidx, N_EXPERTS), axis=0, keepdims=True)
    rest = jnp.where(eidx == i1, -jnp.inf, logits)
    m2 = jnp.max(rest, axis=0, keepdims=True)
    i2 = jnp.min(jnp.where(rest == m2, eidx, N_EXPERTS), axis=0, keepdims=True)
    e2 = jnp.exp(m2 - m1)
    w1 = 1.0 / (1.0 + e2)
    idx_ref[...] = jnp.concatenate([i1, i2], axis=0)
    wts_ref[...] = jnp.concatenate([w1, e2 * w1], axis=0)

    hit1 = eidx == i1
    hit2 = eidx == i2
    hits = jnp.where(hit1 | hit2, 1.0, 0.0)
    before = jnp.dot(hits.astype(BF16), tri_ref[...], preferred_element_type=F32) + cnt_scr[...]
    rank_ref[...] = jnp.concatenate(
        [jnp.sum(jnp.where(hit1, before, 0.0), axis=0, keepdims=True),
         jnp.sum(jnp.where(hit2, before, 0.0), axis=0, keepdims=True)], axis=0).astype(jnp.int32)
    cnt_scr[...] = cnt_scr[...] + jnp.sum(hits, axis=1, keepdims=True)
    cnt_ref[...] = cnt_scr[...]


def _router(u8, router, *, tm):
    rows = u8.shape[0] // ROW_TILE
    d = D_MODEL
    tri = (jnp.arange(tm)[:, None] < jnp.arange(tm)[None, :]).astype(BF16)
    const = lambda shape: pl.BlockSpec(shape, lambda i: (0,) * len(shape))
    lane = pl.BlockSpec((TOP_K, tm), lambda i: (0, i))
    return pl.pallas_call(
        functools.partial(_router_kernel, tm=tm),
        grid=(rows // tm,),
        in_specs=[pl.BlockSpec((tm * ROW_TILE, LANES), lambda i: (i, 0)), const((N_EXPERTS, d)), const((tm, tm))],
        out_specs=[lane, lane, lane, const((N_EXPERTS, 1))],
        out_shape=[jax.ShapeDtypeStruct((TOP_K, rows), jnp.int32), jax.ShapeDtypeStruct((TOP_K, rows), F32),
                   jax.ShapeDtypeStruct((TOP_K, rows), jnp.int32), jax.ShapeDtypeStruct((N_EXPERTS, 1), F32)],
        scratch_shapes=[pltpu.VMEM((N_EXPERTS, 1), F32)],
        compiler_params=_params(("arbitrary",)),
        name="moe_router",
    )(u8, router.T, tri)


DMA_UNROLL = 8


def _token_tile(ref, r):
    return ref.at[pl.ds(pl.multiple_of(r * ROW_TILE, ROW_TILE), ROW_TILE)]


def _dispatch_kernel(pos_ref, u_ref, init_ref, xs_ref, sem, *, tb):
    del init_ref

    def copy(r, s):
        return pltpu.make_async_copy(_token_tile(u_ref, r), _token_tile(xs_ref, pos_ref[s, r]), sem)

    def start(r, carry):
        for s in range(TOP_K):
            copy(r, s).start()
        return carry

    def wait(r, carry):
        for s in range(TOP_K):
            copy(r, s).wait()
        return carry

    lax.fori_loop(0, tb, start, 0, unroll=DMA_UNROLL)
    lax.fori_loop(0, tb, wait, 0, unroll=DMA_UNROLL)


def _dispatch(u8, pos, xs8, *, tb):
    rows = u8.shape[0] // ROW_TILE
    return pl.pallas_call(
        functools.partial(_dispatch_kernel, tb=tb),
        grid=(rows // tb,),
        in_specs=[
            pl.BlockSpec((TOP_K, tb), lambda i: (0, i), memory_space=pltpu.SMEM),
            pl.BlockSpec((tb * ROW_TILE, LANES), lambda i: (i, 0)),
            pl.BlockSpec(memory_space=pl.ANY),
        ],
        out_specs=pl.BlockSpec(memory_space=pl.ANY),
        out_shape=jax.ShapeDtypeStruct(xs8.shape, xs8.dtype),
        scratch_shapes=[pltpu.SemaphoreType.DMA(())],
        input_output_aliases={2: 0},
        compiler_params=_params(("arbitrary",)),
        name="moe_dispatch",
    )(pos, u8, xs8)


def _experts_kernel(te_ref, nu_ref, x_ref, wg_ref, wu_ref, wd_ref, o_ref, x_scr, acc_scr, *, tm):
    del te_ref
    i = pl.program_id(0)
    f = pl.program_id(1)

    @pl.when(i < nu_ref[0])
    def _():
        @pl.when(f == 0)
        def _():
            x_scr[...] = _tiles_to_rows(x_ref, tm).astype(BF16)

        xb = x_scr[...]
        gate = jnp.dot(xb, wg_ref[...], preferred_element_type=F32)
        up = jnp.dot(xb, wu_ref[...], preferred_element_type=F32)
        part = _bdot(_silu(gate) * up, wd_ref[...])

        @pl.when(f == 0)
        def _():
            acc_scr[...] = part

        @pl.when(f > 0)
        def _():
            acc_scr[...] = acc_scr[...] + part

        @pl.when(f == pl.num_programs(1) - 1)
        def _():
            _rows_to_tiles(o_ref, acc_scr[...], tm)

    @pl.when(i >= nu_ref[0])
    def _():
        o_ref[...] = jnp.zeros_like(o_ref)


def _experts(xs8, tile_expert, n_used, w_gate, w_up, w_down, *, tm, tf):
    rows = xs8.shape[0] // ROW_TILE
    d = D_MODEL
    dff = w_gate.shape[-1]
    n_f = dff // tf
    last_f = n_f - 1

    def row_map(i, f, te, nu):
        return (jnp.minimum(i, nu[0] - 1), 0)

    def fsel(i, f, nu):
        return jnp.where(i < nu[0], f, last_f)

    grid_spec = pltpu.PrefetchScalarGridSpec(
        num_scalar_prefetch=2,
        grid=(rows // tm, n_f),
        in_specs=[
            pl.BlockSpec((tm * ROW_TILE, LANES), row_map),
            pl.BlockSpec((None, d, tf), lambda i, f, te, nu: (te[i], 0, fsel(i, f, nu))),
            pl.BlockSpec((None, d, tf), lambda i, f, te, nu: (te[i], 0, fsel(i, f, nu))),
            pl.BlockSpec((None, tf, d), lambda i, f, te, nu: (te[i], fsel(i, f, nu), 0)),
        ],
        out_specs=pl.BlockSpec((tm * ROW_TILE, LANES), lambda i, f, te, nu: (i, 0)),
        scratch_shapes=[pltpu.VMEM((tm, d), BF16), pltpu.VMEM((tm, d), F32)],
    )
    return pl.pallas_call(
        functools.partial(_experts_kernel, tm=tm),
        grid_spec=grid_spec,
        out_shape=jax.ShapeDtypeStruct(xs8.shape, F32),
        compiler_params=_params(("arbitrary", "arbitrary")),
        name="moe_experts",
    )(tile_expert, n_used, xs8, w_gate, w_up, w_down)


def _combine_kernel(pos_ref, h_ref, wts_ref, g_ref, ys_ref, o_ref, y_scr, sem, *, tb, final_norm):
    def copy(r, s):
        return pltpu.make_async_copy(_token_tile(ys_ref, pos_ref[s, r]), _token_tile(y_scr.at[s], r), sem)

    def start(r, carry):
        for s in range(TOP_K):
            copy(r, s).start()
        return carry

    def wait(r, carry):
        for s in range(TOP_K):
            copy(r, s).wait()
        return carry

    lax.fori_loop(0, tb, start, 0, unroll=DMA_UNROLL)
    lax.fori_loop(0, tb, wait, 0, unroll=DMA_UNROLL)
    w1 = wts_ref[:, 0:1]
    w2 = wts_ref[:, 1:2]
    moe = jnp.concatenate([w1 * _tiles_chunk(y_scr.at[0], c, tb) + w2 * _tiles_chunk(y_scr.at[1], c, tb)
                           for c in range(ROW_CHUNKS)], axis=-1)
    out = h_ref[...] + moe
    if final_norm:
        ms = jnp.mean(out * out, axis=-1, keepdims=True)
        out = out * lax.rsqrt(ms + EPS) * g_ref[...]
    o_ref[...] = out


def _combine(h, pos, wts_t, ys8, final_g, *, tb):
    rows, d = h.shape
    final_norm = final_g is not None
    gain = final_g.reshape(1, d) if final_norm else jnp.ones((1, d), F32)
    return pl.pallas_call(
        functools.partial(_combine_kernel, tb=tb, final_norm=final_norm),
        grid=(rows // tb,),
        in_specs=[
            pl.BlockSpec((TOP_K, tb), lambda i: (0, i), memory_space=pltpu.SMEM),
            pl.BlockSpec((tb, d), lambda i: (i, 0)),
            pl.BlockSpec((tb, TOP_K), lambda i: (i, 0)),
            pl.BlockSpec((1, d), lambda i: (0, 0)),
            pl.BlockSpec(memory_space=pl.ANY),
        ],
        out_specs=pl.BlockSpec((tb, d), lambda i: (i, 0)),
        out_shape=jax.ShapeDtypeStruct((rows, d), F32),
        scratch_shapes=[pltpu.VMEM((TOP_K, tb * ROW_TILE, LANES), F32), pltpu.SemaphoreType.DMA(())],
        compiler_params=_params(("arbitrary",)),
        name="moe_combine",
    )(pos, h, wts_t, gain, ys8)


FFN_TF = 1408
MOE_TM = 512
MOE_TF = 1792


def _moe(h_list, u8_list, router, w_gate, w_up, w_down, final_g):
    routes = []
    for u8 in u8_list:
        routes.append(_router(u8, router, tm=min(512, u8.shape[0] // ROW_TILE)))
    counts = [r[3][:, 0].astype(jnp.int32) for r in routes]
    total = sum(counts)
    padded = ((total + MOE_TM - 1) // MOE_TM) * MOE_TM
    ends = jnp.cumsum(padded)
    starts = ends - padded
    n_rows = sum(h.shape[0] for h in h_list) * TOP_K
    n_tiles = n_rows // MOE_TM + N_EXPERTS
    n_used = (ends[-1] // MOE_TM).astype(jnp.int32)
    tile_start = jnp.arange(n_tiles, dtype=jnp.int32) * MOE_TM
    tile_expert = jnp.sum((tile_start[:, None] >= ends[None, :]).astype(jnp.int32), axis=1)
    last_expert = jnp.sum((((n_used - 1) * MOE_TM) >= ends).astype(jnp.int32))
    tile_expert = jnp.where(jnp.arange(n_tiles) < n_used, tile_expert, last_expert).astype(jnp.int32)

    xs8 = jnp.zeros((n_tiles * MOE_TM * ROW_TILE, LANES), F32)
    poss = []
    seen = jnp.zeros((N_EXPERTS,), jnp.int32)
    for u8, (idx, _, rank, _), cnt in zip(u8_list, routes, counts):
        base = starts + seen
        pos = rank
        for e in range(N_EXPERTS):
            pos = pos + jnp.where(idx == e, base[e], 0)
        poss.append(pos)
        seen = seen + cnt
        xs8 = _dispatch(u8, pos, xs8, tb=min(256, u8.shape[0] // ROW_TILE))
    ys8 = _experts(xs8, tile_expert, n_used.reshape(1), w_gate, w_up, w_down, tm=MOE_TM, tf=MOE_TF)
    outs = []
    for h, pos, (_, wts, _, _) in zip(h_list, poss, routes):
        outs.append(_combine(h, pos, wts.T, ys8, final_g, tb=min(256, h.shape[0])))
    return outs


def _rmsnorm_kernel(x_ref, g_ref, o_ref):
    x = x_ref[...]
    ms = jnp.mean(x * x, axis=-1, keepdims=True)
    o_ref[...] = x * lax.rsqrt(ms + EPS) * g_ref[...]


def _rmsnorm(x, g, *, tm):
    rows, d = x.shape
    return pl.pallas_call(
        _rmsnorm_kernel,
        grid=(rows // tm,),
        in_specs=[pl.BlockSpec((tm, d), lambda i: (i, 0)), pl.BlockSpec((1, d), lambda i: (0, 0))],
        out_specs=pl.BlockSpec((tm, d), lambda i: (i, 0)),
        out_shape=jax.ShapeDtypeStruct((rows, d), F32),
        compiler_params=_params(("parallel",)),
        name="final_norm",
    )(x, g.reshape(1, d))


def _pack_s5_state(re, im):
    n = re.shape[0]
    return jnp.concatenate([re.reshape(n, S5_LANES), im.reshape(n, S5_LANES)], axis=-1)


def _unpack_s5_state(h):
    n = h.shape[0]
    return (h[:, :S5_LANES].reshape(n, S5_GROUPS, S5_STATE), h[:, S5_LANES:].reshape(n, S5_GROUPS, S5_STATE))


def _mixer(x, s5_h0, ret_s0, conv_buf, pos_offset, p, *, single_step, tiled_u, layer=0, ret_stack=None):
    bsz, seq, d = x.shape
    tt = min(512, seq)
    z_main = _norm_proj(x, p["norm_g"], p["w_main"], tt=tt, tn=N_MAIN // 4, time_major=False)
    z_s5 = _norm_proj(x, p["norm_g"], p["w_s5"], tt=tt, tn=S5_WIDTH, time_major=True)

    if single_step:
        nb, s5_tt = seq, 1
    else:
        nb, s5_tt = bsz, 64
    s5_out, s5_state = _s5_branch(z_s5, s5_h0, p["a_re"], p["a_im"], p["bmat"], p["cmat"],
                                  p["d_skip"], p["glu_w"], p["glu_b"], nb=nb, tt=s5_tt)

    if single_step:
        zm = z_main.reshape(seq, N_MAIN)
        ret_out, ret_state = _retention_step(zm, ret_s0, layer, ret_stack, pos_offset, bb=8)
        conv_out, conv_state = _conv_step(zm, conv_buf, p["pw_b"], p["dw_w"], p["dw_b"], p["ln_g"], p["ln_b"], bb=32)
        ret_out = ret_out.reshape(bsz, seq, RET_V)
        conv_out = conv_out.reshape(bsz, seq, CONV_WIDTH)
    else:
        ret_out, ret_state = _retention_prompt(z_main, ret_s0, pos_offset, tt=256)
        conv_out, conv_state = _conv_prompt(z_main, conv_buf, p["pw_b"], p["dw_w"], p["dw_b"], p["ln_g"],
                                            p["ln_b"], tt=256)

    h, u = _merge(x, s5_out, ret_out, conv_out, z_main, p["s5_proj"], p["ret_proj"], p["conv_proj"], p["w_out"],
                  p["norm_ffn_g"], tm=min(512, seq), tiled_u=tiled_u)
    return h, u, (s5_state, ret_state, conv_state)


def kernel(x_prompt, x_sample, state_s5_re, state_s5_im, state_ret, state_conv, norm_mix_g, w_in, s5_lambda_re, s5_lambda_im, s5_log_dt, s5_b_re, s5_b_im, s5_c_re, s5_c_im, s5_d, s5_glu_w, s5_glu_b, s5_proj, ret_proj, conv_pw_b, conv_dw_w, conv_dw_b, conv_ln_g, conv_ln_b, conv_proj, w_out, norm_ffn_g, ffn_w_gate, ffn_w_up, ffn_w_down, moe_router, moe_w_gate, moe_w_up, moe_w_down, norm_final_g):
    depth = w_in.shape[0]
    bp, seq, d = x_prompt.shape
    ns = x_sample.shape[0]
    past_len = 16384
    bf = lambda a: a.astype(BF16)

    hp = x_prompt
    hs = x_sample.reshape(1, ns, d)
    zero_s5 = jnp.zeros((bp, 2 * S5_LANES), F32)
    zero_ret = jnp.zeros((bp, RET_HEADS, RET_DK, RET_DV), F32)
    zero_conv = jnp.zeros((bp, CONV_K - 1, CONV_WIDTH), F32)

    p_states, s_states = [], []
    ret_stack = None
    normed = False
    for l in range(depth):
        is_moe = l % 2 == 1
        a_re, a_im, bb_re, bb_im = _s5_discretize(s5_lambda_re[l], s5_lambda_im[l], s5_log_dt[l],
                                                  s5_b_re[l], s5_b_im[l])
        bmat, cmat = _s5_block_mats(bb_re, bb_im, s5_c_re[l], s5_c_im[l])
        w_in_b = bf(w_in[l])
        p = dict(norm_g=norm_mix_g[l], w_s5=w_in_b[:, :S5_WIDTH], w_main=w_in_b[:, S5_WIDTH:],
                 a_re=a_re, a_im=a_im, bmat=bmat, cmat=cmat, d_skip=s5_d[l], glu_w=bf(s5_glu_w[l]),
                 glu_b=s5_glu_b[l], pw_b=conv_pw_b[l], dw_w=conv_dw_w[l], dw_b=conv_dw_b[l], ln_g=conv_ln_g[l],
                 ln_b=conv_ln_b[l], s5_proj=bf(s5_proj[l]), ret_proj=bf(ret_proj[l]), conv_proj=bf(conv_proj[l]),
                 w_out=bf(w_out[l]), norm_ffn_g=norm_ffn_g[l])
        hp, up, st_p = _mixer(hp, zero_s5, zero_ret, zero_conv, 0.0, p, single_step=False, tiled_u=is_moe)
        hs, us, st_s = _mixer(hs, _pack_s5_state(state_s5_re[l], state_s5_im[l]), state_ret, state_conv[l],
                              float(past_len), p, single_step=True, tiled_u=is_moe, layer=l, ret_stack=ret_stack)
        ret_stack = st_s[1]
        p_states.append(st_p)
        s_states.append(st_s)

        hp2, hs2 = hp.reshape(bp * seq, d), hs.reshape(ns, d)
        j = l // 2
        if is_moe:
            final_g = norm_final_g if l == depth - 1 else None
            normed = final_g is not None
            hp2, hs2 = _moe([hp2, hs2], [up, us], moe_router[j], bf(moe_w_gate[j]), bf(moe_w_up[j]),
                            bf(moe_w_down[j]), final_g)
        else:
            wg, wu, wd = bf(ffn_w_gate[j]), bf(ffn_w_up[j]), bf(ffn_w_down[j])
            hp2 = _ffn(hp2, up.reshape(bp * seq, d), wg, wu, wd, tm=512, tf=FFN_TF)
            hs2 = _ffn(hs2, us.reshape(ns, d), wg, wu, wd, tm=ns, tf=FFN_TF)
        hp = hp2.reshape(bp, seq, d)
        hs = hs2.reshape(1, ns, d)

    if normed:
        y_prompt, y_sample = hp, hs.reshape(ns, 1, d)
    else:
        y_prompt = _rmsnorm(hp.reshape(bp * seq, d), norm_final_g, tm=512).reshape(bp, seq, d)
        y_sample = _rmsnorm(hs.reshape(ns, d), norm_final_g, tm=ns).reshape(ns, 1, d)

    def stack_s5(states):
        s5 = [_unpack_s5_state(s[0]) for s in states]
        return jnp.stack([a for a, _ in s5]), jnp.stack([b for _, b in s5])

    p_re, p_im = stack_s5(p_states)
    s_re, s_im = stack_s5(s_states)
    p_ret = jnp.stack([s[1] for s in p_states])
    p_conv = jnp.stack([s[2] for s in p_states])
    s_conv = jnp.stack([s[2] for s in s_states])
    return (y_prompt, y_sample, p_re, p_im, p_ret, p_conv, s_re, s_im, ret_stack, s_conv)
```

```python
import functools
import math

import jax
import jax.numpy as jnp
from jax import lax
from jax.experimental import pallas as pl
from jax.experimental.pallas import tpu as pltpu

F32 = jnp.float32
BF16 = jnp.bfloat16

D_MODEL = 1024
S5_WIDTH = 512
S5_GROUP = 16
S5_GROUPS = 32
S5_STATE = 64
S5_LANES = S5_GROUPS * S5_STATE
RET_HEADS = 4
RET_DK = 128
RET_DV = 256
RET_QK = RET_HEADS * RET_DK
RET_V = RET_HEADS * RET_DV
RET_CHUNK = 128
ROPE_BASE = 10000.0
CONV_WIDTH = 512
CONV_K = 31
CONV_HIST = 32
N_EXPERTS = 8
TOP_K = 2
N_BRANCH = 3
EPS = 1e-6
N_IN = S5_WIDTH + 2 * RET_QK + 2 * RET_V + 2 * CONV_WIDTH + N_BRANCH * D_MODEL
ZB = 512
ZC_S5 = 0
ZC_Q = ZC_S5 + S5_WIDTH // ZB
ZC_K = ZC_Q + RET_QK // ZB
ZC_V = ZC_K + RET_QK // ZB
ZC_G = ZC_V + RET_V // ZB
ZC_CONV = ZC_G + RET_V // ZB
ZC_GATE = ZC_CONV + 2 * CONV_WIDTH // ZB
HEADS_PER_ZB = ZB // RET_DV

ROW_TILE = 8
LANES = 128
ROW_CHUNKS = D_MODEL // LANES
VMEM_LIMIT = 48 * 1024 * 1024


def _params(sem):
    return pltpu.CompilerParams(dimension_semantics=sem, vmem_limit_bytes=VMEM_LIMIT)


def _silu(x):
    return x * jax.nn.sigmoid(x)


def _bdot(a, b):
    return jnp.dot(a.astype(BF16), b.astype(BF16), preferred_element_type=F32)


def _norm_proj_kernel(x_ref, g_ref, w_ref, o_ref, w_scr):
    @pl.when((pl.program_id(1) == 0) & (pl.program_id(2) == 0))
    def _():
        w_scr[...] = w_ref[...].astype(BF16)

    x = x_ref[...]
    ms = jnp.mean(x * x, axis=-1, keepdims=True)
    u = (x * lax.rsqrt(ms + EPS) * g_ref[...]).astype(BF16)
    o_ref[...] = jnp.dot(u, w_scr[...], preferred_element_type=F32).astype(o_ref.dtype)


def _norm_proj(x, g, w_all, layer, *, tt, tn):
    bsz, seq, d = x.shape
    n = w_all.shape[2]
    return pl.pallas_call(
        _norm_proj_kernel,
        grid=(n // tn, bsz, seq // tt),
        in_specs=[
            pl.BlockSpec((None, tt, d), lambda j, b, i: (b, i, 0)),
            pl.BlockSpec((1, d), lambda j, b, i: (0, 0)),
            pl.BlockSpec((None, d, tn), lambda j, b, i: (layer, 0, j)),
        ],
        out_specs=pl.BlockSpec((None, tt, tn), lambda j, b, i: (b, i, j)),
        out_shape=jax.ShapeDtypeStruct((bsz, seq, n), BF16),
        scratch_shapes=[pltpu.VMEM((d, tn), BF16)],
        compiler_params=_params(("arbitrary", "arbitrary", "arbitrary")),
        name="norm_proj",
    )(x, g.reshape(1, d), w_all)


def _s5_disc_kernel(lre_ref, lim_ref, ldt_ref, bre_ref, bim_ref, are_ref, aim_ref, ore_ref, oim_ref):
    lam_re = lre_ref[...]
    lam_im = lim_ref[...]
    dt = jnp.exp(ldt_ref[...])
    mag = jnp.exp(lam_re * dt)
    ang = lam_im * dt
    lbar_re = mag * jnp.cos(ang)
    lbar_im = mag * jnp.sin(ang)
    den = lam_re * lam_re + lam_im * lam_im
    nr = lbar_re - 1.0
    f_re = (nr * lam_re + lbar_im * lam_im) / den
    f_im = (lbar_im * lam_re - nr * lam_im) / den
    b_re = bre_ref[...]
    b_im = bim_ref[...]
    are_ref[...] = lbar_re
    aim_ref[...] = lbar_im
    ore_ref[...] = f_re * b_re - f_im * b_im
    oim_ref[...] = f_re * b_im + f_im * b_re


def _s5_discretize(lam_re, lam_im, log_dt, b_re, b_im):
    g, n = lam_re.shape
    p = b_re.shape[-1]
    rows = g * n
    col = lambda a: a.reshape(rows, 1)
    ldt = jnp.broadcast_to(log_dt[:, None], (g, n))
    outs = pl.pallas_call(
        _s5_disc_kernel,
        out_shape=[jax.ShapeDtypeStruct((rows, 1), F32), jax.ShapeDtypeStruct((rows, 1), F32),
                   jax.ShapeDtypeStruct((rows, p), F32), jax.ShapeDtypeStruct((rows, p), F32)],
        name="s5_discretize",
    )(col(lam_re), col(lam_im), col(ldt), b_re.reshape(rows, p), b_im.reshape(rows, p))
    a_re, a_im, bb_re, bb_im = outs
    return a_re.reshape(g, n), a_im.reshape(g, n), bb_re.reshape(g, n, p), bb_im.reshape(g, n, p)


S5_KCH = 128
S5_NCHUNK = S5_WIDTH // S5_KCH
S5_GPC = S5_KCH // S5_GROUP
S5_SPC = S5_GPC * S5_STATE


def _s5_block_mats(bbar_re, bbar_im, c_re, c_im):
    eye = jnp.eye(S5_GPC, dtype=F32)

    def in_blocks(bb):
        t = bb.reshape(S5_NCHUNK, S5_GPC, S5_STATE, S5_GROUP)
        m = jnp.einsum("cgnp,gh->cgphn", t, eye)
        return m.reshape(S5_NCHUNK, S5_KCH, S5_SPC)

    def out_blocks(cc):
        t = cc.reshape(S5_NCHUNK, S5_GPC, S5_GROUP, S5_STATE)
        m = jnp.einsum("cgpn,gh->cgnhp", t, eye)
        return m.reshape(S5_NCHUNK, S5_SPC, S5_KCH)

    bmat = jnp.concatenate([in_blocks(bbar_re), in_blocks(bbar_im)], axis=-1).astype(BF16)
    cmat = jnp.stack([out_blocks(c_re), -out_blocks(c_im)], axis=1).astype(BF16)
    return bmat, cmat


def _gelu_tanh(x):
    return 0.5 * x * (1.0 + jnp.tanh(math.sqrt(2.0 / math.pi) * (x + 0.044715 * (x * x * x))))


def _s5_kernel(u_ref, h0_ref, are_ref, aim_ref, bmat_ref, cmat_ref, d_ref, gw_ref, gb_ref,
               o_ref, hout_ref, hs_scr, h_scr, io_scr, *, nb, tt, lane_chunk):
    @pl.when(pl.program_id(0) == 0)
    def _():
        h_scr[...] = h0_ref[...]

    def seq_rows(b):
        return pl.ds(b, tt, stride=nb)

    def lanes(c):
        return slice(c * S5_KCH, (c + 1) * S5_KCH)

    if tt == 1:
        for c in range(S5_NCHUNK):
            io_scr[c] = u_ref[:, lanes(c)].astype(F32)
    else:
        for b in range(nb):
            for c in range(S5_NCHUNK):
                io_scr[c, seq_rows(b), :] = u_ref[b, :, lanes(c)].astype(F32)

    u = jnp.concatenate([io_scr[c] for c in range(S5_NCHUNK)], axis=-1)
    for c in range(S5_NCHUNK):
        bu = jnp.dot(io_scr[c].astype(BF16), bmat_ref[c], preferred_element_type=F32)
        hs_scr[:, c * S5_SPC:(c + 1) * S5_SPC] = bu[:, :S5_SPC]
        hs_scr[:, S5_LANES + c * S5_SPC:S5_LANES + (c + 1) * S5_SPC] = bu[:, S5_SPC:]

    for lc in range(S5_LANES // lane_chunk):
        re_sl = slice(lc * lane_chunk, (lc + 1) * lane_chunk)
        im_sl = slice(S5_LANES + lc * lane_chunk, S5_LANES + (lc + 1) * lane_chunk)
        a_re = jnp.broadcast_to(are_ref[:, re_sl], (nb, lane_chunk))
        a_im = jnp.broadcast_to(aim_ref[:, re_sl], (nb, lane_chunk))

        def step(t, carry):
            h_re, h_im = carry
            r0 = t * nb if isinstance(t, int) else pl.multiple_of(t * nb, nb)
            n_re = a_re * h_re - a_im * h_im + hs_scr[pl.ds(r0, nb), re_sl]
            n_im = a_re * h_im + a_im * h_re + hs_scr[pl.ds(r0, nb), im_sl]
            hs_scr[pl.ds(r0, nb), re_sl] = n_re
            hs_scr[pl.ds(r0, nb), im_sl] = n_im
            return n_re, n_im

        carry = (h_scr[:, re_sl], h_scr[:, im_sl])
        if tt == 1:
            carry = step(0, carry)
        else:
            carry = lax.fori_loop(0, tt, step, carry, unroll=4)
        h_scr[:, re_sl] = carry[0]
        h_scr[:, im_sl] = carry[1]

    hout_ref[...] = h_scr[...]

    ys = []
    for c in range(S5_NCHUNK):
        h_re = hs_scr[:, c * S5_SPC:(c + 1) * S5_SPC].astype(BF16)
        h_im = hs_scr[:, S5_LANES + c * S5_SPC:S5_LANES + (c + 1) * S5_SPC].astype(BF16)
        ys.append(jnp.dot(h_re, cmat_ref[c, 0], preferred_element_type=F32)
                  + jnp.dot(h_im, cmat_ref[c, 1], preferred_element_type=F32))
    y = jnp.concatenate(ys, axis=-1) + d_ref[...] * u
    z = _gelu_tanh(y)
    gate = jnp.dot(z.astype(BF16), gw_ref[...], preferred_element_type=F32) + gb_ref[...]
    out = z * jax.nn.sigmoid(gate)
    if tt == 1:
        o_ref[...] = out.astype(o_ref.dtype)
    else:
        for c in range(S5_NCHUNK):
            io_scr[c] = out[:, lanes(c)]
        for b in range(nb):
            for c in range(S5_NCHUNK):
                o_ref[b, :, lanes(c)] = io_scr[c, seq_rows(b), :].astype(o_ref.dtype)


def _s5_branch(z, h0, a_re, a_im, bmat, cmat, d_skip, glu_w, glu_b, *, single_step, tt):
    bsz, seq, _ = z.shape
    if single_step:
        assert bsz == 1 and tt == 1
        nb = seq
        in_spec = pl.BlockSpec((None, nb, S5_WIDTH), lambda i: (0, 0, ZC_S5))
        out_spec = pl.BlockSpec((None, nb, S5_WIDTH), lambda i: (0, 0, 0))
        grid = (1,)
    else:
        nb = bsz
        in_spec = pl.BlockSpec((nb, tt, S5_WIDTH), lambda i: (0, i, ZC_S5))
        out_spec = pl.BlockSpec((nb, tt, S5_WIDTH), lambda i: (0, i, 0))
        grid = (seq // tt,)
    rblk = tt * nb
    lane_chunk = 1024 if nb <= 8 else 512
    const = lambda shape: pl.BlockSpec(shape, lambda i: (0,) * len(shape))
    return pl.pallas_call(
        functools.partial(_s5_kernel, nb=nb, tt=tt, lane_chunk=lane_chunk),
        grid=grid,
        in_specs=[
            in_spec,
            const((nb, 2 * S5_LANES)),
            const((1, S5_LANES)),
            const((1, S5_LANES)),
            const(bmat.shape),
            const(cmat.shape),
            const((1, S5_WIDTH)),
            const((S5_WIDTH, S5_WIDTH)),
            const((1, S5_WIDTH)),
        ],
        out_specs=[out_spec, const((nb, 2 * S5_LANES))],
        out_shape=[jax.ShapeDtypeStruct((bsz, seq, S5_WIDTH), BF16),
                   jax.ShapeDtypeStruct((nb, 2 * S5_LANES), F32)],
        scratch_shapes=[pltpu.VMEM((rblk, 2 * S5_LANES), F32), pltpu.VMEM((nb, 2 * S5_LANES), F32),
                        pltpu.VMEM((S5_NCHUNK, rblk, S5_KCH), F32)],
        compiler_params=_params(("arbitrary",)),
        name="s5_branch",
    )(z, h0, a_re.reshape(1, S5_LANES), a_im.reshape(1, S5_LANES), bmat, cmat,
      d_skip.reshape(1, S5_WIDTH), glu_w, glu_b.reshape(1, S5_WIDTH))


def _rope_tables(pos):
    half = RET_DK // 2
    freqs = ROPE_BASE ** (-jnp.arange(half, dtype=F32) / half)
    ang = pos[:, None] * freqs[None, :]
    cos = jnp.cos(ang)
    sin = jnp.sin(ang)
    return jnp.concatenate([cos, cos], axis=-1), jnp.concatenate([-sin, sin], axis=-1)


def _rope(x, cos, sin):
    return x * cos + pltpu.roll(x, RET_DK // 2, 1) * sin


def _group_norm(o):
    mu = jnp.mean(o, axis=-1, keepdims=True)
    d = o - mu
    var = jnp.mean(d * d, axis=-1, keepdims=True)
    return d * lax.rsqrt(var + EPS)


def _retention_tables(chunk):
    log_gamma = jnp.log(1.0 - 2.0 ** (-5.0 - jnp.arange(RET_HEADS, dtype=F32)))
    idx = jnp.arange(chunk, dtype=F32)
    diff = idx[:, None] - idx[None, :]
    decay = jnp.where(diff >= 0, jnp.exp(jnp.maximum(diff, 0.0)[None] * log_gamma[:, None, None]), 0.0)
    cross = jnp.exp((idx + 1.0)[None, :] * log_gamma[:, None])[:, :, None]
    kdec = jnp.exp((chunk - 1.0 - idx)[None, :] * log_gamma[:, None])[:, :, None]
    full = jnp.exp(chunk * log_gamma)
    return decay, cross, kdec, full


def _head_cols(refs, h, rows):
    lo = (h % HEADS_PER_ZB) * RET_DV
    return refs[h // HEADS_PER_ZB][rows, lo:lo + RET_DV].astype(F32)


def _retention_kernel(q_ref, k_ref, v0_ref, v1_ref, g0_ref, g1_ref, cos_ref, sin_ref, s0_ref, decay_ref, cross_ref,
                      kdec_ref, full_ref, o_ref, sout_ref, s_scr, *, n_chunks):
    @pl.when(pl.program_id(1) == 0)
    def _():
        s_scr[...] = s0_ref[...]

    for c in range(n_chunks):
        rows = slice(c * RET_CHUNK, (c + 1) * RET_CHUNK)
        cos = cos_ref[rows, :]
        sin = sin_ref[rows, :]
        for h in range(RET_HEADS):
            qk_cols = slice(h * RET_DK, (h + 1) * RET_DK)
            v_cols = slice(h * RET_DV, (h + 1) * RET_DV)
            qh = _rope(q_ref[rows, qk_cols].astype(F32), cos, sin)
            kh = _rope(k_ref[rows, qk_cols].astype(F32), cos, sin) * (RET_DK ** -0.5)
            vb = _head_cols((v0_ref, v1_ref), h, rows).astype(BF16)
            qb = qh.astype(BF16)
            state = s_scr[h]
            inner = lax.dot_general(qb, kh.astype(BF16), (((1,), (1,)), ((), ())),
                                    preferred_element_type=F32) * decay_ref[h]
            out = (jnp.dot(inner.astype(BF16), vb, preferred_element_type=F32)
                   + jnp.dot(qb, state.astype(BF16), preferred_element_type=F32) * cross_ref[h])
            kd = (kh * kdec_ref[h]).astype(BF16)
            s_scr[h] = full_ref[h] * state + jnp.dot(kd.T, vb, preferred_element_type=F32)
            gate = _head_cols((g0_ref, g1_ref), h, rows)
            o_ref[rows, v_cols] = (_silu(gate) * _group_norm(out)).astype(o_ref.dtype)

    sout_ref[...] = s_scr[...]


def _retention_prompt(z, state0, pos_offset, *, tt):
    bsz, seq, _ = z.shape
    assert seq % RET_CHUNK == 0 and tt % RET_CHUNK == 0
    cos, sin = _rope_tables(jnp.arange(seq, dtype=F32) + pos_offset)
    decay, cross, kdec, full = _retention_tables(RET_CHUNK)
    full = jnp.broadcast_to(full[:, None, None], (RET_HEADS, 1, RET_DV))
    const = lambda shape: pl.BlockSpec(shape, lambda b, i: (0,) * len(shape))
    zblk = lambda col: pl.BlockSpec((None, tt, ZB), lambda b, i: (b, i, col))
    return pl.pallas_call(
        functools.partial(_retention_kernel, n_chunks=tt // RET_CHUNK),
        grid=(bsz, seq // tt),
        in_specs=[
            zblk(ZC_Q), zblk(ZC_K), zblk(ZC_V), zblk(ZC_V + 1), zblk(ZC_G), zblk(ZC_G + 1),
            pl.BlockSpec((tt, RET_DK), lambda b, i: (i, 0)),
            pl.BlockSpec((tt, RET_DK), lambda b, i: (i, 0)),
            pl.BlockSpec((None, RET_HEADS, RET_DK, RET_DV), lambda b, i: (b, 0, 0, 0)),
            const((RET_HEADS, RET_CHUNK, RET_CHUNK)),
            const((RET_HEADS, RET_CHUNK, 1)),
            const((RET_HEADS, RET_CHUNK, 1)),
            const((RET_HEADS, 1, RET_DV)),
        ],
        out_specs=[
            pl.BlockSpec((None, tt, RET_V), lambda b, i: (b, i, 0)),
            pl.BlockSpec((None, RET_HEADS, RET_DK, RET_DV), lambda b, i: (b, 0, 0, 0)),
        ],
        out_shape=[jax.ShapeDtypeStruct((bsz, seq, RET_V), BF16),
                   jax.ShapeDtypeStruct((bsz, RET_HEADS, RET_DK, RET_DV), F32)],
        scratch_shapes=[pltpu.VMEM((RET_HEADS, RET_DK, RET_DV), F32)],
        compiler_params=_params(("parallel", "arbitrary")),
        name="retention_prompt",
    )(z, z, z, z, z, z, cos, sin, state0, decay, cross, kdec, full)


def _retention_step_kernel(q_ref, k_ref, v0_ref, v1_ref, g0_ref, g1_ref, cos_ref, sin_ref, s_ref, gam_ref, *rest,
                           bb, layer):
    o_ref, sout_ref, o_scr = rest[-3:]

    @pl.when(pl.program_id(1) > 0)
    def _():
        sout_ref[...] = jnp.zeros_like(sout_ref)

    @pl.when(pl.program_id(1) == 0)
    def _():
        _retention_step_body(q_ref, k_ref, v0_ref, v1_ref, g0_ref, g1_ref, cos_ref, sin_ref, s_ref, gam_ref,
                             o_ref, sout_ref, o_scr, bb=bb)


def _retention_step_body(q_ref, k_ref, v0_ref, v1_ref, g0_ref, g1_ref, cos_ref, sin_ref, s_ref, gam_ref,
                         o_ref, sout_ref, o_scr, *, bb):
    cos = cos_ref[...]
    sin = sin_ref[...]
    for h in range(RET_HEADS):
        qk_cols = slice(h * RET_DK, (h + 1) * RET_DK)
        v_cols = slice(h * RET_DV, (h + 1) * RET_DV)
        qh = _rope(q_ref[:, qk_cols].astype(F32), cos, sin)
        kh = _rope(k_ref[:, qk_cols].astype(F32), cos, sin) * (RET_DK ** -0.5)
        qk = jnp.sum(qh * kh, axis=-1, keepdims=True)
        q_t = qh.T
        k_t = kh.T
        gamma = gam_ref[h]
        v_all = _head_cols((v0_ref, v1_ref), h, slice(None))
        for b in range(bb):
            state = s_ref[b, h]
            vrow = v_all[b:b + 1, :]
            qs = jnp.sum(q_t[:, b:b + 1] * state, axis=0, keepdims=True)
            o_scr[b:b + 1, v_cols] = qk[b:b + 1, :] * vrow + qs * gamma
            sout_ref[b, h] = gamma * state + k_t[:, b:b + 1] * vrow
    for h in range(RET_HEADS):
        v_cols = slice(h * RET_DV, (h + 1) * RET_DV)
        gate = _head_cols((g0_ref, g1_ref), h, slice(None))
        o_ref[:, v_cols] = (_silu(gate) * _group_norm(o_scr[:, v_cols])).astype(o_ref.dtype)


def _retention_step(z, states, layer, stack, pos, *, bb):
    n = z.shape[0]
    depth = states.shape[0]
    cos, sin = _rope_tables(jnp.full((1,), pos, F32))
    log_gamma = jnp.log(1.0 - 2.0 ** (-5.0 - jnp.arange(RET_HEADS, dtype=F32)))
    gam = jnp.broadcast_to(jnp.exp(log_gamma)[:, None, None], (RET_HEADS, 1, RET_DV))
    const = lambda shape: pl.BlockSpec(shape, lambda i, s: (0,) * len(shape))
    slab_shape = (None, bb, RET_HEADS, RET_DK, RET_DV)
    zblk = lambda col: pl.BlockSpec((bb, ZB), lambda i, s: (i, col))
    in_specs = [
        zblk(ZC_Q), zblk(ZC_K), zblk(ZC_V), zblk(ZC_V + 1), zblk(ZC_G), zblk(ZC_G + 1),
        const((1, RET_DK)),
        const((1, RET_DK)),
        pl.BlockSpec(slab_shape, lambda i, s: (layer, i, 0, 0, 0)),
        const((RET_HEADS, 1, RET_DV)),
    ]
    args = [z, z, z, z, z, z, cos, sin, states, gam]
    if layer == 0:
        n_slabs = depth
        aliases = {}
    else:
        n_slabs = 1
        in_specs.append(pl.BlockSpec(memory_space=pl.ANY))
        args.append(stack)
        aliases = {len(args) - 1: 1}
    return pl.pallas_call(
        functools.partial(_retention_step_kernel, bb=bb, layer=layer),
        grid=(n // bb, n_slabs),
        in_specs=in_specs,
        out_specs=[pl.BlockSpec((bb, RET_V), lambda i, s: (i, 0)),
                   pl.BlockSpec(slab_shape, lambda i, s: (layer + s, i, 0, 0, 0))],
        out_shape=[jax.ShapeDtypeStruct((n, RET_V), BF16), jax.ShapeDtypeStruct(states.shape, F32)],
        scratch_shapes=[pltpu.VMEM((bb, RET_V), F32)],
        input_output_aliases=aliases,
        compiler_params=_params(("parallel", "arbitrary")),
        name="retention_step",
    )(*args)


def _layer_norm(y, g, b):
    mu = jnp.mean(y, axis=-1, keepdims=True)
    d = y - mu
    var = jnp.mean(d * d, axis=-1, keepdims=True)
    return d * lax.rsqrt(var + EPS) * g + b


CONV_RB = 128
CONV_PITCH = 2
CONV_LCH = CONV_WIDTH // LANES


def _conv_glu(a_ref, b_ref, pwb_ref):
    a = a_ref[...].astype(F32) + pwb_ref[:, :CONV_WIDTH]
    b = b_ref[...].astype(F32) + pwb_ref[:, CONV_WIDTH:]
    return a * jax.nn.sigmoid(b)


def _conv_kernel(a_ref, b_ref, buf_ref, pwb_ref, dww_ref, dwb_ref, lng_ref, lnb_ref, o_ref, hist_ref, x_scr, y_scr,
                 *, tt):
    i = pl.program_id(1)

    def rows(start, n):
        return pl.ds(CONV_PITCH * start, n, stride=CONV_PITCH)

    def lanes(c):
        return slice(c * LANES, (c + 1) * LANES)

    @pl.when(i == 0)
    def _():
        for c in range(CONV_LCH):
            x_scr[c, rows(0, CONV_HIST), :] = buf_ref[:, lanes(c)]

    @pl.when(i > 0)
    def _():
        for c in range(CONV_LCH):
            x_scr[c, rows(0, CONV_HIST), :] = x_scr[c, rows(tt, CONV_HIST), :]

    glu = _conv_glu(a_ref, b_ref, pwb_ref)
    for c in range(CONV_LCH):
        x_scr[c, rows(CONV_HIST, tt), :] = glu[:, lanes(c)]
    hist_ref[...] = glu[tt - CONV_HIST:, :]

    off = CONV_HIST - (CONV_K - 1)
    for c in range(CONV_LCH):
        def row_block(r, carry, c=c):
            base = r * CONV_RB
            n_grp = CONV_RB // ROW_TILE
            accs = [jnp.broadcast_to(dwb_ref[:, lanes(c)], (ROW_TILE, LANES))] * n_grp
            for m in range(CONV_RB - ROW_TILE + CONV_K):
                win = x_scr[c, rows(base + (off + m), ROW_TILE), :]
                for k in range(m % ROW_TILE, CONV_K, ROW_TILE):
                    j = (m - k) // ROW_TILE
                    if 0 <= j < n_grp:
                        accs[j] = accs[j] + dww_ref[k:k + 1, lanes(c)] * win
            y_scr[pl.ds(pl.multiple_of(base, CONV_RB), CONV_RB), lanes(c)] = jnp.concatenate(accs, axis=0)
            return carry

        lax.fori_loop(0, tt // CONV_RB, row_block, 0)

    o_ref[...] = _silu(_layer_norm(y_scr[...], lng_ref[...], lnb_ref[...])).astype(o_ref.dtype)


def _conv_prompt(z, buf, pw_b, dw_w, dw_b, ln_g, ln_b, *, tt):
    bsz, seq, _ = z.shape
    assert seq >= CONV_HIST and tt >= CONV_HIST
    buf32 = jnp.pad(buf, ((0, 0), (CONV_HIST - (CONV_K - 1), 0), (0, 0)))
    const = lambda shape: pl.BlockSpec(shape, lambda b, i: (0,) * len(shape))
    zblk = lambda col: pl.BlockSpec((None, tt, ZB), lambda b, i: (b, i, col))
    out, hist = pl.pallas_call(
        functools.partial(_conv_kernel, tt=tt),
        grid=(bsz, seq // tt),
        in_specs=[
            zblk(ZC_CONV), zblk(ZC_CONV + 1),
            pl.BlockSpec((None, CONV_HIST, CONV_WIDTH), lambda b, i: (b, 0, 0)),
            const((1, 2 * CONV_WIDTH)),
            const((CONV_K, CONV_WIDTH)),
            const((1, CONV_WIDTH)),
            const((1, CONV_WIDTH)),
            const((1, CONV_WIDTH)),
        ],
        out_specs=[
            pl.BlockSpec((None, tt, CONV_WIDTH), lambda b, i: (b, i, 0)),
            pl.BlockSpec((None, CONV_HIST, CONV_WIDTH), lambda b, i: (b, 0, 0)),
        ],
        out_shape=[jax.ShapeDtypeStruct((bsz, seq, CONV_WIDTH), BF16),
                   jax.ShapeDtypeStruct((bsz, CONV_HIST, CONV_WIDTH), F32)],
        scratch_shapes=[pltpu.VMEM((CONV_LCH, CONV_PITCH * (CONV_HIST + tt), LANES), F32),
                        pltpu.VMEM((tt, CONV_WIDTH), F32)],
        compiler_params=_params(("parallel", "arbitrary")),
        name="conv_prompt",
    )(z, z, buf32, pw_b.reshape(1, -1), dw_w, dw_b.reshape(1, -1), ln_g.reshape(1, -1), ln_b.reshape(1, -1))
    return out, hist[:, CONV_HIST - (CONV_K - 1):, :]


def _conv_step_kernel(a_ref, b_ref, buf_ref, pwb_ref, dww_ref, dwb_ref, lng_ref, lnb_ref, o_ref, hist_ref):
    hist_len = CONV_K - 1
    glu = _conv_glu(a_ref, b_ref, pwb_ref)
    acc = dwb_ref[...] + dww_ref[hist_len:hist_len + 1, :] * glu
    for k in range(hist_len):
        acc = acc + dww_ref[k:k + 1, :] * buf_ref[:, k * CONV_WIDTH:(k + 1) * CONV_WIDTH]
    o_ref[...] = _silu(_layer_norm(acc, lng_ref[...], lnb_ref[...])).astype(o_ref.dtype)
    hist_ref[:, :(hist_len - 1) * CONV_WIDTH] = buf_ref[:, CONV_WIDTH:]
    hist_ref[:, (hist_len - 1) * CONV_WIDTH:] = glu


def _conv_step(z, buf, pw_b, dw_w, dw_b, ln_g, ln_b, *, bb):
    n = z.shape[0]
    hist_len = CONV_K - 1
    const = lambda shape: pl.BlockSpec(shape, lambda i: (0,) * len(shape))
    zblk = lambda col: pl.BlockSpec((bb, ZB), lambda i: (i, col))
    out, hist = pl.pallas_call(
        _conv_step_kernel,
        grid=(n // bb,),
        in_specs=[
            zblk(ZC_CONV), zblk(ZC_CONV + 1),
            pl.BlockSpec((bb, hist_len * CONV_WIDTH), lambda i: (i, 0)),
            const((1, 2 * CONV_WIDTH)),
            const((CONV_K, CONV_WIDTH)),
            const((1, CONV_WIDTH)),
            const((1, CONV_WIDTH)),
            const((1, CONV_WIDTH)),
        ],
        out_specs=[
            pl.BlockSpec((bb, CONV_WIDTH), lambda i: (i, 0)),
            pl.BlockSpec((bb, hist_len * CONV_WIDTH), lambda i: (i, 0)),
        ],
        out_shape=[jax.ShapeDtypeStruct((n, CONV_WIDTH), BF16),
                   jax.ShapeDtypeStruct((n, hist_len * CONV_WIDTH), F32)],
        compiler_params=_params(("parallel",)),
        name="conv_step",
    )(z, z, buf.reshape(n, hist_len * CONV_WIDTH), pw_b.reshape(1, -1), dw_w, dw_b.reshape(1, -1),
      ln_g.reshape(1, -1), ln_b.reshape(1, -1))
    return out, hist.reshape(n, hist_len, CONV_WIDTH)


def _rows_to_tiles(tile_ref, x, rows):
    for c in range(ROW_CHUNKS):
        tile_ref[pl.ds(c, rows, stride=ROW_TILE), :] = x[:, c * LANES:(c + 1) * LANES]


def _tiles_chunk(tile_ref, c, rows):
    return tile_ref[pl.ds(c, rows, stride=ROW_TILE), :]


def _tiles_to_rows(tile_ref, rows):
    return jnp.concatenate([_tiles_chunk(tile_ref, c, rows) for c in range(ROW_CHUNKS)], axis=-1)


def _merge_kernel(x_ref, s5_ref, ret_ref, conv_ref, *rest, tiled_u):
    gate_refs = rest[:N_BRANCH * D_MODEL // ZB]
    ps5_ref, pret_ref, pconv_ref, wout_ref, g_ref, h_ref, u_ref = rest[len(gate_refs):]
    per_branch = D_MODEL // ZB

    def gate(n):
        cols = [gate_refs[n * per_branch + j][...] for j in range(per_branch)]
        return jax.nn.sigmoid(jnp.concatenate(cols, axis=-1).astype(F32))

    merged = (gate(0) * jnp.dot(s5_ref[...], ps5_ref[...], preferred_element_type=F32)
              + gate(1) * jnp.dot(ret_ref[...], pret_ref[...], preferred_element_type=F32)
              + gate(2) * jnp.dot(conv_ref[...], pconv_ref[...], preferred_element_type=F32))
    h = x_ref[...] + _bdot(merged, wout_ref[...])
    h_ref[...] = h
    ms = jnp.mean(h * h, axis=-1, keepdims=True)
    u = h * lax.rsqrt(ms + EPS) * g_ref[...]
    if tiled_u:
        _rows_to_tiles(u_ref, u, u.shape[0])
    else:
        u_ref[...] = u.astype(u_ref.dtype)


def _merge(x, s5_out, ret_out, conv_out, z, s5_proj, ret_proj, conv_proj, w_out, norm_g, *, tm, tiled_u):
    bsz, seq, d = x.shape
    const = lambda shape: pl.BlockSpec(shape, lambda b, i: (0,) * len(shape))
    tok = lambda w, col=0: pl.BlockSpec((None, tm, w), lambda b, i: (b, i, col))
    n_i = seq // tm
    n_gate = N_BRANCH * d // ZB
    if tiled_u:
        u_spec = pl.BlockSpec((tm * ROW_TILE, LANES), lambda b, i: (b * n_i + i, 0))
        u_shape = jax.ShapeDtypeStruct((bsz * seq * ROW_TILE, LANES), F32)
    else:
        u_spec = tok(d)
        u_shape = jax.ShapeDtypeStruct((bsz, seq, d), BF16)
    return pl.pallas_call(
        functools.partial(_merge_kernel, tiled_u=tiled_u),
        grid=(bsz, seq // tm),
        in_specs=[
            tok(d),
            tok(S5_WIDTH),
            tok(RET_V),
            tok(CONV_WIDTH),
            *[tok(ZB, ZC_GATE + j) for j in range(n_gate)],
            const((S5_WIDTH, d)),
            const((RET_V, d)),
            const((CONV_WIDTH, d)),
            const((d, d)),
            const((1, d)),
        ],
        out_specs=[tok(d), u_spec],
        out_shape=[jax.ShapeDtypeStruct((bsz, seq, d), F32), u_shape],
        compiler_params=_params(("parallel", "parallel")),
        name="merge",
    )(x, s5_out, ret_out, conv_out, *([z] * n_gate), s5_proj, ret_proj, conv_proj, w_out, norm_g.reshape(1, d))


def _ffn_kernel(h_ref, u_ref, wg_ref, wu_ref, wd_ref, o_ref):
    f = pl.program_id(1)
    ub = u_ref[...].astype(BF16)
    gate = jnp.dot(ub, wg_ref[...], preferred_element_type=F32)
    up = jnp.dot(ub, wu_ref[...], preferred_element_type=F32)
    part = _bdot(_silu(gate) * up, wd_ref[...])

    @pl.when(f == 0)
    def _():
        o_ref[...] = h_ref[...] + part

    @pl.when(f > 0)
    def _():
        o_ref[...] = o_ref[...] + part


def _ffn(h, u, w_gate, w_up, w_down, *, tm, tf):
    rows, d = h.shape
    dff = w_gate.shape[1]
    tok = pl.BlockSpec((tm, d), lambda i, f: (i, 0))
    return pl.pallas_call(
        _ffn_kernel,
        grid=(rows // tm, dff // tf),
        in_specs=[tok, tok,
                  pl.BlockSpec((d, tf), lambda i, f: (0, f)),
                  pl.BlockSpec((d, tf), lambda i, f: (0, f)),
                  pl.BlockSpec((tf, d), lambda i, f: (f, 0))],
        out_specs=tok,
        out_shape=jax.ShapeDtypeStruct((rows, d), F32),
        compiler_params=_params(("parallel", "arbitrary")),
        name="ffn_dense",
    )(h, u, w_gate, w_up, w_down)


def _split_bf16(x):
    hi = x.astype(BF16)
    return hi, (x - hi.astype(F32)).astype(BF16)


def _router_kernel(u_ref, rt_ref, tri_ref, idx_ref, wts_ref, rank_ref, cnt_ref, cnt_scr, *, tm):
    @pl.when(pl.program_id(0) == 0)
    def _():
        cnt_scr[...] = jnp.zeros_like(cnt_scr)

    u_hi, u_lo = _split_bf16(_tiles_to_rows(u_ref, tm))
    r_hi, r_lo = _split_bf16(rt_ref[...])
    dn = (((1,), (1,)), ((), ()))
    logits = (lax.dot_general(r_hi, u_hi, dn, preferred_element_type=F32)
              + lax.dot_general(r_lo, u_hi, dn, preferred_element_type=F32)
              + lax.dot_general(r_hi, u_lo, dn, preferred_element_type=F32))
    eidx = lax.broadcasted_iota(jnp.int32, logits.shape, 0)
    m1 = jnp.max(logits, axis=0, keepdims=True)
    i1 = jnp.min(jnp.where(logits == m1, eidx, N_EXPERTS), axis=0, keepdims=True)
    rest = jnp.where(eidx == i1, -jnp.inf, logits)
    m2 = jnp.max(rest, axis=0, keepdims=True)
    i2 = jnp.min(jnp.where(rest == m2, eidx, N_EXPERTS), axis=0, keepdims=True)
    e2 = jnp.exp(m2 - m1)
    w1 = 1.0 / (1.0 + e2)
    idx_ref[...] = jnp.concatenate([i1, i2], axis=0)
    wts_ref[...] = jnp.concatenate([w1, e2 * w1], axis=0)

    hit1 = eidx == i1
    hit2 = eidx == i2
    hits = jnp.where(hit1 | hit2, 1.0, 0.0)
    before = jnp.dot(hits.astype(BF16), tri_ref[...], preferred_element_type=F32) + cnt_scr[...]
    rank_ref[...] = jnp.concatenate(
        [jnp.sum(jnp.where(hit1, before, 0.0), axis=0, keepdims=True),
         jnp.sum(jnp.where(hit2, before, 0.0), axis=0, keepdims=True)], axis=0).astype(jnp.int32)
    cnt_scr[...] = cnt_scr[...] + jnp.sum(hits, axis=1, keepdims=True)
    cnt_ref[...] = cnt_scr[...]


def _router(u8, router, *, tm):
    rows = u8.shape[0] // ROW_TILE
    d = D_MODEL
    tri = (jnp.arange(tm)[:, None] < jnp.arange(tm)[None, :]).astype(BF16)
    const = lambda shape: pl.BlockSpec(shape, lambda i: (0,) * len(shape))
    lane = pl.BlockSpec((TOP_K, tm), lambda i: (0, i))
    return pl.pallas_call(
        functools.partial(_router_kernel, tm=tm),
        grid=(rows // tm,),
        in_specs=[pl.BlockSpec((tm * ROW_TILE, LANES), lambda i: (i, 0)), const((N_EXPERTS, d)), const((tm, tm))],
        out_specs=[lane, lane, lane, const((N_EXPERTS, 1))],
        out_shape=[jax.ShapeDtypeStruct((TOP_K, rows), jnp.int32), jax.ShapeDtypeStruct((TOP_K, rows), F32),
                   jax.ShapeDtypeStruct((TOP_K, rows), jnp.int32), jax.ShapeDtypeStruct((N_EXPERTS, 1), F32)],
        scratch_shapes=[pltpu.VMEM((N_EXPERTS, 1), F32)],
        compiler_params=_params(("arbitrary",)),
        name="moe_router",
    )(u8, router.T, tri)


DMA_UNROLL = 8


def _token_tile(ref, r):
    return ref.at[pl.ds(pl.multiple_of(r * ROW_TILE, ROW_TILE), ROW_TILE)]


def _dispatch_kernel(pos_ref, u_ref, init_ref, xs_ref, sem, *, tb):
    del init_ref

    def copy(r, s):
        return pltpu.make_async_copy(_token_tile(u_ref, r), _token_tile(xs_ref, pos_ref[s, r]), sem)

    def start(r, carry):
        for s in range(TOP_K):
            copy(r, s).start()
        return carry

    def wait(r, carry):
        for s in range(TOP_K):
            copy(r, s).wait()
        return carry

    lax.fori_loop(0, tb, start, 0, unroll=DMA_UNROLL)
    lax.fori_loop(0, tb, wait, 0, unroll=DMA_UNROLL)


def _dispatch(u8, pos, xs8, *, tb):
    rows = u8.shape[0] // ROW_TILE
    return pl.pallas_call(
        functools.partial(_dispatch_kernel, tb=tb),
        grid=(rows // tb,),
        in_specs=[
            pl.BlockSpec((TOP_K, tb), lambda i: (0, i), memory_space=pltpu.SMEM),
            pl.BlockSpec((tb * ROW_TILE, LANES), lambda i: (i, 0)),
            pl.BlockSpec(memory_space=pl.ANY),
        ],
        out_specs=pl.BlockSpec(memory_space=pl.ANY),
        out_shape=jax.ShapeDtypeStruct(xs8.shape, xs8.dtype),
        scratch_shapes=[pltpu.SemaphoreType.DMA(())],
        input_output_aliases={2: 0},
        compiler_params=_params(("arbitrary",)),
        name="moe_dispatch",
    )(pos, u8, xs8)


def _experts_kernel(te_ref, nu_ref, x_ref, wg_ref, wu_ref, wd_ref, o_ref, x_scr, acc_scr, *, tm):
    del te_ref
    i = pl.program_id(0)
    f = pl.program_id(1)

    @pl.when(i < nu_ref[0])
    def _():
        @pl.when(f == 0)
        def _():
            x_scr[...] = _tiles_to_rows(x_ref, tm).astype(BF16)

        xb = x_scr[...]
        gate = jnp.dot(xb, wg_ref[...], preferred_element_type=F32)
        up = jnp.dot(xb, wu_ref[...], preferred_element_type=F32)
        part = _bdot(_silu(gate) * up, wd_ref[...])

        @pl.when(f == 0)
        def _():
            acc_scr[...] = part

        @pl.when(f > 0)
        def _():
            acc_scr[...] = acc_scr[...] + part

        @pl.when(f == pl.num_programs(1) - 1)
        def _():
            _rows_to_tiles(o_ref, acc_scr[...], tm)

    @pl.when(i >= nu_ref[0])
    def _():
        o_ref[...] = jnp.zeros_like(o_ref)


def _experts(xs8, tile_expert, n_used, w_gate, w_up, w_down, *, tm, tf):
    rows = xs8.shape[0] // ROW_TILE
    d = D_MODEL
    dff = w_gate.shape[-1]
    n_f = dff // tf
    last_f = n_f - 1

    def row_map(i, f, te, nu):
        return (jnp.minimum(i, nu[0] - 1), 0)

    def fsel(i, f, nu):
        return jnp.where(i < nu[0], f, last_f)

    grid_spec = pltpu.PrefetchScalarGridSpec(
        num_scalar_prefetch=2,
        grid=(rows // tm, n_f),
        in_specs=[
            pl.BlockSpec((tm * ROW_TILE, LANES), row_map),
            pl.BlockSpec((None, d, tf), lambda i, f, te, nu: (te[i], 0, fsel(i, f, nu))),
            pl.BlockSpec((None, d, tf), lambda i, f, te, nu: (te[i], 0, fsel(i, f, nu))),
            pl.BlockSpec((None, tf, d), lambda i, f, te, nu: (te[i], fsel(i, f, nu), 0)),
        ],
        out_specs=pl.BlockSpec((tm * ROW_TILE, LANES), lambda i, f, te, nu: (i, 0)),
        scratch_shapes=[pltpu.VMEM((tm, d), BF16), pltpu.VMEM((tm, d), F32)],
    )
    return pl.pallas_call(
        functools.partial(_experts_kernel, tm=tm),
        grid_spec=grid_spec,
        out_shape=jax.ShapeDtypeStruct(xs8.shape, F32),
        compiler_params=_params(("arbitrary", "arbitrary")),
        name="moe_experts",
    )(tile_expert, n_used, xs8, w_gate, w_up, w_down)


def _combine_kernel(pos_ref, h_ref, wts_ref, g_ref, ys_ref, o_ref, y_scr, sem, *, tb, final_norm):
    def copy(r, s):
        return pltpu.make_async_copy(_token_tile(ys_ref, pos_ref[s, r]), _token_tile(y_scr.at[s], r), sem)

    def start(r, carry):
        for s in range(TOP_K):
            copy(r, s).start()
        return carry

    def wait(r, carry):
        for s in range(TOP_K):
            copy(r, s).wait()
        return carry

    lax.fori_loop(0, tb, start, 0, unroll=DMA_UNROLL)
    lax.fori_loop(0, tb, wait, 0, unroll=DMA_UNROLL)
    w1 = wts_ref[:, 0:1]
    w2 = wts_ref[:, 1:2]
    moe = jnp.concatenate([w1 * _tiles_chunk(y_scr.at[0], c, tb) + w2 * _tiles_chunk(y_scr.at[1], c, tb)
                           for c in range(ROW_CHUNKS)], axis=-1)
    out = h_ref[...] + moe
    if final_norm:
        ms = jnp.mean(out * out, axis=-1, keepdims=True)
        out = out * lax.rsqrt(ms + EPS) * g_ref[...]
    o_ref[...] = out


def _combine(h, pos, wts_t, ys8, final_g, *, tb):
    rows, d = h.shape
    final_norm = final_g is not None
    gain = final_g.reshape(1, d) if final_norm else jnp.ones((1, d), F32)
    return pl.pallas_call(
        functools.partial(_combine_kernel, tb=tb, final_norm=final_norm),
        grid=(rows // tb,),
        in_specs=[
            pl.BlockSpec((TOP_K, tb), lambda i: (0, i), memory_space=pltpu.SMEM),
            pl.BlockSpec((tb, d), lambda i: (i, 0)),
            pl.BlockSpec((tb, TOP_K), lambda i: (i, 0)),
            pl.BlockSpec((1, d), lambda i: (0, 0)),
            pl.BlockSpec(memory_space=pl.ANY),
        ],
        out_specs=pl.BlockSpec((tb, d), lambda i: (i, 0)),
        out_shape=jax.ShapeDtypeStruct((rows, d), F32),
        scratch_shapes=[pltpu.VMEM((TOP_K, tb * ROW_TILE, LANES), F32), pltpu.SemaphoreType.DMA(())],
        compiler_params=_params(("arbitrary",)),
        name="moe_combine",
    )(pos, h, wts_t, gain, ys8)


PROJ_TN = 2560
S5_TT = 128
FFN_TF = 1408
MOE_TM = 512
MOE_TF = 1792


def _moe(h_list, u8_list, router, w_gate, w_up, w_down, final_g):
    routes = []
    for u8 in u8_list:
        routes.append(_router(u8, router, tm=min(512, u8.shape[0] // ROW_TILE)))
    counts = [r[3][:, 0].astype(jnp.int32) for r in routes]
    total = sum(counts)
    padded = ((total + MOE_TM - 1) // MOE_TM) * MOE_TM
    ends = jnp.cumsum(padded)
    starts = ends - padded
    n_rows = sum(h.shape[0] for h in h_list) * TOP_K
    n_tiles = n_rows // MOE_TM + N_EXPERTS
    n_used = (ends[-1] // MOE_TM).astype(jnp.int32)
    tile_start = jnp.arange(n_tiles, dtype=jnp.int32) * MOE_TM
    tile_expert = jnp.sum((tile_start[:, None] >= ends[None, :]).astype(jnp.int32), axis=1)
    last_expert = jnp.sum((((n_used - 1) * MOE_TM) >= ends).astype(jnp.int32))
    tile_expert = jnp.where(jnp.arange(n_tiles) < n_used, tile_expert, last_expert).astype(jnp.int32)

    xs8 = jnp.zeros((n_tiles * MOE_TM * ROW_TILE, LANES), F32)
    poss = []
    seen = jnp.zeros((N_EXPERTS,), jnp.int32)
    for u8, (idx, _, rank, _), cnt in zip(u8_list, routes, counts):
        base = starts + seen
        pos = rank
        for e in range(N_EXPERTS):
            pos = pos + jnp.where(idx == e, base[e], 0)
        poss.append(pos)
        seen = seen + cnt
        xs8 = _dispatch(u8, pos, xs8, tb=min(256, u8.shape[0] // ROW_TILE))
    ys8 = _experts(xs8, tile_expert, n_used.reshape(1), w_gate, w_up, w_down, tm=MOE_TM, tf=MOE_TF)
    outs = []
    for h, pos, (_, wts, _, _) in zip(h_list, poss, routes):
        outs.append(_combine(h, pos, wts.T, ys8, final_g, tb=min(256, h.shape[0])))
    return outs


def _rmsnorm_kernel(x_ref, g_ref, o_ref):
    x = x_ref[...]
    ms = jnp.mean(x * x, axis=-1, keepdims=True)
    o_ref[...] = x * lax.rsqrt(ms + EPS) * g_ref[...]


def _rmsnorm(x, g, *, tm):
    rows, d = x.shape
    return pl.pallas_call(
        _rmsnorm_kernel,
        grid=(rows // tm,),
        in_specs=[pl.BlockSpec((tm, d), lambda i: (i, 0)), pl.BlockSpec((1, d), lambda i: (0, 0))],
        out_specs=pl.BlockSpec((tm, d), lambda i: (i, 0)),
        out_shape=jax.ShapeDtypeStruct((rows, d), F32),
        compiler_params=_params(("parallel",)),
        name="final_norm",
    )(x, g.reshape(1, d))


def _pack_s5_state(re, im):
    n = re.shape[0]
    return jnp.concatenate([re.reshape(n, S5_LANES), im.reshape(n, S5_LANES)], axis=-1)


def _unpack_s5_state(h):
    n = h.shape[0]
    return (h[:, :S5_LANES].reshape(n, S5_GROUPS, S5_STATE), h[:, S5_LANES:].reshape(n, S5_GROUPS, S5_STATE))


def _mixer(x, s5_h0, ret_s0, conv_buf, pos_offset, p, *, single_step, tiled_u, layer=0, ret_stack=None):
    bsz, seq, d = x.shape
    z = _norm_proj(x, p["norm_g"], p["w_in"], layer, tt=min(512, seq), tn=PROJ_TN)
    s5_out, s5_state = _s5_branch(z, s5_h0, p["a_re"], p["a_im"], p["bmat"], p["cmat"], p["d_skip"], p["glu_w"],
                                  p["glu_b"], single_step=single_step, tt=1 if single_step else S5_TT)

    if single_step:
        z2 = z.reshape(seq, N_IN)
        ret_out, ret_state = _retention_step(z2, ret_s0, layer, ret_stack, pos_offset, bb=16)
        conv_out, conv_state = _conv_step(z2, conv_buf, p["pw_b"], p["dw_w"], p["dw_b"], p["ln_g"], p["ln_b"], bb=32)
        ret_out = ret_out.reshape(bsz, seq, RET_V)
        conv_out = conv_out.reshape(bsz, seq, CONV_WIDTH)
    else:
        ret_out, ret_state = _retention_prompt(z, ret_s0, pos_offset, tt=256)
        conv_out, conv_state = _conv_prompt(z, conv_buf, p["pw_b"], p["dw_w"], p["dw_b"], p["ln_g"], p["ln_b"],
                                            tt=256)

    h, u = _merge(x, s5_out, ret_out, conv_out, z, p["s5_proj"], p["ret_proj"], p["conv_proj"], p["w_out"],
                  p["norm_ffn_g"], tm=min(512, seq), tiled_u=tiled_u)
    return h, u, (s5_state, ret_state, conv_state)


def kernel(x_prompt, x_sample, state_s5_re, state_s5_im, state_ret, state_conv, norm_mix_g, w_in, s5_lambda_re, s5_lambda_im, s5_log_dt, s5_b_re, s5_b_im, s5_c_re, s5_c_im, s5_d, s5_glu_w, s5_glu_b, s5_proj, ret_proj, conv_pw_b, conv_dw_w, conv_dw_b, conv_ln_g, conv_ln_b, conv_proj, w_out, norm_ffn_g, ffn_w_gate, ffn_w_up, ffn_w_down, moe_router, moe_w_gate, moe_w_up, moe_w_down, norm_final_g):
    depth = w_in.shape[0]
    bp, seq, d = x_prompt.shape
    ns = x_sample.shape[0]
    past_len = 16384
    bf = lambda a: a.astype(BF16)

    hp = x_prompt
    hs = x_sample.reshape(1, ns, d)
    zero_s5 = jnp.zeros((bp, 2 * S5_LANES), F32)
    zero_ret = jnp.zeros((bp, RET_HEADS, RET_DK, RET_DV), F32)
    zero_conv = jnp.zeros((bp, CONV_K - 1, CONV_WIDTH), F32)

    p_states, s_states = [], []
    ret_stack = None
    normed = False
    for l in range(depth):
        is_moe = l % 2 == 1
        a_re, a_im, bb_re, bb_im = _s5_discretize(s5_lambda_re[l], s5_lambda_im[l], s5_log_dt[l],
                                                  s5_b_re[l], s5_b_im[l])
        bmat, cmat = _s5_block_mats(bb_re, bb_im, s5_c_re[l], s5_c_im[l])
        p = dict(norm_g=norm_mix_g[l], w_in=w_in,
                 a_re=a_re, a_im=a_im, bmat=bmat, cmat=cmat, d_skip=s5_d[l], glu_w=bf(s5_glu_w[l]),
                 glu_b=s5_glu_b[l], pw_b=conv_pw_b[l], dw_w=conv_dw_w[l], dw_b=conv_dw_b[l], ln_g=conv_ln_g[l],
                 ln_b=conv_ln_b[l], s5_proj=bf(s5_proj[l]), ret_proj=bf(ret_proj[l]), conv_proj=bf(conv_proj[l]),
                 w_out=bf(w_out[l]), norm_ffn_g=norm_ffn_g[l])
        hp, up, st_p = _mixer(hp, zero_s5, zero_ret, zero_conv, 0.0, p, single_step=False, tiled_u=is_moe, layer=l)
        hs, us, st_s = _mixer(hs, _pack_s5_state(state_s5_re[l], state_s5_im[l]), state_ret, state_conv[l],
                              float(past_len), p, single_step=True, tiled_u=is_moe, layer=l, ret_stack=ret_stack)
        ret_stack = st_s[1]
        p_states.append(st_p)
        s_states.append(st_s)

        hp2, hs2 = hp.reshape(bp * seq, d), hs.reshape(ns, d)
        j = l // 2
        if is_moe:
            final_g = norm_final_g if l == depth - 1 else None
            normed = final_g is not None
            hp2, hs2 = _moe([hp2, hs2], [up, us], moe_router[j], bf(moe_w_gate[j]), bf(moe_w_up[j]),
                            bf(moe_w_down[j]), final_g)
        else:
            wg, wu, wd = bf(ffn_w_gate[j]), bf(ffn_w_up[j]), bf(ffn_w_down[j])
            hp2 = _ffn(hp2, up.reshape(bp * seq, d), wg, wu, wd, tm=512, tf=FFN_TF)
            hs2 = _ffn(hs2, us.reshape(ns, d), wg, wu, wd, tm=ns, tf=FFN_TF)
        hp = hp2.reshape(bp, seq, d)
        hs = hs2.reshape(1, ns, d)

    if normed:
        y_prompt, y_sample = hp, hs.reshape(ns, 1, d)
    else:
        y_prompt = _rmsnorm(hp.reshape(bp * seq, d), norm_final_g, tm=512).reshape(bp, seq, d)
        y_sample = _rmsnorm(hs.reshape(ns, d), norm_final_g, tm=ns).reshape(ns, 1, d)

    def stack_s5(states):
        s5 = [_unpack_s5_state(s[0]) for s in states]
        return jnp.stack([a for a, _ in s5]), jnp.stack([b for _, b in s5])

    p_re, p_im = stack_s5(p_states)
    s_re, s_im = stack_s5(s_states)
    p_ret = jnp.stack([s[1] for s in p_states])
    p_conv = jnp.stack([s[2] for s in p_states])
    s_conv = jnp.stack([s[2] for s in s_states])
    return (y_prompt, y_sample, p_re, p_im, p_ret, p_conv, s_re, s_im, ret_stack, s_conv)
```

```python
import functools
import math

import jax
import jax.numpy as jnp
from jax import lax
from jax.experimental import pallas as pl
from jax.experimental.pallas import tpu as pltpu

F32 = jnp.float32
BF16 = jnp.bfloat16

D_MODEL = 1024
S5_WIDTH = 512
S5_GROUP = 16
S5_GROUPS = 32
S5_STATE = 64
S5_LANES = S5_GROUPS * S5_STATE
RET_HEADS = 4
RET_DK = 128
RET_DV = 256
RET_QK = RET_HEADS * RET_DK
RET_V = RET_HEADS * RET_DV
RET_CHUNK = 128
ROPE_BASE = 10000.0
CONV_WIDTH = 512
CONV_K = 31
CONV_HIST = 32
N_EXPERTS = 8
TOP_K = 2
N_BRANCH = 3
EPS = 1e-6
N_IN = S5_WIDTH + 2 * RET_QK + 2 * RET_V + 2 * CONV_WIDTH + N_BRANCH * D_MODEL
ZB = 512
ZC_S5 = 0
ZC_Q = ZC_S5 + S5_WIDTH // ZB
ZC_K = ZC_Q + RET_QK // ZB
ZC_V = ZC_K + RET_QK // ZB
ZC_G = ZC_V + RET_V // ZB
ZC_CONV = ZC_G + RET_V // ZB
ZC_GATE = ZC_CONV + 2 * CONV_WIDTH // ZB
HEADS_PER_ZB = ZB // RET_DV

ROW_TILE = 8
LANES = 128
ROW_CHUNKS = D_MODEL // LANES
VMEM_LIMIT = 48 * 1024 * 1024
VMEM_LIMIT_BIG = 56 * 1024 * 1024


def _params(sem):
    return pltpu.CompilerParams(dimension_semantics=sem, vmem_limit_bytes=VMEM_LIMIT)


def _silu(x):
    return x * jax.nn.sigmoid(x)


def _bdot(a, b):
    return jnp.dot(a.astype(BF16), b.astype(BF16), preferred_element_type=F32)


def _norm_proj_kernel(x_ref, g_ref, w_ref, o_ref, w_scr):
    @pl.when((pl.program_id(1) == 0) & (pl.program_id(2) == 0))
    def _():
        w_scr[...] = w_ref[...].astype(BF16)

    x = x_ref[...]
    ms = jnp.mean(x * x, axis=-1, keepdims=True)
    u = (x * lax.rsqrt(ms + EPS) * g_ref[...]).astype(BF16)
    o_ref[...] = jnp.dot(u, w_scr[...], preferred_element_type=F32).astype(o_ref.dtype)


def _norm_proj(x, g, w_all, layer, *, tt, tn):
    bsz, seq, d = x.shape
    n = w_all.shape[2]
    return pl.pallas_call(
        _norm_proj_kernel,
        grid=(n // tn, bsz, seq // tt),
        in_specs=[
            pl.BlockSpec((None, tt, d), lambda j, b, i: (b, i, 0)),
            pl.BlockSpec((1, d), lambda j, b, i: (0, 0)),
            pl.BlockSpec((None, d, tn), lambda j, b, i: (layer, 0, j)),
        ],
        out_specs=pl.BlockSpec((None, tt, tn), lambda j, b, i: (b, i, j)),
        out_shape=jax.ShapeDtypeStruct((bsz, seq, n), BF16),
        scratch_shapes=[pltpu.VMEM((d, tn), BF16)],
        compiler_params=_params(("arbitrary", "arbitrary", "arbitrary")),
        name="norm_proj",
    )(x, g.reshape(1, d), w_all)


def _s5_disc_kernel(lre_ref, lim_ref, ldt_ref, bre_ref, bim_ref, are_ref, aim_ref, ore_ref, oim_ref):
    lam_re = lre_ref[...]
    lam_im = lim_ref[...]
    dt = jnp.exp(ldt_ref[...])
    mag = jnp.exp(lam_re * dt)
    ang = lam_im * dt
    lbar_re = mag * jnp.cos(ang)
    lbar_im = mag * jnp.sin(ang)
    den = lam_re * lam_re + lam_im * lam_im
    nr = lbar_re - 1.0
    f_re = (nr * lam_re + lbar_im * lam_im) / den
    f_im = (lbar_im * lam_re - nr * lam_im) / den
    b_re = bre_ref[...]
    b_im = bim_ref[...]
    are_ref[...] = lbar_re
    aim_ref[...] = lbar_im
    ore_ref[...] = f_re * b_re - f_im * b_im
    oim_ref[...] = f_re * b_im + f_im * b_re


def _s5_discretize(lam_re, lam_im, log_dt, b_re, b_im):
    g, n = lam_re.shape
    p = b_re.shape[-1]
    rows = g * n
    col = lambda a: a.reshape(rows, 1)
    ldt = jnp.broadcast_to(log_dt[:, None], (g, n))
    outs = pl.pallas_call(
        _s5_disc_kernel,
        out_shape=[jax.ShapeDtypeStruct((rows, 1), F32), jax.ShapeDtypeStruct((rows, 1), F32),
                   jax.ShapeDtypeStruct((rows, p), F32), jax.ShapeDtypeStruct((rows, p), F32)],
        name="s5_discretize",
    )(col(lam_re), col(lam_im), col(ldt), b_re.reshape(rows, p), b_im.reshape(rows, p))
    a_re, a_im, bb_re, bb_im = outs
    return a_re.reshape(g, n), a_im.reshape(g, n), bb_re.reshape(g, n, p), bb_im.reshape(g, n, p)


S5_KCH = 128
S5_NCHUNK = S5_WIDTH // S5_KCH
S5_GPC = S5_KCH // S5_GROUP
S5_SPC = S5_GPC * S5_STATE


def _s5_block_mats(bbar_re, bbar_im, c_re, c_im):
    eye = jnp.eye(S5_GPC, dtype=F32)

    def in_blocks(bb):
        t = bb.reshape(S5_NCHUNK, S5_GPC, S5_STATE, S5_GROUP)
        m = jnp.einsum("cgnp,gh->cgphn", t, eye)
        return m.reshape(S5_NCHUNK, S5_KCH, S5_SPC)

    def out_blocks(cc):
        t = cc.reshape(S5_NCHUNK, S5_GPC, S5_GROUP, S5_STATE)
        m = jnp.einsum("cgpn,gh->cgnhp", t, eye)
        return m.reshape(S5_NCHUNK, S5_SPC, S5_KCH)

    bmat = jnp.concatenate([in_blocks(bbar_re), in_blocks(bbar_im)], axis=-1).astype(BF16)
    cmat = jnp.stack([out_blocks(c_re), -out_blocks(c_im)], axis=1).astype(BF16)
    return bmat, cmat


def _gelu_tanh(x):
    return 0.5 * x * (1.0 + jnp.tanh(math.sqrt(2.0 / math.pi) * (x + 0.044715 * (x * x * x))))


def _cast_riders(rider_in, rider_out):
    for src, dst in zip(rider_in, rider_out):
        dst[...] = src[...].astype(BF16)


def _rider_specs(riders, steps, step_index):
    specs = []
    for r in riders:
        slab = r.shape[0] // steps
        assert slab * steps == r.shape[0] and slab % 16 == 0
        specs.append(pl.BlockSpec((slab, r.shape[1]), lambda *idx: (step_index(*idx), 0)))
    return specs


def _s5_kernel(u_ref, h0_ref, are_ref, aim_ref, bmat_ref, cmat_ref, d_ref, gw_ref, gb_ref, *rest,
               nb, tt, lane_chunk, n_riders):
    rider_in, rest = rest[:n_riders], rest[n_riders:]
    o_ref, hout_ref = rest[:2]
    rider_out = rest[2:2 + n_riders]
    hs_scr, h_scr, io_scr = rest[2 + n_riders:]
    _cast_riders(rider_in, rider_out)

    @pl.when(pl.program_id(0) == 0)
    def _():
        h_scr[...] = h0_ref[...]

    def seq_rows(b):
        return pl.ds(b, tt, stride=nb)

    def lanes(c):
        return slice(c * S5_KCH, (c + 1) * S5_KCH)

    if tt == 1:
        for c in range(S5_NCHUNK):
            io_scr[c] = u_ref[:, lanes(c)].astype(F32)
    else:
        for b in range(nb):
            for c in range(S5_NCHUNK):
                io_scr[c, seq_rows(b), :] = u_ref[b, :, lanes(c)].astype(F32)

    u = jnp.concatenate([io_scr[c] for c in range(S5_NCHUNK)], axis=-1)
    for c in range(S5_NCHUNK):
        bu = jnp.dot(io_scr[c].astype(BF16), bmat_ref[c], preferred_element_type=F32)
        hs_scr[:, c * S5_SPC:(c + 1) * S5_SPC] = bu[:, :S5_SPC]
        hs_scr[:, S5_LANES + c * S5_SPC:S5_LANES + (c + 1) * S5_SPC] = bu[:, S5_SPC:]

    for lc in range(S5_LANES // lane_chunk):
        re_sl = slice(lc * lane_chunk, (lc + 1) * lane_chunk)
        im_sl = slice(S5_LANES + lc * lane_chunk, S5_LANES + (lc + 1) * lane_chunk)
        a_re = jnp.broadcast_to(are_ref[:, re_sl], (nb, lane_chunk))
        a_im = jnp.broadcast_to(aim_ref[:, re_sl], (nb, lane_chunk))

        def step(t, carry):
            h_re, h_im = carry
            r0 = t * nb if isinstance(t, int) else pl.multiple_of(t * nb, nb)
            n_re = a_re * h_re - a_im * h_im + hs_scr[pl.ds(r0, nb), re_sl]
            n_im = a_re * h_im + a_im * h_re + hs_scr[pl.ds(r0, nb), im_sl]
            hs_scr[pl.ds(r0, nb), re_sl] = n_re
            hs_scr[pl.ds(r0, nb), im_sl] = n_im
            return n_re, n_im

        carry = (h_scr[:, re_sl], h_scr[:, im_sl])
        if tt == 1:
            carry = step(0, carry)
        else:
            carry = lax.fori_loop(0, tt, step, carry, unroll=4)
        h_scr[:, re_sl] = carry[0]
        h_scr[:, im_sl] = carry[1]

    hout_ref[...] = h_scr[...]

    ys = []
    for c in range(S5_NCHUNK):
        h_re = hs_scr[:, c * S5_SPC:(c + 1) * S5_SPC].astype(BF16)
        h_im = hs_scr[:, S5_LANES + c * S5_SPC:S5_LANES + (c + 1) * S5_SPC].astype(BF16)
        ys.append(jnp.dot(h_re, cmat_ref[c, 0], preferred_element_type=F32)
                  + jnp.dot(h_im, cmat_ref[c, 1], preferred_element_type=F32))
    y = jnp.concatenate(ys, axis=-1) + d_ref[...] * u
    z = _gelu_tanh(y)
    gate = jnp.dot(z.astype(BF16), gw_ref[...], preferred_element_type=F32) + gb_ref[...]
    out = z * jax.nn.sigmoid(gate)
    if tt == 1:
        o_ref[...] = out.astype(o_ref.dtype)
    else:
        for c in range(S5_NCHUNK):
            io_scr[c] = out[:, lanes(c)]
        for b in range(nb):
            for c in range(S5_NCHUNK):
                o_ref[b, :, lanes(c)] = io_scr[c, seq_rows(b), :].astype(o_ref.dtype)


def _s5_branch(z, h0, a_re, a_im, bmat, cmat, d_skip, glu_w, glu_b, riders=(), *, single_step, tt):
    bsz, seq, _ = z.shape
    if single_step:
        assert bsz == 1 and tt == 1
        nb = seq
        in_spec = pl.BlockSpec((None, nb, S5_WIDTH), lambda i: (0, 0, ZC_S5))
        out_spec = pl.BlockSpec((None, nb, S5_WIDTH), lambda i: (0, 0, 0))
        grid = (1,)
    else:
        nb = bsz
        in_spec = pl.BlockSpec((nb, tt, S5_WIDTH), lambda i: (0, i, ZC_S5))
        out_spec = pl.BlockSpec((nb, tt, S5_WIDTH), lambda i: (0, i, 0))
        grid = (seq // tt,)
    rblk = tt * nb
    lane_chunk = 1024 if nb <= 8 else 512
    const = lambda shape: pl.BlockSpec(shape, lambda i: (0,) * len(shape))
    rider_specs = _rider_specs(riders, grid[0], lambda i: i)
    outs = pl.pallas_call(
        functools.partial(_s5_kernel, nb=nb, tt=tt, lane_chunk=lane_chunk, n_riders=len(riders)),
        grid=grid,
        in_specs=[
            in_spec,
            const((nb, 2 * S5_LANES)),
            const((1, S5_LANES)),
            const((1, S5_LANES)),
            const(bmat.shape),
            const(cmat.shape),
            const((1, S5_WIDTH)),
            const((S5_WIDTH, S5_WIDTH)),
            const((1, S5_WIDTH)),
            *rider_specs,
        ],
        out_specs=[out_spec, const((nb, 2 * S5_LANES)), *rider_specs],
        out_shape=[jax.ShapeDtypeStruct((bsz, seq, S5_WIDTH), BF16),
                   jax.ShapeDtypeStruct((nb, 2 * S5_LANES), F32)]
                  + [jax.ShapeDtypeStruct(r.shape, BF16) for r in riders],
        scratch_shapes=[pltpu.VMEM((rblk, 2 * S5_LANES), F32), pltpu.VMEM((nb, 2 * S5_LANES), F32),
                        pltpu.VMEM((S5_NCHUNK, rblk, S5_KCH), F32)],
        compiler_params=_params(("arbitrary",)),
        name="s5_branch",
    )(z, h0, a_re.reshape(1, S5_LANES), a_im.reshape(1, S5_LANES), bmat, cmat,
      d_skip.reshape(1, S5_WIDTH), glu_w, glu_b.reshape(1, S5_WIDTH), *riders)
    return outs[0], outs[1], list(outs[2:])


def _rope_tables(pos):
    half = RET_DK // 2
    freqs = ROPE_BASE ** (-jnp.arange(half, dtype=F32) / half)
    ang = pos[:, None] * freqs[None, :]
    cos = jnp.cos(ang)
    sin = jnp.sin(ang)
    return jnp.concatenate([cos, cos], axis=-1), jnp.concatenate([-sin, sin], axis=-1)


def _rope(x, cos, sin):
    return x * cos + pltpu.roll(x, RET_DK // 2, 1) * sin


def _group_norm(o):
    mu = jnp.mean(o, axis=-1, keepdims=True)
    d = o - mu
    var = jnp.mean(d * d, axis=-1, keepdims=True)
    return d * lax.rsqrt(var + EPS)


def _retention_tables(chunk):
    log_gamma = jnp.log(1.0 - 2.0 ** (-5.0 - jnp.arange(RET_HEADS, dtype=F32)))
    idx = jnp.arange(chunk, dtype=F32)
    diff = idx[:, None] - idx[None, :]
    decay = jnp.where(diff >= 0, jnp.exp(jnp.maximum(diff, 0.0)[None] * log_gamma[:, None, None]), 0.0)
    cross = jnp.exp((idx + 1.0)[None, :] * log_gamma[:, None])[:, :, None]
    kdec = jnp.exp((chunk - 1.0 - idx)[None, :] * log_gamma[:, None])[:, :, None]
    full = jnp.exp(chunk * log_gamma)
    return decay, cross, kdec, full


def _head_cols(refs, h, rows):
    lo = (h % HEADS_PER_ZB) * RET_DV
    return refs[h // HEADS_PER_ZB][rows, lo:lo + RET_DV].astype(F32)


def _retention_kernel(q_ref, k_ref, v0_ref, v1_ref, g0_ref, g1_ref, cos_ref, sin_ref, s0_ref, decay_ref, cross_ref,
                      kdec_ref, full_ref, o_ref, sout_ref, s_scr, *, n_chunks):
    @pl.when(pl.program_id(1) == 0)
    def _():
        s_scr[...] = s0_ref[...]

    for c in range(n_chunks):
        rows = slice(c * RET_CHUNK, (c + 1) * RET_CHUNK)
        cos = cos_ref[rows, :]
        sin = sin_ref[rows, :]
        for h in range(RET_HEADS):
            qk_cols = slice(h * RET_DK, (h + 1) * RET_DK)
            v_cols = slice(h * RET_DV, (h + 1) * RET_DV)
            qh = _rope(q_ref[rows, qk_cols].astype(F32), cos, sin)
            kh = _rope(k_ref[rows, qk_cols].astype(F32), cos, sin) * (RET_DK ** -0.5)
            vb = _head_cols((v0_ref, v1_ref), h, rows).astype(BF16)
            qb = qh.astype(BF16)
            state = s_scr[h]
            inner = lax.dot_general(qb, kh.astype(BF16), (((1,), (1,)), ((), ())),
                                    preferred_element_type=F32) * decay_ref[h]
            out = (jnp.dot(inner.astype(BF16), vb, preferred_element_type=F32)
                   + jnp.dot(qb, state.astype(BF16), preferred_element_type=F32) * cross_ref[h])
            kd = (kh * kdec_ref[h]).astype(BF16)
            s_scr[h] = full_ref[h] * state + jnp.dot(kd.T, vb, preferred_element_type=F32)
            gate = _head_cols((g0_ref, g1_ref), h, rows)
            o_ref[rows, v_cols] = (_silu(gate) * _group_norm(out)).astype(o_ref.dtype)

    sout_ref[...] = s_scr[...]


def _retention_prompt(z, state0, pos_offset, *, tt):
    bsz, seq, _ = z.shape
    assert seq % RET_CHUNK == 0 and tt % RET_CHUNK == 0
    cos, sin = _rope_tables(jnp.arange(seq, dtype=F32) + pos_offset)
    decay, cross, kdec, full = _retention_tables(RET_CHUNK)
    full = jnp.broadcast_to(full[:, None, None], (RET_HEADS, 1, RET_DV))
    const = lambda shape: pl.BlockSpec(shape, lambda b, i: (0,) * len(shape))
    zblk = lambda col: pl.BlockSpec((None, tt, ZB), lambda b, i: (b, i, col))
    return pl.pallas_call(
        functools.partial(_retention_kernel, n_chunks=tt // RET_CHUNK),
        grid=(bsz, seq // tt),
        in_specs=[
            zblk(ZC_Q), zblk(ZC_K), zblk(ZC_V), zblk(ZC_V + 1), zblk(ZC_G), zblk(ZC_G + 1),
            pl.BlockSpec((tt, RET_DK), lambda b, i: (i, 0)),
            pl.BlockSpec((tt, RET_DK), lambda b, i: (i, 0)),
            pl.BlockSpec((None, RET_HEADS, RET_DK, RET_DV), lambda b, i: (b, 0, 0, 0)),
            const((RET_HEADS, RET_CHUNK, RET_CHUNK)),
            const((RET_HEADS, RET_CHUNK, 1)),
            const((RET_HEADS, RET_CHUNK, 1)),
            const((RET_HEADS, 1, RET_DV)),
        ],
        out_specs=[
            pl.BlockSpec((None, tt, RET_V), lambda b, i: (b, i, 0)),
            pl.BlockSpec((None, RET_HEADS, RET_DK, RET_DV), lambda b, i: (b, 0, 0, 0)),
        ],
        out_shape=[jax.ShapeDtypeStruct((bsz, seq, RET_V), BF16),
                   jax.ShapeDtypeStruct((bsz, RET_HEADS, RET_DK, RET_DV), F32)],
        scratch_shapes=[pltpu.VMEM((RET_HEADS, RET_DK, RET_DV), F32)],
        compiler_params=_params(("parallel", "arbitrary")),
        name="retention_prompt",
    )(z, z, z, z, z, z, cos, sin, state0, decay, cross, kdec, full)


def _retention_step_kernel(q_ref, k_ref, v0_ref, v1_ref, g0_ref, g1_ref, cos_ref, sin_ref, s_ref, gam_ref, *rest,
                           bb, layer):
    o_ref, sout_ref, o_scr = rest[-3:]

    @pl.when(pl.program_id(1) > 0)
    def _():
        sout_ref[...] = jnp.zeros_like(sout_ref)

    @pl.when(pl.program_id(1) == 0)
    def _():
        _retention_step_body(q_ref, k_ref, v0_ref, v1_ref, g0_ref, g1_ref, cos_ref, sin_ref, s_ref, gam_ref,
                             o_ref, sout_ref, o_scr, bb=bb)


def _retention_step_body(q_ref, k_ref, v0_ref, v1_ref, g0_ref, g1_ref, cos_ref, sin_ref, s_ref, gam_ref,
                         o_ref, sout_ref, o_scr, *, bb):
    cos = cos_ref[...]
    sin = sin_ref[...]
    for h in range(RET_HEADS):
        qk_cols = slice(h * RET_DK, (h + 1) * RET_DK)
        v_cols = slice(h * RET_DV, (h + 1) * RET_DV)
        qh = _rope(q_ref[:, qk_cols].astype(F32), cos, sin)
        kh = _rope(k_ref[:, qk_cols].astype(F32), cos, sin) * (RET_DK ** -0.5)
        qk = jnp.sum(qh * kh, axis=-1, keepdims=True)
        q_t = qh.T
        k_t = kh.T
        gamma = gam_ref[h]
        v_all = _head_cols((v0_ref, v1_ref), h, slice(None))
        for b in range(bb):
            state = s_ref[b, h]
            vrow = v_all[b:b + 1, :]
            qs = jnp.sum(q_t[:, b:b + 1] * state, axis=0, keepdims=True)
            o_scr[b:b + 1, v_cols] = qk[b:b + 1, :] * vrow + qs * gamma
            sout_ref[b, h] = gamma * state + k_t[:, b:b + 1] * vrow
    for h in range(RET_HEADS):
        v_cols = slice(h * RET_DV, (h + 1) * RET_DV)
        gate = _head_cols((g0_ref, g1_ref), h, slice(None))
        o_ref[:, v_cols] = (_silu(gate) * _group_norm(o_scr[:, v_cols])).astype(o_ref.dtype)


def _retention_step(z, states, layer, stack, pos, *, bb):
    n = z.shape[0]
    depth = states.shape[0]
    cos, sin = _rope_tables(jnp.full((1,), pos, F32))
    log_gamma = jnp.log(1.0 - 2.0 ** (-5.0 - jnp.arange(RET_HEADS, dtype=F32)))
    gam = jnp.broadcast_to(jnp.exp(log_gamma)[:, None, None], (RET_HEADS, 1, RET_DV))
    const = lambda shape: pl.BlockSpec(shape, lambda i, s: (0,) * len(shape))
    slab_shape = (None, bb, RET_HEADS, RET_DK, RET_DV)
    zblk = lambda col: pl.BlockSpec((bb, ZB), lambda i, s: (i, col))
    in_specs = [
        zblk(ZC_Q), zblk(ZC_K), zblk(ZC_V), zblk(ZC_V + 1), zblk(ZC_G), zblk(ZC_G + 1),
        const((1, RET_DK)),
        const((1, RET_DK)),
        pl.BlockSpec(slab_shape, lambda i, s: (layer, i, 0, 0, 0)),
        const((RET_HEADS, 1, RET_DV)),
    ]
    args = [z, z, z, z, z, z, cos, sin, states, gam]
    if layer == 0:
        n_slabs = depth
        aliases = {}
    else:
        n_slabs = 1
        in_specs.append(pl.BlockSpec(memory_space=pl.ANY))
        args.append(stack)
        aliases = {len(args) - 1: 1}
    return pl.pallas_call(
        functools.partial(_retention_step_kernel, bb=bb, layer=layer),
        grid=(n // bb, n_slabs),
        in_specs=in_specs,
        out_specs=[pl.BlockSpec((bb, RET_V), lambda i, s: (i, 0)),
                   pl.BlockSpec(slab_shape, lambda i, s: (layer + s, i, 0, 0, 0))],
        out_shape=[jax.ShapeDtypeStruct((n, RET_V), BF16), jax.ShapeDtypeStruct(states.shape, F32)],
        scratch_shapes=[pltpu.VMEM((bb, RET_V), F32)],
        input_output_aliases=aliases,
        compiler_params=_params(("parallel", "arbitrary")),
        name="retention_step",
    )(*args)


def _layer_norm(y, g, b):
    mu = jnp.mean(y, axis=-1, keepdims=True)
    d = y - mu
    var = jnp.mean(d * d, axis=-1, keepdims=True)
    return d * lax.rsqrt(var + EPS) * g + b


CONV_RB = 128
CONV_PITCH = 2
CONV_LCH = CONV_WIDTH // LANES


def _conv_glu(a_ref, b_ref, pwb_ref):
    a = a_ref[...].astype(F32) + pwb_ref[:, :CONV_WIDTH]
    b = b_ref[...].astype(F32) + pwb_ref[:, CONV_WIDTH:]
    return a * jax.nn.sigmoid(b)


def _conv_kernel(a_ref, b_ref, buf_ref, pwb_ref, dww_ref, dwb_ref, lng_ref, lnb_ref, o_ref, hist_ref, x_scr, y_scr,
                 *, tt):
    i = pl.program_id(1)

    def rows(start, n):
        return pl.ds(CONV_PITCH * start, n, stride=CONV_PITCH)

    def lanes(c):
        return slice(c * LANES, (c + 1) * LANES)

    @pl.when(i == 0)
    def _():
        for c in range(CONV_LCH):
            x_scr[c, rows(0, CONV_HIST), :] = buf_ref[:, lanes(c)]

    @pl.when(i > 0)
    def _():
        for c in range(CONV_LCH):
            x_scr[c, rows(0, CONV_HIST), :] = x_scr[c, rows(tt, CONV_HIST), :]

    glu = _conv_glu(a_ref, b_ref, pwb_ref)
    for c in range(CONV_LCH):
        x_scr[c, rows(CONV_HIST, tt), :] = glu[:, lanes(c)]
    hist_ref[...] = glu[tt - CONV_HIST:, :]

    off = CONV_HIST - (CONV_K - 1)
    for c in range(CONV_LCH):
        def row_block(r, carry, c=c):
            base = r * CONV_RB
            n_grp = CONV_RB // ROW_TILE
            accs = [jnp.broadcast_to(dwb_ref[:, lanes(c)], (ROW_TILE, LANES))] * n_grp
            for m in range(CONV_RB - ROW_TILE + CONV_K):
                win = x_scr[c, rows(base + (off + m), ROW_TILE), :]
                for k in range(m % ROW_TILE, CONV_K, ROW_TILE):
                    j = (m - k) // ROW_TILE
                    if 0 <= j < n_grp:
                        accs[j] = accs[j] + dww_ref[k:k + 1, lanes(c)] * win
            y_scr[pl.ds(pl.multiple_of(base, CONV_RB), CONV_RB), lanes(c)] = jnp.concatenate(accs, axis=0)
            return carry

        lax.fori_loop(0, tt // CONV_RB, row_block, 0)

    o_ref[...] = _silu(_layer_norm(y_scr[...], lng_ref[...], lnb_ref[...])).astype(o_ref.dtype)


def _conv_prompt(z, buf, pw_b, dw_w, dw_b, ln_g, ln_b, *, tt):
    bsz, seq, _ = z.shape
    assert seq >= CONV_HIST and tt >= CONV_HIST
    buf32 = jnp.pad(buf, ((0, 0), (CONV_HIST - (CONV_K - 1), 0), (0, 0)))
    const = lambda shape: pl.BlockSpec(shape, lambda b, i: (0,) * len(shape))
    zblk = lambda col: pl.BlockSpec((None, tt, ZB), lambda b, i: (b, i, col))
    out, hist = pl.pallas_call(
        functools.partial(_conv_kernel, tt=tt),
        grid=(bsz, seq // tt),
        in_specs=[
            zblk(ZC_CONV), zblk(ZC_CONV + 1),
            pl.BlockSpec((None, CONV_HIST, CONV_WIDTH), lambda b, i: (b, 0, 0)),
            const((1, 2 * CONV_WIDTH)),
            const((CONV_K, CONV_WIDTH)),
            const((1, CONV_WIDTH)),
            const((1, CONV_WIDTH)),
            const((1, CONV_WIDTH)),
        ],
        out_specs=[
            pl.BlockSpec((None, tt, CONV_WIDTH), lambda b, i: (b, i, 0)),
            pl.BlockSpec((None, CONV_HIST, CONV_WIDTH), lambda b, i: (b, 0, 0)),
        ],
        out_shape=[jax.ShapeDtypeStruct((bsz, seq, CONV_WIDTH), BF16),
                   jax.ShapeDtypeStruct((bsz, CONV_HIST, CONV_WIDTH), F32)],
        scratch_shapes=[pltpu.VMEM((CONV_LCH, CONV_PITCH * (CONV_HIST + tt), LANES), F32),
                        pltpu.VMEM((tt, CONV_WIDTH), F32)],
        compiler_params=_params(("parallel", "arbitrary")),
        name="conv_prompt",
    )(z, z, buf32, pw_b.reshape(1, -1), dw_w, dw_b.reshape(1, -1), ln_g.reshape(1, -1), ln_b.reshape(1, -1))
    return out, hist[:, CONV_HIST - (CONV_K - 1):, :]


def _conv_step_kernel(a_ref, b_ref, buf_ref, pwb_ref, dww_ref, dwb_ref, lng_ref, lnb_ref, o_ref, hist_ref):
    hist_len = CONV_K - 1
    glu = _conv_glu(a_ref, b_ref, pwb_ref)
    acc = dwb_ref[...] + dww_ref[hist_len:hist_len + 1, :] * glu
    for k in range(hist_len):
        acc = acc + dww_ref[k:k + 1, :] * buf_ref[:, k * CONV_WIDTH:(k + 1) * CONV_WIDTH]
    o_ref[...] = _silu(_layer_norm(acc, lng_ref[...], lnb_ref[...])).astype(o_ref.dtype)
    hist_ref[:, :(hist_len - 1) * CONV_WIDTH] = buf_ref[:, CONV_WIDTH:]
    hist_ref[:, (hist_len - 1) * CONV_WIDTH:] = glu


def _conv_step(z, buf, pw_b, dw_w, dw_b, ln_g, ln_b, *, bb):
    n = z.shape[0]
    hist_len = CONV_K - 1
    const = lambda shape: pl.BlockSpec(shape, lambda i: (0,) * len(shape))
    zblk = lambda col: pl.BlockSpec((bb, ZB), lambda i: (i, col))
    out, hist = pl.pallas_call(
        _conv_step_kernel,
        grid=(n // bb,),
        in_specs=[
            zblk(ZC_CONV), zblk(ZC_CONV + 1),
            pl.BlockSpec((bb, hist_len * CONV_WIDTH), lambda i: (i, 0)),
            const((1, 2 * CONV_WIDTH)),
            const((CONV_K, CONV_WIDTH)),
            const((1, CONV_WIDTH)),
            const((1, CONV_WIDTH)),
            const((1, CONV_WIDTH)),
        ],
        out_specs=[
            pl.BlockSpec((bb, CONV_WIDTH), lambda i: (i, 0)),
            pl.BlockSpec((bb, hist_len * CONV_WIDTH), lambda i: (i, 0)),
        ],
        out_shape=[jax.ShapeDtypeStruct((n, CONV_WIDTH), BF16),
                   jax.ShapeDtypeStruct((n, hist_len * CONV_WIDTH), F32)],
        compiler_params=_params(("parallel",)),
        name="conv_step",
    )(z, z, buf.reshape(n, hist_len * CONV_WIDTH), pw_b.reshape(1, -1), dw_w, dw_b.reshape(1, -1),
      ln_g.reshape(1, -1), ln_b.reshape(1, -1))
    return out, hist.reshape(n, hist_len, CONV_WIDTH)


def _rows_to_tiles(tile_ref, x, rows):
    for c in range(ROW_CHUNKS):
        tile_ref[pl.ds(c, rows, stride=ROW_TILE), :] = x[:, c * LANES:(c + 1) * LANES]


def _tiles_chunk(tile_ref, c, rows):
    return tile_ref[pl.ds(c, rows, stride=ROW_TILE), :]


def _tiles_to_rows(tile_ref, rows):
    return jnp.concatenate([_tiles_chunk(tile_ref, c, rows) for c in range(ROW_CHUNKS)], axis=-1)


def _merge_kernel(x_ref, s5_ref, ret_ref, conv_ref, *rest, tiled_u):
    gate_refs = rest[:N_BRANCH * D_MODEL // ZB]
    ps5_ref, pret_ref, pconv_ref, wout_ref, g_ref, h_ref, u_ref = rest[len(gate_refs):]
    per_branch = D_MODEL // ZB

    def gate(n):
        cols = [gate_refs[n * per_branch + j][...] for j in range(per_branch)]
        return jax.nn.sigmoid(jnp.concatenate(cols, axis=-1).astype(F32))

    merged = (gate(0) * jnp.dot(s5_ref[...], ps5_ref[...], preferred_element_type=F32)
              + gate(1) * jnp.dot(ret_ref[...], pret_ref[...], preferred_element_type=F32)
              + gate(2) * jnp.dot(conv_ref[...], pconv_ref[...], preferred_element_type=F32))
    h = x_ref[...] + _bdot(merged, wout_ref[...])
    h_ref[...] = h
    ms = jnp.mean(h * h, axis=-1, keepdims=True)
    u = h * lax.rsqrt(ms + EPS) * g_ref[...]
    if tiled_u:
        _rows_to_tiles(u_ref, u, u.shape[0])
    else:
        u_ref[...] = u.astype(u_ref.dtype)


def _merge(x, s5_out, ret_out, conv_out, z, s5_proj, ret_proj, conv_proj, w_out, norm_g, *, tm, tiled_u):
    bsz, seq, d = x.shape
    const = lambda shape: pl.BlockSpec(shape, lambda b, i: (0,) * len(shape))
    tok = lambda w, col=0: pl.BlockSpec((None, tm, w), lambda b, i: (b, i, col))
    n_i = seq // tm
    n_gate = N_BRANCH * d // ZB
    if tiled_u:
        u_spec = pl.BlockSpec((tm * ROW_TILE, LANES), lambda b, i: (b * n_i + i, 0))
        u_shape = jax.ShapeDtypeStruct((bsz * seq * ROW_TILE, LANES), F32)
    else:
        u_spec = tok(d)
        u_shape = jax.ShapeDtypeStruct((bsz, seq, d), BF16)
    return pl.pallas_call(
        functools.partial(_merge_kernel, tiled_u=tiled_u),
        grid=(bsz, seq // tm),
        in_specs=[
            tok(d),
            tok(S5_WIDTH),
            tok(RET_V),
            tok(CONV_WIDTH),
            *[tok(ZB, ZC_GATE + j) for j in range(n_gate)],
            const((S5_WIDTH, d)),
            const((RET_V, d)),
            const((CONV_WIDTH, d)),
            const((d, d)),
            const((1, d)),
        ],
        out_specs=[tok(d), u_spec],
        out_shape=[jax.ShapeDtypeStruct((bsz, seq, d), F32), u_shape],
        compiler_params=_params(("parallel", "parallel")),
        name="merge",
    )(x, s5_out, ret_out, conv_out, *([z] * n_gate), s5_proj, ret_proj, conv_proj, w_out, norm_g.reshape(1, d))


def _ffn_kernel(h_ref, u_ref, wg_ref, wu_ref, wd_ref, *rest, n_riders):
    rider_in, o_ref, rider_out = rest[:n_riders], rest[n_riders], rest[n_riders + 1:]
    f = pl.program_id(1)
    ub = u_ref[...].astype(BF16)
    gate = jnp.dot(ub, wg_ref[...], preferred_element_type=F32)
    up = jnp.dot(ub, wu_ref[...], preferred_element_type=F32)
    part = _bdot(_silu(gate) * up, wd_ref[...])

    @pl.when(f == 0)
    def _():
        o_ref[...] = h_ref[...] + part

    @pl.when(f > 0)
    def _():
        o_ref[...] = o_ref[...] + part

    _cast_riders(rider_in, rider_out)


def _ffn(h, u, w_gate, w_up, w_down, riders=(), *, tm, tf):
    rows, d = h.shape
    dff = w_gate.shape[1]
    n_i, n_f = rows // tm, dff // tf
    tok = pl.BlockSpec((tm, d), lambda i, f: (i, 0))
    rider_specs = _rider_specs(riders, n_i * n_f, lambda i, f: i * n_f + f)
    outs = pl.pallas_call(
        functools.partial(_ffn_kernel, n_riders=len(riders)),
        grid=(n_i, n_f),
        in_specs=[tok, tok,
                  pl.BlockSpec((d, tf), lambda i, f: (0, f)),
                  pl.BlockSpec((d, tf), lambda i, f: (0, f)),
                  pl.BlockSpec((tf, d), lambda i, f: (f, 0)),
                  *rider_specs],
        out_specs=[tok, *rider_specs],
        out_shape=[jax.ShapeDtypeStruct((rows, d), F32)] + [jax.ShapeDtypeStruct(r.shape, BF16) for r in riders],
        compiler_params=pltpu.CompilerParams(dimension_semantics=("arbitrary", "arbitrary"),
                                             vmem_limit_bytes=VMEM_LIMIT_BIG if riders else VMEM_LIMIT),
        name="ffn_dense",
    )(h, u, w_gate, w_up, w_down, *riders)
    return outs[0], list(outs[1:])


def _split_bf16(x):
    hi = x.astype(BF16)
    return hi, (x - hi.astype(F32)).astype(BF16)


def _router_kernel(u_ref, rt_ref, tri_ref, idx_ref, wts_ref, rank_ref, cnt_ref, cnt_scr, *, tm):
    @pl.when(pl.program_id(0) == 0)
    def _():
        cnt_scr[...] = jnp.zeros_like(cnt_scr)

    u_hi, u_lo = _split_bf16(_tiles_to_rows(u_ref, tm))
    r_hi, r_lo = _split_bf16(rt_ref[...])
    dn = (((1,), (1,)), ((), ()))
    logits = (lax.dot_general(r_hi, u_hi, dn, preferred_element_type=F32)
              + lax.dot_general(r_lo, u_hi, dn, preferred_element_type=F32)
              + lax.dot_general(r_hi, u_lo, dn, preferred_element_type=F32))
    eidx = lax.broadcasted_iota(jnp.int32, logits.shape, 0)
    m1 = jnp.max(logits, axis=0, keepdims=True)
    i1 = jnp.min(jnp.where(logits == m1, eidx, N_EXPERTS), axis=0, keepdims=True)
    rest = jnp.where(eidx == i1, -jnp.inf, logits)
    m2 = jnp.max(rest, axis=0, keepdims=True)
    i2 = jnp.min(jnp.where(rest == m2, eidx, N_EXPERTS), axis=0, keepdims=True)
    e2 = jnp.exp(m2 - m1)
    w1 = 1.0 / (1.0 + e2)
    idx_ref[...] = jnp.concatenate([i1, i2], axis=0)
    wts_ref[...] = jnp.concatenate([w1, e2 * w1], axis=0)

    hit1 = eidx == i1
    hit2 = eidx == i2
    hits = jnp.where(hit1 | hit2, 1.0, 0.0)
    before = jnp.dot(hits.astype(BF16), tri_ref[...], preferred_element_type=F32) + cnt_scr[...]
    rank_ref[...] = jnp.concatenate(
        [jnp.sum(jnp.where(hit1, before, 0.0), axis=0, keepdims=True),
         jnp.sum(jnp.where(hit2, before, 0.0), axis=0, keepdims=True)], axis=0).astype(jnp.int32)
    cnt_scr[...] = cnt_scr[...] + jnp.sum(hits, axis=1, keepdims=True)
    cnt_ref[...] = cnt_scr[...]


def _router(u8, router, *, tm):
    rows = u8.shape[0] // ROW_TILE
    d = D_MODEL
    tri = (jnp.arange(tm)[:, None] < jnp.arange(tm)[None, :]).astype(BF16)
    const = lambda shape: pl.BlockSpec(shape, lambda i: (0,) * len(shape))
    lane = pl.BlockSpec((TOP_K, tm), lambda i: (0, i))
    return pl.pallas_call(
        functools.partial(_router_kernel, tm=tm),
        grid=(rows // tm,),
        in_specs=[pl.BlockSpec((tm * ROW_TILE, LANES), lambda i: (i, 0)), const((N_EXPERTS, d)), const((tm, tm))],
        out_specs=[lane, lane, lane, const((N_EXPERTS, 1))],
        out_shape=[jax.ShapeDtypeStruct((TOP_K, rows), jnp.int32), jax.ShapeDtypeStruct((TOP_K, rows), F32),
                   jax.ShapeDtypeStruct((TOP_K, rows), jnp.int32), jax.ShapeDtypeStruct((N_EXPERTS, 1), F32)],
        scratch_shapes=[pltpu.VMEM((N_EXPERTS, 1), F32)],
        compiler_params=_params(("arbitrary",)),
        name="moe_router",
    )(u8, router.T, tri)


DMA_UNROLL = 8


def _token_tile(ref, r):
    return ref.at[pl.ds(pl.multiple_of(r * ROW_TILE, ROW_TILE), ROW_TILE)]


def _dispatch_kernel(pos_ref, u_ref, init_ref, xs_ref, sem, *, tb):
    del init_ref

    def copy(r, s):
        return pltpu.make_async_copy(_token_tile(u_ref, r), _token_tile(xs_ref, pos_ref[s, r]), sem)

    def start(r, carry):
        for s in range(TOP_K):
            copy(r, s).start()
        return carry

    def wait(r, carry):
        for s in range(TOP_K):
            copy(r, s).wait()
        return carry

    lax.fori_loop(0, tb, start, 0, unroll=DMA_UNROLL)
    lax.fori_loop(0, tb, wait, 0, unroll=DMA_UNROLL)


def _dispatch(u8, pos, xs8, *, tb):
    rows = u8.shape[0] // ROW_TILE
    return pl.pallas_call(
        functools.partial(_dispatch_kernel, tb=tb),
        grid=(rows // tb,),
        in_specs=[
            pl.BlockSpec((TOP_K, tb), lambda i: (0, i), memory_space=pltpu.SMEM),
            pl.BlockSpec((tb * ROW_TILE, LANES), lambda i: (i, 0)),
            pl.BlockSpec(memory_space=pl.ANY),
        ],
        out_specs=pl.BlockSpec(memory_space=pl.ANY),
        out_shape=jax.ShapeDtypeStruct(xs8.shape, xs8.dtype),
        scratch_shapes=[pltpu.SemaphoreType.DMA(())],
        input_output_aliases={2: 0},
        compiler_params=_params(("arbitrary",)),
        name="moe_dispatch",
    )(pos, u8, xs8)


def _experts_kernel(te_ref, nu_ref, x_ref, wg_ref, wu_ref, wd_ref, o_ref, x_scr, acc_scr, *, tm):
    del te_ref
    i = pl.program_id(0)
    f = pl.program_id(1)

    @pl.when(i < nu_ref[0])
    def _():
        @pl.when(f == 0)
        def _():
            x_scr[...] = _tiles_to_rows(x_ref, tm).astype(BF16)

        xb = x_scr[...]
        gate = jnp.dot(xb, wg_ref[...], preferred_element_type=F32)
        up = jnp.dot(xb, wu_ref[...], preferred_element_type=F32)
        part = _bdot(_silu(gate) * up, wd_ref[...])

        @pl.when(f == 0)
        def _():
            acc_scr[...] = part

        @pl.when(f > 0)
        def _():
            acc_scr[...] = acc_scr[...] + part

        @pl.when(f == pl.num_programs(1) - 1)
        def _():
            _rows_to_tiles(o_ref, acc_scr[...], tm)

    @pl.when(i >= nu_ref[0])
    def _():
        o_ref[...] = jnp.zeros_like(o_ref)


def _experts(xs8, tile_expert, n_used, w_gate, w_up, w_down, *, tm, tf):
    rows = xs8.shape[0] // ROW_TILE
    d = D_MODEL
    dff = w_gate.shape[-1]
    n_f = dff // tf
    last_f = n_f - 1

    def row_map(i, f, te, nu):
        return (jnp.minimum(i, nu[0] - 1), 0)

    def fsel(i, f, nu):
        return jnp.where(i < nu[0], f, last_f)

    grid_spec = pltpu.PrefetchScalarGridSpec(
        num_scalar_prefetch=2,
        grid=(rows // tm, n_f),
        in_specs=[
            pl.BlockSpec((tm * ROW_TILE, LANES), row_map),
            pl.BlockSpec((None, d, tf), lambda i, f, te, nu: (te[i], 0, fsel(i, f, nu))),
            pl.BlockSpec((None, d, tf), lambda i, f, te, nu: (te[i], 0, fsel(i, f, nu))),
            pl.BlockSpec((None, tf, d), lambda i, f, te, nu: (te[i], fsel(i, f, nu), 0)),
        ],
        out_specs=pl.BlockSpec((tm * ROW_TILE, LANES), lambda i, f, te, nu: (i, 0)),
        scratch_shapes=[pltpu.VMEM((tm, d), BF16), pltpu.VMEM((tm, d), F32)],
    )
    return pl.pallas_call(
        functools.partial(_experts_kernel, tm=tm),
        grid_spec=grid_spec,
        out_shape=jax.ShapeDtypeStruct(xs8.shape, F32),
        compiler_params=_params(("arbitrary", "arbitrary")),
        name="moe_experts",
    )(tile_expert, n_used, xs8, w_gate, w_up, w_down)


def _combine_kernel(pos_ref, h_ref, wts_ref, g_ref, ys_ref, o_ref, y_scr, sem, *, tb, final_norm):
    def copy(r, s):
        return pltpu.make_async_copy(_token_tile(ys_ref, pos_ref[s, r]), _token_tile(y_scr.at[s], r), sem)

    def start(r, carry):
        for s in range(TOP_K):
            copy(r, s).start()
        return carry

    def wait(r, carry):
        for s in range(TOP_K):
            copy(r, s).wait()
        return carry

    lax.fori_loop(0, tb, start, 0, unroll=DMA_UNROLL)
    lax.fori_loop(0, tb, wait, 0, unroll=DMA_UNROLL)
    w1 = wts_ref[:, 0:1]
    w2 = wts_ref[:, 1:2]
    moe = jnp.concatenate([w1 * _tiles_chunk(y_scr.at[0], c, tb) + w2 * _tiles_chunk(y_scr.at[1], c, tb)
                           for c in range(ROW_CHUNKS)], axis=-1)
    out = h_ref[...] + moe
    if final_norm:
        ms = jnp.mean(out * out, axis=-1, keepdims=True)
        out = out * lax.rsqrt(ms + EPS) * g_ref[...]
    o_ref[...] = out


def _combine(h, pos, wts_t, ys8, final_g, *, tb):
    rows, d = h.shape
    final_norm = final_g is not None
    gain = final_g.reshape(1, d) if final_norm else jnp.ones((1, d), F32)
    return pl.pallas_call(
        functools.partial(_combine_kernel, tb=tb, final_norm=final_norm),
        grid=(rows // tb,),
        in_specs=[
            pl.BlockSpec((TOP_K, tb), lambda i: (0, i), memory_space=pltpu.SMEM),
            pl.BlockSpec((tb, d), lambda i: (i, 0)),
            pl.BlockSpec((tb, TOP_K), lambda i: (i, 0)),
            pl.BlockSpec((1, d), lambda i: (0, 0)),
            pl.BlockSpec(memory_space=pl.ANY),
        ],
        out_specs=pl.BlockSpec((tb, d), lambda i: (i, 0)),
        out_shape=jax.ShapeDtypeStruct((rows, d), F32),
        scratch_shapes=[pltpu.VMEM((TOP_K, tb * ROW_TILE, LANES), F32), pltpu.SemaphoreType.DMA(())],
        compiler_params=_params(("arbitrary",)),
        name="moe_combine",
    )(pos, h, wts_t, gain, ys8)


PROJ_TN = 2560
S5_TT = 128
FFN_TF = 1408
MOE_TM = 512
MOE_TF = 1792
MOE_TB = 1024


def _moe(h_list, u8_list, router, w_gate, w_up, w_down, final_g):
    routes = []
    for u8 in u8_list:
        routes.append(_router(u8, router, tm=min(512, u8.shape[0] // ROW_TILE)))
    counts = [r[3][:, 0].astype(jnp.int32) for r in routes]
    total = sum(counts)
    padded = ((total + MOE_TM - 1) // MOE_TM) * MOE_TM
    ends = jnp.cumsum(padded)
    starts = ends - padded
    n_rows = sum(h.shape[0] for h in h_list) * TOP_K
    n_tiles = n_rows // MOE_TM + N_EXPERTS
    n_used = (ends[-1] // MOE_TM).astype(jnp.int32)
    tile_start = jnp.arange(n_tiles, dtype=jnp.int32) * MOE_TM
    tile_expert = jnp.sum((tile_start[:, None] >= ends[None, :]).astype(jnp.int32), axis=1)
    last_expert = jnp.sum((((n_used - 1) * MOE_TM) >= ends).astype(jnp.int32))
    tile_expert = jnp.where(jnp.arange(n_tiles) < n_used, tile_expert, last_expert).astype(jnp.int32)

    xs8 = jnp.zeros((n_tiles * MOE_TM * ROW_TILE, LANES), F32)
    poss = []
    seen = jnp.zeros((N_EXPERTS,), jnp.int32)
    for u8, (idx, _, rank, _), cnt in zip(u8_list, routes, counts):
        base = starts + seen
        pos = rank
        for e in range(N_EXPERTS):
            pos = pos + jnp.where(idx == e, base[e], 0)
        poss.append(pos)
        seen = seen + cnt
        xs8 = _dispatch(u8, pos, xs8, tb=min(MOE_TB, u8.shape[0] // ROW_TILE))
    ys8 = _experts(xs8, tile_expert, n_used.reshape(1), w_gate, w_up, w_down, tm=MOE_TM, tf=MOE_TF)
    outs = []
    for h, pos, (_, wts, _, _) in zip(h_list, poss, routes):
        outs.append(_combine(h, pos, wts.T, ys8, final_g, tb=min(MOE_TB, h.shape[0])))
    return outs


def _rmsnorm_kernel(x_ref, g_ref, o_ref):
    x = x_ref[...]
    ms = jnp.mean(x * x, axis=-1, keepdims=True)
    o_ref[...] = x * lax.rsqrt(ms + EPS) * g_ref[...]


def _rmsnorm(x, g, *, tm):
    rows, d = x.shape
    return pl.pallas_call(
        _rmsnorm_kernel,
        grid=(rows // tm,),
        in_specs=[pl.BlockSpec((tm, d), lambda i: (i, 0)), pl.BlockSpec((1, d), lambda i: (0, 0))],
        out_specs=pl.BlockSpec((tm, d), lambda i: (i, 0)),
        out_shape=jax.ShapeDtypeStruct((rows, d), F32),
        compiler_params=_params(("parallel",)),
        name="final_norm",
    )(x, g.reshape(1, d))


def _pack_s5_state(re, im):
    n = re.shape[0]
    return jnp.concatenate([re.reshape(n, S5_LANES), im.reshape(n, S5_LANES)], axis=-1)


def _unpack_s5_state(h):
    n = h.shape[0]
    return (h[:, :S5_LANES].reshape(n, S5_GROUPS, S5_STATE), h[:, S5_LANES:].reshape(n, S5_GROUPS, S5_STATE))


def _mixer(x, s5_h0, ret_s0, conv_buf, pos_offset, p, *, single_step, tiled_u, layer=0, ret_stack=None, riders=()):
    bsz, seq, d = x.shape
    z = _norm_proj(x, p["norm_g"], p["w_in"], layer, tt=min(512, seq), tn=PROJ_TN)
    s5_out, s5_state, cast = _s5_branch(z, s5_h0, p["a_re"], p["a_im"], p["bmat"], p["cmat"], p["d_skip"], p["glu_w"],
                                        p["glu_b"], riders, single_step=single_step, tt=1 if single_step else S5_TT)

    if single_step:
        z2 = z.reshape(seq, N_IN)
        ret_out, ret_state = _retention_step(z2, ret_s0, layer, ret_stack, pos_offset, bb=16)
        conv_out, conv_state = _conv_step(z2, conv_buf, p["pw_b"], p["dw_w"], p["dw_b"], p["ln_g"], p["ln_b"], bb=32)
        ret_out = ret_out.reshape(bsz, seq, RET_V)
        conv_out = conv_out.reshape(bsz, seq, CONV_WIDTH)
    else:
        ret_out, ret_state = _retention_prompt(z, ret_s0, pos_offset, tt=256)
        conv_out, conv_state = _conv_prompt(z, conv_buf, p["pw_b"], p["dw_w"], p["dw_b"], p["ln_g"], p["ln_b"],
                                            tt=256)

    h, u = _merge(x, s5_out, ret_out, conv_out, z, p["s5_proj"], p["ret_proj"], p["conv_proj"], p["w_out"],
                  p["norm_ffn_g"], tm=min(512, seq), tiled_u=tiled_u)
    return h, u, (s5_state, ret_state, conv_state), cast


def kernel(x_prompt, x_sample, state_s5_re, state_s5_im, state_ret, state_conv, norm_mix_g, w_in, s5_lambda_re, s5_lambda_im, s5_log_dt, s5_b_re, s5_b_im, s5_c_re, s5_c_im, s5_d, s5_glu_w, s5_glu_b, s5_proj, ret_proj, conv_pw_b, conv_dw_w, conv_dw_b, conv_ln_g, conv_ln_b, conv_proj, w_out, norm_ffn_g, ffn_w_gate, ffn_w_up, ffn_w_down, moe_router, moe_w_gate, moe_w_up, moe_w_down, norm_final_g):
    depth = w_in.shape[0]
    bp, seq, d = x_prompt.shape
    ns = x_sample.shape[0]
    past_len = 16384
    bf = lambda a: a.astype(BF16)

    hp = x_prompt
    hs = x_sample.reshape(1, ns, d)
    zero_s5 = jnp.zeros((bp, 2 * S5_LANES), F32)
    zero_ret = jnp.zeros((bp, RET_HEADS, RET_DK, RET_DV), F32)
    zero_conv = jnp.zeros((bp, CONV_K - 1, CONV_WIDTH), F32)

    p_states, s_states = [], []
    ret_stack = None
    normed = False
    moe_bf16 = None
    for l in range(depth):
        is_moe = l % 2 == 1
        a_re, a_im, bb_re, bb_im = _s5_discretize(s5_lambda_re[l], s5_lambda_im[l], s5_log_dt[l],
                                                  s5_b_re[l], s5_b_im[l])
        bmat, cmat = _s5_block_mats(bb_re, bb_im, s5_c_re[l], s5_c_im[l])
        p = dict(norm_g=norm_mix_g[l], w_in=w_in,
                 a_re=a_re, a_im=a_im, bmat=bmat, cmat=cmat, d_skip=s5_d[l], glu_w=bf(s5_glu_w[l]),
                 glu_b=s5_glu_b[l], pw_b=conv_pw_b[l], dw_w=conv_dw_w[l], dw_b=conv_dw_b[l], ln_g=conv_ln_g[l],
                 ln_b=conv_ln_b[l], s5_proj=bf(s5_proj[l]), ret_proj=bf(ret_proj[l]), conv_proj=bf(conv_proj[l]),
                 w_out=bf(w_out[l]), norm_ffn_g=norm_ffn_g[l])
        ffn_f32 = [] if is_moe else [ffn_w_gate[l // 2], ffn_w_up[l // 2], ffn_w_down[l // 2]]
        hp, up, st_p, ffn_bf16 = _mixer(hp, zero_s5, zero_ret, zero_conv, 0.0, p, single_step=False, tiled_u=is_moe,
                                        layer=l, riders=ffn_f32)
        hs, us, st_s, _ = _mixer(hs, _pack_s5_state(state_s5_re[l], state_s5_im[l]), state_ret, state_conv[l],
                                 float(past_len), p, single_step=True, tiled_u=is_moe, layer=l, ret_stack=ret_stack)
        ret_stack = st_s[1]
        p_states.append(st_p)
        s_states.append(st_s)

        hp2, hs2 = hp.reshape(bp * seq, d), hs.reshape(ns, d)
        j = l // 2
        if is_moe:
            final_g = norm_final_g if l == depth - 1 else None
            normed = final_g is not None
            if moe_bf16 is None:
                moe_bf16 = [bf(moe_w_gate[j]), bf(moe_w_up[j]), bf(moe_w_down[j])]
            hp2, hs2 = _moe([hp2, hs2], [up, us], moe_router[j], *moe_bf16, final_g)
            moe_bf16 = None
        else:
            wg, wu, wd = ffn_bf16
            riders = []
            if l + 1 < depth:
                jm = (l + 1) // 2
                shapes = [w.shape[1:] for w in (moe_w_gate, moe_w_up, moe_w_down)]
                riders = [w[jm].reshape(-1, w.shape[-1]) for w in (moe_w_gate, moe_w_up, moe_w_down)]
            hp2, cast = _ffn(hp2, up.reshape(bp * seq, d), wg, wu, wd, riders, tm=512, tf=FFN_TF)
            if riders:
                moe_bf16 = [c.reshape(s) for c, s in zip(cast, shapes)]
            hs2, _ = _ffn(hs2, us.reshape(ns, d), wg, wu, wd, tm=ns, tf=FFN_TF)
        hp = hp2.reshape(bp, seq, d)
        hs = hs2.reshape(1, ns, d)

    if normed:
        y_prompt, y_sample = hp, hs.reshape(ns, 1, d)
    else:
        y_prompt = _rmsnorm(hp.reshape(bp * seq, d), norm_final_g, tm=512).reshape(bp, seq, d)
        y_sample = _rmsnorm(hs.reshape(ns, d), norm_final_g, tm=ns).reshape(ns, 1, d)

    def stack_s5(states):
        s5 = [_unpack_s5_state(s[0]) for s in states]
        return jnp.stack([a for a, _ in s5]), jnp.stack([b for _, b in s5])

    p_re, p_im = stack_s5(p_states)
    s_re, s_im = stack_s5(s_states)
    p_ret = jnp.stack([s[1] for s in p_states])
    p_conv = jnp.stack([s[2] for s in p_states])
    s_conv = jnp.stack([s[2] for s in s_states])
    return (y_prompt, y_sample, p_re, p_im, p_ret, p_conv, s_re, s_im, ret_stack, s_conv)
```

```python
import functools
import math

import jax
import jax.numpy as jnp
from jax import lax
from jax.experimental import pallas as pl
from jax.experimental.pallas import tpu as pltpu

F32 = jnp.float32
BF16 = jnp.bfloat16

D_MODEL = 1024
S5_WIDTH = 512
S5_GROUP = 16
S5_GROUPS = 32
S5_STATE = 64
S5_LANES = S5_GROUPS * S5_STATE
RET_HEADS = 4
RET_DK = 128
RET_DV = 256
RET_QK = RET_HEADS * RET_DK
RET_V = RET_HEADS * RET_DV
RET_CHUNK = 128
ROPE_BASE = 10000.0
CONV_WIDTH = 512
CONV_K = 31
CONV_HIST = 32
N_EXPERTS = 8
TOP_K = 2
N_BRANCH = 3
EPS = 1e-6
N_IN = S5_WIDTH + 2 * RET_QK + 2 * RET_V + 2 * CONV_WIDTH + N_BRANCH * D_MODEL
ZB = 512
ZC_S5 = 0
ZC_Q = ZC_S5 + S5_WIDTH // ZB
ZC_K = ZC_Q + RET_QK // ZB
ZC_V = ZC_K + RET_QK // ZB
ZC_G = ZC_V + RET_V // ZB
ZC_CONV = ZC_G + RET_V // ZB
ZC_GATE = ZC_CONV + 2 * CONV_WIDTH // ZB
HEADS_PER_ZB = ZB // RET_DV

ROW_TILE = 8
LANES = 128
ROW_CHUNKS = D_MODEL // LANES
VMEM_LIMIT = 48 * 1024 * 1024
VMEM_LIMIT_BIG = 56 * 1024 * 1024
ROW_SPLIT = 2


def _params(sem):
    return pltpu.CompilerParams(dimension_semantics=sem, vmem_limit_bytes=VMEM_LIMIT)


def _silu(x):
    return x * jax.nn.sigmoid(x)


def _bdot(a, b):
    return jnp.dot(a.astype(BF16), b.astype(BF16), preferred_element_type=F32)


def _norm_proj_kernel(x_ref, g_ref, w_ref, o_ref, w_scr):
    @pl.when((pl.program_id(1) == 0) & (pl.program_id(2) == 0))
    def _():
        w_scr[...] = w_ref[...].astype(BF16)

    half = x_ref.shape[0] // ROW_SPLIT
    for r in range(ROW_SPLIT):
        rows = slice(r * half, (r + 1) * half)
        x = x_ref[rows, :]
        ms = jnp.mean(x * x, axis=-1, keepdims=True)
        u = (x * lax.rsqrt(ms + EPS) * g_ref[...]).astype(BF16)
        o_ref[rows, :] = jnp.dot(u, w_scr[...], preferred_element_type=F32).astype(o_ref.dtype)


def _norm_proj(x, g, w_all, layer, *, tt, tn):
    bsz, seq, d = x.shape
    n = w_all.shape[2]
    return pl.pallas_call(
        _norm_proj_kernel,
        grid=(n // tn, bsz, seq // tt),
        in_specs=[
            pl.BlockSpec((None, tt, d), lambda j, b, i: (b, i, 0)),
            pl.BlockSpec((1, d), lambda j, b, i: (0, 0)),
            pl.BlockSpec((None, d, tn), lambda j, b, i: (layer, 0, j)),
        ],
        out_specs=pl.BlockSpec((None, tt, tn), lambda j, b, i: (b, i, j)),
        out_shape=jax.ShapeDtypeStruct((bsz, seq, n), BF16),
        scratch_shapes=[pltpu.VMEM((d, tn), BF16)],
        compiler_params=_params(("arbitrary", "arbitrary", "arbitrary")),
        name="norm_proj",
    )(x, g.reshape(1, d), w_all)


def _s5_disc_kernel(lre_ref, lim_ref, ldt_ref, bre_ref, bim_ref, are_ref, aim_ref, ore_ref, oim_ref):
    lam_re = lre_ref[...]
    lam_im = lim_ref[...]
    dt = jnp.exp(ldt_ref[...])
    mag = jnp.exp(lam_re * dt)
    ang = lam_im * dt
    lbar_re = mag * jnp.cos(ang)
    lbar_im = mag * jnp.sin(ang)
    den = lam_re * lam_re + lam_im * lam_im
    nr = lbar_re - 1.0
    f_re = (nr * lam_re + lbar_im * lam_im) / den
    f_im = (lbar_im * lam_re - nr * lam_im) / den
    b_re = bre_ref[...]
    b_im = bim_ref[...]
    are_ref[...] = lbar_re
    aim_ref[...] = lbar_im
    ore_ref[...] = f_re * b_re - f_im * b_im
    oim_ref[...] = f_re * b_im + f_im * b_re


def _s5_discretize(lam_re, lam_im, log_dt, b_re, b_im):
    g, n = lam_re.shape
    p = b_re.shape[-1]
    rows = g * n
    col = lambda a: a.reshape(rows, 1)
    ldt = jnp.broadcast_to(log_dt[:, None], (g, n))
    outs = pl.pallas_call(
        _s5_disc_kernel,
        out_shape=[jax.ShapeDtypeStruct((rows, 1), F32), jax.ShapeDtypeStruct((rows, 1), F32),
                   jax.ShapeDtypeStruct((rows, p), F32), jax.ShapeDtypeStruct((rows, p), F32)],
        name="s5_discretize",
    )(col(lam_re), col(lam_im), col(ldt), b_re.reshape(rows, p), b_im.reshape(rows, p))
    a_re, a_im, bb_re, bb_im = outs
    return a_re.reshape(g, n), a_im.reshape(g, n), bb_re.reshape(g, n, p), bb_im.reshape(g, n, p)


S5_KCH = 128
S5_NCHUNK = S5_WIDTH // S5_KCH
S5_GPC = S5_KCH // S5_GROUP
S5_SPC = S5_GPC * S5_STATE


def _s5_block_mats(bbar_re, bbar_im, c_re, c_im):
    eye = jnp.eye(S5_GPC, dtype=F32)

    def in_blocks(bb):
        t = bb.reshape(S5_NCHUNK, S5_GPC, S5_STATE, S5_GROUP)
        m = jnp.einsum("cgnp,gh->cgphn", t, eye)
        return m.reshape(S5_NCHUNK, S5_KCH, S5_SPC)

    def out_blocks(cc):
        t = cc.reshape(S5_NCHUNK, S5_GPC, S5_GROUP, S5_STATE)
        m = jnp.einsum("cgpn,gh->cgnhp", t, eye)
        return m.reshape(S5_NCHUNK, S5_SPC, S5_KCH)

    bmat = jnp.concatenate([in_blocks(bbar_re), in_blocks(bbar_im)], axis=-1).astype(BF16)
    cmat = jnp.stack([out_blocks(c_re), -out_blocks(c_im)], axis=1).astype(BF16)
    return bmat, cmat


def _gelu_tanh(x):
    return 0.5 * x * (1.0 + jnp.tanh(math.sqrt(2.0 / math.pi) * (x + 0.044715 * (x * x * x))))


def _cast_riders(rider_in, rider_out):
    for src, dst in zip(rider_in, rider_out):
        dst[...] = src[...].astype(BF16)


def _rider_specs(riders, steps, step_index):
    specs = []
    for r in riders:
        slab = r.shape[0] // steps
        assert slab * steps == r.shape[0] and slab % 16 == 0
        specs.append(pl.BlockSpec((slab, r.shape[1]), lambda *idx: (step_index(*idx), 0)))
    return specs


def _s5_kernel(u_ref, h0_ref, are_ref, aim_ref, bmat_ref, cmat_ref, d_ref, gw_ref, gb_ref, *rest,
               nb, tt, lane_chunk, n_riders):
    rider_in, rest = rest[:n_riders], rest[n_riders:]
    o_ref, hout_ref = rest[:2]
    rider_out = rest[2:2 + n_riders]
    hs_scr, h_scr, io_scr = rest[2 + n_riders:]
    _cast_riders(rider_in, rider_out)

    @pl.when(pl.program_id(0) == 0)
    def _():
        h_scr[...] = h0_ref[...]

    def seq_rows(b):
        return pl.ds(b, tt, stride=nb)

    def lanes(c):
        return slice(c * S5_KCH, (c + 1) * S5_KCH)

    if tt == 1:
        for c in range(S5_NCHUNK):
            io_scr[c] = u_ref[:, lanes(c)].astype(F32)
    else:
        for b in range(nb):
            for c in range(S5_NCHUNK):
                io_scr[c, seq_rows(b), :] = u_ref[b, :, lanes(c)].astype(F32)

    u = jnp.concatenate([io_scr[c] for c in range(S5_NCHUNK)], axis=-1)
    for c in range(S5_NCHUNK):
        bu = jnp.dot(io_scr[c].astype(BF16), bmat_ref[c], preferred_element_type=F32)
        hs_scr[:, c * S5_SPC:(c + 1) * S5_SPC] = bu[:, :S5_SPC]
        hs_scr[:, S5_LANES + c * S5_SPC:S5_LANES + (c + 1) * S5_SPC] = bu[:, S5_SPC:]

    for lc in range(S5_LANES // lane_chunk):
        re_sl = slice(lc * lane_chunk, (lc + 1) * lane_chunk)
        im_sl = slice(S5_LANES + lc * lane_chunk, S5_LANES + (lc + 1) * lane_chunk)
        a_re = jnp.broadcast_to(are_ref[:, re_sl], (nb, lane_chunk))
        a_im = jnp.broadcast_to(aim_ref[:, re_sl], (nb, lane_chunk))

        def step(t, carry):
            h_re, h_im = carry
            r0 = t * nb if isinstance(t, int) else pl.multiple_of(t * nb, nb)
            n_re = a_re * h_re - a_im * h_im + hs_scr[pl.ds(r0, nb), re_sl]
            n_im = a_re * h_im + a_im * h_re + hs_scr[pl.ds(r0, nb), im_sl]
            hs_scr[pl.ds(r0, nb), re_sl] = n_re
            hs_scr[pl.ds(r0, nb), im_sl] = n_im
            return n_re, n_im

        carry = (h_scr[:, re_sl], h_scr[:, im_sl])
        if tt == 1:
            carry = step(0, carry)
        else:
            carry = lax.fori_loop(0, tt, step, carry, unroll=4)
        h_scr[:, re_sl] = carry[0]
        h_scr[:, im_sl] = carry[1]

    hout_ref[...] = h_scr[...]

    ys = []
    for c in range(S5_NCHUNK):
        h_re = hs_scr[:, c * S5_SPC:(c + 1) * S5_SPC].astype(BF16)
        h_im = hs_scr[:, S5_LANES + c * S5_SPC:S5_LANES + (c + 1) * S5_SPC].astype(BF16)
        ys.append(jnp.dot(h_re, cmat_ref[c, 0], preferred_element_type=F32)
                  + jnp.dot(h_im, cmat_ref[c, 1], preferred_element_type=F32))
    y = jnp.concatenate(ys, axis=-1) + d_ref[...] * u
    z = _gelu_tanh(y)
    gate = jnp.dot(z.astype(BF16), gw_ref[...], preferred_element_type=F32) + gb_ref[...]
    out = z * jax.nn.sigmoid(gate)
    if tt == 1:
        o_ref[...] = out.astype(o_ref.dtype)
    else:
        for c in range(S5_NCHUNK):
            io_scr[c] = out[:, lanes(c)]
        for b in range(nb):
            for c in range(S5_NCHUNK):
                o_ref[b, :, lanes(c)] = io_scr[c, seq_rows(b), :].astype(o_ref.dtype)


def _s5_branch(z, h0, a_re, a_im, bmat, cmat, d_skip, glu_w, glu_b, riders=(), *, single_step, tt):
    bsz, seq, _ = z.shape
    if single_step:
        assert bsz == 1 and tt == 1
        nb = seq
        in_spec = pl.BlockSpec((None, nb, S5_WIDTH), lambda i: (0, 0, ZC_S5))
        out_spec = pl.BlockSpec((None, nb, S5_WIDTH), lambda i: (0, 0, 0))
        grid = (1,)
    else:
        nb = bsz
        in_spec = pl.BlockSpec((nb, tt, S5_WIDTH), lambda i: (0, i, ZC_S5))
        out_spec = pl.BlockSpec((nb, tt, S5_WIDTH), lambda i: (0, i, 0))
        grid = (seq // tt,)
    rblk = tt * nb
    lane_chunk = 1024 if nb <= 8 else 512
    const = lambda shape: pl.BlockSpec(shape, lambda i: (0,) * len(shape))
    rider_specs = _rider_specs(riders, grid[0], lambda i: i)
    outs = pl.pallas_call(
        functools.partial(_s5_kernel, nb=nb, tt=tt, lane_chunk=lane_chunk, n_riders=len(riders)),
        grid=grid,
        in_specs=[
            in_spec,
            const((nb, 2 * S5_LANES)),
            const((1, S5_LANES)),
            const((1, S5_LANES)),
            const(bmat.shape),
            const(cmat.shape),
            const((1, S5_WIDTH)),
            const((S5_WIDTH, S5_WIDTH)),
            const((1, S5_WIDTH)),
            *rider_specs,
        ],
        out_specs=[out_spec, const((nb, 2 * S5_LANES)), *rider_specs],
        out_shape=[jax.ShapeDtypeStruct((bsz, seq, S5_WIDTH), BF16),
                   jax.ShapeDtypeStruct((nb, 2 * S5_LANES), F32)]
                  + [jax.ShapeDtypeStruct(r.shape, BF16) for r in riders],
        scratch_shapes=[pltpu.VMEM((rblk, 2 * S5_LANES), F32), pltpu.VMEM((nb, 2 * S5_LANES), F32),
                        pltpu.VMEM((S5_NCHUNK, rblk, S5_KCH), F32)],
        compiler_params=_params(("arbitrary",)),
        name="s5_branch",
    )(z, h0, a_re.reshape(1, S5_LANES), a_im.reshape(1, S5_LANES), bmat, cmat,
      d_skip.reshape(1, S5_WIDTH), glu_w, glu_b.reshape(1, S5_WIDTH), *riders)
    return outs[0], outs[1], list(outs[2:])


def _rope_tables(pos):
    half = RET_DK // 2
    freqs = ROPE_BASE ** (-jnp.arange(half, dtype=F32) / half)
    ang = pos[:, None] * freqs[None, :]
    cos = jnp.cos(ang)
    sin = jnp.sin(ang)
    return jnp.concatenate([cos, cos], axis=-1), jnp.concatenate([-sin, sin], axis=-1)


def _rope(x, cos, sin):
    return x * cos + pltpu.roll(x, RET_DK // 2, 1) * sin


def _group_norm(o):
    mu = jnp.mean(o, axis=-1, keepdims=True)
    d = o - mu
    var = jnp.mean(d * d, axis=-1, keepdims=True)
    return d * lax.rsqrt(var + EPS)


def _retention_tables(chunk):
    log_gamma = jnp.log(1.0 - 2.0 ** (-5.0 - jnp.arange(RET_HEADS, dtype=F32)))
    idx = jnp.arange(chunk, dtype=F32)
    diff = idx[:, None] - idx[None, :]
    decay = jnp.where(diff >= 0, jnp.exp(jnp.maximum(diff, 0.0)[None] * log_gamma[:, None, None]), 0.0)
    cross = jnp.exp((idx + 1.0)[None, :] * log_gamma[:, None])[:, :, None]
    kdec = jnp.exp((chunk - 1.0 - idx)[None, :] * log_gamma[:, None])[:, :, None]
    full = jnp.exp(chunk * log_gamma)
    return decay, cross, kdec, full


def _head_cols(refs, h, rows):
    lo = (h % HEADS_PER_ZB) * RET_DV
    return refs[h // HEADS_PER_ZB][rows, lo:lo + RET_DV].astype(F32)


def _retention_kernel(q_ref, k_ref, v0_ref, v1_ref, g0_ref, g1_ref, cos_ref, sin_ref, s0_ref, decay_ref, cross_ref,
                      kdec_ref, full_ref, o_ref, sout_ref, s_scr, *, n_chunks):
    @pl.when(pl.program_id(1) == 0)
    def _():
        s_scr[...] = s0_ref[...]

    for c in range(n_chunks):
        rows = slice(c * RET_CHUNK, (c + 1) * RET_CHUNK)
        cos = cos_ref[rows, :]
        sin = sin_ref[rows, :]
        for h in range(RET_HEADS):
            qk_cols = slice(h * RET_DK, (h + 1) * RET_DK)
            v_cols = slice(h * RET_DV, (h + 1) * RET_DV)
            qh = _rope(q_ref[rows, qk_cols].astype(F32), cos, sin)
            kh = _rope(k_ref[rows, qk_cols].astype(F32), cos, sin) * (RET_DK ** -0.5)
            vb = _head_cols((v0_ref, v1_ref), h, rows).astype(BF16)
            qb = qh.astype(BF16)
            state = s_scr[h]
            inner = lax.dot_general(qb, kh.astype(BF16), (((1,), (1,)), ((), ())),
                                    preferred_element_type=F32) * decay_ref[h]
            out = (jnp.dot(inner.astype(BF16), vb, preferred_element_type=F32)
                   + jnp.dot(qb, state.astype(BF16), preferred_element_type=F32) * cross_ref[h])
            kd = (kh * kdec_ref[h]).astype(BF16)
            s_scr[h] = full_ref[h] * state + jnp.dot(kd.T, vb, preferred_element_type=F32)
            gate = _head_cols((g0_ref, g1_ref), h, rows)
            o_ref[rows, v_cols] = (_silu(gate) * _group_norm(out)).astype(o_ref.dtype)

    sout_ref[...] = s_scr[...]


def _retention_prompt(z, state0, pos_offset, *, tt):
    bsz, seq, _ = z.shape
    assert seq % RET_CHUNK == 0 and tt % RET_CHUNK == 0
    cos, sin = _rope_tables(jnp.arange(seq, dtype=F32) + pos_offset)
    decay, cross, kdec, full = _retention_tables(RET_CHUNK)
    full = jnp.broadcast_to(full[:, None, None], (RET_HEADS, 1, RET_DV))
    const = lambda shape: pl.BlockSpec(shape, lambda b, i: (0,) * len(shape))
    zblk = lambda col: pl.BlockSpec((None, tt, ZB), lambda b, i: (b, i, col))
    return pl.pallas_call(
        functools.partial(_retention_kernel, n_chunks=tt // RET_CHUNK),
        grid=(bsz, seq // tt),
        in_specs=[
            zblk(ZC_Q), zblk(ZC_K), zblk(ZC_V), zblk(ZC_V + 1), zblk(ZC_G), zblk(ZC_G + 1),
            pl.BlockSpec((tt, RET_DK), lambda b, i: (i, 0)),
            pl.BlockSpec((tt, RET_DK), lambda b, i: (i, 0)),
            pl.BlockSpec((None, RET_HEADS, RET_DK, RET_DV), lambda b, i: (b, 0, 0, 0)),
            const((RET_HEADS, RET_CHUNK, RET_CHUNK)),
            const((RET_HEADS, RET_CHUNK, 1)),
            const((RET_HEADS, RET_CHUNK, 1)),
            const((RET_HEADS, 1, RET_DV)),
        ],
        out_specs=[
            pl.BlockSpec((None, tt, RET_V), lambda b, i: (b, i, 0)),
            pl.BlockSpec((None, RET_HEADS, RET_DK, RET_DV), lambda b, i: (b, 0, 0, 0)),
        ],
        out_shape=[jax.ShapeDtypeStruct((bsz, seq, RET_V), BF16),
                   jax.ShapeDtypeStruct((bsz, RET_HEADS, RET_DK, RET_DV), F32)],
        scratch_shapes=[pltpu.VMEM((RET_HEADS, RET_DK, RET_DV), F32)],
        compiler_params=_params(("parallel", "arbitrary")),
        name="retention_prompt",
    )(z, z, z, z, z, z, cos, sin, state0, decay, cross, kdec, full)


def _retention_step_kernel(q_ref, k_ref, v0_ref, v1_ref, g0_ref, g1_ref, cos_ref, sin_ref, s_ref, gam_ref, *rest,
                           bb, layer):
    o_ref, sout_ref, o_scr = rest[-3:]

    @pl.when(pl.program_id(1) > 0)
    def _():
        sout_ref[...] = jnp.zeros_like(sout_ref)

    @pl.when(pl.program_id(1) == 0)
    def _():
        _retention_step_body(q_ref, k_ref, v0_ref, v1_ref, g0_ref, g1_ref, cos_ref, sin_ref, s_ref, gam_ref,
                             o_ref, sout_ref, o_scr, bb=bb)


def _retention_step_body(q_ref, k_ref, v0_ref, v1_ref, g0_ref, g1_ref, cos_ref, sin_ref, s_ref, gam_ref,
                         o_ref, sout_ref, o_scr, *, bb):
    cos = cos_ref[...]
    sin = sin_ref[...]
    for h in range(RET_HEADS):
        qk_cols = slice(h * RET_DK, (h + 1) * RET_DK)
        v_cols = slice(h * RET_DV, (h + 1) * RET_DV)
        qh = _rope(q_ref[:, qk_cols].astype(F32), cos, sin)
        kh = _rope(k_ref[:, qk_cols].astype(F32), cos, sin) * (RET_DK ** -0.5)
        qk = jnp.sum(qh * kh, axis=-1, keepdims=True)
        q_t = qh.T
        k_t = kh.T
        gamma = gam_ref[h]
        v_all = _head_cols((v0_ref, v1_ref), h, slice(None))
        for b in range(bb):
            state = s_ref[b, h]
            vrow = v_all[b:b + 1, :]
            qs = jnp.sum(q_t[:, b:b + 1] * state, axis=0, keepdims=True)
            o_scr[b:b + 1, v_cols] = qk[b:b + 1, :] * vrow + qs * gamma
            sout_ref[b, h] = gamma * state + k_t[:, b:b + 1] * vrow
    for h in range(RET_HEADS):
        v_cols = slice(h * RET_DV, (h + 1) * RET_DV)
        gate = _head_cols((g0_ref, g1_ref), h, slice(None))
        o_ref[:, v_cols] = (_silu(gate) * _group_norm(o_scr[:, v_cols])).astype(o_ref.dtype)


def _retention_step(z, states, layer, stack, pos, *, bb):
    n = z.shape[0]
    depth = states.shape[0]
    cos, sin = _rope_tables(jnp.full((1,), pos, F32))
    log_gamma = jnp.log(1.0 - 2.0 ** (-5.0 - jnp.arange(RET_HEADS, dtype=F32)))
    gam = jnp.broadcast_to(jnp.exp(log_gamma)[:, None, None], (RET_HEADS, 1, RET_DV))
    const = lambda shape: pl.BlockSpec(shape, lambda i, s: (0,) * len(shape))
    slab_shape = (None, bb, RET_HEADS, RET_DK, RET_DV)
    zblk = lambda col: pl.BlockSpec((bb, ZB), lambda i, s: (i, col))
    in_specs = [
        zblk(ZC_Q), zblk(ZC_K), zblk(ZC_V), zblk(ZC_V + 1), zblk(ZC_G), zblk(ZC_G + 1),
        const((1, RET_DK)),
        const((1, RET_DK)),
        pl.BlockSpec(slab_shape, lambda i, s: (layer, i, 0, 0, 0)),
        const((RET_HEADS, 1, RET_DV)),
    ]
    args = [z, z, z, z, z, z, cos, sin, states, gam]
    if layer == 0:
        n_slabs = depth
        aliases = {}
    else:
        n_slabs = 1
        in_specs.append(pl.BlockSpec(memory_space=pl.ANY))
        args.append(stack)
        aliases = {len(args) - 1: 1}
    return pl.pallas_call(
        functools.partial(_retention_step_kernel, bb=bb, layer=layer),
        grid=(n // bb, n_slabs),
        in_specs=in_specs,
        out_specs=[pl.BlockSpec((bb, RET_V), lambda i, s: (i, 0)),
                   pl.BlockSpec(slab_shape, lambda i, s: (layer + s, i, 0, 0, 0))],
        out_shape=[jax.ShapeDtypeStruct((n, RET_V), BF16), jax.ShapeDtypeStruct(states.shape, F32)],
        scratch_shapes=[pltpu.VMEM((bb, RET_V), F32)],
        input_output_aliases=aliases,
        compiler_params=_params(("parallel", "arbitrary")),
        name="retention_step",
    )(*args)


def _layer_norm(y, g, b):
    mu = jnp.mean(y, axis=-1, keepdims=True)
    d = y - mu
    var = jnp.mean(d * d, axis=-1, keepdims=True)
    return d * lax.rsqrt(var + EPS) * g + b


CONV_RB = 128
CONV_PITCH = 2
CONV_LCH = CONV_WIDTH // LANES


def _conv_glu(a_ref, b_ref, pwb_ref):
    a = a_ref[...].astype(F32) + pwb_ref[:, :CONV_WIDTH]
    b = b_ref[...].astype(F32) + pwb_ref[:, CONV_WIDTH:]
    return a * jax.nn.sigmoid(b)


def _conv_kernel(a_ref, b_ref, buf_ref, pwb_ref, dww_ref, dwb_ref, lng_ref, lnb_ref, o_ref, hist_ref, x_scr, y_scr,
                 *, tt):
    i = pl.program_id(1)

    def rows(start, n):
        return pl.ds(CONV_PITCH * start, n, stride=CONV_PITCH)

    def lanes(c):
        return slice(c * LANES, (c + 1) * LANES)

    @pl.when(i == 0)
    def _():
        for c in range(CONV_LCH):
            x_scr[c, rows(0, CONV_HIST), :] = buf_ref[:, lanes(c)]

    @pl.when(i > 0)
    def _():
        for c in range(CONV_LCH):
            x_scr[c, rows(0, CONV_HIST), :] = x_scr[c, rows(tt, CONV_HIST), :]

    glu = _conv_glu(a_ref, b_ref, pwb_ref)
    for c in range(CONV_LCH):
        x_scr[c, rows(CONV_HIST, tt), :] = glu[:, lanes(c)]
    hist_ref[...] = glu[tt - CONV_HIST:, :]

    off = CONV_HIST - (CONV_K - 1)
    for c in range(CONV_LCH):
        def row_block(r, carry, c=c):
            base = r * CONV_RB
            n_grp = CONV_RB // ROW_TILE
            accs = [jnp.broadcast_to(dwb_ref[:, lanes(c)], (ROW_TILE, LANES))] * n_grp
            for m in range(CONV_RB - ROW_TILE + CONV_K):
                win = x_scr[c, rows(base + (off + m), ROW_TILE), :]
                for k in range(m % ROW_TILE, CONV_K, ROW_TILE):
                    j = (m - k) // ROW_TILE
                    if 0 <= j < n_grp:
                        accs[j] = accs[j] + dww_ref[k:k + 1, lanes(c)] * win
            y_scr[pl.ds(pl.multiple_of(base, CONV_RB), CONV_RB), lanes(c)] = jnp.concatenate(accs, axis=0)
            return carry

        lax.fori_loop(0, tt // CONV_RB, row_block, 0)

    o_ref[...] = _silu(_layer_norm(y_scr[...], lng_ref[...], lnb_ref[...])).astype(o_ref.dtype)


def _conv_prompt(z, buf, pw_b, dw_w, dw_b, ln_g, ln_b, *, tt):
    bsz, seq, _ = z.shape
    assert seq >= CONV_HIST and tt >= CONV_HIST
    buf32 = jnp.pad(buf, ((0, 0), (CONV_HIST - (CONV_K - 1), 0), (0, 0)))
    const = lambda shape: pl.BlockSpec(shape, lambda b, i: (0,) * len(shape))
    zblk = lambda col: pl.BlockSpec((None, tt, ZB), lambda b, i: (b, i, col))
    out, hist = pl.pallas_call(
        functools.partial(_conv_kernel, tt=tt),
        grid=(bsz, seq // tt),
        in_specs=[
            zblk(ZC_CONV), zblk(ZC_CONV + 1),
            pl.BlockSpec((None, CONV_HIST, CONV_WIDTH), lambda b, i: (b, 0, 0)),
            const((1, 2 * CONV_WIDTH)),
            const((CONV_K, CONV_WIDTH)),
            const((1, CONV_WIDTH)),
            const((1, CONV_WIDTH)),
            const((1, CONV_WIDTH)),
        ],
        out_specs=[
            pl.BlockSpec((None, tt, CONV_WIDTH), lambda b, i: (b, i, 0)),
            pl.BlockSpec((None, CONV_HIST, CONV_WIDTH), lambda b, i: (b, 0, 0)),
        ],
        out_shape=[jax.ShapeDtypeStruct((bsz, seq, CONV_WIDTH), BF16),
                   jax.ShapeDtypeStruct((bsz, CONV_HIST, CONV_WIDTH), F32)],
        scratch_shapes=[pltpu.VMEM((CONV_LCH, CONV_PITCH * (CONV_HIST + tt), LANES), F32),
                        pltpu.VMEM((tt, CONV_WIDTH), F32)],
        compiler_params=_params(("parallel", "arbitrary")),
        name="conv_prompt",
    )(z, z, buf32, pw_b.reshape(1, -1), dw_w, dw_b.reshape(1, -1), ln_g.reshape(1, -1), ln_b.reshape(1, -1))
    return out, hist[:, CONV_HIST - (CONV_K - 1):, :]


def _conv_step_kernel(a_ref, b_ref, buf_ref, pwb_ref, dww_ref, dwb_ref, lng_ref, lnb_ref, o_ref, hist_ref):
    hist_len = CONV_K - 1
    glu = _conv_glu(a_ref, b_ref, pwb_ref)
    acc = dwb_ref[...] + dww_ref[hist_len:hist_len + 1, :] * glu
    for k in range(hist_len):
        acc = acc + dww_ref[k:k + 1, :] * buf_ref[:, k * CONV_WIDTH:(k + 1) * CONV_WIDTH]
    o_ref[...] = _silu(_layer_norm(acc, lng_ref[...], lnb_ref[...])).astype(o_ref.dtype)
    hist_ref[:, :(hist_len - 1) * CONV_WIDTH] = buf_ref[:, CONV_WIDTH:]
    hist_ref[:, (hist_len - 1) * CONV_WIDTH:] = glu


def _conv_step(z, buf, pw_b, dw_w, dw_b, ln_g, ln_b, *, bb):
    n = z.shape[0]
    hist_len = CONV_K - 1
    const = lambda shape: pl.BlockSpec(shape, lambda i: (0,) * len(shape))
    zblk = lambda col: pl.BlockSpec((bb, ZB), lambda i: (i, col))
    out, hist = pl.pallas_call(
        _conv_step_kernel,
        grid=(n // bb,),
        in_specs=[
            zblk(ZC_CONV), zblk(ZC_CONV + 1),
            pl.BlockSpec((bb, hist_len * CONV_WIDTH), lambda i: (i, 0)),
            const((1, 2 * CONV_WIDTH)),
            const((CONV_K, CONV_WIDTH)),
            const((1, CONV_WIDTH)),
            const((1, CONV_WIDTH)),
            const((1, CONV_WIDTH)),
        ],
        out_specs=[
            pl.BlockSpec((bb, CONV_WIDTH), lambda i: (i, 0)),
            pl.BlockSpec((bb, hist_len * CONV_WIDTH), lambda i: (i, 0)),
        ],
        out_shape=[jax.ShapeDtypeStruct((n, CONV_WIDTH), BF16),
                   jax.ShapeDtypeStruct((n, hist_len * CONV_WIDTH), F32)],
        compiler_params=_params(("parallel",)),
        name="conv_step",
    )(z, z, buf.reshape(n, hist_len * CONV_WIDTH), pw_b.reshape(1, -1), dw_w, dw_b.reshape(1, -1),
      ln_g.reshape(1, -1), ln_b.reshape(1, -1))
    return out, hist.reshape(n, hist_len, CONV_WIDTH)


def _rows_to_tiles(tile_ref, x, rows):
    for c in range(ROW_CHUNKS):
        tile_ref[pl.ds(c, rows, stride=ROW_TILE), :] = x[:, c * LANES:(c + 1) * LANES]


def _tiles_chunk(tile_ref, c, rows):
    return tile_ref[pl.ds(c, rows, stride=ROW_TILE), :]


def _tiles_to_rows(tile_ref, rows):
    return jnp.concatenate([_tiles_chunk(tile_ref, c, rows) for c in range(ROW_CHUNKS)], axis=-1)


def _merge_kernel(x_ref, s5_ref, ret_ref, conv_ref, *rest, tiled_u):
    gate_refs = rest[:N_BRANCH * D_MODEL // ZB]
    ps5_ref, pret_ref, pconv_ref, wout_ref, g_ref, h_ref, u_ref = rest[len(gate_refs):]
    per_branch = D_MODEL // ZB

    def gate(n):
        cols = [gate_refs[n * per_branch + j][...] for j in range(per_branch)]
        return jax.nn.sigmoid(jnp.concatenate(cols, axis=-1).astype(F32))

    merged = (gate(0) * jnp.dot(s5_ref[...], ps5_ref[...], preferred_element_type=F32)
              + gate(1) * jnp.dot(ret_ref[...], pret_ref[...], preferred_element_type=F32)
              + gate(2) * jnp.dot(conv_ref[...], pconv_ref[...], preferred_element_type=F32))
    h = x_ref[...] + _bdot(merged, wout_ref[...])
    h_ref[...] = h
    ms = jnp.mean(h * h, axis=-1, keepdims=True)
    u = h * lax.rsqrt(ms + EPS) * g_ref[...]
    if tiled_u:
        _rows_to_tiles(u_ref, u, u.shape[0])
    else:
        u_ref[...] = u.astype(u_ref.dtype)


def _merge(x, s5_out, ret_out, conv_out, z, s5_proj, ret_proj, conv_proj, w_out, norm_g, *, tm, tiled_u):
    bsz, seq, d = x.shape
    const = lambda shape: pl.BlockSpec(shape, lambda b, i: (0,) * len(shape))
    tok = lambda w, col=0: pl.BlockSpec((None, tm, w), lambda b, i: (b, i, col))
    n_i = seq // tm
    n_gate = N_BRANCH * d // ZB
    if tiled_u:
        u_spec = pl.BlockSpec((tm * ROW_TILE, LANES), lambda b, i: (b * n_i + i, 0))
        u_shape = jax.ShapeDtypeStruct((bsz * seq * ROW_TILE, LANES), F32)
    else:
        u_spec = tok(d)
        u_shape = jax.ShapeDtypeStruct((bsz, seq, d), BF16)
    return pl.pallas_call(
        functools.partial(_merge_kernel, tiled_u=tiled_u),
        grid=(bsz, seq // tm),
        in_specs=[
            tok(d),
            tok(S5_WIDTH),
            tok(RET_V),
            tok(CONV_WIDTH),
            *[tok(ZB, ZC_GATE + j) for j in range(n_gate)],
            const((S5_WIDTH, d)),
            const((RET_V, d)),
            const((CONV_WIDTH, d)),
            const((d, d)),
            const((1, d)),
        ],
        out_specs=[tok(d), u_spec],
        out_shape=[jax.ShapeDtypeStruct((bsz, seq, d), F32), u_shape],
        compiler_params=_params(("parallel", "parallel")),
        name="merge",
    )(x, s5_out, ret_out, conv_out, *([z] * n_gate), s5_proj, ret_proj, conv_proj, w_out, norm_g.reshape(1, d))


def _ffn_kernel(h_ref, u_ref, wg_ref, wu_ref, wd_ref, *rest, n_riders):
    rider_in, o_ref, rider_out = rest[:n_riders], rest[n_riders], rest[n_riders + 1:]
    f = pl.program_id(1)
    ub = u_ref[...].astype(BF16)
    gate = jnp.dot(ub, wg_ref[...], preferred_element_type=F32)
    up = jnp.dot(ub, wu_ref[...], preferred_element_type=F32)
    part = _bdot(_silu(gate) * up, wd_ref[...])

    @pl.when(f == 0)
    def _():
        o_ref[...] = h_ref[...] + part

    @pl.when(f > 0)
    def _():
        o_ref[...] = o_ref[...] + part

    _cast_riders(rider_in, rider_out)


def _ffn(h, u, w_gate, w_up, w_down, riders=(), *, tm, tf):
    rows, d = h.shape
    dff = w_gate.shape[1]
    n_i, n_f = rows // tm, dff // tf
    tok = pl.BlockSpec((tm, d), lambda i, f: (i, 0))
    rider_specs = _rider_specs(riders, n_i * n_f, lambda i, f: i * n_f + f)
    outs = pl.pallas_call(
        functools.partial(_ffn_kernel, n_riders=len(riders)),
        grid=(n_i, n_f),
        in_specs=[tok, tok,
                  pl.BlockSpec((d, tf), lambda i, f: (0, f)),
                  pl.BlockSpec((d, tf), lambda i, f: (0, f)),
                  pl.BlockSpec((tf, d), lambda i, f: (f, 0)),
                  *rider_specs],
        out_specs=[tok, *rider_specs],
        out_shape=[jax.ShapeDtypeStruct((rows, d), F32)] + [jax.ShapeDtypeStruct(r.shape, BF16) for r in riders],
        compiler_params=pltpu.CompilerParams(dimension_semantics=("arbitrary", "arbitrary"),
                                             vmem_limit_bytes=VMEM_LIMIT_BIG if riders else VMEM_LIMIT),
        name="ffn_dense",
    )(h, u, w_gate, w_up, w_down, *riders)
    return outs[0], list(outs[1:])


def _split_bf16(x):
    hi = x.astype(BF16)
    return hi, (x - hi.astype(F32)).astype(BF16)


def _router_kernel(u_ref, rt_ref, tri_ref, idx_ref, wts_ref, rank_ref, cnt_ref, cnt_scr, *, tm):
    @pl.when(pl.program_id(0) == 0)
    def _():
        cnt_scr[...] = jnp.zeros_like(cnt_scr)

    u_hi, u_lo = _split_bf16(_tiles_to_rows(u_ref, tm))
    r_hi, r_lo = _split_bf16(rt_ref[...])
    dn = (((1,), (1,)), ((), ()))
    logits = (lax.dot_general(r_hi, u_hi, dn, preferred_element_type=F32)
              + lax.dot_general(r_lo, u_hi, dn, preferred_element_type=F32)
              + lax.dot_general(r_hi, u_lo, dn, preferred_element_type=F32))
    eidx = lax.broadcasted_iota(jnp.int32, logits.shape, 0)
    m1 = jnp.max(logits, axis=0, keepdims=True)
    i1 = jnp.min(jnp.where(logits == m1, eidx, N_EXPERTS), axis=0, keepdims=True)
    rest = jnp.where(eidx == i1, -jnp.inf, logits)
    m2 = jnp.max(rest, axis=0, keepdims=True)
    i2 = jnp.min(jnp.where(rest == m2, eidx, N_EXPERTS), axis=0, keepdims=True)
    e2 = jnp.exp(m2 - m1)
    w1 = 1.0 / (1.0 + e2)
    idx_ref[...] = jnp.concatenate([i1, i2], axis=0)
    wts_ref[...] = jnp.concatenate([w1, e2 * w1], axis=0)

    hit1 = eidx == i1
    hit2 = eidx == i2
    hits = jnp.where(hit1 | hit2, 1.0, 0.0)
    before = jnp.dot(hits.astype(BF16), tri_ref[...], preferred_element_type=F32) + cnt_scr[...]
    rank_ref[...] = jnp.concatenate(
        [jnp.sum(jnp.where(hit1, before, 0.0), axis=0, keepdims=True),
         jnp.sum(jnp.where(hit2, before, 0.0), axis=0, keepdims=True)], axis=0).astype(jnp.int32)
    cnt_scr[...] = cnt_scr[...] + jnp.sum(hits, axis=1, keepdims=True)
    cnt_ref[...] = cnt_scr[...]


def _router(u8, router, *, tm):
    rows = u8.shape[0] // ROW_TILE
    d = D_MODEL
    tri = (jnp.arange(tm)[:, None] < jnp.arange(tm)[None, :]).astype(BF16)
    const = lambda shape: pl.BlockSpec(shape, lambda i: (0,) * len(shape))
    lane = pl.BlockSpec((TOP_K, tm), lambda i: (0, i))
    return pl.pallas_call(
        functools.partial(_router_kernel, tm=tm),
        grid=(rows // tm,),
        in_specs=[pl.BlockSpec((tm * ROW_TILE, LANES), lambda i: (i, 0)), const((N_EXPERTS, d)), const((tm, tm))],
        out_specs=[lane, lane, lane, const((N_EXPERTS, 1))],
        out_shape=[jax.ShapeDtypeStruct((TOP_K, rows), jnp.int32), jax.ShapeDtypeStruct((TOP_K, rows), F32),
                   jax.ShapeDtypeStruct((TOP_K, rows), jnp.int32), jax.ShapeDtypeStruct((N_EXPERTS, 1), F32)],
        scratch_shapes=[pltpu.VMEM((N_EXPERTS, 1), F32)],
        compiler_params=_params(("arbitrary",)),
        name="moe_router",
    )(u8, router.T, tri)


DMA_UNROLL = 8


def _token_tile(ref, r):
    return ref.at[pl.ds(pl.multiple_of(r * ROW_TILE, ROW_TILE), ROW_TILE)]


def _dispatch_kernel(pos_ref, u_ref, init_ref, xs_ref, sem, *, tb):
    del init_ref

    def start(r, carry):
        for s in range(TOP_K):
            pltpu.make_async_copy(_token_tile(u_ref, r), _token_tile(xs_ref, pos_ref[s, r]),
                                  sem.at[s]).start(priority=s)
        return carry

    lax.fori_loop(0, tb, start, 0, unroll=DMA_UNROLL)
    for s in range(TOP_K):
        pltpu.make_async_copy(u_ref, xs_ref.at[pl.ds(0, tb * ROW_TILE)], sem.at[s]).wait()


def _dispatch(u8, pos, xs8, *, tb):
    rows = u8.shape[0] // ROW_TILE
    return pl.pallas_call(
        functools.partial(_dispatch_kernel, tb=tb),
        grid=(rows // tb,),
        in_specs=[
            pl.BlockSpec((TOP_K, tb), lambda i: (0, i), memory_space=pltpu.SMEM),
            pl.BlockSpec((tb * ROW_TILE, LANES), lambda i: (i, 0)),
            pl.BlockSpec(memory_space=pl.ANY),
        ],
        out_specs=pl.BlockSpec(memory_space=pl.ANY),
        out_shape=jax.ShapeDtypeStruct(xs8.shape, xs8.dtype),
        scratch_shapes=[pltpu.SemaphoreType.DMA((TOP_K,))],
        input_output_aliases={2: 0},
        compiler_params=_params(("arbitrary",)),
        name="moe_dispatch",
    )(pos, u8, xs8)


def _experts_kernel(te_ref, nu_ref, x_ref, wg_ref, wu_ref, wd_ref, o_ref, x_scr, acc_scr, *, tm):
    del te_ref
    i = pl.program_id(0)
    f = pl.program_id(1)

    @pl.when(i < nu_ref[0])
    def _():
        @pl.when(f == 0)
        def _():
            x_scr[...] = _tiles_to_rows(x_ref, tm).astype(BF16)

        xb = x_scr[...]
        gate = jnp.dot(xb, wg_ref[...], preferred_element_type=F32)
        up = jnp.dot(xb, wu_ref[...], preferred_element_type=F32)
        part = _bdot(_silu(gate) * up, wd_ref[...])

        @pl.when(f == 0)
        def _():
            acc_scr[...] = part

        @pl.when(f > 0)
        def _():
            acc_scr[...] = acc_scr[...] + part

        @pl.when(f == pl.num_programs(1) - 1)
        def _():
            _rows_to_tiles(o_ref, acc_scr[...], tm)

    @pl.when(i >= nu_ref[0])
    def _():
        o_ref[...] = jnp.zeros_like(o_ref)


def _experts(xs8, tile_expert, n_used, w_gate, w_up, w_down, *, tm, tf):
    rows = xs8.shape[0] // ROW_TILE
    d = D_MODEL
    dff = w_gate.shape[-1]
    n_f = dff // tf
    last_f = n_f - 1

    def row_map(i, f, te, nu):
        return (jnp.minimum(i, nu[0] - 1), 0)

    def fsel(i, f, nu):
        return jnp.where(i < nu[0], f, last_f)

    grid_spec = pltpu.PrefetchScalarGridSpec(
        num_scalar_prefetch=2,
        grid=(rows // tm, n_f),
        in_specs=[
            pl.BlockSpec((tm * ROW_TILE, LANES), row_map),
            pl.BlockSpec((None, d, tf), lambda i, f, te, nu: (te[i], 0, fsel(i, f, nu))),
            pl.BlockSpec((None, d, tf), lambda i, f, te, nu: (te[i], 0, fsel(i, f, nu))),
            pl.BlockSpec((None, tf, d), lambda i, f, te, nu: (te[i], fsel(i, f, nu), 0)),
        ],
        out_specs=pl.BlockSpec((tm * ROW_TILE, LANES), lambda i, f, te, nu: (i, 0)),
        scratch_shapes=[pltpu.VMEM((tm, d), BF16), pltpu.VMEM((tm, d), F32)],
    )
    return pl.pallas_call(
        functools.partial(_experts_kernel, tm=tm),
        grid_spec=grid_spec,
        out_shape=jax.ShapeDtypeStruct(xs8.shape, F32),
        compiler_params=_params(("arbitrary", "arbitrary")),
        name="moe_experts",
    )(tile_expert, n_used, xs8, w_gate, w_up, w_down)


def _combine_kernel(pos_ref, h_ref, wts_ref, g_ref, ys_ref, o_ref, y_scr, sem, *, tb, final_norm):
    def start(r, carry):
        for s in range(TOP_K):
            pltpu.make_async_copy(_token_tile(ys_ref, pos_ref[s, r]), _token_tile(y_scr.at[s], r),
                                  sem.at[s]).start(priority=s)
        return carry

    lax.fori_loop(0, tb, start, 0, unroll=DMA_UNROLL)
    for s in range(TOP_K):
        pltpu.make_async_copy(ys_ref.at[pl.ds(0, tb * ROW_TILE)], y_scr.at[s], sem.at[s]).wait()
    w1 = wts_ref[:, 0:1]
    w2 = wts_ref[:, 1:2]
    moe = jnp.concatenate([w1 * _tiles_chunk(y_scr.at[0], c, tb) + w2 * _tiles_chunk(y_scr.at[1], c, tb)
                           for c in range(ROW_CHUNKS)], axis=-1)
    out = h_ref[...] + moe
    if final_norm:
        ms = jnp.mean(out * out, axis=-1, keepdims=True)
        out = out * lax.rsqrt(ms + EPS) * g_ref[...]
    o_ref[...] = out


def _combine(h, pos, wts_t, ys8, final_g, *, tb):
    rows, d = h.shape
    final_norm = final_g is not None
    gain = final_g.reshape(1, d) if final_norm else jnp.ones((1, d), F32)
    return pl.pallas_call(
        functools.partial(_combine_kernel, tb=tb, final_norm=final_norm),
        grid=(rows // tb,),
        in_specs=[
            pl.BlockSpec((TOP_K, tb), lambda i: (0, i), memory_space=pltpu.SMEM),
            pl.BlockSpec((tb, d), lambda i: (i, 0)),
            pl.BlockSpec((tb, TOP_K), lambda i: (i, 0)),
            pl.BlockSpec((1, d), lambda i: (0, 0)),
            pl.BlockSpec(memory_space=pl.ANY),
        ],
        out_specs=pl.BlockSpec((tb, d), lambda i: (i, 0)),
        out_shape=jax.ShapeDtypeStruct((rows, d), F32),
        scratch_shapes=[pltpu.VMEM((TOP_K, tb * ROW_TILE, LANES), F32), pltpu.SemaphoreType.DMA((TOP_K,))],
        compiler_params=_params(("arbitrary",)),
        name="moe_combine",
    )(pos, h, wts_t, gain, ys8)


PROJ_TN = 2560
S5_TT = 128
FFN_TF = 1408
MOE_TM = 512
MOE_TF = 1792
MOE_TB = 1024


def _moe(h_list, u8_list, router, w_gate, w_up, w_down, final_g):
    routes = []
    for u8 in u8_list:
        routes.append(_router(u8, router, tm=min(512, u8.shape[0] // ROW_TILE)))
    counts = [r[3][:, 0].astype(jnp.int32) for r in routes]
    total = sum(counts)
    padded = ((total + MOE_TM - 1) // MOE_TM) * MOE_TM
    ends = jnp.cumsum(padded)
    starts = ends - padded
    n_rows = sum(h.shape[0] for h in h_list) * TOP_K
    n_tiles = n_rows // MOE_TM + N_EXPERTS
    n_used = (ends[-1] // MOE_TM).astype(jnp.int32)
    tile_start = jnp.arange(n_tiles, dtype=jnp.int32) * MOE_TM
    tile_expert = jnp.sum((tile_start[:, None] >= ends[None, :]).astype(jnp.int32), axis=1)
    last_expert = jnp.sum((((n_used - 1) * MOE_TM) >= ends).astype(jnp.int32))
    tile_expert = jnp.where(jnp.arange(n_tiles) < n_used, tile_expert, last_expert).astype(jnp.int32)

    xs8 = jnp.zeros((n_tiles * MOE_TM * ROW_TILE, LANES), F32)
    poss = []
    seen = jnp.zeros((N_EXPERTS,), jnp.int32)
    for u8, (idx, _, rank, _), cnt in zip(u8_list, routes, counts):
        base = starts + seen
        pos = rank
        for e in range(N_EXPERTS):
            pos = pos + jnp.where(idx == e, base[e], 0)
        poss.append(pos)
        seen = seen + cnt
        xs8 = _dispatch(u8, pos, xs8, tb=min(MOE_TB, u8.shape[0] // ROW_TILE))
    ys8 = _experts(xs8, tile_expert, n_used.reshape(1), w_gate, w_up, w_down, tm=MOE_TM, tf=MOE_TF)
    outs = []
    for h, pos, (_, wts, _, _) in zip(h_list, poss, routes):
        outs.append(_combine(h, pos, wts.T, ys8, final_g, tb=min(MOE_TB, h.shape[0])))
    return outs


def _rmsnorm_kernel(x_ref, g_ref, o_ref):
    x = x_ref[...]
    ms = jnp.mean(x * x, axis=-1, keepdims=True)
    o_ref[...] = x * lax.rsqrt(ms + EPS) * g_ref[...]


def _rmsnorm(x, g, *, tm):
    rows, d = x.shape
    return pl.pallas_call(
        _rmsnorm_kernel,
        grid=(rows // tm,),
        in_specs=[pl.BlockSpec((tm, d), lambda i: (i, 0)), pl.BlockSpec((1, d), lambda i: (0, 0))],
        out_specs=pl.BlockSpec((tm, d), lambda i: (i, 0)),
        out_shape=jax.ShapeDtypeStruct((rows, d), F32),
        compiler_params=_params(("parallel",)),
        name="final_norm",
    )(x, g.reshape(1, d))


def _pack_s5_state(re, im):
    n = re.shape[0]
    return jnp.concatenate([re.reshape(n, S5_LANES), im.reshape(n, S5_LANES)], axis=-1)


def _unpack_s5_state(h):
    n = h.shape[0]
    return (h[:, :S5_LANES].reshape(n, S5_GROUPS, S5_STATE), h[:, S5_LANES:].reshape(n, S5_GROUPS, S5_STATE))


def _mixer(x, s5_h0, ret_s0, conv_buf, pos_offset, p, *, single_step, tiled_u, layer=0, ret_stack=None, riders=()):
    bsz, seq, d = x.shape
    z = _norm_proj(x, p["norm_g"], p["w_in"], layer, tt=min(512, seq), tn=PROJ_TN)
    s5_out, s5_state, cast = _s5_branch(z, s5_h0, p["a_re"], p["a_im"], p["bmat"], p["cmat"], p["d_skip"], p["glu_w"],
                                        p["glu_b"], riders, single_step=single_step, tt=1 if single_step else S5_TT)

    if single_step:
        z2 = z.reshape(seq, N_IN)
        ret_out, ret_state = _retention_step(z2, ret_s0, layer, ret_stack, pos_offset, bb=16)
        conv_out, conv_state = _conv_step(z2, conv_buf, p["pw_b"], p["dw_w"], p["dw_b"], p["ln_g"], p["ln_b"], bb=32)
        ret_out = ret_out.reshape(bsz, seq, RET_V)
        conv_out = conv_out.reshape(bsz, seq, CONV_WIDTH)
    else:
        ret_out, ret_state = _retention_prompt(z, ret_s0, pos_offset, tt=256)
        conv_out, conv_state = _conv_prompt(z, conv_buf, p["pw_b"], p["dw_w"], p["dw_b"], p["ln_g"], p["ln_b"],
                                            tt=256)

    h, u = _merge(x, s5_out, ret_out, conv_out, z, p["s5_proj"], p["ret_proj"], p["conv_proj"], p["w_out"],
                  p["norm_ffn_g"], tm=min(512, seq), tiled_u=tiled_u)
    return h, u, (s5_state, ret_state, conv_state), cast


def kernel(x_prompt, x_sample, state_s5_re, state_s5_im, state_ret, state_conv, norm_mix_g, w_in, s5_lambda_re, s5_lambda_im, s5_log_dt, s5_b_re, s5_b_im, s5_c_re, s5_c_im, s5_d, s5_glu_w, s5_glu_b, s5_proj, ret_proj, conv_pw_b, conv_dw_w, conv_dw_b, conv_ln_g, conv_ln_b, conv_proj, w_out, norm_ffn_g, ffn_w_gate, ffn_w_up, ffn_w_down, moe_router, moe_w_gate, moe_w_up, moe_w_down, norm_final_g):
    depth = w_in.shape[0]
    bp, seq, d = x_prompt.shape
    ns = x_sample.shape[0]
    past_len = 16384
    bf = lambda a: a.astype(BF16)

    hp = x_prompt
    hs = x_sample.reshape(1, ns, d)
    zero_s5 = jnp.zeros((bp, 2 * S5_LANES), F32)
    zero_ret = jnp.zeros((bp, RET_HEADS, RET_DK, RET_DV), F32)
    zero_conv = jnp.zeros((bp, CONV_K - 1, CONV_WIDTH), F32)

    p_states, s_states = [], []
    ret_stack = None
    normed = False
    moe_bf16 = None
    groups = depth * S5_GROUPS
    disc = _s5_discretize(s5_lambda_re.reshape(groups, S5_STATE), s5_lambda_im.reshape(groups, S5_STATE),
                          s5_log_dt.reshape(groups), s5_b_re.reshape(groups, S5_STATE, S5_GROUP),
                          s5_b_im.reshape(groups, S5_STATE, S5_GROUP))
    disc = [a.reshape((depth, S5_GROUPS) + a.shape[1:]) for a in disc]
    for l in range(depth):
        is_moe = l % 2 == 1
        a_re, a_im, bb_re, bb_im = (a[l] for a in disc)
        bmat, cmat = _s5_block_mats(bb_re, bb_im, s5_c_re[l], s5_c_im[l])
        p = dict(norm_g=norm_mix_g[l], w_in=w_in,
                 a_re=a_re, a_im=a_im, bmat=bmat, cmat=cmat, d_skip=s5_d[l], glu_w=bf(s5_glu_w[l]),
                 glu_b=s5_glu_b[l], pw_b=conv_pw_b[l], dw_w=conv_dw_w[l], dw_b=conv_dw_b[l], ln_g=conv_ln_g[l],
                 ln_b=conv_ln_b[l], s5_proj=bf(s5_proj[l]), ret_proj=bf(ret_proj[l]), conv_proj=bf(conv_proj[l]),
                 w_out=bf(w_out[l]), norm_ffn_g=norm_ffn_g[l])
        ffn_f32 = [] if is_moe else [ffn_w_gate[l // 2], ffn_w_up[l // 2], ffn_w_down[l // 2]]
        hp, up, st_p, ffn_bf16 = _mixer(hp, zero_s5, zero_ret, zero_conv, 0.0, p, single_step=False, tiled_u=is_moe,
                                        layer=l, riders=ffn_f32)
        hs, us, st_s, _ = _mixer(hs, _pack_s5_state(state_s5_re[l], state_s5_im[l]), state_ret, state_conv[l],
                                 float(past_len), p, single_step=True, tiled_u=is_moe, layer=l, ret_stack=ret_stack)
        ret_stack = st_s[1]
        p_states.append(st_p)
        s_states.append(st_s)

        hp2, hs2 = hp.reshape(bp * seq, d), hs.reshape(ns, d)
        j = l // 2
        if is_moe:
            final_g = norm_final_g if l == depth - 1 else None
            normed = final_g is not None
            if moe_bf16 is None:
                moe_bf16 = [bf(moe_w_gate[j]), bf(moe_w_up[j]), bf(moe_w_down[j])]
            hp2, hs2 = _moe([hp2, hs2], [up, us], moe_router[j], *moe_bf16, final_g)
            moe_bf16 = None
        else:
            wg, wu, wd = ffn_bf16
            riders = []
            if l + 1 < depth:
                jm = (l + 1) // 2
                shapes = [w.shape[1:] for w in (moe_w_gate, moe_w_up, moe_w_down)]
                riders = [w[jm].reshape(-1, w.shape[-1]) for w in (moe_w_gate, moe_w_up, moe_w_down)]
            hp2, cast = _ffn(hp2, up.reshape(bp * seq, d), wg, wu, wd, riders, tm=512, tf=FFN_TF)
            if riders:
                moe_bf16 = [c.reshape(s) for c, s in zip(cast, shapes)]
            hs2, _ = _ffn(hs2, us.reshape(ns, d), wg, wu, wd, tm=ns, tf=FFN_TF)
        hp = hp2.reshape(bp, seq, d)
        hs = hs2.reshape(1, ns, d)

    if normed:
        y_prompt, y_sample = hp, hs.reshape(ns, 1, d)
    else:
        y_prompt = _rmsnorm(hp.reshape(bp * seq, d), norm_final_g, tm=512).reshape(bp, seq, d)
        y_sample = _rmsnorm(hs.reshape(ns, d), norm_final_g, tm=ns).reshape(ns, 1, d)

    def stack_s5(states):
        s5 = [_unpack_s5_state(s[0]) for s in states]
        return jnp.stack([a for a, _ in s5]), jnp.stack([b for _, b in s5])

    p_re, p_im = stack_s5(p_states)
    s_re, s_im = stack_s5(s_states)
    p_ret = jnp.stack([s[1] for s in p_states])
    p_conv = jnp.stack([s[2] for s in p_states])
    s_conv = jnp.stack([s[2] for s in s_states])
    return (y_prompt, y_sample, p_re, p_im, p_ret, p_conv, s_re, s_im, ret_stack, s_conv)
```

```python
import functools
import math

import jax
import jax.numpy as jnp
from jax import lax
from jax.experimental import pallas as pl
from jax.experimental.pallas import tpu as pltpu

F32 = jnp.float32
BF16 = jnp.bfloat16

D_MODEL = 1024
S5_WIDTH = 512
S5_GROUP = 16
S5_GROUPS = 32
S5_STATE = 64
S5_LANES = S5_GROUPS * S5_STATE
RET_HEADS = 4
RET_DK = 128
RET_DV = 256
RET_QK = RET_HEADS * RET_DK
RET_V = RET_HEADS * RET_DV
RET_CHUNK = 128
ROPE_BASE = 10000.0
CONV_WIDTH = 512
CONV_K = 31
CONV_HIST = 32
N_EXPERTS = 8
TOP_K = 2
N_BRANCH = 3
EPS = 1e-6
N_IN = S5_WIDTH + 2 * RET_QK + 2 * RET_V + 2 * CONV_WIDTH + N_BRANCH * D_MODEL
ZB = 512
ZC_S5 = 0
ZC_Q = ZC_S5 + S5_WIDTH // ZB
ZC_K = ZC_Q + RET_QK // ZB
ZC_V = ZC_K + RET_QK // ZB
ZC_G = ZC_V + RET_V // ZB
ZC_CONV = ZC_G + RET_V // ZB
ZC_GATE = ZC_CONV + 2 * CONV_WIDTH // ZB
HEADS_PER_ZB = ZB // RET_DV

ROW_TILE = 8
LANES = 128
ROW_CHUNKS = D_MODEL // LANES
VMEM_LIMIT = 48 * 1024 * 1024
ROW_SPLIT = 2


def _params(sem):
    return pltpu.CompilerParams(dimension_semantics=sem, vmem_limit_bytes=VMEM_LIMIT)


def _silu(x):
    return x * jax.nn.sigmoid(x)


def _bdot(a, b):
    return jnp.dot(a.astype(BF16), b.astype(BF16), preferred_element_type=F32)


def _norm_proj_kernel(x_ref, g_ref, w_ref, *rest, riders):
    rider_in, o_ref = rest[:riders.n_in], rest[riders.n_in]
    rider_out, w_scr = rest[riders.n_in + 1:-1], rest[-1]
    riders.run(rider_in, rider_out)

    @pl.when((pl.program_id(1) == 0) & (pl.program_id(2) == 0))
    def _():
        w_scr[...] = w_ref[...].astype(BF16)

    half = x_ref.shape[0] // ROW_SPLIT
    for r in range(ROW_SPLIT):
        rows = slice(r * half, (r + 1) * half)
        x = x_ref[rows, :]
        ms = jnp.mean(x * x, axis=-1, keepdims=True)
        u = (x * lax.rsqrt(ms + EPS) * g_ref[...]).astype(BF16)
        o_ref[rows, :] = jnp.dot(u, w_scr[...], preferred_element_type=F32).astype(o_ref.dtype)


def _norm_proj(x, g, w_all, layer, jobs=(), *, tt, tn):
    bsz, seq, d = x.shape
    n = w_all.shape[2]
    grid = (n // tn, bsz, seq // tt)
    riders = _Riders(jobs, grid, lambda j, b, i: (j * grid[1] + b) * grid[2] + i)
    outs = pl.pallas_call(
        functools.partial(_norm_proj_kernel, riders=riders),
        grid=grid,
        in_specs=[
            pl.BlockSpec((None, tt, d), lambda j, b, i: (b, i, 0)),
            pl.BlockSpec((1, d), lambda j, b, i: (0, 0)),
            pl.BlockSpec((None, d, tn), lambda j, b, i: (layer, 0, j)),
            *riders.in_specs,
        ],
        out_specs=[pl.BlockSpec((None, tt, tn), lambda j, b, i: (b, i, j)), *riders.out_specs],
        out_shape=[jax.ShapeDtypeStruct((bsz, seq, n), BF16), *riders.out_shapes],
        scratch_shapes=[pltpu.VMEM((d, tn), BF16)],
        compiler_params=_params(("arbitrary", "arbitrary", "arbitrary")),
        name="norm_proj",
    )(x, g.reshape(1, d), w_all, *riders.inputs)
    return outs[0], list(outs[1:])


def _s5_disc_kernel(lre_ref, lim_ref, ldt_ref, bre_ref, bim_ref, are_ref, aim_ref, ore_ref, oim_ref):
    lam_re = lre_ref[...]
    lam_im = lim_ref[...]
    dt = jnp.exp(ldt_ref[...])
    mag = jnp.exp(lam_re * dt)
    ang = lam_im * dt
    lbar_re = mag * jnp.cos(ang)
    lbar_im = mag * jnp.sin(ang)
    den = lam_re * lam_re + lam_im * lam_im
    nr = lbar_re - 1.0
    f_re = (nr * lam_re + lbar_im * lam_im) / den
    f_im = (lbar_im * lam_re - nr * lam_im) / den
    b_re = bre_ref[...]
    b_im = bim_ref[...]
    are_ref[...] = lbar_re
    aim_ref[...] = lbar_im
    ore_ref[...] = f_re * b_re - f_im * b_im
    oim_ref[...] = f_re * b_im + f_im * b_re


def _s5_discretize(lam_re, lam_im, log_dt, b_re, b_im):
    g, n = lam_re.shape
    p = b_re.shape[-1]
    rows = g * n
    col = lambda a: a.reshape(rows, 1)
    ldt = jnp.broadcast_to(log_dt[:, None], (g, n))
    outs = pl.pallas_call(
        _s5_disc_kernel,
        out_shape=[jax.ShapeDtypeStruct((rows, 1), F32), jax.ShapeDtypeStruct((rows, 1), F32),
                   jax.ShapeDtypeStruct((rows, p), F32), jax.ShapeDtypeStruct((rows, p), F32)],
        name="s5_discretize",
    )(col(lam_re), col(lam_im), col(ldt), b_re.reshape(rows, p), b_im.reshape(rows, p))
    a_re, a_im, bb_re, bb_im = outs
    return a_re.reshape(g, n), a_im.reshape(g, n), bb_re.reshape(g, n, p), bb_im.reshape(g, n, p)


S5_KCH = 128
S5_NCHUNK = S5_WIDTH // S5_KCH
S5_GPC = S5_KCH // S5_GROUP
S5_SPC = S5_GPC * S5_STATE


def _s5_block_mats(bbar_re, bbar_im, c_re, c_im):
    eye = jnp.eye(S5_GPC, dtype=F32)

    def in_blocks(bb):
        t = bb.reshape(S5_NCHUNK, S5_GPC, S5_STATE, S5_GROUP)
        m = jnp.einsum("cgnp,gh->cgphn", t, eye)
        return m.reshape(S5_NCHUNK, S5_KCH, S5_SPC)

    def out_blocks(cc):
        t = cc.reshape(S5_NCHUNK, S5_GPC, S5_GROUP, S5_STATE)
        m = jnp.einsum("cgpn,gh->cgnhp", t, eye)
        return m.reshape(S5_NCHUNK, S5_SPC, S5_KCH)

    bmat = jnp.concatenate([in_blocks(bbar_re), in_blocks(bbar_im)], axis=-1).astype(BF16)
    cmat = jnp.stack([out_blocks(c_re), -out_blocks(c_im)], axis=1).astype(BF16)
    return bmat, cmat


def _gelu_tanh(x):
    return 0.5 * x * (1.0 + jnp.tanh(math.sqrt(2.0 / math.pi) * (x + 0.044715 * (x * x * x))))


class _Riders:
    def __init__(self, jobs, grid, step_index):
        self.steps = math.prod(grid)
        self.n_axes = len(grid)
        self.step_index = step_index
        self.inputs, self.in_specs, self.out_specs, self.out_shapes, self.src_steps = [], [], [], [], []
        for src, shape, dtype in jobs:
            slab = shape[0] // self.steps
            assert slab * self.steps == shape[0] and slab % 16 == 0
            n_src = 0 if src is None else src.shape[0] // slab
            if src is not None:
                assert n_src * slab == src.shape[0] and src.shape[1] == shape[1]
                self.inputs.append(src)
                self.in_specs.append(pl.BlockSpec(
                    (slab, shape[1]), lambda *idx, n=n_src: (jnp.minimum(step_index(*idx[:self.n_axes]), n - 1), 0)))
            self.out_specs.append(pl.BlockSpec((slab, shape[1]), lambda *idx: (step_index(*idx[:self.n_axes]), 0)))
            self.out_shapes.append(jax.ShapeDtypeStruct(shape, dtype))
            self.src_steps.append(n_src)

    @property
    def n_in(self):
        return len(self.inputs)

    @property
    def n_out(self):
        return len(self.out_shapes)

    def run(self, in_refs, out_refs):
        step = self.step_index(*(pl.program_id(a) for a in range(self.n_axes)))
        srcs = iter(in_refs)
        for dst, n_src in zip(out_refs, self.src_steps):
            if n_src == 0:
                dst[...] = jnp.zeros_like(dst)
                continue
            src = next(srcs)
            if n_src == self.steps:
                dst[...] = src[...].astype(dst.dtype)
                continue

            @pl.when(step < n_src)
            def _(src=src, dst=dst):
                dst[...] = src[...].astype(dst.dtype)

            @pl.when(step >= n_src)
            def _(dst=dst):
                dst[...] = jnp.zeros_like(dst)


def _cast_job(w):
    return (w, w.shape, BF16)


def _s5_kernel(u_ref, h0_ref, are_ref, aim_ref, bmat_ref, cmat_ref, d_ref, gw_ref, gb_ref, *rest,
               nb, tt, lane_chunk, riders):
    rider_in, rest = rest[:riders.n_in], rest[riders.n_in:]
    o_ref, hout_ref = rest[:2]
    rider_out = rest[2:2 + riders.n_out]
    hs_scr, h_scr, io_scr = rest[2 + riders.n_out:]
    riders.run(rider_in, rider_out)

    @pl.when(pl.program_id(0) == 0)
    def _():
        h_scr[...] = h0_ref[...]

    def seq_rows(b):
        return pl.ds(b, tt, stride=nb)

    def lanes(c):
        return slice(c * S5_KCH, (c + 1) * S5_KCH)

    if tt == 1:
        for c in range(S5_NCHUNK):
            io_scr[c] = u_ref[:, lanes(c)].astype(F32)
    else:
        for b in range(nb):
            for c in range(S5_NCHUNK):
                io_scr[c, seq_rows(b), :] = u_ref[b, :, lanes(c)].astype(F32)

    u = jnp.concatenate([io_scr[c] for c in range(S5_NCHUNK)], axis=-1)
    for c in range(S5_NCHUNK):
        bu = jnp.dot(io_scr[c].astype(BF16), bmat_ref[c], preferred_element_type=F32)
        hs_scr[:, c * S5_SPC:(c + 1) * S5_SPC] = bu[:, :S5_SPC]
        hs_scr[:, S5_LANES + c * S5_SPC:S5_LANES + (c + 1) * S5_SPC] = bu[:, S5_SPC:]

    for lc in range(S5_LANES // lane_chunk):
        re_sl = slice(lc * lane_chunk, (lc + 1) * lane_chunk)
        im_sl = slice(S5_LANES + lc * lane_chunk, S5_LANES + (lc + 1) * lane_chunk)
        a_re = jnp.broadcast_to(are_ref[:, re_sl], (nb, lane_chunk))
        a_im = jnp.broadcast_to(aim_ref[:, re_sl], (nb, lane_chunk))

        def step(t, carry):
            h_re, h_im = carry
            r0 = t * nb if isinstance(t, int) else pl.multiple_of(t * nb, nb)
            n_re = a_re * h_re - a_im * h_im + hs_scr[pl.ds(r0, nb), re_sl]
            n_im = a_re * h_im + a_im * h_re + hs_scr[pl.ds(r0, nb), im_sl]
            hs_scr[pl.ds(r0, nb), re_sl] = n_re
            hs_scr[pl.ds(r0, nb), im_sl] = n_im
            return n_re, n_im

        carry = (h_scr[:, re_sl], h_scr[:, im_sl])
        if tt == 1:
            carry = step(0, carry)
        else:
            carry = lax.fori_loop(0, tt, step, carry, unroll=4)
        h_scr[:, re_sl] = carry[0]
        h_scr[:, im_sl] = carry[1]

    hout_ref[...] = h_scr[...]

    ys = []
    for c in range(S5_NCHUNK):
        h_re = hs_scr[:, c * S5_SPC:(c + 1) * S5_SPC].astype(BF16)
        h_im = hs_scr[:, S5_LANES + c * S5_SPC:S5_LANES + (c + 1) * S5_SPC].astype(BF16)
        ys.append(jnp.dot(h_re, cmat_ref[c, 0], preferred_element_type=F32)
                  + jnp.dot(h_im, cmat_ref[c, 1], preferred_element_type=F32))
    y = jnp.concatenate(ys, axis=-1) + d_ref[...] * u
    z = _gelu_tanh(y)
    gate = jnp.dot(z.astype(BF16), gw_ref[...], preferred_element_type=F32) + gb_ref[...]
    out = z * jax.nn.sigmoid(gate)
    if tt == 1:
        o_ref[...] = out.astype(o_ref.dtype)
    else:
        for c in range(S5_NCHUNK):
            io_scr[c] = out[:, lanes(c)]
        for b in range(nb):
            for c in range(S5_NCHUNK):
                o_ref[b, :, lanes(c)] = io_scr[c, seq_rows(b), :].astype(o_ref.dtype)


def _s5_branch(z, h0, a_re, a_im, bmat, cmat, d_skip, glu_w, glu_b, jobs=(), *, single_step, tt):
    bsz, seq, _ = z.shape
    if single_step:
        assert bsz == 1 and tt == 1
        nb = seq
        in_spec = pl.BlockSpec((None, nb, S5_WIDTH), lambda i: (0, 0, ZC_S5))
        out_spec = pl.BlockSpec((None, nb, S5_WIDTH), lambda i: (0, 0, 0))
        grid = (1,)
    else:
        nb = bsz
        in_spec = pl.BlockSpec((nb, tt, S5_WIDTH), lambda i: (0, i, ZC_S5))
        out_spec = pl.BlockSpec((nb, tt, S5_WIDTH), lambda i: (0, i, 0))
        grid = (seq // tt,)
    rblk = tt * nb
    lane_chunk = 1024 if nb <= 8 else 512
    const = lambda shape: pl.BlockSpec(shape, lambda i: (0,) * len(shape))
    riders = _Riders(jobs, grid, lambda i: i)
    outs = pl.pallas_call(
        functools.partial(_s5_kernel, nb=nb, tt=tt, lane_chunk=lane_chunk, riders=riders),
        grid=grid,
        in_specs=[
            in_spec,
            const((nb, 2 * S5_LANES)),
            const((1, S5_LANES)),
            const((1, S5_LANES)),
            const(bmat.shape),
            const(cmat.shape),
            const((1, S5_WIDTH)),
            const((S5_WIDTH, S5_WIDTH)),
            const((1, S5_WIDTH)),
            *riders.in_specs,
        ],
        out_specs=[out_spec, const((nb, 2 * S5_LANES)), *riders.out_specs],
        out_shape=[jax.ShapeDtypeStruct((bsz, seq, S5_WIDTH), BF16),
                   jax.ShapeDtypeStruct((nb, 2 * S5_LANES), F32), *riders.out_shapes],
        scratch_shapes=[pltpu.VMEM((rblk, 2 * S5_LANES), F32), pltpu.VMEM((nb, 2 * S5_LANES), F32),
                        pltpu.VMEM((S5_NCHUNK, rblk, S5_KCH), F32)],
        compiler_params=_params(("arbitrary",)),
        name="s5_branch",
    )(z, h0, a_re.reshape(1, S5_LANES), a_im.reshape(1, S5_LANES), bmat, cmat,
      d_skip.reshape(1, S5_WIDTH), glu_w, glu_b.reshape(1, S5_WIDTH), *riders.inputs)
    return outs[0], outs[1], list(outs[2:])


def _rope_tables(pos):
    half = RET_DK // 2
    freqs = ROPE_BASE ** (-jnp.arange(half, dtype=F32) / half)
    ang = pos[:, None] * freqs[None, :]
    cos = jnp.cos(ang)
    sin = jnp.sin(ang)
    return jnp.concatenate([cos, cos], axis=-1), jnp.concatenate([-sin, sin], axis=-1)


def _rope(x, cos, sin):
    return x * cos + pltpu.roll(x, RET_DK // 2, 1) * sin


def _group_norm(o):
    mu = jnp.mean(o, axis=-1, keepdims=True)
    d = o - mu
    var = jnp.mean(d * d, axis=-1, keepdims=True)
    return d * lax.rsqrt(var + EPS)


def _retention_tables(chunk):
    log_gamma = jnp.log(1.0 - 2.0 ** (-5.0 - jnp.arange(RET_HEADS, dtype=F32)))
    idx = jnp.arange(chunk, dtype=F32)
    diff = idx[:, None] - idx[None, :]
    decay = jnp.where(diff >= 0, jnp.exp(jnp.maximum(diff, 0.0)[None] * log_gamma[:, None, None]), 0.0)
    cross = jnp.exp((idx + 1.0)[None, :] * log_gamma[:, None])[:, :, None]
    kdec = jnp.exp((chunk - 1.0 - idx)[None, :] * log_gamma[:, None])[:, :, None]
    full = jnp.exp(chunk * log_gamma)
    return decay, cross, kdec, full


def _head_cols(refs, h, rows):
    lo = (h % HEADS_PER_ZB) * RET_DV
    return refs[h // HEADS_PER_ZB][rows, lo:lo + RET_DV].astype(F32)


def _retention_kernel(q_ref, k_ref, v0_ref, v1_ref, g0_ref, g1_ref, cos_ref, sin_ref, s0_ref, decay_ref, cross_ref,
                      kdec_ref, full_ref, *rest, n_chunks, riders):
    rider_in, rest = rest[:riders.n_in], rest[riders.n_in:]
    o_ref, sout_ref = rest[:2]
    rider_out, s_scr = rest[2:-1], rest[-1]
    riders.run(rider_in, rider_out)

    @pl.when(pl.program_id(1) == 0)
    def _():
        s_scr[...] = s0_ref[...]

    for c in range(n_chunks):
        rows = slice(c * RET_CHUNK, (c + 1) * RET_CHUNK)
        cos = cos_ref[rows, :]
        sin = sin_ref[rows, :]
        for h in range(RET_HEADS):
            qk_cols = slice(h * RET_DK, (h + 1) * RET_DK)
            v_cols = slice(h * RET_DV, (h + 1) * RET_DV)
            qh = _rope(q_ref[rows, qk_cols].astype(F32), cos, sin)
            kh = _rope(k_ref[rows, qk_cols].astype(F32), cos, sin) * (RET_DK ** -0.5)
            vb = _head_cols((v0_ref, v1_ref), h, rows).astype(BF16)
            qb = qh.astype(BF16)
            state = s_scr[h]
            inner = lax.dot_general(qb, kh.astype(BF16), (((1,), (1,)), ((), ())),
                                    preferred_element_type=F32) * decay_ref[h]
            out = (jnp.dot(inner.astype(BF16), vb, preferred_element_type=F32)
                   + jnp.dot(qb, state.astype(BF16), preferred_element_type=F32) * cross_ref[h])
            kd = (kh * kdec_ref[h]).astype(BF16)
            s_scr[h] = full_ref[h] * state + jnp.dot(kd.T, vb, preferred_element_type=F32)
            gate = _head_cols((g0_ref, g1_ref), h, rows)
            o_ref[rows, v_cols] = (_silu(gate) * _group_norm(out)).astype(o_ref.dtype)

    sout_ref[...] = s_scr[...]


def _retention_prompt(z, state0, pos_offset, jobs=(), *, tt):
    bsz, seq, _ = z.shape
    assert seq % RET_CHUNK == 0 and tt % RET_CHUNK == 0
    cos, sin = _rope_tables(jnp.arange(seq, dtype=F32) + pos_offset)
    decay, cross, kdec, full = _retention_tables(RET_CHUNK)
    full = jnp.broadcast_to(full[:, None, None], (RET_HEADS, 1, RET_DV))
    const = lambda shape: pl.BlockSpec(shape, lambda b, i: (0,) * len(shape))
    zblk = lambda col: pl.BlockSpec((None, tt, ZB), lambda b, i: (b, i, col))
    n_i = seq // tt
    riders = _Riders(jobs, (bsz, n_i), lambda b, i: b * n_i + i)
    outs = pl.pallas_call(
        functools.partial(_retention_kernel, n_chunks=tt // RET_CHUNK, riders=riders),
        grid=(bsz, n_i),
        in_specs=[
            zblk(ZC_Q), zblk(ZC_K), zblk(ZC_V), zblk(ZC_V + 1), zblk(ZC_G), zblk(ZC_G + 1),
            pl.BlockSpec((tt, RET_DK), lambda b, i: (i, 0)),
            pl.BlockSpec((tt, RET_DK), lambda b, i: (i, 0)),
            pl.BlockSpec((None, RET_HEADS, RET_DK, RET_DV), lambda b, i: (b, 0, 0, 0)),
            const((RET_HEADS, RET_CHUNK, RET_CHUNK)),
            const((RET_HEADS, RET_CHUNK, 1)),
            const((RET_HEADS, RET_CHUNK, 1)),
            const((RET_HEADS, 1, RET_DV)),
            *riders.in_specs,
        ],
        out_specs=[
            pl.BlockSpec((None, tt, RET_V), lambda b, i: (b, i, 0)),
            pl.BlockSpec((None, RET_HEADS, RET_DK, RET_DV), lambda b, i: (b, 0, 0, 0)),
            *riders.out_specs,
        ],
        out_shape=[jax.ShapeDtypeStruct((bsz, seq, RET_V), BF16),
                   jax.ShapeDtypeStruct((bsz, RET_HEADS, RET_DK, RET_DV), F32), *riders.out_shapes],
        scratch_shapes=[pltpu.VMEM((RET_HEADS, RET_DK, RET_DV), F32)],
        compiler_params=_params(("arbitrary", "arbitrary")),
        name="retention_prompt",
    )(z, z, z, z, z, z, cos, sin, state0, decay, cross, kdec, full, *riders.inputs)
    return outs[0], outs[1], list(outs[2:])


def _retention_step_kernel(q_ref, k_ref, v0_ref, v1_ref, g0_ref, g1_ref, cos_ref, sin_ref, s_ref, gam_ref, *rest,
                           bb, layer):
    del layer
    o_ref, sout_ref, o_scr = rest[-3:]
    cos = cos_ref[...]
    sin = sin_ref[...]
    for h in range(RET_HEADS):
        qk_cols = slice(h * RET_DK, (h + 1) * RET_DK)
        v_cols = slice(h * RET_DV, (h + 1) * RET_DV)
        qh = _rope(q_ref[:, qk_cols].astype(F32), cos, sin)
        kh = _rope(k_ref[:, qk_cols].astype(F32), cos, sin) * (RET_DK ** -0.5)
        qk = jnp.sum(qh * kh, axis=-1, keepdims=True)
        q_t = qh.T
        k_t = kh.T
        gamma = gam_ref[h]
        v_all = _head_cols((v0_ref, v1_ref), h, slice(None))
        for b in range(bb):
            state = s_ref[b, h]
            vrow = v_all[b:b + 1, :]
            qs = jnp.sum(q_t[:, b:b + 1] * state, axis=0, keepdims=True)
            o_scr[b:b + 1, v_cols] = qk[b:b + 1, :] * vrow + qs * gamma
            sout_ref[b, h] = gamma * state + k_t[:, b:b + 1] * vrow
    for h in range(RET_HEADS):
        v_cols = slice(h * RET_DV, (h + 1) * RET_DV)
        gate = _head_cols((g0_ref, g1_ref), h, slice(None))
        o_ref[:, v_cols] = (_silu(gate) * _group_norm(o_scr[:, v_cols])).astype(o_ref.dtype)


def _retention_step(z, states, layer, stack, pos, *, bb):
    n = z.shape[0]
    cos, sin = _rope_tables(jnp.full((1,), pos, F32))
    log_gamma = jnp.log(1.0 - 2.0 ** (-5.0 - jnp.arange(RET_HEADS, dtype=F32)))
    gam = jnp.broadcast_to(jnp.exp(log_gamma)[:, None, None], (RET_HEADS, 1, RET_DV))
    const = lambda shape: pl.BlockSpec(shape, lambda i: (0,) * len(shape))
    slab = pl.BlockSpec((None, bb, RET_HEADS, RET_DK, RET_DV), lambda i: (layer, i, 0, 0, 0))
    zblk = lambda col: pl.BlockSpec((bb, ZB), lambda i: (i, col))
    in_specs = [
        zblk(ZC_Q), zblk(ZC_K), zblk(ZC_V), zblk(ZC_V + 1), zblk(ZC_G), zblk(ZC_G + 1),
        const((1, RET_DK)),
        const((1, RET_DK)),
        slab,
        const((RET_HEADS, 1, RET_DV)),
    ]
    args = [z, z, z, z, z, z, cos, sin, states, gam]
    if stack is None:
        state_spec = pl.BlockSpec((bb, RET_HEADS, RET_DK, RET_DV), lambda i: (i, 0, 0, 0))
        state_shape = jax.ShapeDtypeStruct(states.shape[1:], F32)
        aliases = {}
    else:
        in_specs.append(pl.BlockSpec(memory_space=pl.ANY))
        args.append(stack)
        state_spec = slab
        state_shape = jax.ShapeDtypeStruct(stack.shape, F32)
        aliases = {len(args) - 1: 1}
    return pl.pallas_call(
        functools.partial(_retention_step_kernel, bb=bb, layer=layer),
        grid=(n // bb,),
        in_specs=in_specs,
        out_specs=[pl.BlockSpec((bb, RET_V), lambda i: (i, 0)), state_spec],
        out_shape=[jax.ShapeDtypeStruct((n, RET_V), BF16), state_shape],
        scratch_shapes=[pltpu.VMEM((bb, RET_V), F32)],
        input_output_aliases=aliases,
        compiler_params=_params(("parallel",)),
        name="retention_step",
    )(*args)


def _layer_norm(y, g, b):
    mu = jnp.mean(y, axis=-1, keepdims=True)
    d = y - mu
    var = jnp.mean(d * d, axis=-1, keepdims=True)
    return d * lax.rsqrt(var + EPS) * g + b


CONV_RB = 128
CONV_PITCH = 2
CONV_LCH = CONV_WIDTH // LANES


def _conv_glu(a_ref, b_ref, pwb_ref):
    a = a_ref[...].astype(F32) + pwb_ref[:, :CONV_WIDTH]
    b = b_ref[...].astype(F32) + pwb_ref[:, CONV_WIDTH:]
    return a * jax.nn.sigmoid(b)


def _conv_kernel(a_ref, b_ref, buf_ref, pwb_ref, dww_ref, dwb_ref, lng_ref, lnb_ref, *rest, tt, riders):
    rider_in, rest = rest[:riders.n_in], rest[riders.n_in:]
    o_ref, hist_ref = rest[:2]
    rider_out, (x_scr, y_scr) = rest[2:-2], rest[-2:]
    riders.run(rider_in, rider_out)
    _conv_body(a_ref, b_ref, buf_ref, pwb_ref, dww_ref, dwb_ref, lng_ref, lnb_ref, o_ref, hist_ref, x_scr, y_scr, tt=tt)


def _conv_body(a_ref, b_ref, buf_ref, pwb_ref, dww_ref, dwb_ref, lng_ref, lnb_ref, o_ref, hist_ref, x_scr, y_scr,
               *, tt):
    i = pl.program_id(1)

    def rows(start, n):
        return pl.ds(CONV_PITCH * start, n, stride=CONV_PITCH)

    def lanes(c):
        return slice(c * LANES, (c + 1) * LANES)

    @pl.when(i == 0)
    def _():
        for c in range(CONV_LCH):
            x_scr[c, rows(0, CONV_HIST), :] = buf_ref[:, lanes(c)]

    @pl.when(i > 0)
    def _():
        for c in range(CONV_LCH):
            x_scr[c, rows(0, CONV_HIST), :] = x_scr[c, rows(tt, CONV_HIST), :]

    glu = _conv_glu(a_ref, b_ref, pwb_ref)
    for c in range(CONV_LCH):
        x_scr[c, rows(CONV_HIST, tt), :] = glu[:, lanes(c)]
    hist_ref[...] = glu[tt - CONV_HIST:, :]

    off = CONV_HIST - (CONV_K - 1)
    for c in range(CONV_LCH):
        def row_block(r, carry, c=c):
            base = r * CONV_RB
            n_grp = CONV_RB // ROW_TILE
            accs = [jnp.broadcast_to(dwb_ref[:, lanes(c)], (ROW_TILE, LANES))] * n_grp
            for m in range(CONV_RB - ROW_TILE + CONV_K):
                win = x_scr[c, rows(base + (off + m), ROW_TILE), :]
                for k in range(m % ROW_TILE, CONV_K, ROW_TILE):
                    j = (m - k) // ROW_TILE
                    if 0 <= j < n_grp:
                        accs[j] = accs[j] + dww_ref[k:k + 1, lanes(c)] * win
            y_scr[pl.ds(pl.multiple_of(base, CONV_RB), CONV_RB), lanes(c)] = jnp.concatenate(accs, axis=0)
            return carry

        lax.fori_loop(0, tt // CONV_RB, row_block, 0)

    o_ref[...] = _silu(_layer_norm(y_scr[...], lng_ref[...], lnb_ref[...])).astype(o_ref.dtype)


def _conv_prompt(z, buf, pw_b, dw_w, dw_b, ln_g, ln_b, jobs=(), *, tt):
    bsz, seq, _ = z.shape
    assert seq >= CONV_HIST and tt >= CONV_HIST
    buf32 = jnp.pad(buf, ((0, 0), (CONV_HIST - (CONV_K - 1), 0), (0, 0)))
    const = lambda shape: pl.BlockSpec(shape, lambda b, i: (0,) * len(shape))
    zblk = lambda col: pl.BlockSpec((None, tt, ZB), lambda b, i: (b, i, col))
    n_i = seq // tt
    riders = _Riders(jobs, (bsz, n_i), lambda b, i: b * n_i + i)
    out, hist, *cast = pl.pallas_call(
        functools.partial(_conv_kernel, tt=tt, riders=riders),
        grid=(bsz, n_i),
        in_specs=[
            zblk(ZC_CONV), zblk(ZC_CONV + 1),
            pl.BlockSpec((None, CONV_HIST, CONV_WIDTH), lambda b, i: (b, 0, 0)),
            const((1, 2 * CONV_WIDTH)),
            const((CONV_K, CONV_WIDTH)),
            const((1, CONV_WIDTH)),
            const((1, CONV_WIDTH)),
            const((1, CONV_WIDTH)),
            *riders.in_specs,
        ],
        out_specs=[
            pl.BlockSpec((None, tt, CONV_WIDTH), lambda b, i: (b, i, 0)),
            pl.BlockSpec((None, CONV_HIST, CONV_WIDTH), lambda b, i: (b, 0, 0)),
            *riders.out_specs,
        ],
        out_shape=[jax.ShapeDtypeStruct((bsz, seq, CONV_WIDTH), BF16),
                   jax.ShapeDtypeStruct((bsz, CONV_HIST, CONV_WIDTH), F32), *riders.out_shapes],
        scratch_shapes=[pltpu.VMEM((CONV_LCH, CONV_PITCH * (CONV_HIST + tt), LANES), F32),
                        pltpu.VMEM((tt, CONV_WIDTH), F32)],
        compiler_params=_params(("arbitrary", "arbitrary")),
        name="conv_prompt",
    )(z, z, buf32, pw_b.reshape(1, -1), dw_w, dw_b.reshape(1, -1), ln_g.reshape(1, -1), ln_b.reshape(1, -1),
      *riders.inputs)
    return out, hist[:, CONV_HIST - (CONV_K - 1):, :], cast


def _conv_step_kernel(a_ref, b_ref, buf_ref, pwb_ref, dww_ref, dwb_ref, lng_ref, lnb_ref, o_ref, hist_ref):
    hist_len = CONV_K - 1
    glu = _conv_glu(a_ref, b_ref, pwb_ref)
    acc = dwb_ref[...] + dww_ref[hist_len:hist_len + 1, :] * glu
    for k in range(hist_len):
        acc = acc + dww_ref[k:k + 1, :] * buf_ref[:, k * CONV_WIDTH:(k + 1) * CONV_WIDTH]
    o_ref[...] = _silu(_layer_norm(acc, lng_ref[...], lnb_ref[...])).astype(o_ref.dtype)
    hist_ref[:, :(hist_len - 1) * CONV_WIDTH] = buf_ref[:, CONV_WIDTH:]
    hist_ref[:, (hist_len - 1) * CONV_WIDTH:] = glu


def _conv_step(z, buf, pw_b, dw_w, dw_b, ln_g, ln_b, *, bb):
    n = z.shape[0]
    hist_len = CONV_K - 1
    const = lambda shape: pl.BlockSpec(shape, lambda i: (0,) * len(shape))
    zblk = lambda col: pl.BlockSpec((bb, ZB), lambda i: (i, col))
    out, hist = pl.pallas_call(
        _conv_step_kernel,
        grid=(n // bb,),
        in_specs=[
            zblk(ZC_CONV), zblk(ZC_CONV + 1),
            pl.BlockSpec((bb, hist_len * CONV_WIDTH), lambda i: (i, 0)),
            const((1, 2 * CONV_WIDTH)),
            const((CONV_K, CONV_WIDTH)),
            const((1, CONV_WIDTH)),
            const((1, CONV_WIDTH)),
            const((1, CONV_WIDTH)),
        ],
        out_specs=[
            pl.BlockSpec((bb, CONV_WIDTH), lambda i: (i, 0)),
            pl.BlockSpec((bb, hist_len * CONV_WIDTH), lambda i: (i, 0)),
        ],
        out_shape=[jax.ShapeDtypeStruct((n, CONV_WIDTH), BF16),
                   jax.ShapeDtypeStruct((n, hist_len * CONV_WIDTH), F32)],
        compiler_params=_params(("parallel",)),
        name="conv_step",
    )(z, z, buf.reshape(n, hist_len * CONV_WIDTH), pw_b.reshape(1, -1), dw_w, dw_b.reshape(1, -1),
      ln_g.reshape(1, -1), ln_b.reshape(1, -1))
    return out, hist.reshape(n, hist_len, CONV_WIDTH)


def _rows_to_tiles(tile_ref, x, rows):
    for c in range(ROW_CHUNKS):
        tile_ref[pl.ds(c, rows, stride=ROW_TILE), :] = x[:, c * LANES:(c + 1) * LANES]


def _tiles_chunk(tile_ref, c, rows):
    return tile_ref[pl.ds(c, rows, stride=ROW_TILE), :]


def _tiles_to_rows(tile_ref, rows):
    return jnp.concatenate([_tiles_chunk(tile_ref, c, rows) for c in range(ROW_CHUNKS)], axis=-1)


def _merge_kernel(x_ref, s5_ref, ret_ref, conv_ref, *rest, tiled_u):
    gate_refs = rest[:N_BRANCH * D_MODEL // ZB]
    ps5_ref, pret_ref, pconv_ref, wout_ref, g_ref, h_ref, u_ref = rest[len(gate_refs):]
    per_branch = D_MODEL // ZB

    def gate(n):
        cols = [gate_refs[n * per_branch + j][...] for j in range(per_branch)]
        return jax.nn.sigmoid(jnp.concatenate(cols, axis=-1).astype(F32))

    merged = (gate(0) * jnp.dot(s5_ref[...], ps5_ref[...], preferred_element_type=F32)
              + gate(1) * jnp.dot(ret_ref[...], pret_ref[...], preferred_element_type=F32)
              + gate(2) * jnp.dot(conv_ref[...], pconv_ref[...], preferred_element_type=F32))
    h = x_ref[...] + _bdot(merged, wout_ref[...])
    h_ref[...] = h
    ms = jnp.mean(h * h, axis=-1, keepdims=True)
    u = h * lax.rsqrt(ms + EPS) * g_ref[...]
    if tiled_u:
        _rows_to_tiles(u_ref, u, u.shape[0])
    else:
        u_ref[...] = u.astype(u_ref.dtype)


def _merge(x, s5_out, ret_out, conv_out, z, s5_proj, ret_proj, conv_proj, w_out, norm_g, *, tm, tiled_u):
    bsz, seq, d = x.shape
    const = lambda shape: pl.BlockSpec(shape, lambda b, i: (0,) * len(shape))
    tok = lambda w, col=0: pl.BlockSpec((None, tm, w), lambda b, i: (b, i, col))
    n_i = seq // tm
    n_gate = N_BRANCH * d // ZB
    if tiled_u:
        u_spec = pl.BlockSpec((tm * ROW_TILE, LANES), lambda b, i: (b * n_i + i, 0))
        u_shape = jax.ShapeDtypeStruct((bsz * seq * ROW_TILE, LANES), F32)
    else:
        u_spec = tok(d)
        u_shape = jax.ShapeDtypeStruct((bsz, seq, d), BF16)
    return pl.pallas_call(
        functools.partial(_merge_kernel, tiled_u=tiled_u),
        grid=(bsz, seq // tm),
        in_specs=[
            tok(d),
            tok(S5_WIDTH),
            tok(RET_V),
            tok(CONV_WIDTH),
            *[tok(ZB, ZC_GATE + j) for j in range(n_gate)],
            const((S5_WIDTH, d)),
            const((RET_V, d)),
            const((CONV_WIDTH, d)),
            const((d, d)),
            const((1, d)),
        ],
        out_specs=[tok(d), u_spec],
        out_shape=[jax.ShapeDtypeStruct((bsz, seq, d), F32), u_shape],
        compiler_params=_params(("parallel", "parallel")),
        name="merge",
    )(x, s5_out, ret_out, conv_out, *([z] * n_gate), s5_proj, ret_proj, conv_proj, w_out, norm_g.reshape(1, d))


def _ffn_kernel(h_ref, u_ref, wg_ref, wu_ref, wd_ref, *rest, riders):
    rider_in, o_ref, rider_out = rest[:riders.n_in], rest[riders.n_in], rest[riders.n_in + 1:]
    f = pl.program_id(1)
    ub = u_ref[...].astype(BF16)
    gate = jnp.dot(ub, wg_ref[...], preferred_element_type=F32)
    up = jnp.dot(ub, wu_ref[...], preferred_element_type=F32)
    part = _bdot(_silu(gate) * up, wd_ref[...])

    @pl.when(f == 0)
    def _():
        o_ref[...] = h_ref[...] + part

    @pl.when(f > 0)
    def _():
        o_ref[...] = o_ref[...] + part

    riders.run(rider_in, rider_out)


def _ffn(h, u, w_gate, w_up, w_down, jobs=(), *, tm, tf):
    rows, d = h.shape
    dff = w_gate.shape[1]
    n_i, n_f = rows // tm, dff // tf
    tok = pl.BlockSpec((tm, d), lambda i, f: (i, 0))
    riders = _Riders(jobs, (n_i, n_f), lambda i, f: i * n_f + f)
    outs = pl.pallas_call(
        functools.partial(_ffn_kernel, riders=riders),
        grid=(n_i, n_f),
        in_specs=[tok, tok,
                  pl.BlockSpec((d, tf), lambda i, f: (0, f)),
                  pl.BlockSpec((d, tf), lambda i, f: (0, f)),
                  pl.BlockSpec((tf, d), lambda i, f: (f, 0)),
                  *riders.in_specs],
        out_specs=[tok, *riders.out_specs],
        out_shape=[jax.ShapeDtypeStruct((rows, d), F32), *riders.out_shapes],
        compiler_params=_params(("arbitrary", "arbitrary")),
        name="ffn_dense",
    )(h, u, w_gate, w_up, w_down, *riders.inputs)
    return outs[0], list(outs[1:])


def _split_bf16(x):
    hi = x.astype(BF16)
    return hi, (x - hi.astype(F32)).astype(BF16)


def _router_kernel(u_ref, rt_ref, tri_ref, idx_ref, wts_ref, rank_ref, cnt_ref, cnt_scr, *, tm):
    @pl.when(pl.program_id(0) == 0)
    def _():
        cnt_scr[...] = jnp.zeros_like(cnt_scr)

    u_hi, u_lo = _split_bf16(_tiles_to_rows(u_ref, tm))
    r_hi, r_lo = _split_bf16(rt_ref[...])
    dn = (((1,), (1,)), ((), ()))
    logits = (lax.dot_general(r_hi, u_hi, dn, preferred_element_type=F32)
              + lax.dot_general(r_lo, u_hi, dn, preferred_element_type=F32)
              + lax.dot_general(r_hi, u_lo, dn, preferred_element_type=F32))
    eidx = lax.broadcasted_iota(jnp.int32, logits.shape, 0)
    m1 = jnp.max(logits, axis=0, keepdims=True)
    i1 = jnp.min(jnp.where(logits == m1, eidx, N_EXPERTS), axis=0, keepdims=True)
    rest = jnp.where(eidx == i1, -jnp.inf, logits)
    m2 = jnp.max(rest, axis=0, keepdims=True)
    i2 = jnp.min(jnp.where(rest == m2, eidx, N_EXPERTS), axis=0, keepdims=True)
    e2 = jnp.exp(m2 - m1)
    w1 = 1.0 / (1.0 + e2)
    idx_ref[...] = jnp.concatenate([i1, i2], axis=0)
    wts_ref[...] = jnp.concatenate([w1, e2 * w1], axis=0)

    hit1 = eidx == i1
    hit2 = eidx == i2
    hits = jnp.where(hit1 | hit2, 1.0, 0.0)
    before = jnp.dot(hits.astype(BF16), tri_ref[...], preferred_element_type=F32) + cnt_scr[...]
    rank_ref[...] = jnp.concatenate(
        [jnp.sum(jnp.where(hit1, before, 0.0), axis=0, keepdims=True),
         jnp.sum(jnp.where(hit2, before, 0.0), axis=0, keepdims=True)], axis=0).astype(jnp.int32)
    cnt_scr[...] = cnt_scr[...] + jnp.sum(hits, axis=1, keepdims=True)
    cnt_ref[...] = cnt_scr[...]


def _router(u8, router, *, tm):
    rows = u8.shape[0] // ROW_TILE
    d = D_MODEL
    tri = (jnp.arange(tm)[:, None] < jnp.arange(tm)[None, :]).astype(BF16)
    const = lambda shape: pl.BlockSpec(shape, lambda i: (0,) * len(shape))
    lane = pl.BlockSpec((TOP_K, tm), lambda i: (0, i))
    return pl.pallas_call(
        functools.partial(_router_kernel, tm=tm),
        grid=(rows // tm,),
        in_specs=[pl.BlockSpec((tm * ROW_TILE, LANES), lambda i: (i, 0)), const((N_EXPERTS, d)), const((tm, tm))],
        out_specs=[lane, lane, lane, const((N_EXPERTS, 1))],
        out_shape=[jax.ShapeDtypeStruct((TOP_K, rows), jnp.int32), jax.ShapeDtypeStruct((TOP_K, rows), F32),
                   jax.ShapeDtypeStruct((TOP_K, rows), jnp.int32), jax.ShapeDtypeStruct((N_EXPERTS, 1), F32)],
        scratch_shapes=[pltpu.VMEM((N_EXPERTS, 1), F32)],
        compiler_params=_params(("arbitrary",)),
        name="moe_router",
    )(u8, router.T, tri)


DMA_UNROLL = 8


def _token_tile(ref, r):
    return ref.at[pl.ds(pl.multiple_of(r * ROW_TILE, ROW_TILE), ROW_TILE)]


def _dispatch_kernel(pos_ref, u_ref, init_ref, xs_ref, sem, *, tb):
    del init_ref

    def start(r, carry):
        for s in range(TOP_K):
            pltpu.make_async_copy(_token_tile(u_ref, r), _token_tile(xs_ref, pos_ref[s, r]),
                                  sem.at[s]).start(priority=s)
        return carry

    lax.fori_loop(0, tb, start, 0, unroll=DMA_UNROLL)
    for s in range(TOP_K):
        pltpu.make_async_copy(u_ref, xs_ref.at[pl.ds(0, tb * ROW_TILE)], sem.at[s]).wait()


def _dispatch(u8, pos, xs8, *, tb):
    rows = u8.shape[0] // ROW_TILE
    return pl.pallas_call(
        functools.partial(_dispatch_kernel, tb=tb),
        grid=(rows // tb,),
        in_specs=[
            pl.BlockSpec((TOP_K, tb), lambda i: (0, i), memory_space=pltpu.SMEM),
            pl.BlockSpec((tb * ROW_TILE, LANES), lambda i: (i, 0)),
            pl.BlockSpec(memory_space=pl.ANY),
        ],
        out_specs=pl.BlockSpec(memory_space=pl.ANY),
        out_shape=jax.ShapeDtypeStruct(xs8.shape, xs8.dtype),
        scratch_shapes=[pltpu.SemaphoreType.DMA((TOP_K,))],
        input_output_aliases={2: 0},
        compiler_params=_params(("arbitrary",)),
        name="moe_dispatch",
    )(pos, u8, xs8)


def _experts_kernel(te_ref, nu_ref, x_ref, wg_ref, wu_ref, wd_ref, o_ref, x_scr, acc_scr, *, tm):
    del te_ref
    i = pl.program_id(0)
    f = pl.program_id(1)

    @pl.when(i < nu_ref[0])
    def _():
        @pl.when(f == 0)
        def _():
            x_scr[...] = _tiles_to_rows(x_ref, tm).astype(BF16)

        xb = x_scr[...]
        gate = jnp.dot(xb, wg_ref[...], preferred_element_type=F32)
        up = jnp.dot(xb, wu_ref[...], preferred_element_type=F32)
        part = _bdot(_silu(gate) * up, wd_ref[...])

        @pl.when(f == 0)
        def _():
            acc_scr[...] = part

        @pl.when(f > 0)
        def _():
            acc_scr[...] = acc_scr[...] + part

        @pl.when(f == pl.num_programs(1) - 1)
        def _():
            _rows_to_tiles(o_ref, acc_scr[...], tm)

    @pl.when(i >= nu_ref[0])
    def _():
        o_ref[...] = jnp.zeros_like(o_ref)


def _experts(xs8, tile_expert, n_used, w_gate, w_up, w_down, *, tm, tf):
    rows = xs8.shape[0] // ROW_TILE
    d = D_MODEL
    dff = w_gate.shape[-1]
    n_f = dff // tf
    last_f = n_f - 1

    def row_map(i, f, te, nu):
        return (jnp.minimum(i, nu[0] - 1), 0)

    def fsel(i, f, nu):
        return jnp.where(i < nu[0], f, last_f)

    grid_spec = pltpu.PrefetchScalarGridSpec(
        num_scalar_prefetch=2,
        grid=(rows // tm, n_f),
        in_specs=[
            pl.BlockSpec((tm * ROW_TILE, LANES), row_map),
            pl.BlockSpec((None, d, tf), lambda i, f, te, nu: (te[i], 0, fsel(i, f, nu))),
            pl.BlockSpec((None, d, tf), lambda i, f, te, nu: (te[i], 0, fsel(i, f, nu))),
            pl.BlockSpec((None, tf, d), lambda i, f, te, nu: (te[i], fsel(i, f, nu), 0)),
        ],
        out_specs=pl.BlockSpec((tm * ROW_TILE, LANES), lambda i, f, te, nu: (i, 0)),
        scratch_shapes=[pltpu.VMEM((tm, d), BF16), pltpu.VMEM((tm, d), F32)],
    )
    return pl.pallas_call(
        functools.partial(_experts_kernel, tm=tm),
        grid_spec=grid_spec,
        out_shape=jax.ShapeDtypeStruct(xs8.shape, F32),
        compiler_params=_params(("arbitrary", "arbitrary")),
        name="moe_experts",
    )(tile_expert, n_used, xs8, w_gate, w_up, w_down)


def _combine_kernel(pos_ref, h_ref, wts_ref, g_ref, ys_ref, o_ref, y_scr, sem, *, tb, final_norm):
    def start(r, carry):
        for s in range(TOP_K):
            pltpu.make_async_copy(_token_tile(ys_ref, pos_ref[s, r]), _token_tile(y_scr.at[s], r),
                                  sem.at[s]).start(priority=s)
        return carry

    lax.fori_loop(0, tb, start, 0, unroll=DMA_UNROLL)
    for s in range(TOP_K):
        pltpu.make_async_copy(ys_ref.at[pl.ds(0, tb * ROW_TILE)], y_scr.at[s], sem.at[s]).wait()
    w1 = wts_ref[:, 0:1]
    w2 = wts_ref[:, 1:2]
    moe = jnp.concatenate([w1 * _tiles_chunk(y_scr.at[0], c, tb) + w2 * _tiles_chunk(y_scr.at[1], c, tb)
                           for c in range(ROW_CHUNKS)], axis=-1)
    out = h_ref[...] + moe
    if final_norm:
        ms = jnp.mean(out * out, axis=-1, keepdims=True)
        out = out * lax.rsqrt(ms + EPS) * g_ref[...]
    o_ref[...] = out


def _combine(h, pos, wts_t, ys8, final_g, *, tb):
    rows, d = h.shape
    final_norm = final_g is not None
    gain = final_g.reshape(1, d) if final_norm else jnp.ones((1, d), F32)
    return pl.pallas_call(
        functools.partial(_combine_kernel, tb=tb, final_norm=final_norm),
        grid=(rows // tb,),
        in_specs=[
            pl.BlockSpec((TOP_K, tb), lambda i: (0, i), memory_space=pltpu.SMEM),
            pl.BlockSpec((tb, d), lambda i: (i, 0)),
            pl.BlockSpec((tb, TOP_K), lambda i: (i, 0)),
            pl.BlockSpec((1, d), lambda i: (0, 0)),
            pl.BlockSpec(memory_space=pl.ANY),
        ],
        out_specs=pl.BlockSpec((tb, d), lambda i: (i, 0)),
        out_shape=jax.ShapeDtypeStruct((rows, d), F32),
        scratch_shapes=[pltpu.VMEM((TOP_K, tb * ROW_TILE, LANES), F32), pltpu.SemaphoreType.DMA((TOP_K,))],
        compiler_params=_params(("arbitrary",)),
        name="moe_combine",
    )(pos, h, wts_t, gain, ys8)


PROJ_TN = 2560
S5_TT = 128
FFN_TF = 1408
MOE_TM = 512
MOE_TF = 1792
MOE_TB = 1024


def _moe_tiles(n_rows):
    return n_rows // MOE_TM + N_EXPERTS


def _moe_buffer_shape(n_rows):
    return (_moe_tiles(n_rows) * MOE_TM * ROW_TILE, LANES)


def _moe(h_list, u8_list, router, w_gate, w_up, w_down, xs8, final_g):
    routes = []
    for u8 in u8_list:
        routes.append(_router(u8, router, tm=min(512, u8.shape[0] // ROW_TILE)))
    counts = [r[3][:, 0].astype(jnp.int32) for r in routes]
    total = sum(counts)
    padded = ((total + MOE_TM - 1) // MOE_TM) * MOE_TM
    ends = jnp.cumsum(padded)
    starts = ends - padded
    n_rows = sum(h.shape[0] for h in h_list) * TOP_K
    n_tiles = _moe_tiles(n_rows)
    assert xs8.shape == _moe_buffer_shape(n_rows)
    n_used = (ends[-1] // MOE_TM).astype(jnp.int32)
    tile_start = jnp.arange(n_tiles, dtype=jnp.int32) * MOE_TM
    tile_expert = jnp.sum((tile_start[:, None] >= ends[None, :]).astype(jnp.int32), axis=1)
    last_expert = jnp.sum((((n_used - 1) * MOE_TM) >= ends).astype(jnp.int32))
    tile_expert = jnp.where(jnp.arange(n_tiles) < n_used, tile_expert, last_expert).astype(jnp.int32)

    poss = []
    seen = jnp.zeros((N_EXPERTS,), jnp.int32)
    for u8, (idx, _, rank, _), cnt in zip(u8_list, routes, counts):
        base = starts + seen
        pos = rank
        for e in range(N_EXPERTS):
            pos = pos + jnp.where(idx == e, base[e], 0)
        poss.append(pos)
        seen = seen + cnt
        xs8 = _dispatch(u8, pos, xs8, tb=min(MOE_TB, u8.shape[0] // ROW_TILE))
    ys8 = _experts(xs8, tile_expert, n_used.reshape(1), w_gate, w_up, w_down, tm=MOE_TM, tf=MOE_TF)
    outs = []
    for h, pos, (_, wts, _, _) in zip(h_list, poss, routes):
        outs.append(_combine(h, pos, wts.T, ys8, final_g, tb=min(MOE_TB, h.shape[0])))
    return outs


def _rmsnorm_kernel(x_ref, g_ref, o_ref):
    x = x_ref[...]
    ms = jnp.mean(x * x, axis=-1, keepdims=True)
    o_ref[...] = x * lax.rsqrt(ms + EPS) * g_ref[...]


def _rmsnorm(x, g, *, tm):
    rows, d = x.shape
    return pl.pallas_call(
        _rmsnorm_kernel,
        grid=(rows // tm,),
        in_specs=[pl.BlockSpec((tm, d), lambda i: (i, 0)), pl.BlockSpec((1, d), lambda i: (0, 0))],
        out_specs=pl.BlockSpec((tm, d), lambda i: (i, 0)),
        out_shape=jax.ShapeDtypeStruct((rows, d), F32),
        compiler_params=_params(("parallel",)),
        name="final_norm",
    )(x, g.reshape(1, d))


def _pack_s5_state(re, im):
    n = re.shape[0]
    return jnp.concatenate([re.reshape(n, S5_LANES), im.reshape(n, S5_LANES)], axis=-1)


def _unpack_s5_state(h):
    n = h.shape[0]
    return (h[:, :S5_LANES].reshape(n, S5_GROUPS, S5_STATE), h[:, S5_LANES:].reshape(n, S5_GROUPS, S5_STATE))


def _mixer(x, s5_h0, ret_s0, conv_buf, pos_offset, p, *, single_step, tiled_u, layer=0, ret_stack=None, jobs=None):
    jobs = jobs or {}
    done = {}
    bsz, seq, d = x.shape
    z, done["proj"] = _norm_proj(x, p["norm_g"], p["w_in"], layer, jobs.get("proj", ()), tt=min(512, seq), tn=PROJ_TN)
    s5_out, s5_state, done["s5"] = _s5_branch(z, s5_h0, p["a_re"], p["a_im"], p["bmat"], p["cmat"], p["d_skip"],
                                              p["glu_w"], p["glu_b"], jobs.get("s5", ()), single_step=single_step,
                                              tt=1 if single_step else S5_TT)

    if single_step:
        z2 = z.reshape(seq, N_IN)
        ret_out, ret_state = _retention_step(z2, ret_s0, layer, ret_stack, pos_offset, bb=16)
        conv_out, conv_state = _conv_step(z2, conv_buf, p["pw_b"], p["dw_w"], p["dw_b"], p["ln_g"], p["ln_b"], bb=32)
        ret_out = ret_out.reshape(bsz, seq, RET_V)
        conv_out = conv_out.reshape(bsz, seq, CONV_WIDTH)
    else:
        ret_out, ret_state, done["ret"] = _retention_prompt(z, ret_s0, pos_offset, jobs.get("ret", ()), tt=256)
        conv_out, conv_state, done["conv"] = _conv_prompt(z, conv_buf, p["pw_b"], p["dw_w"], p["dw_b"], p["ln_g"],
                                                          p["ln_b"], jobs.get("conv", ()), tt=256)

    h, u = _merge(x, s5_out, ret_out, conv_out, z, p["s5_proj"], p["ret_proj"], p["conv_proj"], p["w_out"],
                  p["norm_ffn_g"], tm=min(512, seq), tiled_u=tiled_u)
    return h, u, (s5_state, ret_state, conv_state), done


def kernel(x_prompt, x_sample, state_s5_re, state_s5_im, state_ret, state_conv, norm_mix_g, w_in, s5_lambda_re, s5_lambda_im, s5_log_dt, s5_b_re, s5_b_im, s5_c_re, s5_c_im, s5_d, s5_glu_w, s5_glu_b, s5_proj, ret_proj, conv_pw_b, conv_dw_w, conv_dw_b, conv_ln_g, conv_ln_b, conv_proj, w_out, norm_ffn_g, ffn_w_gate, ffn_w_up, ffn_w_down, moe_router, moe_w_gate, moe_w_up, moe_w_down, norm_final_g):
    depth = w_in.shape[0]
    bp, seq, d = x_prompt.shape
    ns = x_sample.shape[0]
    past_len = 16384
    bf = lambda a: a.astype(BF16)

    hp = x_prompt
    hs = x_sample.reshape(1, ns, d)
    zero_s5 = jnp.zeros((bp, 2 * S5_LANES), F32)
    zero_ret = jnp.zeros((bp, RET_HEADS, RET_DK, RET_DV), F32)
    zero_conv = jnp.zeros((bp, CONV_K - 1, CONV_WIDTH), F32)

    p_states, s_states = [], []
    ret_stack = None
    normed = False
    groups = depth * S5_GROUPS
    disc = _s5_discretize(s5_lambda_re.reshape(groups, S5_STATE), s5_lambda_im.reshape(groups, S5_STATE),
                          s5_log_dt.reshape(groups), s5_b_re.reshape(groups, S5_STATE, S5_GROUP),
                          s5_b_im.reshape(groups, S5_STATE, S5_GROUP))
    disc = [a.reshape((depth, S5_GROUPS) + a.shape[1:]) for a in disc]
    for l in range(depth):
        is_moe = l % 2 == 1
        a_re, a_im, bb_re, bb_im = (a[l] for a in disc)
        bmat, cmat = _s5_block_mats(bb_re, bb_im, s5_c_re[l], s5_c_im[l])
        p = dict(norm_g=norm_mix_g[l], w_in=w_in,
                 a_re=a_re, a_im=a_im, bmat=bmat, cmat=cmat, d_skip=s5_d[l], glu_w=bf(s5_glu_w[l]),
                 glu_b=s5_glu_b[l], pw_b=conv_pw_b[l], dw_w=conv_dw_w[l], dw_b=conv_dw_b[l], ln_g=conv_ln_g[l],
                 ln_b=conv_ln_b[l], s5_proj=bf(s5_proj[l]), ret_proj=bf(ret_proj[l]), conv_proj=bf(conv_proj[l]),
                 w_out=bf(w_out[l]), norm_ffn_g=norm_ffn_g[l])
        j = l // 2
        flat = lambda w: w.reshape(-1, w.shape[-1])
        if is_moe:
            jobs = {"ret": [_cast_job(flat(moe_w_gate[j])), _cast_job(flat(moe_w_down[j]))],
                    "conv": [_cast_job(flat(moe_w_up[j]))],
                    "proj": [(None, _moe_buffer_shape((bp * seq + ns) * TOP_K), F32)]}
        else:
            jobs = {"s5": [_cast_job(ffn_w_gate[j]), _cast_job(ffn_w_up[j]), _cast_job(ffn_w_down[j])]}
        hp, up, st_p, done = _mixer(hp, zero_s5, zero_ret, zero_conv, 0.0, p, single_step=False, tiled_u=is_moe,
                                    layer=l, jobs=jobs)
        hs, us, st_s, _ = _mixer(hs, _pack_s5_state(state_s5_re[l], state_s5_im[l]), state_ret, state_conv[l],
                                 float(past_len), p, single_step=True, tiled_u=is_moe, layer=l, ret_stack=ret_stack)
        p_states.append(st_p)
        s_states.append(st_s)

        hp2, hs2 = hp.reshape(bp * seq, d), hs.reshape(ns, d)
        if is_moe:
            ret_stack = st_s[1]
            final_g = norm_final_g if l == depth - 1 else None
            normed = final_g is not None
            w_gate_b, w_down_b = (c.reshape(w.shape[1:]) for c, w in zip(done["ret"], (moe_w_gate, moe_w_down)))
            w_up_b = done["conv"][0].reshape(moe_w_up.shape[1:])
            hp2, hs2 = _moe([hp2, hs2], [up, us], moe_router[j], w_gate_b, w_up_b, w_down_b, done["proj"][0], final_g)
        else:
            wg, wu, wd = done["s5"]
            ffn_jobs = []
            if l == 0 and depth > 1:
                rows = st_s[1].size // RET_DV
                ffn_jobs = [(st_s[1].reshape(rows, RET_DV), (depth * rows, RET_DV), F32)]
            hp2, made = _ffn(hp2, up.reshape(bp * seq, d), wg, wu, wd, ffn_jobs, tm=512, tf=FFN_TF)
            if l == 0:
                ret_stack = made[0].reshape(state_ret.shape) if ffn_jobs else st_s[1][None]
            else:
                ret_stack = st_s[1]
            hs2, _ = _ffn(hs2, us.reshape(ns, d), wg, wu, wd, tm=ns, tf=FFN_TF)
        hp = hp2.reshape(bp, seq, d)
        hs = hs2.reshape(1, ns, d)

    if normed:
        y_prompt, y_sample = hp, hs.reshape(ns, 1, d)
    else:
        y_prompt = _rmsnorm(hp.reshape(bp * seq, d), norm_final_g, tm=512).reshape(bp, seq, d)
        y_sample = _rmsnorm(hs.reshape(ns, d), norm_final_g, tm=ns).reshape(ns, 1, d)

    def stack_s5(states):
        s5 = [_unpack_s5_state(s[0]) for s in states]
        return jnp.stack([a for a, _ in s5]), jnp.stack([b for _, b in s5])

    p_re, p_im = stack_s5(p_states)
    s_re, s_im = stack_s5(s_states)
    p_ret = jnp.stack([s[1] for s in p_states])
    p_conv = jnp.stack([s[2] for s in p_states])
    s_conv = jnp.stack([s[2] for s in s_states])
    return (y_prompt, y_sample, p_re, p_im, p_ret, p_conv, s_re, s_im, ret_stack, s_conv)
```

```python
import functools
import math

import jax
import jax.numpy as jnp
from jax import lax
from jax.experimental import pallas as pl
from jax.experimental.pallas import tpu as pltpu

F32 = jnp.float32
BF16 = jnp.bfloat16

D_MODEL = 1024
S5_WIDTH = 512
S5_GROUP = 16
S5_GROUPS = 32
S5_STATE = 64
S5_LANES = S5_GROUPS * S5_STATE
RET_HEADS = 4
RET_DK = 128
RET_DV = 256
RET_QK = RET_HEADS * RET_DK
RET_V = RET_HEADS * RET_DV
RET_CHUNK = 128
ROPE_BASE = 10000.0
CONV_WIDTH = 512
CONV_K = 31
CONV_HIST = 32
N_EXPERTS = 8
TOP_K = 2
N_BRANCH = 3
EPS = 1e-6
N_IN = S5_WIDTH + 2 * RET_QK + 2 * RET_V + 2 * CONV_WIDTH + N_BRANCH * D_MODEL
ZB = 512
ZC_S5 = 0
ZC_Q = ZC_S5 + S5_WIDTH // ZB
ZC_K = ZC_Q + RET_QK // ZB
ZC_V = ZC_K + RET_QK // ZB
ZC_G = ZC_V + RET_V // ZB
ZC_CONV = ZC_G + RET_V // ZB
ZC_GATE = ZC_CONV + 2 * CONV_WIDTH // ZB
HEADS_PER_ZB = ZB // RET_DV

ROW_TILE = 8
LANES = 128
ROW_CHUNKS = D_MODEL // LANES
VMEM_LIMIT = 48 * 1024 * 1024
ROW_SPLIT = 2


def _params(sem):
    return pltpu.CompilerParams(dimension_semantics=sem, vmem_limit_bytes=VMEM_LIMIT)


def _silu(x):
    return x * jax.nn.sigmoid(x)


def _bdot(a, b):
    return jnp.dot(a.astype(BF16), b.astype(BF16), preferred_element_type=F32)


def _norm_proj_kernel(x_ref, g_ref, w_ref, *rest, riders):
    rider_in, o_ref = rest[:riders.n_in], rest[riders.n_in]
    rider_out, w_scr = rest[riders.n_in + 1:-1], rest[-1]
    riders.run(rider_in, rider_out)

    @pl.when((pl.program_id(1) == 0) & (pl.program_id(2) == 0))
    def _():
        w_scr[...] = w_ref[...].astype(BF16)

    half = x_ref.shape[0] // ROW_SPLIT
    for r in range(ROW_SPLIT):
        rows = slice(r * half, (r + 1) * half)
        x = x_ref[rows, :]
        ms = jnp.mean(x * x, axis=-1, keepdims=True)
        u = (x * lax.rsqrt(ms + EPS) * g_ref[...]).astype(BF16)
        o_ref[rows, :] = jnp.dot(u, w_scr[...], preferred_element_type=F32).astype(o_ref.dtype)


def _norm_proj(x, g, w_all, layer, jobs=(), *, tt, tn):
    bsz, seq, d = x.shape
    n = w_all.shape[2]
    grid = (n // tn, bsz, seq // tt)
    riders = _Riders(jobs, grid, lambda j, b, i: (j * grid[1] + b) * grid[2] + i)
    outs = pl.pallas_call(
        functools.partial(_norm_proj_kernel, riders=riders),
        grid=grid,
        in_specs=[
            pl.BlockSpec((None, tt, d), lambda j, b, i: (b, i, 0)),
            pl.BlockSpec((1, d), lambda j, b, i: (0, 0)),
            pl.BlockSpec((None, d, tn), lambda j, b, i: (layer, 0, j)),
            *riders.in_specs,
        ],
        out_specs=[pl.BlockSpec((None, tt, tn), lambda j, b, i: (b, i, j)), *riders.out_specs],
        out_shape=[jax.ShapeDtypeStruct((bsz, seq, n), BF16), *riders.out_shapes],
        scratch_shapes=[pltpu.VMEM((d, tn), BF16)],
        compiler_params=_params(("arbitrary", "arbitrary", "arbitrary")),
        name="norm_proj",
    )(x, g.reshape(1, d), w_all, *riders.inputs)
    return outs[0], list(outs[1:])


def _s5_disc_kernel(lre_ref, lim_ref, ldt_ref, bre_ref, bim_ref, are_ref, aim_ref, ore_ref, oim_ref):
    lam_re = lre_ref[...]
    lam_im = lim_ref[...]
    dt = jnp.exp(ldt_ref[...])
    mag = jnp.exp(lam_re * dt)
    ang = lam_im * dt
    lbar_re = mag * jnp.cos(ang)
    lbar_im = mag * jnp.sin(ang)
    den = lam_re * lam_re + lam_im * lam_im
    nr = lbar_re - 1.0
    f_re = (nr * lam_re + lbar_im * lam_im) / den
    f_im = (lbar_im * lam_re - nr * lam_im) / den
    b_re = bre_ref[...]
    b_im = bim_ref[...]
    are_ref[...] = lbar_re
    aim_ref[...] = lbar_im
    ore_ref[...] = f_re * b_re - f_im * b_im
    oim_ref[...] = f_re * b_im + f_im * b_re


def _s5_discretize(lam_re, lam_im, log_dt, b_re, b_im):
    g, n = lam_re.shape
    p = b_re.shape[-1]
    rows = g * n
    col = lambda a: a.reshape(rows, 1)
    ldt = jnp.broadcast_to(log_dt[:, None], (g, n))
    outs = pl.pallas_call(
        _s5_disc_kernel,
        out_shape=[jax.ShapeDtypeStruct((rows, 1), F32), jax.ShapeDtypeStruct((rows, 1), F32),
                   jax.ShapeDtypeStruct((rows, p), F32), jax.ShapeDtypeStruct((rows, p), F32)],
        name="s5_discretize",
    )(col(lam_re), col(lam_im), col(ldt), b_re.reshape(rows, p), b_im.reshape(rows, p))
    a_re, a_im, bb_re, bb_im = outs
    return a_re.reshape(g, n), a_im.reshape(g, n), bb_re.reshape(g, n, p), bb_im.reshape(g, n, p)


S5_KCH = 128
S5_NCHUNK = S5_WIDTH // S5_KCH
S5_GPC = S5_KCH // S5_GROUP
S5_SPC = S5_GPC * S5_STATE


def _s5_block_mats(bbar_re, bbar_im, c_re, c_im):
    eye = jnp.eye(S5_GPC, dtype=F32)

    def in_blocks(bb):
        t = bb.reshape(S5_NCHUNK, S5_GPC, S5_STATE, S5_GROUP)
        m = jnp.einsum("cgnp,gh->cgphn", t, eye)
        return m.reshape(S5_NCHUNK, S5_KCH, S5_SPC)

    def out_blocks(cc):
        t = cc.reshape(S5_NCHUNK, S5_GPC, S5_GROUP, S5_STATE)
        m = jnp.einsum("cgpn,gh->cgnhp", t, eye)
        return m.reshape(S5_NCHUNK, S5_SPC, S5_KCH)

    bmat = jnp.concatenate([in_blocks(bbar_re), in_blocks(bbar_im)], axis=-1).astype(BF16)
    cmat = jnp.stack([out_blocks(c_re), -out_blocks(c_im)], axis=1).astype(BF16)
    return bmat, cmat


def _gelu_tanh(x):
    return 0.5 * x * (1.0 + jnp.tanh(math.sqrt(2.0 / math.pi) * (x + 0.044715 * (x * x * x))))


class _Riders:
    def __init__(self, jobs, grid, step_index):
        self.steps = math.prod(grid)
        self.n_axes = len(grid)
        self.step_index = step_index
        self.inputs, self.in_specs, self.out_specs, self.out_shapes, self.src_steps = [], [], [], [], []
        for src, shape, dtype in jobs:
            slab = shape[0] // self.steps
            assert slab * self.steps == shape[0] and slab % 16 == 0
            n_src = 0 if src is None else src.shape[0] // slab
            if src is not None:
                assert n_src * slab == src.shape[0] and src.shape[1] == shape[1]
                self.inputs.append(src)
                self.in_specs.append(pl.BlockSpec(
                    (slab, shape[1]), lambda *idx, n=n_src: (jnp.minimum(step_index(*idx[:self.n_axes]), n - 1), 0)))
            self.out_specs.append(pl.BlockSpec((slab, shape[1]), lambda *idx: (step_index(*idx[:self.n_axes]), 0)))
            self.out_shapes.append(jax.ShapeDtypeStruct(shape, dtype))
            self.src_steps.append(n_src)

    @property
    def n_in(self):
        return len(self.inputs)

    @property
    def n_out(self):
        return len(self.out_shapes)

    def run(self, in_refs, out_refs):
        step = self.step_index(*(pl.program_id(a) for a in range(self.n_axes)))
        srcs = iter(in_refs)
        for dst, n_src in zip(out_refs, self.src_steps):
            if n_src == 0:
                dst[...] = jnp.zeros_like(dst)
                continue
            src = next(srcs)
            if n_src == self.steps:
                dst[...] = src[...].astype(dst.dtype)
                continue

            @pl.when(step < n_src)
            def _(src=src, dst=dst):
                dst[...] = src[...].astype(dst.dtype)

            @pl.when(step >= n_src)
            def _(dst=dst):
                dst[...] = jnp.zeros_like(dst)


def _cast_job(w):
    return (w, w.shape, BF16)


def _s5_kernel(u_ref, h0_ref, are_ref, aim_ref, bmat_ref, cmat_ref, d_ref, gw_ref, gb_ref, *rest,
               nb, tt, lane_chunk, riders):
    rider_in, rest = rest[:riders.n_in], rest[riders.n_in:]
    o_ref, hout_ref = rest[:2]
    rider_out = rest[2:2 + riders.n_out]
    hs_scr, h_scr, io_scr = rest[2 + riders.n_out:]
    riders.run(rider_in, rider_out)

    @pl.when(pl.program_id(0) == 0)
    def _():
        h_scr[...] = h0_ref[...]

    def seq_rows(b):
        return pl.ds(b, tt, stride=nb)

    def lanes(c):
        return slice(c * S5_KCH, (c + 1) * S5_KCH)

    if tt == 1:
        for c in range(S5_NCHUNK):
            io_scr[c] = u_ref[:, lanes(c)].astype(F32)
    else:
        for b in range(nb):
            for c in range(S5_NCHUNK):
                io_scr[c, seq_rows(b), :] = u_ref[b, :, lanes(c)].astype(F32)

    u = jnp.concatenate([io_scr[c] for c in range(S5_NCHUNK)], axis=-1)
    for c in range(S5_NCHUNK):
        bu = jnp.dot(io_scr[c].astype(BF16), bmat_ref[c], preferred_element_type=F32)
        hs_scr[:, c * S5_SPC:(c + 1) * S5_SPC] = bu[:, :S5_SPC]
        hs_scr[:, S5_LANES + c * S5_SPC:S5_LANES + (c + 1) * S5_SPC] = bu[:, S5_SPC:]

    for lc in range(S5_LANES // lane_chunk):
        re_sl = slice(lc * lane_chunk, (lc + 1) * lane_chunk)
        im_sl = slice(S5_LANES + lc * lane_chunk, S5_LANES + (lc + 1) * lane_chunk)
        a_re = jnp.broadcast_to(are_ref[:, re_sl], (nb, lane_chunk))
        a_im = jnp.broadcast_to(aim_ref[:, re_sl], (nb, lane_chunk))

        def step(t, carry):
            h_re, h_im = carry
            r0 = t * nb if isinstance(t, int) else pl.multiple_of(t * nb, nb)
            n_re = a_re * h_re - a_im * h_im + hs_scr[pl.ds(r0, nb), re_sl]
            n_im = a_re * h_im + a_im * h_re + hs_scr[pl.ds(r0, nb), im_sl]
            hs_scr[pl.ds(r0, nb), re_sl] = n_re
            hs_scr[pl.ds(r0, nb), im_sl] = n_im
            return n_re, n_im

        carry = (h_scr[:, re_sl], h_scr[:, im_sl])
        if tt == 1:
            carry = step(0, carry)
        else:
            carry = lax.fori_loop(0, tt, step, carry, unroll=4)
        h_scr[:, re_sl] = carry[0]
        h_scr[:, im_sl] = carry[1]

    hout_ref[...] = h_scr[...]

    ys = []
    for c in range(S5_NCHUNK):
        h_re = hs_scr[:, c * S5_SPC:(c + 1) * S5_SPC].astype(BF16)
        h_im = hs_scr[:, S5_LANES + c * S5_SPC:S5_LANES + (c + 1) * S5_SPC].astype(BF16)
        ys.append(jnp.dot(h_re, cmat_ref[c, 0], preferred_element_type=F32)
                  + jnp.dot(h_im, cmat_ref[c, 1], preferred_element_type=F32))
    y = jnp.concatenate(ys, axis=-1) + d_ref[...] * u
    z = _gelu_tanh(y)
    gate = jnp.dot(z.astype(BF16), gw_ref[...], preferred_element_type=F32) + gb_ref[...]
    out = z * jax.nn.sigmoid(gate)
    if tt == 1:
        o_ref[...] = out.astype(o_ref.dtype)
    else:
        for c in range(S5_NCHUNK):
            io_scr[c] = out[:, lanes(c)]
        for b in range(nb):
            for c in range(S5_NCHUNK):
                o_ref[b, :, lanes(c)] = io_scr[c, seq_rows(b), :].astype(o_ref.dtype)


def _s5_branch(z, h0, a_re, a_im, bmat, cmat, d_skip, glu_w, glu_b, jobs=(), *, single_step, tt):
    bsz, seq, _ = z.shape
    if single_step:
        assert bsz == 1 and tt == 1
        nb = seq
        in_spec = pl.BlockSpec((None, nb, S5_WIDTH), lambda i: (0, 0, ZC_S5))
        out_spec = pl.BlockSpec((None, nb, S5_WIDTH), lambda i: (0, 0, 0))
        grid = (1,)
    else:
        nb = bsz
        in_spec = pl.BlockSpec((nb, tt, S5_WIDTH), lambda i: (0, i, ZC_S5))
        out_spec = pl.BlockSpec((nb, tt, S5_WIDTH), lambda i: (0, i, 0))
        grid = (seq // tt,)
    rblk = tt * nb
    lane_chunk = 1024 if nb <= 8 else 512
    const = lambda shape: pl.BlockSpec(shape, lambda i: (0,) * len(shape))
    riders = _Riders(jobs, grid, lambda i: i)
    outs = pl.pallas_call(
        functools.partial(_s5_kernel, nb=nb, tt=tt, lane_chunk=lane_chunk, riders=riders),
        grid=grid,
        in_specs=[
            in_spec,
            const((nb, 2 * S5_LANES)),
            const((1, S5_LANES)),
            const((1, S5_LANES)),
            const(bmat.shape),
            const(cmat.shape),
            const((1, S5_WIDTH)),
            const((S5_WIDTH, S5_WIDTH)),
            const((1, S5_WIDTH)),
            *riders.in_specs,
        ],
        out_specs=[out_spec, const((nb, 2 * S5_LANES)), *riders.out_specs],
        out_shape=[jax.ShapeDtypeStruct((bsz, seq, S5_WIDTH), BF16),
                   jax.ShapeDtypeStruct((nb, 2 * S5_LANES), F32), *riders.out_shapes],
        scratch_shapes=[pltpu.VMEM((rblk, 2 * S5_LANES), F32), pltpu.VMEM((nb, 2 * S5_LANES), F32),
                        pltpu.VMEM((S5_NCHUNK, rblk, S5_KCH), F32)],
        compiler_params=_params(("arbitrary",)),
        name="s5_branch",
    )(z, h0, a_re.reshape(1, S5_LANES), a_im.reshape(1, S5_LANES), bmat, cmat,
      d_skip.reshape(1, S5_WIDTH), glu_w, glu_b.reshape(1, S5_WIDTH), *riders.inputs)
    return outs[0], outs[1], list(outs[2:])


def _rope_tables(pos):
    half = RET_DK // 2
    freqs = ROPE_BASE ** (-jnp.arange(half, dtype=F32) / half)
    ang = pos[:, None] * freqs[None, :]
    cos = jnp.cos(ang)
    sin = jnp.sin(ang)
    return jnp.concatenate([cos, cos], axis=-1), jnp.concatenate([-sin, sin], axis=-1)


def _rope(x, cos, sin):
    return x * cos + pltpu.roll(x, RET_DK // 2, 1) * sin


def _group_norm(o):
    mu = jnp.mean(o, axis=-1, keepdims=True)
    d = o - mu
    var = jnp.mean(d * d, axis=-1, keepdims=True)
    return d * lax.rsqrt(var + EPS)


def _retention_tables(chunk):
    log_gamma = jnp.log(1.0 - 2.0 ** (-5.0 - jnp.arange(RET_HEADS, dtype=F32)))
    idx = jnp.arange(chunk, dtype=F32)
    diff = idx[:, None] - idx[None, :]
    decay = jnp.where(diff >= 0, jnp.exp(jnp.maximum(diff, 0.0)[None] * log_gamma[:, None, None]), 0.0)
    cross = jnp.exp((idx + 1.0)[None, :] * log_gamma[:, None])[:, :, None]
    kdec = jnp.exp((chunk - 1.0 - idx)[None, :] * log_gamma[:, None])[:, :, None]
    full = jnp.exp(chunk * log_gamma)
    return decay, cross, kdec, full


def _head_cols(refs, h, rows):
    lo = (h % HEADS_PER_ZB) * RET_DV
    return refs[h // HEADS_PER_ZB][rows, lo:lo + RET_DV].astype(F32)


def _retention_kernel(q_ref, k_ref, v0_ref, v1_ref, g0_ref, g1_ref, cos_ref, sin_ref, s0_ref, decay_ref, cross_ref,
                      kdec_ref, full_ref, *rest, n_chunks, riders):
    rider_in, rest = rest[:riders.n_in], rest[riders.n_in:]
    o_ref, sout_ref = rest[:2]
    rider_out, s_scr = rest[2:-1], rest[-1]
    riders.run(rider_in, rider_out)

    @pl.when(pl.program_id(1) == 0)
    def _():
        s_scr[...] = s0_ref[...]

    for c in range(n_chunks):
        rows = slice(c * RET_CHUNK, (c + 1) * RET_CHUNK)
        cos = cos_ref[rows, :]
        sin = sin_ref[rows, :]
        for h in range(RET_HEADS):
            qk_cols = slice(h * RET_DK, (h + 1) * RET_DK)
            v_cols = slice(h * RET_DV, (h + 1) * RET_DV)
            qh = _rope(q_ref[rows, qk_cols].astype(F32), cos, sin)
            kh = _rope(k_ref[rows, qk_cols].astype(F32), cos, sin) * (RET_DK ** -0.5)
            vb = _head_cols((v0_ref, v1_ref), h, rows).astype(BF16)
            qb = qh.astype(BF16)
            state = s_scr[h]
            inner = lax.dot_general(qb, kh.astype(BF16), (((1,), (1,)), ((), ())),
                                    preferred_element_type=F32) * decay_ref[h]
            out = (jnp.dot(inner.astype(BF16), vb, preferred_element_type=F32)
                   + jnp.dot(qb, state.astype(BF16), preferred_element_type=F32) * cross_ref[h])
            kd = (kh * kdec_ref[h]).astype(BF16)
            s_scr[h] = full_ref[h] * state + jnp.dot(kd.T, vb, preferred_element_type=F32)
            gate = _head_cols((g0_ref, g1_ref), h, rows)
            o_ref[rows, v_cols] = (_silu(gate) * _group_norm(out)).astype(o_ref.dtype)

    sout_ref[...] = s_scr[...]


def _retention_prompt(z, state0, pos_offset, jobs=(), *, tt):
    bsz, seq, _ = z.shape
    assert seq % RET_CHUNK == 0 and tt % RET_CHUNK == 0
    cos, sin = _rope_tables(jnp.arange(seq, dtype=F32) + pos_offset)
    decay, cross, kdec, full = _retention_tables(RET_CHUNK)
    full = jnp.broadcast_to(full[:, None, None], (RET_HEADS, 1, RET_DV))
    const = lambda shape: pl.BlockSpec(shape, lambda b, i: (0,) * len(shape))
    zblk = lambda col: pl.BlockSpec((None, tt, ZB), lambda b, i: (b, i, col))
    n_i = seq // tt
    riders = _Riders(jobs, (bsz, n_i), lambda b, i: b * n_i + i)
    outs = pl.pallas_call(
        functools.partial(_retention_kernel, n_chunks=tt // RET_CHUNK, riders=riders),
        grid=(bsz, n_i),
        in_specs=[
            zblk(ZC_Q), zblk(ZC_K), zblk(ZC_V), zblk(ZC_V + 1), zblk(ZC_G), zblk(ZC_G + 1),
            pl.BlockSpec((tt, RET_DK), lambda b, i: (i, 0)),
            pl.BlockSpec((tt, RET_DK), lambda b, i: (i, 0)),
            pl.BlockSpec((None, RET_HEADS, RET_DK, RET_DV), lambda b, i: (b, 0, 0, 0)),
            const((RET_HEADS, RET_CHUNK, RET_CHUNK)),
            const((RET_HEADS, RET_CHUNK, 1)),
            const((RET_HEADS, RET_CHUNK, 1)),
            const((RET_HEADS, 1, RET_DV)),
            *riders.in_specs,
        ],
        out_specs=[
            pl.BlockSpec((None, tt, RET_V), lambda b, i: (b, i, 0)),
            pl.BlockSpec((None, RET_HEADS, RET_DK, RET_DV), lambda b, i: (b, 0, 0, 0)),
            *riders.out_specs,
        ],
        out_shape=[jax.ShapeDtypeStruct((bsz, seq, RET_V), BF16),
                   jax.ShapeDtypeStruct((bsz, RET_HEADS, RET_DK, RET_DV), F32), *riders.out_shapes],
        scratch_shapes=[pltpu.VMEM((RET_HEADS, RET_DK, RET_DV), F32)],
        compiler_params=_params(("arbitrary", "arbitrary")),
        name="retention_prompt",
    )(z, z, z, z, z, z, cos, sin, state0, decay, cross, kdec, full, *riders.inputs)
    return outs[0], outs[1], list(outs[2:])


def _retention_step_kernel(q_ref, k_ref, v0_ref, v1_ref, g0_ref, g1_ref, cos_ref, sin_ref, s_ref, gam_ref, *rest,
                           bb, layer):
    del layer
    o_ref, sout_ref, o_scr = rest[-3:]
    cos = cos_ref[...]
    sin = sin_ref[...]
    for h in range(RET_HEADS):
        qk_cols = slice(h * RET_DK, (h + 1) * RET_DK)
        v_cols = slice(h * RET_DV, (h + 1) * RET_DV)
        qh = _rope(q_ref[:, qk_cols].astype(F32), cos, sin)
        kh = _rope(k_ref[:, qk_cols].astype(F32), cos, sin) * (RET_DK ** -0.5)
        qk = jnp.sum(qh * kh, axis=-1, keepdims=True)
        q_t = qh.T
        k_t = kh.T
        gamma = gam_ref[h]
        v_all = _head_cols((v0_ref, v1_ref), h, slice(None))
        for b in range(bb):
            state = s_ref[b, h]
            vrow = v_all[b:b + 1, :]
            qs = jnp.sum(q_t[:, b:b + 1] * state, axis=0, keepdims=True)
            o_scr[b:b + 1, v_cols] = qk[b:b + 1, :] * vrow + qs * gamma
            sout_ref[b, h] = gamma * state + k_t[:, b:b + 1] * vrow
    for h in range(RET_HEADS):
        v_cols = slice(h * RET_DV, (h + 1) * RET_DV)
        gate = _head_cols((g0_ref, g1_ref), h, slice(None))
        o_ref[:, v_cols] = (_silu(gate) * _group_norm(o_scr[:, v_cols])).astype(o_ref.dtype)


def _retention_step(z, states, layer, stack, pos, *, bb):
    n = z.shape[0]
    cos, sin = _rope_tables(jnp.full((1,), pos, F32))
    log_gamma = jnp.log(1.0 - 2.0 ** (-5.0 - jnp.arange(RET_HEADS, dtype=F32)))
    gam = jnp.broadcast_to(jnp.exp(log_gamma)[:, None, None], (RET_HEADS, 1, RET_DV))
    const = lambda shape: pl.BlockSpec(shape, lambda i: (0,) * len(shape))
    slab = pl.BlockSpec((None, bb, RET_HEADS, RET_DK, RET_DV), lambda i: (layer, i, 0, 0, 0))
    zblk = lambda col: pl.BlockSpec((bb, ZB), lambda i: (i, col))
    in_specs = [
        zblk(ZC_Q), zblk(ZC_K), zblk(ZC_V), zblk(ZC_V + 1), zblk(ZC_G), zblk(ZC_G + 1),
        const((1, RET_DK)),
        const((1, RET_DK)),
        slab,
        const((RET_HEADS, 1, RET_DV)),
    ]
    args = [z, z, z, z, z, z, cos, sin, states, gam]
    if stack is None:
        state_spec = pl.BlockSpec((bb, RET_HEADS, RET_DK, RET_DV), lambda i: (i, 0, 0, 0))
        state_shape = jax.ShapeDtypeStruct(states.shape[1:], F32)
        aliases = {}
    else:
        in_specs.append(pl.BlockSpec(memory_space=pl.ANY))
        args.append(stack)
        state_spec = slab
        state_shape = jax.ShapeDtypeStruct(stack.shape, F32)
        aliases = {len(args) - 1: 1}
    return pl.pallas_call(
        functools.partial(_retention_step_kernel, bb=bb, layer=layer),
        grid=(n // bb,),
        in_specs=in_specs,
        out_specs=[pl.BlockSpec((bb, RET_V), lambda i: (i, 0)), state_spec],
        out_shape=[jax.ShapeDtypeStruct((n, RET_V), BF16), state_shape],
        scratch_shapes=[pltpu.VMEM((bb, RET_V), F32)],
        input_output_aliases=aliases,
        compiler_params=_params(("parallel",)),
        name="retention_step",
    )(*args)


def _layer_norm(y, g, b):
    mu = jnp.mean(y, axis=-1, keepdims=True)
    d = y - mu
    var = jnp.mean(d * d, axis=-1, keepdims=True)
    return d * lax.rsqrt(var + EPS) * g + b


CONV_RB = 128
CONV_PITCH = 2
CONV_LCH = CONV_WIDTH // LANES


def _conv_glu(a_ref, b_ref, pwb_ref):
    a = a_ref[...].astype(F32) + pwb_ref[:, :CONV_WIDTH]
    b = b_ref[...].astype(F32) + pwb_ref[:, CONV_WIDTH:]
    return a * jax.nn.sigmoid(b)


def _conv_kernel(a_ref, b_ref, buf_ref, pwb_ref, dww_ref, dwb_ref, lng_ref, lnb_ref, *rest, tt, riders):
    rider_in, rest = rest[:riders.n_in], rest[riders.n_in:]
    o_ref, hist_ref = rest[:2]
    rider_out, (x_scr, y_scr) = rest[2:-2], rest[-2:]
    riders.run(rider_in, rider_out)
    _conv_body(a_ref, b_ref, buf_ref, pwb_ref, dww_ref, dwb_ref, lng_ref, lnb_ref, o_ref, hist_ref, x_scr, y_scr, tt=tt)


def _conv_body(a_ref, b_ref, buf_ref, pwb_ref, dww_ref, dwb_ref, lng_ref, lnb_ref, o_ref, hist_ref, x_scr, y_scr,
               *, tt):
    i = pl.program_id(1)

    def rows(start, n):
        return pl.ds(CONV_PITCH * start, n, stride=CONV_PITCH)

    def lanes(c):
        return slice(c * LANES, (c + 1) * LANES)

    @pl.when(i == 0)
    def _():
        for c in range(CONV_LCH):
            x_scr[c, rows(0, CONV_HIST), :] = buf_ref[:, lanes(c)]

    @pl.when(i > 0)
    def _():
        for c in range(CONV_LCH):
            x_scr[c, rows(0, CONV_HIST), :] = x_scr[c, rows(tt, CONV_HIST), :]

    glu = _conv_glu(a_ref, b_ref, pwb_ref)
    for c in range(CONV_LCH):
        x_scr[c, rows(CONV_HIST, tt), :] = glu[:, lanes(c)]
    hist_ref[...] = glu[tt - CONV_HIST:, :]

    off = CONV_HIST - (CONV_K - 1)
    for c in range(CONV_LCH):
        def row_block(r, carry, c=c):
            base = r * CONV_RB
            n_grp = CONV_RB // ROW_TILE
            accs = [jnp.broadcast_to(dwb_ref[:, lanes(c)], (ROW_TILE, LANES))] * n_grp
            for m in range(CONV_RB - ROW_TILE + CONV_K):
                win = x_scr[c, rows(base + (off + m), ROW_TILE), :]
                for k in range(m % ROW_TILE, CONV_K, ROW_TILE):
                    j = (m - k) // ROW_TILE
                    if 0 <= j < n_grp:
                        accs[j] = accs[j] + dww_ref[k:k + 1, lanes(c)] * win
            y_scr[pl.ds(pl.multiple_of(base, CONV_RB), CONV_RB), lanes(c)] = jnp.concatenate(accs, axis=0)
            return carry

        lax.fori_loop(0, tt // CONV_RB, row_block, 0)

    o_ref[...] = _silu(_layer_norm(y_scr[...], lng_ref[...], lnb_ref[...])).astype(o_ref.dtype)


def _conv_prompt(z, buf, pw_b, dw_w, dw_b, ln_g, ln_b, jobs=(), *, tt):
    bsz, seq, _ = z.shape
    assert seq >= CONV_HIST and tt >= CONV_HIST
    buf32 = jnp.pad(buf, ((0, 0), (CONV_HIST - (CONV_K - 1), 0), (0, 0)))
    const = lambda shape: pl.BlockSpec(shape, lambda b, i: (0,) * len(shape))
    zblk = lambda col: pl.BlockSpec((None, tt, ZB), lambda b, i: (b, i, col))
    n_i = seq // tt
    riders = _Riders(jobs, (bsz, n_i), lambda b, i: b * n_i + i)
    out, hist, *cast = pl.pallas_call(
        functools.partial(_conv_kernel, tt=tt, riders=riders),
        grid=(bsz, n_i),
        in_specs=[
            zblk(ZC_CONV), zblk(ZC_CONV + 1),
            pl.BlockSpec((None, CONV_HIST, CONV_WIDTH), lambda b, i: (b, 0, 0)),
            const((1, 2 * CONV_WIDTH)),
            const((CONV_K, CONV_WIDTH)),
            const((1, CONV_WIDTH)),
            const((1, CONV_WIDTH)),
            const((1, CONV_WIDTH)),
            *riders.in_specs,
        ],
        out_specs=[
            pl.BlockSpec((None, tt, CONV_WIDTH), lambda b, i: (b, i, 0)),
            pl.BlockSpec((None, CONV_HIST, CONV_WIDTH), lambda b, i: (b, 0, 0)),
            *riders.out_specs,
        ],
        out_shape=[jax.ShapeDtypeStruct((bsz, seq, CONV_WIDTH), BF16),
                   jax.ShapeDtypeStruct((bsz, CONV_HIST, CONV_WIDTH), F32), *riders.out_shapes],
        scratch_shapes=[pltpu.VMEM((CONV_LCH, CONV_PITCH * (CONV_HIST + tt), LANES), F32),
                        pltpu.VMEM((tt, CONV_WIDTH), F32)],
        compiler_params=_params(("arbitrary", "arbitrary")),
        name="conv_prompt",
    )(z, z, buf32, pw_b.reshape(1, -1), dw_w, dw_b.reshape(1, -1), ln_g.reshape(1, -1), ln_b.reshape(1, -1),
      *riders.inputs)
    return out, hist[:, CONV_HIST - (CONV_K - 1):, :], cast


def _conv_step_kernel(a_ref, b_ref, buf_ref, pwb_ref, dww_ref, dwb_ref, lng_ref, lnb_ref, o_ref, hist_ref, acc_scr, *,
                      bb):
    hist_len = CONV_K - 1
    glu = _conv_glu(a_ref, b_ref, pwb_ref)
    w_hist = dww_ref[0:hist_len, :]
    for b in range(bb):
        acc_scr[b:b + 1, :] = jnp.sum(buf_ref[b] * w_hist, axis=0, keepdims=True)
        hist_ref[b, 0:hist_len - 1, :] = buf_ref[b, 1:hist_len, :]
        hist_ref[b, hist_len - 1:hist_len, :] = glu[b:b + 1, :]
    acc = acc_scr[...] + dwb_ref[...] + dww_ref[hist_len:hist_len + 1, :] * glu
    o_ref[...] = _silu(_layer_norm(acc, lng_ref[...], lnb_ref[...])).astype(o_ref.dtype)


def _conv_step(z, bufs, layer, pw_b, dw_w, dw_b, ln_g, ln_b, *, bb):
    n = z.shape[0]
    hist_len = CONV_K - 1
    const = lambda shape: pl.BlockSpec(shape, lambda i: (0,) * len(shape))
    zblk = lambda col: pl.BlockSpec((bb, ZB), lambda i: (i, col))
    return pl.pallas_call(
        functools.partial(_conv_step_kernel, bb=bb),
        grid=(n // bb,),
        in_specs=[
            zblk(ZC_CONV), zblk(ZC_CONV + 1),
            pl.BlockSpec((None, bb, hist_len, CONV_WIDTH), lambda i: (layer, i, 0, 0)),
            const((1, 2 * CONV_WIDTH)),
            const((CONV_K, CONV_WIDTH)),
            const((1, CONV_WIDTH)),
            const((1, CONV_WIDTH)),
            const((1, CONV_WIDTH)),
        ],
        out_specs=[
            pl.BlockSpec((bb, CONV_WIDTH), lambda i: (i, 0)),
            pl.BlockSpec((bb, hist_len, CONV_WIDTH), lambda i: (i, 0, 0)),
        ],
        out_shape=[jax.ShapeDtypeStruct((n, CONV_WIDTH), BF16),
                   jax.ShapeDtypeStruct((n, hist_len, CONV_WIDTH), F32)],
        scratch_shapes=[pltpu.VMEM((bb, CONV_WIDTH), F32)],
        compiler_params=_params(("parallel",)),
        name="conv_step",
    )(z, z, bufs, pw_b.reshape(1, -1), dw_w, dw_b.reshape(1, -1), ln_g.reshape(1, -1), ln_b.reshape(1, -1))


def _rows_to_tiles(tile_ref, x, rows):
    for c in range(ROW_CHUNKS):
        tile_ref[pl.ds(c, rows, stride=ROW_TILE), :] = x[:, c * LANES:(c + 1) * LANES]


def _tiles_chunk(tile_ref, c, rows):
    return tile_ref[pl.ds(c, rows, stride=ROW_TILE), :]


def _tiles_to_rows(tile_ref, rows):
    return jnp.concatenate([_tiles_chunk(tile_ref, c, rows) for c in range(ROW_CHUNKS)], axis=-1)


def _merge_kernel(x_ref, s5_ref, ret_ref, conv_ref, *rest, tiled_u):
    gate_refs = rest[:N_BRANCH * D_MODEL // ZB]
    ps5_ref, pret_ref, pconv_ref, wout_ref, g_ref, h_ref, u_ref = rest[len(gate_refs):]
    per_branch = D_MODEL // ZB

    def gate(n):
        cols = [gate_refs[n * per_branch + j][...] for j in range(per_branch)]
        return jax.nn.sigmoid(jnp.concatenate(cols, axis=-1).astype(F32))

    merged = (gate(0) * jnp.dot(s5_ref[...], ps5_ref[...], preferred_element_type=F32)
              + gate(1) * jnp.dot(ret_ref[...], pret_ref[...], preferred_element_type=F32)
              + gate(2) * jnp.dot(conv_ref[...], pconv_ref[...], preferred_element_type=F32))
    h = x_ref[...] + _bdot(merged, wout_ref[...])
    h_ref[...] = h
    ms = jnp.mean(h * h, axis=-1, keepdims=True)
    u = h * lax.rsqrt(ms + EPS) * g_ref[...]
    if tiled_u:
        _rows_to_tiles(u_ref, u, u.shape[0])
    else:
        u_ref[...] = u.astype(u_ref.dtype)


def _merge(x, s5_out, ret_out, conv_out, z, s5_proj, ret_proj, conv_proj, w_out, norm_g, *, tm, tiled_u):
    bsz, seq, d = x.shape
    const = lambda shape: pl.BlockSpec(shape, lambda b, i: (0,) * len(shape))
    tok = lambda w, col=0: pl.BlockSpec((None, tm, w), lambda b, i: (b, i, col))
    n_i = seq // tm
    n_gate = N_BRANCH * d // ZB
    if tiled_u:
        u_spec = pl.BlockSpec((tm * ROW_TILE, LANES), lambda b, i: (b * n_i + i, 0))
        u_shape = jax.ShapeDtypeStruct((bsz * seq * ROW_TILE, LANES), F32)
    else:
        u_spec = tok(d)
        u_shape = jax.ShapeDtypeStruct((bsz, seq, d), BF16)
    return pl.pallas_call(
        functools.partial(_merge_kernel, tiled_u=tiled_u),
        grid=(bsz, seq // tm),
        in_specs=[
            tok(d),
            tok(S5_WIDTH),
            tok(RET_V),
            tok(CONV_WIDTH),
            *[tok(ZB, ZC_GATE + j) for j in range(n_gate)],
            const((S5_WIDTH, d)),
            const((RET_V, d)),
            const((CONV_WIDTH, d)),
            const((d, d)),
            const((1, d)),
        ],
        out_specs=[tok(d), u_spec],
        out_shape=[jax.ShapeDtypeStruct((bsz, seq, d), F32), u_shape],
        compiler_params=_params(("parallel", "parallel")),
        name="merge",
    )(x, s5_out, ret_out, conv_out, *([z] * n_gate), s5_proj, ret_proj, conv_proj, w_out, norm_g.reshape(1, d))


def _ffn_kernel(h_ref, u_ref, wg_ref, wu_ref, wd_ref, *rest, riders):
    rider_in, o_ref, rider_out = rest[:riders.n_in], rest[riders.n_in], rest[riders.n_in + 1:]
    f = pl.program_id(1)
    ub = u_ref[...].astype(BF16)
    gate = jnp.dot(ub, wg_ref[...], preferred_element_type=F32)
    up = jnp.dot(ub, wu_ref[...], preferred_element_type=F32)
    part = _bdot(_silu(gate) * up, wd_ref[...])

    @pl.when(f == 0)
    def _():
        o_ref[...] = h_ref[...] + part

    @pl.when(f > 0)
    def _():
        o_ref[...] = o_ref[...] + part

    riders.run(rider_in, rider_out)


def _ffn(h, u, w_gate, w_up, w_down, jobs=(), *, tm, tf):
    rows, d = h.shape
    dff = w_gate.shape[1]
    n_i, n_f = rows // tm, dff // tf
    tok = pl.BlockSpec((tm, d), lambda i, f: (i, 0))
    riders = _Riders(jobs, (n_i, n_f), lambda i, f: i * n_f + f)
    outs = pl.pallas_call(
        functools.partial(_ffn_kernel, riders=riders),
        grid=(n_i, n_f),
        in_specs=[tok, tok,
                  pl.BlockSpec((d, tf), lambda i, f: (0, f)),
                  pl.BlockSpec((d, tf), lambda i, f: (0, f)),
                  pl.BlockSpec((tf, d), lambda i, f: (f, 0)),
                  *riders.in_specs],
        out_specs=[tok, *riders.out_specs],
        out_shape=[jax.ShapeDtypeStruct((rows, d), F32), *riders.out_shapes],
        compiler_params=_params(("arbitrary", "arbitrary")),
        name="ffn_dense",
    )(h, u, w_gate, w_up, w_down, *riders.inputs)
    return outs[0], list(outs[1:])


def _split_bf16(x):
    hi = x.astype(BF16)
    return hi, (x - hi.astype(F32)).astype(BF16)


def _router_kernel(u_ref, rt_ref, tri_ref, idx_ref, wts_ref, rank_ref, cnt_ref, cnt_scr, *, tm):
    @pl.when(pl.program_id(0) == 0)
    def _():
        cnt_scr[...] = jnp.zeros_like(cnt_scr)

    u_hi, u_lo = _split_bf16(_tiles_to_rows(u_ref, tm))
    r_hi, r_lo = _split_bf16(rt_ref[...])
    dn = (((1,), (1,)), ((), ()))
    logits = (lax.dot_general(r_hi, u_hi, dn, preferred_element_type=F32)
              + lax.dot_general(r_lo, u_hi, dn, preferred_element_type=F32)
              + lax.dot_general(r_hi, u_lo, dn, preferred_element_type=F32))
    eidx = lax.broadcasted_iota(jnp.int32, logits.shape, 0)
    m1 = jnp.max(logits, axis=0, keepdims=True)
    i1 = jnp.min(jnp.where(logits == m1, eidx, N_EXPERTS), axis=0, keepdims=True)
    rest = jnp.where(eidx == i1, -jnp.inf, logits)
    m2 = jnp.max(rest, axis=0, keepdims=True)
    i2 = jnp.min(jnp.where(rest == m2, eidx, N_EXPERTS), axis=0, keepdims=True)
    e2 = jnp.exp(m2 - m1)
    w1 = 1.0 / (1.0 + e2)
    idx_ref[...] = jnp.concatenate([i1, i2], axis=0)
    wts_ref[...] = jnp.concatenate([w1, e2 * w1], axis=0)

    hit1 = eidx == i1
    hit2 = eidx == i2
    hits = jnp.where(hit1 | hit2, 1.0, 0.0)
    before = jnp.dot(hits.astype(BF16), tri_ref[...], preferred_element_type=F32) + cnt_scr[...]
    rank_ref[...] = jnp.concatenate(
        [jnp.sum(jnp.where(hit1, before, 0.0), axis=0, keepdims=True),
         jnp.sum(jnp.where(hit2, before, 0.0), axis=0, keepdims=True)], axis=0).astype(jnp.int32)
    cnt_scr[...] = cnt_scr[...] + jnp.sum(hits, axis=1, keepdims=True)
    cnt_ref[...] = cnt_scr[...]


def _router(u8, router, *, tm):
    rows = u8.shape[0] // ROW_TILE
    d = D_MODEL
    tri = (jnp.arange(tm)[:, None] < jnp.arange(tm)[None, :]).astype(BF16)
    const = lambda shape: pl.BlockSpec(shape, lambda i: (0,) * len(shape))
    lane = pl.BlockSpec((TOP_K, tm), lambda i: (0, i))
    return pl.pallas_call(
        functools.partial(_router_kernel, tm=tm),
        grid=(rows // tm,),
        in_specs=[pl.BlockSpec((tm * ROW_TILE, LANES), lambda i: (i, 0)), const((N_EXPERTS, d)), const((tm, tm))],
        out_specs=[lane, lane, lane, const((N_EXPERTS, 1))],
        out_shape=[jax.ShapeDtypeStruct((TOP_K, rows), jnp.int32), jax.ShapeDtypeStruct((TOP_K, rows), F32),
                   jax.ShapeDtypeStruct((TOP_K, rows), jnp.int32), jax.ShapeDtypeStruct((N_EXPERTS, 1), F32)],
        scratch_shapes=[pltpu.VMEM((N_EXPERTS, 1), F32)],
        compiler_params=_params(("arbitrary",)),
        name="moe_router",
    )(u8, router.T, tri)


DMA_UNROLL = 8


def _token_tile(ref, r):
    return ref.at[pl.ds(pl.multiple_of(r * ROW_TILE, ROW_TILE), ROW_TILE)]


def _dispatch_kernel(pos_ref, u_ref, init_ref, xs_ref, sem, *, tb):
    del init_ref

    def start(r, carry):
        for s in range(TOP_K):
            pltpu.make_async_copy(_token_tile(u_ref, r), _token_tile(xs_ref, pos_ref[s, r]),
                                  sem.at[s]).start(priority=s)
        return carry

    lax.fori_loop(0, tb, start, 0, unroll=DMA_UNROLL)
    for s in range(TOP_K):
        pltpu.make_async_copy(u_ref, xs_ref.at[pl.ds(0, tb * ROW_TILE)], sem.at[s]).wait()


def _dispatch(u8, pos, xs8, *, tb):
    rows = u8.shape[0] // ROW_TILE
    return pl.pallas_call(
        functools.partial(_dispatch_kernel, tb=tb),
        grid=(rows // tb,),
        in_specs=[
            pl.BlockSpec((TOP_K, tb), lambda i: (0, i), memory_space=pltpu.SMEM),
            pl.BlockSpec((tb * ROW_TILE, LANES), lambda i: (i, 0)),
            pl.BlockSpec(memory_space=pl.ANY),
        ],
        out_specs=pl.BlockSpec(memory_space=pl.ANY),
        out_shape=jax.ShapeDtypeStruct(xs8.shape, xs8.dtype),
        scratch_shapes=[pltpu.SemaphoreType.DMA((TOP_K,))],
        input_output_aliases={2: 0},
        compiler_params=_params(("arbitrary",)),
        name="moe_dispatch",
    )(pos, u8, xs8)


def _experts_kernel(te_ref, nu_ref, x_ref, wg_ref, wu_ref, wd_ref, o_ref, x_scr, acc_scr, *, tm):
    del te_ref
    i = pl.program_id(0)
    f = pl.program_id(1)

    @pl.when(i < nu_ref[0])
    def _():
        @pl.when(f == 0)
        def _():
            x_scr[...] = _tiles_to_rows(x_ref, tm).astype(BF16)

        xb = x_scr[...]
        gate = jnp.dot(xb, wg_ref[...], preferred_element_type=F32)
        up = jnp.dot(xb, wu_ref[...], preferred_element_type=F32)
        part = _bdot(_silu(gate) * up, wd_ref[...])

        @pl.when(f == 0)
        def _():
            acc_scr[...] = part

        @pl.when(f > 0)
        def _():
            acc_scr[...] = acc_scr[...] + part

        @pl.when(f == pl.num_programs(1) - 1)
        def _():
            _rows_to_tiles(o_ref, acc_scr[...], tm)

    @pl.when(i >= nu_ref[0])
    def _():
        o_ref[...] = jnp.zeros_like(o_ref)


def _experts(xs8, tile_expert, n_used, w_gate, w_up, w_down, *, tm, tf):
    rows = xs8.shape[0] // ROW_TILE
    d = D_MODEL
    dff = w_gate.shape[-1]
    n_f = dff // tf
    last_f = n_f - 1

    def row_map(i, f, te, nu):
        return (jnp.minimum(i, nu[0] - 1), 0)

    def fsel(i, f, nu):
        return jnp.where(i < nu[0], f, last_f)

    grid_spec = pltpu.PrefetchScalarGridSpec(
        num_scalar_prefetch=2,
        grid=(rows // tm, n_f),
        in_specs=[
            pl.BlockSpec((tm * ROW_TILE, LANES), row_map),
            pl.BlockSpec((None, d, tf), lambda i, f, te, nu: (te[i], 0, fsel(i, f, nu))),
            pl.BlockSpec((None, d, tf), lambda i, f, te, nu: (te[i], 0, fsel(i, f, nu))),
            pl.BlockSpec((None, tf, d), lambda i, f, te, nu: (te[i], fsel(i, f, nu), 0)),
        ],
        out_specs=pl.BlockSpec((tm * ROW_TILE, LANES), lambda i, f, te, nu: (i, 0)),
        scratch_shapes=[pltpu.VMEM((tm, d), BF16), pltpu.VMEM((tm, d), F32)],
    )
    return pl.pallas_call(
        functools.partial(_experts_kernel, tm=tm),
        grid_spec=grid_spec,
        out_shape=jax.ShapeDtypeStruct(xs8.shape, F32),
        compiler_params=_params(("arbitrary", "arbitrary")),
        name="moe_experts",
    )(tile_expert, n_used, xs8, w_gate, w_up, w_down)


def _combine_kernel(pos_ref, h_ref, wts_ref, g_ref, ys_ref, o_ref, y_scr, sem, *, tb, final_norm):
    def start(r, carry):
        for s in range(TOP_K):
            pltpu.make_async_copy(_token_tile(ys_ref, pos_ref[s, r]), _token_tile(y_scr.at[s], r),
                                  sem.at[s]).start(priority=s)
        return carry

    lax.fori_loop(0, tb, start, 0, unroll=DMA_UNROLL)
    for s in range(TOP_K):
        pltpu.make_async_copy(ys_ref.at[pl.ds(0, tb * ROW_TILE)], y_scr.at[s], sem.at[s]).wait()
    w1 = wts_ref[:, 0:1]
    w2 = wts_ref[:, 1:2]
    moe = jnp.concatenate([w1 * _tiles_chunk(y_scr.at[0], c, tb) + w2 * _tiles_chunk(y_scr.at[1], c, tb)
                           for c in range(ROW_CHUNKS)], axis=-1)
    out = h_ref[...] + moe
    if final_norm:
        ms = jnp.mean(out * out, axis=-1, keepdims=True)
        out = out * lax.rsqrt(ms + EPS) * g_ref[...]
    o_ref[...] = out


def _combine(h, pos, wts_t, ys8, final_g, *, tb):
    rows, d = h.shape
    final_norm = final_g is not None
    gain = final_g.reshape(1, d) if final_norm else jnp.ones((1, d), F32)
    return pl.pallas_call(
        functools.partial(_combine_kernel, tb=tb, final_norm=final_norm),
        grid=(rows // tb,),
        in_specs=[
            pl.BlockSpec((TOP_K, tb), lambda i: (0, i), memory_space=pltpu.SMEM),
            pl.BlockSpec((tb, d), lambda i: (i, 0)),
            pl.BlockSpec((tb, TOP_K), lambda i: (i, 0)),
            pl.BlockSpec((1, d), lambda i: (0, 0)),
            pl.BlockSpec(memory_space=pl.ANY),
        ],
        out_specs=pl.BlockSpec((tb, d), lambda i: (i, 0)),
        out_shape=jax.ShapeDtypeStruct((rows, d), F32),
        scratch_shapes=[pltpu.VMEM((TOP_K, tb * ROW_TILE, LANES), F32), pltpu.SemaphoreType.DMA((TOP_K,))],
        compiler_params=_params(("arbitrary",)),
        name="moe_combine",
    )(pos, h, wts_t, gain, ys8)


PROJ_TN = 2560
S5_TT = 128
FFN_TF = 1408
MOE_TM = 512
MOE_TF = 1792
MOE_TB = 1024


def _moe_tiles(n_rows):
    return n_rows // MOE_TM + N_EXPERTS


def _moe_buffer_shape(n_rows):
    return (_moe_tiles(n_rows) * MOE_TM * ROW_TILE, LANES)


def _moe(h_list, u8_list, router, w_gate, w_up, w_down, xs8, final_g):
    routes = []
    for u8 in u8_list:
        routes.append(_router(u8, router, tm=min(512, u8.shape[0] // ROW_TILE)))
    counts = [r[3][:, 0].astype(jnp.int32) for r in routes]
    total = sum(counts)
    padded = ((total + MOE_TM - 1) // MOE_TM) * MOE_TM
    ends = jnp.cumsum(padded)
    starts = ends - padded
    n_rows = sum(h.shape[0] for h in h_list) * TOP_K
    n_tiles = _moe_tiles(n_rows)
    assert xs8.shape == _moe_buffer_shape(n_rows)
    n_used = (ends[-1] // MOE_TM).astype(jnp.int32)
    tile_start = jnp.arange(n_tiles, dtype=jnp.int32) * MOE_TM
    tile_expert = jnp.sum((tile_start[:, None] >= ends[None, :]).astype(jnp.int32), axis=1)
    last_expert = jnp.sum((((n_used - 1) * MOE_TM) >= ends).astype(jnp.int32))
    tile_expert = jnp.where(jnp.arange(n_tiles) < n_used, tile_expert, last_expert).astype(jnp.int32)

    poss = []
    seen = jnp.zeros((N_EXPERTS,), jnp.int32)
    for u8, (idx, _, rank, _), cnt in zip(u8_list, routes, counts):
        base = starts + seen
        pos = rank
        for e in range(N_EXPERTS):
            pos = pos + jnp.where(idx == e, base[e], 0)
        poss.append(pos)
        seen = seen + cnt
        xs8 = _dispatch(u8, pos, xs8, tb=min(MOE_TB, u8.shape[0] // ROW_TILE))
    ys8 = _experts(xs8, tile_expert, n_used.reshape(1), w_gate, w_up, w_down, tm=MOE_TM, tf=MOE_TF)
    outs = []
    for h, pos, (_, wts, _, _) in zip(h_list, poss, routes):
        outs.append(_combine(h, pos, wts.T, ys8, final_g, tb=min(MOE_TB, h.shape[0])))
    return outs


def _rmsnorm_kernel(x_ref, g_ref, o_ref):
    x = x_ref[...]
    ms = jnp.mean(x * x, axis=-1, keepdims=True)
    o_ref[...] = x * lax.rsqrt(ms + EPS) * g_ref[...]


def _rmsnorm(x, g, *, tm):
    rows, d = x.shape
    return pl.pallas_call(
        _rmsnorm_kernel,
        grid=(rows // tm,),
        in_specs=[pl.BlockSpec((tm, d), lambda i: (i, 0)), pl.BlockSpec((1, d), lambda i: (0, 0))],
        out_specs=pl.BlockSpec((tm, d), lambda i: (i, 0)),
        out_shape=jax.ShapeDtypeStruct((rows, d), F32),
        compiler_params=_params(("parallel",)),
        name="final_norm",
    )(x, g.reshape(1, d))


def _pack_s5_state(re, im):
    n = re.shape[0]
    return jnp.concatenate([re.reshape(n, S5_LANES), im.reshape(n, S5_LANES)], axis=-1)


def _unpack_s5_state(h):
    n = h.shape[0]
    return (h[:, :S5_LANES].reshape(n, S5_GROUPS, S5_STATE), h[:, S5_LANES:].reshape(n, S5_GROUPS, S5_STATE))


def _mixer(x, s5_h0, ret_s0, conv_buf, pos_offset, p, *, single_step, tiled_u, layer=0, ret_stack=None, jobs=None):
    jobs = jobs or {}
    done = {}
    bsz, seq, d = x.shape
    z, done["proj"] = _norm_proj(x, p["norm_g"], p["w_in"], layer, jobs.get("proj", ()), tt=min(512, seq), tn=PROJ_TN)
    s5_out, s5_state, done["s5"] = _s5_branch(z, s5_h0, p["a_re"], p["a_im"], p["bmat"], p["cmat"], p["d_skip"],
                                              p["glu_w"], p["glu_b"], jobs.get("s5", ()), single_step=single_step,
                                              tt=1 if single_step else S5_TT)

    if single_step:
        z2 = z.reshape(seq, N_IN)
        ret_out, ret_state = _retention_step(z2, ret_s0, layer, ret_stack, pos_offset, bb=16)
        conv_out, conv_state = _conv_step(z2, conv_buf, layer, p["pw_b"], p["dw_w"], p["dw_b"], p["ln_g"], p["ln_b"],
                                          bb=32)
        ret_out = ret_out.reshape(bsz, seq, RET_V)
        conv_out = conv_out.reshape(bsz, seq, CONV_WIDTH)
    else:
        ret_out, ret_state, done["ret"] = _retention_prompt(z, ret_s0, pos_offset, jobs.get("ret", ()), tt=256)
        conv_out, conv_state, done["conv"] = _conv_prompt(z, conv_buf, p["pw_b"], p["dw_w"], p["dw_b"], p["ln_g"],
                                                          p["ln_b"], jobs.get("conv", ()), tt=256)

    h, u = _merge(x, s5_out, ret_out, conv_out, z, p["s5_proj"], p["ret_proj"], p["conv_proj"], p["w_out"],
                  p["norm_ffn_g"], tm=min(512, seq), tiled_u=tiled_u)
    return h, u, (s5_state, ret_state, conv_state), done


def kernel(x_prompt, x_sample, state_s5_re, state_s5_im, state_ret, state_conv, norm_mix_g, w_in, s5_lambda_re, s5_lambda_im, s5_log_dt, s5_b_re, s5_b_im, s5_c_re, s5_c_im, s5_d, s5_glu_w, s5_glu_b, s5_proj, ret_proj, conv_pw_b, conv_dw_w, conv_dw_b, conv_ln_g, conv_ln_b, conv_proj, w_out, norm_ffn_g, ffn_w_gate, ffn_w_up, ffn_w_down, moe_router, moe_w_gate, moe_w_up, moe_w_down, norm_final_g):
    depth = w_in.shape[0]
    bp, seq, d = x_prompt.shape
    ns = x_sample.shape[0]
    past_len = 16384
    bf = lambda a: a.astype(BF16)

    hp = x_prompt
    hs = x_sample.reshape(1, ns, d)
    zero_s5 = jnp.zeros((bp, 2 * S5_LANES), F32)
    zero_ret = jnp.zeros((bp, RET_HEADS, RET_DK, RET_DV), F32)
    zero_conv = jnp.zeros((bp, CONV_K - 1, CONV_WIDTH), F32)

    p_states, s_states = [], []
    ret_stack = None
    normed = False
    groups = depth * S5_GROUPS
    disc = _s5_discretize(s5_lambda_re.reshape(groups, S5_STATE), s5_lambda_im.reshape(groups, S5_STATE),
                          s5_log_dt.reshape(groups), s5_b_re.reshape(groups, S5_STATE, S5_GROUP),
                          s5_b_im.reshape(groups, S5_STATE, S5_GROUP))
    disc = [a.reshape((depth, S5_GROUPS) + a.shape[1:]) for a in disc]
    for l in range(depth):
        is_moe = l % 2 == 1
        a_re, a_im, bb_re, bb_im = (a[l] for a in disc)
        bmat, cmat = _s5_block_mats(bb_re, bb_im, s5_c_re[l], s5_c_im[l])
        p = dict(norm_g=norm_mix_g[l], w_in=w_in,
                 a_re=a_re, a_im=a_im, bmat=bmat, cmat=cmat, d_skip=s5_d[l], glu_w=bf(s5_glu_w[l]),
                 glu_b=s5_glu_b[l], pw_b=conv_pw_b[l], dw_w=conv_dw_w[l], dw_b=conv_dw_b[l], ln_g=conv_ln_g[l],
                 ln_b=conv_ln_b[l], s5_proj=bf(s5_proj[l]), ret_proj=bf(ret_proj[l]), conv_proj=bf(conv_proj[l]),
                 w_out=bf(w_out[l]), norm_ffn_g=norm_ffn_g[l])
        j = l // 2
        flat = lambda w: w.reshape(-1, w.shape[-1])
        if is_moe:
            jobs = {"conv": [_cast_job(flat(moe_w_up[j]))],
                    "proj": [(None, _moe_buffer_shape((bp * seq + ns) * TOP_K), F32)]}
        else:
            jobs = {"s5": [_cast_job(ffn_w_gate[j]), _cast_job(ffn_w_up[j]), _cast_job(ffn_w_down[j])]}
            if l + 1 < depth:
                jm = (l + 1) // 2
                jobs["conv"] = [_cast_job(flat(moe_w_gate[jm]))]
                jobs["ret"] = [_cast_job(flat(moe_w_down[jm]))]
        hp, up, st_p, done = _mixer(hp, zero_s5, zero_ret, zero_conv, 0.0, p, single_step=False, tiled_u=is_moe,
                                    layer=l, jobs=jobs)
        hs, us, st_s, _ = _mixer(hs, _pack_s5_state(state_s5_re[l], state_s5_im[l]), state_ret, state_conv,
                                 float(past_len), p, single_step=True, tiled_u=is_moe, layer=l, ret_stack=ret_stack)
        p_states.append(st_p)
        s_states.append(st_s)

        hp2, hs2 = hp.reshape(bp * seq, d), hs.reshape(ns, d)
        if is_moe:
            ret_stack = st_s[1]
            final_g = norm_final_g if l == depth - 1 else None
            normed = final_g is not None
            w_gate_b, w_down_b = (c.reshape(w.shape[1:]) for c, w in zip(early, (moe_w_gate, moe_w_down)))
            w_up_b = done["conv"][0].reshape(moe_w_up.shape[1:])
            hp2, hs2 = _moe([hp2, hs2], [up, us], moe_router[j], w_gate_b, w_up_b, w_down_b, done["proj"][0], final_g)
        else:
            wg, wu, wd = done["s5"]
            if l + 1 < depth:
                early = [done["conv"][0], done["ret"][0]]
            ffn_jobs = []
            if l == 0 and depth > 1:
                rows = st_s[1].size // RET_DV
                ffn_jobs = [(st_s[1].reshape(rows, RET_DV), (depth * rows, RET_DV), F32)]
            hp2, made = _ffn(hp2, up.reshape(bp * seq, d), wg, wu, wd, ffn_jobs, tm=512, tf=FFN_TF)
            if l == 0:
                ret_stack = made[0].reshape(state_ret.shape) if ffn_jobs else st_s[1][None]
            else:
                ret_stack = st_s[1]
            hs2, _ = _ffn(hs2, us.reshape(ns, d), wg, wu, wd, tm=ns, tf=FFN_TF)
        hp = hp2.reshape(bp, seq, d)
        hs = hs2.reshape(1, ns, d)

    if normed:
        y_prompt, y_sample = hp, hs.reshape(ns, 1, d)
    else:
        y_prompt = _rmsnorm(hp.reshape(bp * seq, d), norm_final_g, tm=512).reshape(bp, seq, d)
        y_sample = _rmsnorm(hs.reshape(ns, d), norm_final_g, tm=ns).reshape(ns, 1, d)

    def stack_s5(states):
        s5 = [_unpack_s5_state(s[0]) for s in states]
        return jnp.stack([a for a, _ in s5]), jnp.stack([b for _, b in s5])

    p_re, p_im = stack_s5(p_states)
    s_re, s_im = stack_s5(s_states)
    p_ret = jnp.stack([s[1] for s in p_states])
    p_conv = jnp.stack([s[2] for s in p_states])
    s_conv = jnp.stack([s[2] for s in s_states])
    return (y_prompt, y_sample, p_re, p_im, p_ret, p_conv, s_re, s_im, ret_stack, s_conv)
```

```python
import functools
import math

import jax
import jax.numpy as jnp
from jax import lax
from jax.experimental import pallas as pl
from jax.experimental.pallas import tpu as pltpu

F32 = jnp.float32
BF16 = jnp.bfloat16

D_MODEL = 1024
S5_WIDTH = 512
S5_GROUP = 16
S5_GROUPS = 32
S5_STATE = 64
S5_LANES = S5_GROUPS * S5_STATE
RET_HEADS = 4
RET_DK = 128
RET_DV = 256
RET_QK = RET_HEADS * RET_DK
RET_V = RET_HEADS * RET_DV
RET_CHUNK = 128
ROPE_BASE = 10000.0
CONV_WIDTH = 512
CONV_K = 31
CONV_HIST = 32
N_EXPERTS = 8
TOP_K = 2
N_BRANCH = 3
EPS = 1e-6
N_IN = S5_WIDTH + 2 * RET_QK + 2 * RET_V + 2 * CONV_WIDTH + N_BRANCH * D_MODEL
ZB = 512
ZC_S5 = 0
ZC_Q = ZC_S5 + S5_WIDTH // ZB
ZC_K = ZC_Q + RET_QK // ZB
ZC_V = ZC_K + RET_QK // ZB
ZC_G = ZC_V + RET_V // ZB
ZC_CONV = ZC_G + RET_V // ZB
ZC_GATE = ZC_CONV + 2 * CONV_WIDTH // ZB
HEADS_PER_ZB = ZB // RET_DV

ROW_TILE = 8
LANES = 128
ROW_CHUNKS = D_MODEL // LANES
VMEM_LIMIT = 48 * 1024 * 1024
ROW_SPLIT = 2


def _params(sem):
    return pltpu.CompilerParams(dimension_semantics=sem, vmem_limit_bytes=VMEM_LIMIT)


def _silu(x):
    return x * jax.nn.sigmoid(x)


def _bdot(a, b):
    return jnp.dot(a.astype(BF16), b.astype(BF16), preferred_element_type=F32)


def _norm_proj_kernel(x_ref, g_ref, w_ref, *rest, riders):
    rider_in, o_ref = rest[:riders.n_in], rest[riders.n_in]
    rider_out, w_scr = rest[riders.n_in + 1:-1], rest[-1]
    riders.run(rider_in, rider_out)

    @pl.when((pl.program_id(1) == 0) & (pl.program_id(2) == 0))
    def _():
        w_scr[...] = w_ref[...].astype(BF16)

    half = x_ref.shape[0] // ROW_SPLIT
    for r in range(ROW_SPLIT):
        rows = slice(r * half, (r + 1) * half)
        x = x_ref[rows, :]
        ms = jnp.mean(x * x, axis=-1, keepdims=True)
        u = (x * lax.rsqrt(ms + EPS) * g_ref[...]).astype(BF16)
        o_ref[rows, :] = jnp.dot(u, w_scr[...], preferred_element_type=F32).astype(o_ref.dtype)


def _norm_proj(x, g, w_all, layer, jobs=(), *, tt, tn):
    bsz, seq, d = x.shape
    n = w_all.shape[2]
    grid = (n // tn, bsz, seq // tt)
    riders = _Riders(jobs, grid, lambda j, b, i: (j * grid[1] + b) * grid[2] + i)
    outs = pl.pallas_call(
        functools.partial(_norm_proj_kernel, riders=riders),
        grid=grid,
        in_specs=[
            pl.BlockSpec((None, tt, d), lambda j, b, i: (b, i, 0)),
            pl.BlockSpec((1, d), lambda j, b, i: (0, 0)),
            pl.BlockSpec((None, d, tn), lambda j, b, i: (layer, 0, j)),
            *riders.in_specs,
        ],
        out_specs=[pl.BlockSpec((None, tt, tn), lambda j, b, i: (b, i, j)), *riders.out_specs],
        out_shape=[jax.ShapeDtypeStruct((bsz, seq, n), BF16), *riders.out_shapes],
        scratch_shapes=[pltpu.VMEM((d, tn), BF16)],
        compiler_params=_params(("arbitrary", "arbitrary", "arbitrary")),
        name="norm_proj",
    )(x, g.reshape(1, d), w_all, *riders.inputs)
    return outs[0], list(outs[1:])


def _s5_disc_kernel(lre_ref, lim_ref, ldt_ref, bre_ref, bim_ref, are_ref, aim_ref, ore_ref, oim_ref):
    lam_re = lre_ref[...]
    lam_im = lim_ref[...]
    dt = jnp.exp(ldt_ref[...])
    mag = jnp.exp(lam_re * dt)
    ang = lam_im * dt
    lbar_re = mag * jnp.cos(ang)
    lbar_im = mag * jnp.sin(ang)
    den = lam_re * lam_re + lam_im * lam_im
    nr = lbar_re - 1.0
    f_re = (nr * lam_re + lbar_im * lam_im) / den
    f_im = (lbar_im * lam_re - nr * lam_im) / den
    b_re = bre_ref[...]
    b_im = bim_ref[...]
    are_ref[...] = lbar_re
    aim_ref[...] = lbar_im
    ore_ref[...] = f_re * b_re - f_im * b_im
    oim_ref[...] = f_re * b_im + f_im * b_re


def _s5_discretize(lam_re, lam_im, log_dt, b_re, b_im):
    g, n = lam_re.shape
    p = b_re.shape[-1]
    rows = g * n
    col = lambda a: a.reshape(rows, 1)
    ldt = jnp.broadcast_to(log_dt[:, None], (g, n))
    outs = pl.pallas_call(
        _s5_disc_kernel,
        out_shape=[jax.ShapeDtypeStruct((rows, 1), F32), jax.ShapeDtypeStruct((rows, 1), F32),
                   jax.ShapeDtypeStruct((rows, p), F32), jax.ShapeDtypeStruct((rows, p), F32)],
        name="s5_discretize",
    )(col(lam_re), col(lam_im), col(ldt), b_re.reshape(rows, p), b_im.reshape(rows, p))
    a_re, a_im, bb_re, bb_im = outs
    return a_re.reshape(g, n), a_im.reshape(g, n), bb_re.reshape(g, n, p), bb_im.reshape(g, n, p)


S5_KCH = 128
S5_NCHUNK = S5_WIDTH // S5_KCH
S5_GPC = S5_KCH // S5_GROUP
S5_SPC = S5_GPC * S5_STATE


def _s5_block_mats(bbar_re, bbar_im, c_re, c_im):
    eye = jnp.eye(S5_GPC, dtype=F32)

    def in_blocks(bb):
        t = bb.reshape(S5_NCHUNK, S5_GPC, S5_STATE, S5_GROUP)
        m = jnp.einsum("cgnp,gh->cgphn", t, eye)
        return m.reshape(S5_NCHUNK, S5_KCH, S5_SPC)

    def out_blocks(cc):
        t = cc.reshape(S5_NCHUNK, S5_GPC, S5_GROUP, S5_STATE)
        m = jnp.einsum("cgpn,gh->cgnhp", t, eye)
        return m.reshape(S5_NCHUNK, S5_SPC, S5_KCH)

    bmat = jnp.concatenate([in_blocks(bbar_re), in_blocks(bbar_im)], axis=-1).astype(BF16)
    cmat = jnp.stack([out_blocks(c_re), -out_blocks(c_im)], axis=1).astype(BF16)
    return bmat, cmat


def _gelu_tanh(x):
    return 0.5 * x * (1.0 + jnp.tanh(math.sqrt(2.0 / math.pi) * (x + 0.044715 * (x * x * x))))


class _Riders:
    def __init__(self, jobs, grid, step_index):
        self.steps = math.prod(grid)
        self.n_axes = len(grid)
        self.step_index = step_index
        self.inputs, self.in_specs, self.out_specs, self.out_shapes, self.src_steps = [], [], [], [], []
        for src, shape, dtype in jobs:
            slab = shape[0] // self.steps
            assert slab * self.steps == shape[0] and slab % 16 == 0
            n_src = 0 if src is None else src.shape[0] // slab
            if src is not None:
                assert n_src * slab == src.shape[0] and src.shape[1] == shape[1]
                self.inputs.append(src)
                self.in_specs.append(pl.BlockSpec(
                    (slab, shape[1]), lambda *idx, n=n_src: (jnp.minimum(step_index(*idx[:self.n_axes]), n - 1), 0)))
            self.out_specs.append(pl.BlockSpec((slab, shape[1]), lambda *idx: (step_index(*idx[:self.n_axes]), 0)))
            self.out_shapes.append(jax.ShapeDtypeStruct(shape, dtype))
            self.src_steps.append(n_src)

    @property
    def n_in(self):
        return len(self.inputs)

    @property
    def n_out(self):
        return len(self.out_shapes)

    def run(self, in_refs, out_refs):
        step = self.step_index(*(pl.program_id(a) for a in range(self.n_axes)))
        srcs = iter(in_refs)
        for dst, n_src in zip(out_refs, self.src_steps):
            if n_src == 0:
                dst[...] = jnp.zeros_like(dst)
                continue
            src = next(srcs)
            if n_src == self.steps:
                dst[...] = src[...].astype(dst.dtype)
                continue

            @pl.when(step < n_src)
            def _(src=src, dst=dst):
                dst[...] = src[...].astype(dst.dtype)

            @pl.when(step >= n_src)
            def _(dst=dst):
                dst[...] = jnp.zeros_like(dst)


def _cast_job(w):
    return (w, w.shape, BF16)


def _s5_kernel(u_ref, h0_ref, are_ref, aim_ref, bmat_ref, cmat_ref, d_ref, gw_ref, gb_ref, *rest,
               nb, tt, lane_chunk, riders):
    rider_in, rest = rest[:riders.n_in], rest[riders.n_in:]
    o_ref, hout_ref = rest[:2]
    rider_out = rest[2:2 + riders.n_out]
    hs_scr, h_scr, io_scr = rest[2 + riders.n_out:]
    riders.run(rider_in, rider_out)

    @pl.when(pl.program_id(0) == 0)
    def _():
        h_scr[...] = h0_ref[...]

    def seq_rows(b):
        return pl.ds(b, tt, stride=nb)

    def lanes(c):
        return slice(c * S5_KCH, (c + 1) * S5_KCH)

    if tt == 1:
        for c in range(S5_NCHUNK):
            io_scr[c] = u_ref[:, lanes(c)].astype(F32)
    else:
        for b in range(nb):
            for c in range(S5_NCHUNK):
                io_scr[c, seq_rows(b), :] = u_ref[b, :, lanes(c)].astype(F32)

    u = jnp.concatenate([io_scr[c] for c in range(S5_NCHUNK)], axis=-1)
    for c in range(S5_NCHUNK):
        bu = jnp.dot(io_scr[c].astype(BF16), bmat_ref[c], preferred_element_type=F32)
        hs_scr[:, c * S5_SPC:(c + 1) * S5_SPC] = bu[:, :S5_SPC]
        hs_scr[:, S5_LANES + c * S5_SPC:S5_LANES + (c + 1) * S5_SPC] = bu[:, S5_SPC:]

    for lc in range(S5_LANES // lane_chunk):
        re_sl = slice(lc * lane_chunk, (lc + 1) * lane_chunk)
        im_sl = slice(S5_LANES + lc * lane_chunk, S5_LANES + (lc + 1) * lane_chunk)
        a_re = jnp.broadcast_to(are_ref[:, re_sl], (nb, lane_chunk))
        a_im = jnp.broadcast_to(aim_ref[:, re_sl], (nb, lane_chunk))

        def step(t, carry):
            h_re, h_im = carry
            r0 = t * nb if isinstance(t, int) else pl.multiple_of(t * nb, nb)
            n_re = a_re * h_re - a_im * h_im + hs_scr[pl.ds(r0, nb), re_sl]
            n_im = a_re * h_im + a_im * h_re + hs_scr[pl.ds(r0, nb), im_sl]
            hs_scr[pl.ds(r0, nb), re_sl] = n_re
            hs_scr[pl.ds(r0, nb), im_sl] = n_im
            return n_re, n_im

        carry = (h_scr[:, re_sl], h_scr[:, im_sl])
        if tt == 1:
            carry = step(0, carry)
        else:
            carry = lax.fori_loop(0, tt, step, carry, unroll=4)
        h_scr[:, re_sl] = carry[0]
        h_scr[:, im_sl] = carry[1]

    hout_ref[...] = h_scr[...]

    ys = []
    for c in range(S5_NCHUNK):
        h_re = hs_scr[:, c * S5_SPC:(c + 1) * S5_SPC].astype(BF16)
        h_im = hs_scr[:, S5_LANES + c * S5_SPC:S5_LANES + (c + 1) * S5_SPC].astype(BF16)
        ys.append(jnp.dot(h_re, cmat_ref[c, 0], preferred_element_type=F32)
                  + jnp.dot(h_im, cmat_ref[c, 1], preferred_element_type=F32))
    y = jnp.concatenate(ys, axis=-1) + d_ref[...] * u
    z = _gelu_tanh(y)
    gate = jnp.dot(z.astype(BF16), gw_ref[...], preferred_element_type=F32) + gb_ref[...]
    out = z * jax.nn.sigmoid(gate)
    if tt == 1:
        o_ref[...] = out.astype(o_ref.dtype)
    else:
        for c in range(S5_NCHUNK):
            io_scr[c] = out[:, lanes(c)]
        for b in range(nb):
            for c in range(S5_NCHUNK):
                o_ref[b, :, lanes(c)] = io_scr[c, seq_rows(b), :].astype(o_ref.dtype)


def _s5_branch(z, h0, a_re, a_im, bmat, cmat, d_skip, glu_w, glu_b, jobs=(), *, single_step, tt):
    bsz, seq, _ = z.shape
    if single_step:
        assert bsz == 1 and tt == 1
        nb = seq
        in_spec = pl.BlockSpec((None, nb, S5_WIDTH), lambda i: (0, 0, ZC_S5))
        out_spec = pl.BlockSpec((None, nb, S5_WIDTH), lambda i: (0, 0, 0))
        grid = (1,)
    else:
        nb = bsz
        in_spec = pl.BlockSpec((nb, tt, S5_WIDTH), lambda i: (0, i, ZC_S5))
        out_spec = pl.BlockSpec((nb, tt, S5_WIDTH), lambda i: (0, i, 0))
        grid = (seq // tt,)
    rblk = tt * nb
    lane_chunk = 1024 if nb <= 8 else 512
    const = lambda shape: pl.BlockSpec(shape, lambda i: (0,) * len(shape))
    riders = _Riders(jobs, grid, lambda i: i)
    outs = pl.pallas_call(
        functools.partial(_s5_kernel, nb=nb, tt=tt, lane_chunk=lane_chunk, riders=riders),
        grid=grid,
        in_specs=[
            in_spec,
            const((nb, 2 * S5_LANES)),
            const((1, S5_LANES)),
            const((1, S5_LANES)),
            const(bmat.shape),
            const(cmat.shape),
            const((1, S5_WIDTH)),
            const((S5_WIDTH, S5_WIDTH)),
            const((1, S5_WIDTH)),
            *riders.in_specs,
        ],
        out_specs=[out_spec, const((nb, 2 * S5_LANES)), *riders.out_specs],
        out_shape=[jax.ShapeDtypeStruct((bsz, seq, S5_WIDTH), BF16),
                   jax.ShapeDtypeStruct((nb, 2 * S5_LANES), F32), *riders.out_shapes],
        scratch_shapes=[pltpu.VMEM((rblk, 2 * S5_LANES), F32), pltpu.VMEM((nb, 2 * S5_LANES), F32),
                        pltpu.VMEM((S5_NCHUNK, rblk, S5_KCH), F32)],
        compiler_params=_params(("arbitrary",)),
        name="s5_branch",
    )(z, h0, a_re.reshape(1, S5_LANES), a_im.reshape(1, S5_LANES), bmat, cmat,
      d_skip.reshape(1, S5_WIDTH), glu_w, glu_b.reshape(1, S5_WIDTH), *riders.inputs)
    return outs[0], outs[1], list(outs[2:])


def _rope_tables(pos):
    half = RET_DK // 2
    freqs = ROPE_BASE ** (-jnp.arange(half, dtype=F32) / half)
    ang = pos[:, None] * freqs[None, :]
    cos = jnp.cos(ang)
    sin = jnp.sin(ang)
    return jnp.concatenate([cos, cos], axis=-1), jnp.concatenate([-sin, sin], axis=-1)


def _rope(x, cos, sin):
    return x * cos + pltpu.roll(x, RET_DK // 2, 1) * sin


def _group_norm(o):
    mu = jnp.mean(o, axis=-1, keepdims=True)
    d = o - mu
    var = jnp.mean(d * d, axis=-1, keepdims=True)
    return d * lax.rsqrt(var + EPS)


def _retention_tables(chunk):
    log_gamma = jnp.log(1.0 - 2.0 ** (-5.0 - jnp.arange(RET_HEADS, dtype=F32)))
    idx = jnp.arange(chunk, dtype=F32)
    diff = idx[:, None] - idx[None, :]
    decay = jnp.where(diff >= 0, jnp.exp(jnp.maximum(diff, 0.0)[None] * log_gamma[:, None, None]), 0.0)
    cross = jnp.exp((idx + 1.0)[None, :] * log_gamma[:, None])[:, :, None]
    kdec = jnp.exp((chunk - 1.0 - idx)[None, :] * log_gamma[:, None])[:, :, None]
    full = jnp.exp(chunk * log_gamma)
    return decay, cross, kdec, full


def _head_cols(refs, h, rows):
    lo = (h % HEADS_PER_ZB) * RET_DV
    return refs[h // HEADS_PER_ZB][rows, lo:lo + RET_DV].astype(F32)


def _retention_kernel(q_ref, k_ref, v0_ref, v1_ref, g0_ref, g1_ref, cos_ref, sin_ref, s0_ref, decay_ref, cross_ref,
                      kdec_ref, full_ref, *rest, n_chunks, riders):
    rider_in, rest = rest[:riders.n_in], rest[riders.n_in:]
    o_ref, sout_ref = rest[:2]
    rider_out, s_scr = rest[2:-1], rest[-1]
    riders.run(rider_in, rider_out)

    @pl.when(pl.program_id(1) == 0)
    def _():
        s_scr[...] = s0_ref[...]

    for c in range(n_chunks):
        rows = slice(c * RET_CHUNK, (c + 1) * RET_CHUNK)
        cos = cos_ref[rows, :]
        sin = sin_ref[rows, :]
        for h in range(RET_HEADS):
            qk_cols = slice(h * RET_DK, (h + 1) * RET_DK)
            v_cols = slice(h * RET_DV, (h + 1) * RET_DV)
            qh = _rope(q_ref[rows, qk_cols].astype(F32), cos, sin)
            kh = _rope(k_ref[rows, qk_cols].astype(F32), cos, sin) * (RET_DK ** -0.5)
            vb = _head_cols((v0_ref, v1_ref), h, rows).astype(BF16)
            qb = qh.astype(BF16)
            state = s_scr[h]
            inner = lax.dot_general(qb, kh.astype(BF16), (((1,), (1,)), ((), ())),
                                    preferred_element_type=F32) * decay_ref[h]
            out = (jnp.dot(inner.astype(BF16), vb, preferred_element_type=F32)
                   + jnp.dot(qb, state.astype(BF16), preferred_element_type=F32) * cross_ref[h])
            kd = (kh * kdec_ref[h]).astype(BF16)
            s_scr[h] = full_ref[h] * state + jnp.dot(kd.T, vb, preferred_element_type=F32)
            gate = _head_cols((g0_ref, g1_ref), h, rows)
            o_ref[rows, v_cols] = (_silu(gate) * _group_norm(out)).astype(o_ref.dtype)

    sout_ref[...] = s_scr[...]


def _retention_prompt(z, state0, pos_offset, jobs=(), *, tt):
    bsz, seq, _ = z.shape
    assert seq % RET_CHUNK == 0 and tt % RET_CHUNK == 0
    cos, sin = _rope_tables(jnp.arange(seq, dtype=F32) + pos_offset)
    decay, cross, kdec, full = _retention_tables(RET_CHUNK)
    full = jnp.broadcast_to(full[:, None, None], (RET_HEADS, 1, RET_DV))
    const = lambda shape: pl.BlockSpec(shape, lambda b, i: (0,) * len(shape))
    zblk = lambda col: pl.BlockSpec((None, tt, ZB), lambda b, i: (b, i, col))
    n_i = seq // tt
    riders = _Riders(jobs, (bsz, n_i), lambda b, i: b * n_i + i)
    outs = pl.pallas_call(
        functools.partial(_retention_kernel, n_chunks=tt // RET_CHUNK, riders=riders),
        grid=(bsz, n_i),
        in_specs=[
            zblk(ZC_Q), zblk(ZC_K), zblk(ZC_V), zblk(ZC_V + 1), zblk(ZC_G), zblk(ZC_G + 1),
            pl.BlockSpec((tt, RET_DK), lambda b, i: (i, 0)),
            pl.BlockSpec((tt, RET_DK), lambda b, i: (i, 0)),
            pl.BlockSpec((None, RET_HEADS, RET_DK, RET_DV), lambda b, i: (b, 0, 0, 0)),
            const((RET_HEADS, RET_CHUNK, RET_CHUNK)),
            const((RET_HEADS, RET_CHUNK, 1)),
            const((RET_HEADS, RET_CHUNK, 1)),
            const((RET_HEADS, 1, RET_DV)),
            *riders.in_specs,
        ],
        out_specs=[
            pl.BlockSpec((None, tt, RET_V), lambda b, i: (b, i, 0)),
            pl.BlockSpec((None, RET_HEADS, RET_DK, RET_DV), lambda b, i: (b, 0, 0, 0)),
            *riders.out_specs,
        ],
        out_shape=[jax.ShapeDtypeStruct((bsz, seq, RET_V), BF16),
                   jax.ShapeDtypeStruct((bsz, RET_HEADS, RET_DK, RET_DV), F32), *riders.out_shapes],
        scratch_shapes=[pltpu.VMEM((RET_HEADS, RET_DK, RET_DV), F32)],
        compiler_params=_params(("arbitrary", "arbitrary")),
        name="retention_prompt",
    )(z, z, z, z, z, z, cos, sin, state0, decay, cross, kdec, full, *riders.inputs)
    return outs[0], outs[1], list(outs[2:])


def _retention_step_kernel(q_ref, k_ref, v0_ref, v1_ref, g0_ref, g1_ref, cos_ref, sin_ref, s_ref, gam_ref, *rest,
                           bb, layer):
    del layer
    o_ref, sout_ref, o_scr = rest[-3:]
    cos = cos_ref[...]
    sin = sin_ref[...]
    for h in range(RET_HEADS):
        qk_cols = slice(h * RET_DK, (h + 1) * RET_DK)
        v_cols = slice(h * RET_DV, (h + 1) * RET_DV)
        qh = _rope(q_ref[:, qk_cols].astype(F32), cos, sin)
        kh = _rope(k_ref[:, qk_cols].astype(F32), cos, sin) * (RET_DK ** -0.5)
        qk = jnp.sum(qh * kh, axis=-1, keepdims=True)
        q_t = qh.T
        k_t = kh.T
        gamma = gam_ref[h]
        v_all = _head_cols((v0_ref, v1_ref), h, slice(None))
        for b in range(bb):
            state = s_ref[b, h]
            vrow = v_all[b:b + 1, :]
            qs = jnp.sum(q_t[:, b:b + 1] * state, axis=0, keepdims=True)
            o_scr[b:b + 1, v_cols] = qk[b:b + 1, :] * vrow + qs * gamma
            sout_ref[b, h] = gamma * state + k_t[:, b:b + 1] * vrow
    for h in range(RET_HEADS):
        v_cols = slice(h * RET_DV, (h + 1) * RET_DV)
        gate = _head_cols((g0_ref, g1_ref), h, slice(None))
        o_ref[:, v_cols] = (_silu(gate) * _group_norm(o_scr[:, v_cols])).astype(o_ref.dtype)


def _retention_step(z, states, layer, stack, pos, *, bb):
    n = z.shape[0]
    cos, sin = _rope_tables(jnp.full((1,), pos, F32))
    log_gamma = jnp.log(1.0 - 2.0 ** (-5.0 - jnp.arange(RET_HEADS, dtype=F32)))
    gam = jnp.broadcast_to(jnp.exp(log_gamma)[:, None, None], (RET_HEADS, 1, RET_DV))
    const = lambda shape: pl.BlockSpec(shape, lambda i: (0,) * len(shape))
    slab = pl.BlockSpec((None, bb, RET_HEADS, RET_DK, RET_DV), lambda i: (layer, i, 0, 0, 0))
    zblk = lambda col: pl.BlockSpec((bb, ZB), lambda i: (i, col))
    in_specs = [
        zblk(ZC_Q), zblk(ZC_K), zblk(ZC_V), zblk(ZC_V + 1), zblk(ZC_G), zblk(ZC_G + 1),
        const((1, RET_DK)),
        const((1, RET_DK)),
        slab,
        const((RET_HEADS, 1, RET_DV)),
    ]
    args = [z, z, z, z, z, z, cos, sin, states, gam]
    if stack is None:
        state_spec = pl.BlockSpec((bb, RET_HEADS, RET_DK, RET_DV), lambda i: (i, 0, 0, 0))
        state_shape = jax.ShapeDtypeStruct(states.shape[1:], F32)
        aliases = {}
    else:
        in_specs.append(pl.BlockSpec(memory_space=pl.ANY))
        args.append(stack)
        state_spec = slab
        state_shape = jax.ShapeDtypeStruct(stack.shape, F32)
        aliases = {len(args) - 1: 1}
    return pl.pallas_call(
        functools.partial(_retention_step_kernel, bb=bb, layer=layer),
        grid=(n // bb,),
        in_specs=in_specs,
        out_specs=[pl.BlockSpec((bb, RET_V), lambda i: (i, 0)), state_spec],
        out_shape=[jax.ShapeDtypeStruct((n, RET_V), BF16), state_shape],
        scratch_shapes=[pltpu.VMEM((bb, RET_V), F32)],
        input_output_aliases=aliases,
        compiler_params=_params(("parallel",)),
        name="retention_step",
    )(*args)


def _layer_norm(y, g, b):
    mu = jnp.mean(y, axis=-1, keepdims=True)
    d = y - mu
    var = jnp.mean(d * d, axis=-1, keepdims=True)
    return d * lax.rsqrt(var + EPS) * g + b


CONV_RB = 128
CONV_PITCH = 2
CONV_LCH = CONV_WIDTH // LANES


def _conv_glu(a_ref, b_ref, pwb_ref):
    a = a_ref[...].astype(F32) + pwb_ref[:, :CONV_WIDTH]
    b = b_ref[...].astype(F32) + pwb_ref[:, CONV_WIDTH:]
    return a * jax.nn.sigmoid(b)


def _conv_kernel(a_ref, b_ref, buf_ref, pwb_ref, dww_ref, dwb_ref, lng_ref, lnb_ref, *rest, tt, riders):
    rider_in, rest = rest[:riders.n_in], rest[riders.n_in:]
    o_ref, hist_ref = rest[:2]
    rider_out, (x_scr, y_scr) = rest[2:-2], rest[-2:]
    riders.run(rider_in, rider_out)
    _conv_body(a_ref, b_ref, buf_ref, pwb_ref, dww_ref, dwb_ref, lng_ref, lnb_ref, o_ref, hist_ref, x_scr, y_scr, tt=tt)


def _conv_body(a_ref, b_ref, buf_ref, pwb_ref, dww_ref, dwb_ref, lng_ref, lnb_ref, o_ref, hist_ref, x_scr, y_scr,
               *, tt):
    i = pl.program_id(1)

    def rows(start, n):
        return pl.ds(CONV_PITCH * start, n, stride=CONV_PITCH)

    def lanes(c):
        return slice(c * LANES, (c + 1) * LANES)

    @pl.when(i == 0)
    def _():
        for c in range(CONV_LCH):
            x_scr[c, rows(0, CONV_HIST), :] = buf_ref[:, lanes(c)]

    @pl.when(i > 0)
    def _():
        for c in range(CONV_LCH):
            x_scr[c, rows(0, CONV_HIST), :] = x_scr[c, rows(tt, CONV_HIST), :]

    glu = _conv_glu(a_ref, b_ref, pwb_ref)
    for c in range(CONV_LCH):
        x_scr[c, rows(CONV_HIST, tt), :] = glu[:, lanes(c)]
    hist_ref[...] = glu[tt - CONV_HIST:, :]

    off = CONV_HIST - (CONV_K - 1)
    for c in range(CONV_LCH):
        def row_block(r, carry, c=c):
            base = r * CONV_RB
            n_grp = CONV_RB // ROW_TILE
            accs = [jnp.broadcast_to(dwb_ref[:, lanes(c)], (ROW_TILE, LANES))] * n_grp
            for m in range(CONV_RB - ROW_TILE + CONV_K):
                win = x_scr[c, rows(base + (off + m), ROW_TILE), :]
                for k in range(m % ROW_TILE, CONV_K, ROW_TILE):
                    j = (m - k) // ROW_TILE
                    if 0 <= j < n_grp:
                        accs[j] = accs[j] + dww_ref[k:k + 1, lanes(c)] * win
            y_scr[pl.ds(pl.multiple_of(base, CONV_RB), CONV_RB), lanes(c)] = jnp.concatenate(accs, axis=0)
            return carry

        lax.fori_loop(0, tt // CONV_RB, row_block, 0)

    o_ref[...] = _silu(_layer_norm(y_scr[...], lng_ref[...], lnb_ref[...])).astype(o_ref.dtype)


def _conv_prompt(z, buf, pw_b, dw_w, dw_b, ln_g, ln_b, jobs=(), *, tt):
    bsz, seq, _ = z.shape
    assert seq >= CONV_HIST and tt >= CONV_HIST
    buf32 = jnp.pad(buf, ((0, 0), (CONV_HIST - (CONV_K - 1), 0), (0, 0)))
    const = lambda shape: pl.BlockSpec(shape, lambda b, i: (0,) * len(shape))
    zblk = lambda col: pl.BlockSpec((None, tt, ZB), lambda b, i: (b, i, col))
    n_i = seq // tt
    riders = _Riders(jobs, (bsz, n_i), lambda b, i: b * n_i + i)
    out, hist, *cast = pl.pallas_call(
        functools.partial(_conv_kernel, tt=tt, riders=riders),
        grid=(bsz, n_i),
        in_specs=[
            zblk(ZC_CONV), zblk(ZC_CONV + 1),
            pl.BlockSpec((None, CONV_HIST, CONV_WIDTH), lambda b, i: (b, 0, 0)),
            const((1, 2 * CONV_WIDTH)),
            const((CONV_K, CONV_WIDTH)),
            const((1, CONV_WIDTH)),
            const((1, CONV_WIDTH)),
            const((1, CONV_WIDTH)),
            *riders.in_specs,
        ],
        out_specs=[
            pl.BlockSpec((None, tt, CONV_WIDTH), lambda b, i: (b, i, 0)),
            pl.BlockSpec((None, CONV_HIST, CONV_WIDTH), lambda b, i: (b, 0, 0)),
            *riders.out_specs,
        ],
        out_shape=[jax.ShapeDtypeStruct((bsz, seq, CONV_WIDTH), BF16),
                   jax.ShapeDtypeStruct((bsz, CONV_HIST, CONV_WIDTH), F32), *riders.out_shapes],
        scratch_shapes=[pltpu.VMEM((CONV_LCH, CONV_PITCH * (CONV_HIST + tt), LANES), F32),
                        pltpu.VMEM((tt, CONV_WIDTH), F32)],
        compiler_params=_params(("arbitrary", "arbitrary")),
        name="conv_prompt",
    )(z, z, buf32, pw_b.reshape(1, -1), dw_w, dw_b.reshape(1, -1), ln_g.reshape(1, -1), ln_b.reshape(1, -1),
      *riders.inputs)
    return out, hist[:, CONV_HIST - (CONV_K - 1):, :], cast


def _conv_step_kernel(a_ref, b_ref, buf_ref, pwb_ref, dww_ref, dwb_ref, lng_ref, lnb_ref, o_ref, hist_ref):
    hist_len = CONV_K - 1
    glu = _conv_glu(a_ref, b_ref, pwb_ref)
    acc = dwb_ref[...] + dww_ref[hist_len:hist_len + 1, :] * glu
    for k in range(hist_len):
        acc = acc + dww_ref[k:k + 1, :] * buf_ref[k]
    o_ref[...] = _silu(_layer_norm(acc, lng_ref[...], lnb_ref[...])).astype(o_ref.dtype)
    for k in range(hist_len - 1):
        hist_ref[k] = buf_ref[k + 1]
    hist_ref[hist_len - 1] = glu


def _conv_step(z, bufs, layer, pw_b, dw_w, dw_b, ln_g, ln_b, *, bb):
    n = z.shape[0]
    hist_len = CONV_K - 1
    const = lambda shape: pl.BlockSpec(shape, lambda i: (0,) * len(shape))
    zblk = lambda col: pl.BlockSpec((bb, ZB), lambda i: (i, col))
    return pl.pallas_call(
        _conv_step_kernel,
        grid=(n // bb,),
        in_specs=[
            zblk(ZC_CONV), zblk(ZC_CONV + 1),
            pl.BlockSpec((None, hist_len, bb, CONV_WIDTH), lambda i: (layer, 0, i, 0)),
            const((1, 2 * CONV_WIDTH)),
            const((CONV_K, CONV_WIDTH)),
            const((1, CONV_WIDTH)),
            const((1, CONV_WIDTH)),
            const((1, CONV_WIDTH)),
        ],
        out_specs=[
            pl.BlockSpec((bb, CONV_WIDTH), lambda i: (i, 0)),
            pl.BlockSpec((hist_len, bb, CONV_WIDTH), lambda i: (0, i, 0)),
        ],
        out_shape=[jax.ShapeDtypeStruct((n, CONV_WIDTH), BF16),
                   jax.ShapeDtypeStruct((hist_len, n, CONV_WIDTH), F32)],
        compiler_params=_params(("parallel",)),
        name="conv_step",
    )(z, z, bufs, pw_b.reshape(1, -1), dw_w, dw_b.reshape(1, -1), ln_g.reshape(1, -1), ln_b.reshape(1, -1))


def _rows_to_tiles(tile_ref, x, rows):
    for c in range(ROW_CHUNKS):
        tile_ref[pl.ds(c, rows, stride=ROW_TILE), :] = x[:, c * LANES:(c + 1) * LANES]


def _tiles_chunk(tile_ref, c, rows):
    return tile_ref[pl.ds(c, rows, stride=ROW_TILE), :]


def _tiles_to_rows(tile_ref, rows):
    return jnp.concatenate([_tiles_chunk(tile_ref, c, rows) for c in range(ROW_CHUNKS)], axis=-1)


def _merge_kernel(x_ref, s5_ref, ret_ref, conv_ref, *rest, tiled_u):
    gate_refs = rest[:N_BRANCH * D_MODEL // ZB]
    ps5_ref, pret_ref, pconv_ref, wout_ref, g_ref, h_ref, u_ref = rest[len(gate_refs):]
    per_branch = D_MODEL // ZB

    def gate(n):
        cols = [gate_refs[n * per_branch + j][...] for j in range(per_branch)]
        return jax.nn.sigmoid(jnp.concatenate(cols, axis=-1).astype(F32))

    merged = (gate(0) * jnp.dot(s5_ref[...], ps5_ref[...], preferred_element_type=F32)
              + gate(1) * jnp.dot(ret_ref[...], pret_ref[...], preferred_element_type=F32)
              + gate(2) * jnp.dot(conv_ref[...], pconv_ref[...], preferred_element_type=F32))
    h = x_ref[...] + _bdot(merged, wout_ref[...])
    h_ref[...] = h
    ms = jnp.mean(h * h, axis=-1, keepdims=True)
    u = h * lax.rsqrt(ms + EPS) * g_ref[...]
    if tiled_u:
        _rows_to_tiles(u_ref, u, u.shape[0])
    else:
        u_ref[...] = u.astype(u_ref.dtype)


def _merge(x, s5_out, ret_out, conv_out, z, s5_proj, ret_proj, conv_proj, w_out, norm_g, *, tm, tiled_u):
    bsz, seq, d = x.shape
    const = lambda shape: pl.BlockSpec(shape, lambda b, i: (0,) * len(shape))
    tok = lambda w, col=0: pl.BlockSpec((None, tm, w), lambda b, i: (b, i, col))
    n_i = seq // tm
    n_gate = N_BRANCH * d // ZB
    if tiled_u:
        u_spec = pl.BlockSpec((tm * ROW_TILE, LANES), lambda b, i: (b * n_i + i, 0))
        u_shape = jax.ShapeDtypeStruct((bsz * seq * ROW_TILE, LANES), F32)
    else:
        u_spec = tok(d)
        u_shape = jax.ShapeDtypeStruct((bsz, seq, d), BF16)
    return pl.pallas_call(
        functools.partial(_merge_kernel, tiled_u=tiled_u),
        grid=(bsz, seq // tm),
        in_specs=[
            tok(d),
            tok(S5_WIDTH),
            tok(RET_V),
            tok(CONV_WIDTH),
            *[tok(ZB, ZC_GATE + j) for j in range(n_gate)],
            const((S5_WIDTH, d)),
            const((RET_V, d)),
            const((CONV_WIDTH, d)),
            const((d, d)),
            const((1, d)),
        ],
        out_specs=[tok(d), u_spec],
        out_shape=[jax.ShapeDtypeStruct((bsz, seq, d), F32), u_shape],
        compiler_params=_params(("parallel", "parallel")),
        name="merge",
    )(x, s5_out, ret_out, conv_out, *([z] * n_gate), s5_proj, ret_proj, conv_proj, w_out, norm_g.reshape(1, d))


def _ffn_kernel(h_ref, u_ref, wg_ref, wu_ref, wd_ref, *rest, riders):
    rider_in, o_ref, rider_out = rest[:riders.n_in], rest[riders.n_in], rest[riders.n_in + 1:]
    f = pl.program_id(1)
    ub = u_ref[...].astype(BF16)
    gate = jnp.dot(ub, wg_ref[...], preferred_element_type=F32)
    up = jnp.dot(ub, wu_ref[...], preferred_element_type=F32)
    part = _bdot(_silu(gate) * up, wd_ref[...])

    @pl.when(f == 0)
    def _():
        o_ref[...] = h_ref[...] + part

    @pl.when(f > 0)
    def _():
        o_ref[...] = o_ref[...] + part

    riders.run(rider_in, rider_out)


def _ffn(h, u, w_gate, w_up, w_down, jobs=(), *, tm, tf):
    rows, d = h.shape
    dff = w_gate.shape[1]
    n_i, n_f = rows // tm, dff // tf
    tok = pl.BlockSpec((tm, d), lambda i, f: (i, 0))
    riders = _Riders(jobs, (n_i, n_f), lambda i, f: i * n_f + f)
    outs = pl.pallas_call(
        functools.partial(_ffn_kernel, riders=riders),
        grid=(n_i, n_f),
        in_specs=[tok, tok,
                  pl.BlockSpec((d, tf), lambda i, f: (0, f)),
                  pl.BlockSpec((d, tf), lambda i, f: (0, f)),
                  pl.BlockSpec((tf, d), lambda i, f: (f, 0)),
                  *riders.in_specs],
        out_specs=[tok, *riders.out_specs],
        out_shape=[jax.ShapeDtypeStruct((rows, d), F32), *riders.out_shapes],
        compiler_params=_params(("arbitrary", "arbitrary")),
        name="ffn_dense",
    )(h, u, w_gate, w_up, w_down, *riders.inputs)
    return outs[0], list(outs[1:])


def _split_bf16(x):
    hi = x.astype(BF16)
    return hi, (x - hi.astype(F32)).astype(BF16)


def _router_kernel(u_ref, rt_ref, tri_ref, idx_ref, wts_ref, rank_ref, cnt_ref, cnt_scr, *, tm):
    @pl.when(pl.program_id(0) == 0)
    def _():
        cnt_scr[...] = jnp.zeros_like(cnt_scr)

    u_hi, u_lo = _split_bf16(_tiles_to_rows(u_ref, tm))
    r_hi, r_lo = _split_bf16(rt_ref[...])
    dn = (((1,), (1,)), ((), ()))
    logits = (lax.dot_general(r_hi, u_hi, dn, preferred_element_type=F32)
              + lax.dot_general(r_lo, u_hi, dn, preferred_element_type=F32)
              + lax.dot_general(r_hi, u_lo, dn, preferred_element_type=F32))
    eidx = lax.broadcasted_iota(jnp.int32, logits.shape, 0)
    m1 = jnp.max(logits, axis=0, keepdims=True)
    i1 = jnp.min(jnp.where(logits == m1, eidx, N_EXPERTS), axis=0, keepdims=True)
    rest = jnp.where(eidx == i1, -jnp.inf, logits)
    m2 = jnp.max(rest, axis=0, keepdims=True)
    i2 = jnp.min(jnp.where(rest == m2, eidx, N_EXPERTS), axis=0, keepdims=True)
    e2 = jnp.exp(m2 - m1)
    w1 = 1.0 / (1.0 + e2)
    idx_ref[...] = jnp.concatenate([i1, i2], axis=0)
    wts_ref[...] = jnp.concatenate([w1, e2 * w1], axis=0)

    hit1 = eidx == i1
    hit2 = eidx == i2
    hits = jnp.where(hit1 | hit2, 1.0, 0.0)
    before = jnp.dot(hits.astype(BF16), tri_ref[...], preferred_element_type=F32) + cnt_scr[...]
    rank_ref[...] = jnp.concatenate(
        [jnp.sum(jnp.where(hit1, before, 0.0), axis=0, keepdims=True),
         jnp.sum(jnp.where(hit2, before, 0.0), axis=0, keepdims=True)], axis=0).astype(jnp.int32)
    cnt_scr[...] = cnt_scr[...] + jnp.sum(hits, axis=1, keepdims=True)
    cnt_ref[...] = cnt_scr[...]


def _router(u8, router, *, tm):
    rows = u8.shape[0] // ROW_TILE
    d = D_MODEL
    tri = (jnp.arange(tm)[:, None] < jnp.arange(tm)[None, :]).astype(BF16)
    const = lambda shape: pl.BlockSpec(shape, lambda i: (0,) * len(shape))
    lane = pl.BlockSpec((TOP_K, tm), lambda i: (0, i))
    return pl.pallas_call(
        functools.partial(_router_kernel, tm=tm),
        grid=(rows // tm,),
        in_specs=[pl.BlockSpec((tm * ROW_TILE, LANES), lambda i: (i, 0)), const((N_EXPERTS, d)), const((tm, tm))],
        out_specs=[lane, lane, lane, const((N_EXPERTS, 1))],
        out_shape=[jax.ShapeDtypeStruct((TOP_K, rows), jnp.int32), jax.ShapeDtypeStruct((TOP_K, rows), F32),
                   jax.ShapeDtypeStruct((TOP_K, rows), jnp.int32), jax.ShapeDtypeStruct((N_EXPERTS, 1), F32)],
        scratch_shapes=[pltpu.VMEM((N_EXPERTS, 1), F32)],
        compiler_params=_params(("arbitrary",)),
        name="moe_router",
    )(u8, router.T, tri)


DMA_UNROLL = 8


def _token_tile(ref, r):
    return ref.at[pl.ds(pl.multiple_of(r * ROW_TILE, ROW_TILE), ROW_TILE)]


def _dispatch_kernel(pos_ref, u_ref, init_ref, xs_ref, sem, *, tb):
    del init_ref

    def start(r, carry):
        for s in range(TOP_K):
            pltpu.make_async_copy(_token_tile(u_ref, r), _token_tile(xs_ref, pos_ref[s, r]),
                                  sem.at[s]).start(priority=s)
        return carry

    lax.fori_loop(0, tb, start, 0, unroll=DMA_UNROLL)
    for s in range(TOP_K):
        pltpu.make_async_copy(u_ref, xs_ref.at[pl.ds(0, tb * ROW_TILE)], sem.at[s]).wait()


def _dispatch(u8, pos, xs8, *, tb):
    rows = u8.shape[0] // ROW_TILE
    return pl.pallas_call(
        functools.partial(_dispatch_kernel, tb=tb),
        grid=(rows // tb,),
        in_specs=[
            pl.BlockSpec((TOP_K, tb), lambda i: (0, i), memory_space=pltpu.SMEM),
            pl.BlockSpec((tb * ROW_TILE, LANES), lambda i: (i, 0)),
            pl.BlockSpec(memory_space=pl.ANY),
        ],
        out_specs=pl.BlockSpec(memory_space=pl.ANY),
        out_shape=jax.ShapeDtypeStruct(xs8.shape, xs8.dtype),
        scratch_shapes=[pltpu.SemaphoreType.DMA((TOP_K,))],
        input_output_aliases={2: 0},
        compiler_params=_params(("arbitrary",)),
        name="moe_dispatch",
    )(pos, u8, xs8)


def _experts_kernel(te_ref, nu_ref, x_ref, wg_ref, wu_ref, wd_ref, o_ref, x_scr, acc_scr, *, tm):
    del te_ref
    i = pl.program_id(0)
    f = pl.program_id(1)

    def swiglu_rows(n):
        xb = x_scr[0:n, :]
        gate = jnp.dot(xb, wg_ref[...], preferred_element_type=F32)
        up = jnp.dot(xb, wu_ref[...], preferred_element_type=F32)
        part = _bdot(_silu(gate) * up, wd_ref[...])

        @pl.when(f == 0)
        def _():
            acc_scr[0:n, :] = part
            if n < tm:
                acc_scr[n:tm, :] = jnp.zeros((tm - n, D_MODEL), F32)

        @pl.when(f > 0)
        def _():
            acc_scr[0:n, :] = acc_scr[0:n, :] + part

    @pl.when(i < nu_ref[0])
    def _():
        @pl.when(f == 0)
        def _():
            x_scr[...] = _tiles_to_rows(x_ref, tm).astype(BF16)

        half = tm // 2
        valid = nu_ref[1 + i]

        @pl.when(valid > half)
        def _():
            swiglu_rows(tm)

        @pl.when(valid <= half)
        def _():
            swiglu_rows(half)

        @pl.when(f == pl.num_programs(1) - 1)
        def _():
            _rows_to_tiles(o_ref, acc_scr[...], tm)

    @pl.when(i >= nu_ref[0])
    def _():
        o_ref[...] = jnp.zeros_like(o_ref)


def _experts(xs8, tile_expert, n_used, w_gate, w_up, w_down, *, tm, tf):
    rows = xs8.shape[0] // ROW_TILE
    d = D_MODEL
    dff = w_gate.shape[-1]
    n_f = dff // tf
    last_f = n_f - 1

    def row_map(i, f, te, nu):
        return (jnp.minimum(i, nu[0] - 1), 0)

    def fsel(i, f, nu):
        return jnp.where(i < nu[0], f, last_f)

    grid_spec = pltpu.PrefetchScalarGridSpec(
        num_scalar_prefetch=2,
        grid=(rows // tm, n_f),
        in_specs=[
            pl.BlockSpec((tm * ROW_TILE, LANES), row_map),
            pl.BlockSpec((None, d, tf), lambda i, f, te, nu: (te[i], 0, fsel(i, f, nu))),
            pl.BlockSpec((None, d, tf), lambda i, f, te, nu: (te[i], 0, fsel(i, f, nu))),
            pl.BlockSpec((None, tf, d), lambda i, f, te, nu: (te[i], fsel(i, f, nu), 0)),
        ],
        out_specs=pl.BlockSpec((tm * ROW_TILE, LANES), lambda i, f, te, nu: (i, 0)),
        scratch_shapes=[pltpu.VMEM((tm, d), BF16), pltpu.VMEM((tm, d), F32)],
    )
    return pl.pallas_call(
        functools.partial(_experts_kernel, tm=tm),
        grid_spec=grid_spec,
        out_shape=jax.ShapeDtypeStruct(xs8.shape, F32),
        compiler_params=_params(("arbitrary", "arbitrary")),
        name="moe_experts",
    )(tile_expert, n_used, xs8, w_gate, w_up, w_down)


def _combine_kernel(pos_ref, h_ref, wts_ref, g_ref, ys_ref, o_ref, y_scr, sem, *, tb, final_norm):
    def start(r, carry):
        for s in range(TOP_K):
            pltpu.make_async_copy(_token_tile(ys_ref, pos_ref[s, r]), _token_tile(y_scr.at[s], r),
                                  sem.at[s]).start(priority=s)
        return carry

    lax.fori_loop(0, tb, start, 0, unroll=DMA_UNROLL)
    for s in range(TOP_K):
        pltpu.make_async_copy(ys_ref.at[pl.ds(0, tb * ROW_TILE)], y_scr.at[s], sem.at[s]).wait()
    w1 = wts_ref[:, 0:1]
    w2 = wts_ref[:, 1:2]
    moe = jnp.concatenate([w1 * _tiles_chunk(y_scr.at[0], c, tb) + w2 * _tiles_chunk(y_scr.at[1], c, tb)
                           for c in range(ROW_CHUNKS)], axis=-1)
    out = h_ref[...] + moe
    if final_norm:
        ms = jnp.mean(out * out, axis=-1, keepdims=True)
        out = out * lax.rsqrt(ms + EPS) * g_ref[...]
    o_ref[...] = out


def _combine(h, pos, wts_t, ys8, final_g, *, tb):
    rows, d = h.shape
    final_norm = final_g is not None
    gain = final_g.reshape(1, d) if final_norm else jnp.ones((1, d), F32)
    return pl.pallas_call(
        functools.partial(_combine_kernel, tb=tb, final_norm=final_norm),
        grid=(rows // tb,),
        in_specs=[
            pl.BlockSpec((TOP_K, tb), lambda i: (0, i), memory_space=pltpu.SMEM),
            pl.BlockSpec((tb, d), lambda i: (i, 0)),
            pl.BlockSpec((tb, TOP_K), lambda i: (i, 0)),
            pl.BlockSpec((1, d), lambda i: (0, 0)),
            pl.BlockSpec(memory_space=pl.ANY),
        ],
        out_specs=pl.BlockSpec((tb, d), lambda i: (i, 0)),
        out_shape=jax.ShapeDtypeStruct((rows, d), F32),
        scratch_shapes=[pltpu.VMEM((TOP_K, tb * ROW_TILE, LANES), F32), pltpu.SemaphoreType.DMA((TOP_K,))],
        compiler_params=_params(("arbitrary",)),
        name="moe_combine",
    )(pos, h, wts_t, gain, ys8)


PROJ_TN = 2560
S5_TT = 128
FFN_TF = 1408
MOE_TM = 512
MOE_TF = 1792
MOE_TB = 1024


def _moe_tiles(n_rows):
    return n_rows // MOE_TM + N_EXPERTS


def _moe_buffer_shape(n_rows):
    return (_moe_tiles(n_rows) * MOE_TM * ROW_TILE, LANES)


def _moe(h_list, u8_list, router, w_gate, w_up, w_down, xs8, final_g):
    routes = []
    for u8 in u8_list:
        routes.append(_router(u8, router, tm=min(512, u8.shape[0] // ROW_TILE)))
    counts = [r[3][:, 0].astype(jnp.int32) for r in routes]
    total = sum(counts)
    padded = ((total + MOE_TM - 1) // MOE_TM) * MOE_TM
    ends = jnp.cumsum(padded)
    starts = ends - padded
    n_rows = sum(h.shape[0] for h in h_list) * TOP_K
    n_tiles = _moe_tiles(n_rows)
    assert xs8.shape == _moe_buffer_shape(n_rows)
    n_used = (ends[-1] // MOE_TM).astype(jnp.int32)
    tile_start = jnp.arange(n_tiles, dtype=jnp.int32) * MOE_TM
    tile_expert = jnp.sum((tile_start[:, None] >= ends[None, :]).astype(jnp.int32), axis=1)
    last_expert = jnp.sum((((n_used - 1) * MOE_TM) >= ends).astype(jnp.int32))
    tile_expert = jnp.where(jnp.arange(n_tiles) < n_used, tile_expert, last_expert).astype(jnp.int32)

    poss = []
    seen = jnp.zeros((N_EXPERTS,), jnp.int32)
    for u8, (idx, _, rank, _), cnt in zip(u8_list, routes, counts):
        base = starts + seen
        pos = rank
        for e in range(N_EXPERTS):
            pos = pos + jnp.where(idx == e, base[e], 0)
        poss.append(pos)
        seen = seen + cnt
        xs8 = _dispatch(u8, pos, xs8, tb=min(MOE_TB, u8.shape[0] // ROW_TILE))
    group_rows = jnp.sum(jnp.where(jnp.arange(N_EXPERTS)[None, :] == tile_expert[:, None],
                                   (starts + total)[None, :], 0), axis=1)
    tile_valid = jnp.clip(group_rows - tile_start, 0, MOE_TM)
    tile_info = jnp.concatenate([n_used.reshape(1), tile_valid]).astype(jnp.int32)
    ys8 = _experts(xs8, tile_expert, tile_info, w_gate, w_up, w_down, tm=MOE_TM, tf=MOE_TF)
    outs = []
    for h, pos, (_, wts, _, _) in zip(h_list, poss, routes):
        outs.append(_combine(h, pos, wts.T, ys8, final_g, tb=min(MOE_TB, h.shape[0])))
    return outs


def _rmsnorm_kernel(x_ref, g_ref, o_ref):
    x = x_ref[...]
    ms = jnp.mean(x * x, axis=-1, keepdims=True)
    o_ref[...] = x * lax.rsqrt(ms + EPS) * g_ref[...]


def _rmsnorm(x, g, *, tm):
    rows, d = x.shape
    return pl.pallas_call(
        _rmsnorm_kernel,
        grid=(rows // tm,),
        in_specs=[pl.BlockSpec((tm, d), lambda i: (i, 0)), pl.BlockSpec((1, d), lambda i: (0, 0))],
        out_specs=pl.BlockSpec((tm, d), lambda i: (i, 0)),
        out_shape=jax.ShapeDtypeStruct((rows, d), F32),
        compiler_params=_params(("parallel",)),
        name="final_norm",
    )(x, g.reshape(1, d))


def _pack_s5_state(re, im):
    n = re.shape[0]
    return jnp.concatenate([re.reshape(n, S5_LANES), im.reshape(n, S5_LANES)], axis=-1)


def _unpack_s5_state(h):
    n = h.shape[0]
    return (h[:, :S5_LANES].reshape(n, S5_GROUPS, S5_STATE), h[:, S5_LANES:].reshape(n, S5_GROUPS, S5_STATE))


def _mixer(x, s5_h0, ret_s0, conv_buf, pos_offset, p, *, single_step, tiled_u, layer=0, ret_stack=None, jobs=None):
    jobs = jobs or {}
    done = {}
    bsz, seq, d = x.shape
    z, done["proj"] = _norm_proj(x, p["norm_g"], p["w_in"], layer, jobs.get("proj", ()), tt=min(512, seq), tn=PROJ_TN)
    s5_out, s5_state, done["s5"] = _s5_branch(z, s5_h0, p["a_re"], p["a_im"], p["bmat"], p["cmat"], p["d_skip"],
                                              p["glu_w"], p["glu_b"], jobs.get("s5", ()), single_step=single_step,
                                              tt=1 if single_step else S5_TT)

    if single_step:
        z2 = z.reshape(seq, N_IN)
        ret_out, ret_state = _retention_step(z2, ret_s0, layer, ret_stack, pos_offset, bb=16)
        conv_out, conv_state = _conv_step(z2, conv_buf, layer, p["pw_b"], p["dw_w"], p["dw_b"], p["ln_g"], p["ln_b"],
                                          bb=32)
        ret_out = ret_out.reshape(bsz, seq, RET_V)
        conv_out = conv_out.reshape(bsz, seq, CONV_WIDTH)
    else:
        ret_out, ret_state, done["ret"] = _retention_prompt(z, ret_s0, pos_offset, jobs.get("ret", ()), tt=256)
        conv_out, conv_state, done["conv"] = _conv_prompt(z, conv_buf, p["pw_b"], p["dw_w"], p["dw_b"], p["ln_g"],
                                                          p["ln_b"], jobs.get("conv", ()), tt=256)

    h, u = _merge(x, s5_out, ret_out, conv_out, z, p["s5_proj"], p["ret_proj"], p["conv_proj"], p["w_out"],
                  p["norm_ffn_g"], tm=min(512, seq), tiled_u=tiled_u)
    return h, u, (s5_state, ret_state, conv_state), done


def kernel(x_prompt, x_sample, state_s5_re, state_s5_im, state_ret, state_conv, norm_mix_g, w_in, s5_lambda_re, s5_lambda_im, s5_log_dt, s5_b_re, s5_b_im, s5_c_re, s5_c_im, s5_d, s5_glu_w, s5_glu_b, s5_proj, ret_proj, conv_pw_b, conv_dw_w, conv_dw_b, conv_ln_g, conv_ln_b, conv_proj, w_out, norm_ffn_g, ffn_w_gate, ffn_w_up, ffn_w_down, moe_router, moe_w_gate, moe_w_up, moe_w_down, norm_final_g):
    depth = w_in.shape[0]
    bp, seq, d = x_prompt.shape
    ns = x_sample.shape[0]
    past_len = 16384
    bf = lambda a: a.astype(BF16)

    hp = x_prompt
    hs = x_sample.reshape(1, ns, d)
    zero_s5 = jnp.zeros((bp, 2 * S5_LANES), F32)
    zero_ret = jnp.zeros((bp, RET_HEADS, RET_DK, RET_DV), F32)
    zero_conv = jnp.zeros((bp, CONV_K - 1, CONV_WIDTH), F32)

    conv_rows = jnp.transpose(state_conv, (0, 2, 1, 3))
    p_states, s_states = [], []
    ret_stack = None
    normed = False
    groups = depth * S5_GROUPS
    disc = _s5_discretize(s5_lambda_re.reshape(groups, S5_STATE), s5_lambda_im.reshape(groups, S5_STATE),
                          s5_log_dt.reshape(groups), s5_b_re.reshape(groups, S5_STATE, S5_GROUP),
                          s5_b_im.reshape(groups, S5_STATE, S5_GROUP))
    disc = [a.reshape((depth, S5_GROUPS) + a.shape[1:]) for a in disc]
    for l in range(depth):
        is_moe = l % 2 == 1
        a_re, a_im, bb_re, bb_im = (a[l] for a in disc)
        bmat, cmat = _s5_block_mats(bb_re, bb_im, s5_c_re[l], s5_c_im[l])
        p = dict(norm_g=norm_mix_g[l], w_in=w_in,
                 a_re=a_re, a_im=a_im, bmat=bmat, cmat=cmat, d_skip=s5_d[l], glu_w=bf(s5_glu_w[l]),
                 glu_b=s5_glu_b[l], pw_b=conv_pw_b[l], dw_w=conv_dw_w[l], dw_b=conv_dw_b[l], ln_g=conv_ln_g[l],
                 ln_b=conv_ln_b[l], s5_proj=bf(s5_proj[l]), ret_proj=bf(ret_proj[l]), conv_proj=bf(conv_proj[l]),
                 w_out=bf(w_out[l]), norm_ffn_g=norm_ffn_g[l])
        j = l // 2
        flat = lambda w: w.reshape(-1, w.shape[-1])
        if is_moe:
            jobs = {"conv": [_cast_job(flat(moe_w_up[j]))],
                    "proj": [(None, _moe_buffer_shape((bp * seq + ns) * TOP_K), F32)]}
        else:
            jobs = {"s5": [_cast_job(ffn_w_gate[j]), _cast_job(ffn_w_up[j]), _cast_job(ffn_w_down[j])]}
            if l + 1 < depth:
                jm = (l + 1) // 2
                jobs["conv"] = [_cast_job(flat(moe_w_gate[jm]))]
                jobs["ret"] = [_cast_job(flat(moe_w_down[jm]))]
        hp, up, st_p, done = _mixer(hp, zero_s5, zero_ret, zero_conv, 0.0, p, single_step=False, tiled_u=is_moe,
                                    layer=l, jobs=jobs)
        hs, us, st_s, _ = _mixer(hs, _pack_s5_state(state_s5_re[l], state_s5_im[l]), state_ret, conv_rows,
                                 float(past_len), p, single_step=True, tiled_u=is_moe, layer=l, ret_stack=ret_stack)
        p_states.append(st_p)
        s_states.append(st_s)

        hp2, hs2 = hp.reshape(bp * seq, d), hs.reshape(ns, d)
        if is_moe:
            ret_stack = st_s[1]
            final_g = norm_final_g if l == depth - 1 else None
            normed = final_g is not None
            w_gate_b, w_down_b = (c.reshape(w.shape[1:]) for c, w in zip(early, (moe_w_gate, moe_w_down)))
            w_up_b = done["conv"][0].reshape(moe_w_up.shape[1:])
            hp2, hs2 = _moe([hp2, hs2], [up, us], moe_router[j], w_gate_b, w_up_b, w_down_b, done["proj"][0], final_g)
        else:
            wg, wu, wd = done["s5"]
            if l + 1 < depth:
                early = [done["conv"][0], done["ret"][0]]
            ffn_jobs = []
            if l == 0 and depth > 1:
                rows = st_s[1].size // RET_DV
                ffn_jobs = [(st_s[1].reshape(rows, RET_DV), (depth * rows, RET_DV), F32)]
            hp2, made = _ffn(hp2, up.reshape(bp * seq, d), wg, wu, wd, ffn_jobs, tm=512, tf=FFN_TF)
            if l == 0:
                ret_stack = made[0].reshape(state_ret.shape) if ffn_jobs else st_s[1][None]
            else:
                ret_stack = st_s[1]
            hs2, _ = _ffn(hs2, us.reshape(ns, d), wg, wu, wd, tm=ns, tf=FFN_TF)
        hp = hp2.reshape(bp, seq, d)
        hs = hs2.reshape(1, ns, d)

    if normed:
        y_prompt, y_sample = hp, hs.reshape(ns, 1, d)
    else:
        y_prompt = _rmsnorm(hp.reshape(bp * seq, d), norm_final_g, tm=512).reshape(bp, seq, d)
        y_sample = _rmsnorm(hs.reshape(ns, d), norm_final_g, tm=ns).reshape(ns, 1, d)

    def stack_s5(states):
        s5 = [_unpack_s5_state(s[0]) for s in states]
        return jnp.stack([a for a, _ in s5]), jnp.stack([b for _, b in s5])

    p_re, p_im = stack_s5(p_states)
    s_re, s_im = stack_s5(s_states)
    p_ret = jnp.stack([s[1] for s in p_states])
    p_conv = jnp.stack([s[2] for s in p_states])
    s_conv = jnp.transpose(jnp.stack([s[2] for s in s_states]), (0, 2, 1, 3))
    return (y_prompt, y_sample, p_re, p_im, p_ret, p_conv, s_re, s_im, ret_stack, s_conv)
```

```python
import functools
import math

import jax
import jax.numpy as jnp
from jax import lax
from jax.experimental import pallas as pl
from jax.experimental.pallas import tpu as pltpu

F32 = jnp.float32
BF16 = jnp.bfloat16

D_MODEL = 1024
S5_WIDTH = 512
S5_GROUP = 16
S5_GROUPS = 32
S5_STATE = 64
S5_LANES = S5_GROUPS * S5_STATE
RET_HEADS = 4
RET_DK = 128
RET_DV = 256
RET_QK = RET_HEADS * RET_DK
RET_V = RET_HEADS * RET_DV
RET_CHUNK = 128
ROPE_BASE = 10000.0
CONV_WIDTH = 512
CONV_K = 31
CONV_HIST = 32
N_EXPERTS = 8
TOP_K = 2
N_BRANCH = 3
EPS = 1e-6
N_IN = S5_WIDTH + 2 * RET_QK + 2 * RET_V + 2 * CONV_WIDTH + N_BRANCH * D_MODEL
ZB = 512
ZC_S5 = 0
ZC_Q = ZC_S5 + S5_WIDTH // ZB
ZC_K = ZC_Q + RET_QK // ZB
ZC_V = ZC_K + RET_QK // ZB
ZC_G = ZC_V + RET_V // ZB
ZC_CONV = ZC_G + RET_V // ZB
ZC_GATE = ZC_CONV + 2 * CONV_WIDTH // ZB
HEADS_PER_ZB = ZB // RET_DV

ROW_TILE = 8
LANES = 128
ROW_CHUNKS = D_MODEL // LANES
VMEM_LIMIT = 48 * 1024 * 1024
ROW_SPLIT = 2


def _params(sem):
    return pltpu.CompilerParams(dimension_semantics=sem, vmem_limit_bytes=VMEM_LIMIT)


def _silu(x):
    return x * jax.nn.sigmoid(x)


def _bdot(a, b):
    return jnp.dot(a.astype(BF16), b.astype(BF16), preferred_element_type=F32)


def _norm_proj_kernel(x_ref, g_ref, w_ref, x2_ref, *rest, riders):
    rider_in, (o_ref, o2_ref) = rest[:riders.n_in], rest[riders.n_in:riders.n_in + 2]
    rider_out, w_scr = rest[riders.n_in + 2:-1], rest[-1]
    riders.run(rider_in, rider_out)

    def project(x):
        ms = jnp.mean(x * x, axis=-1, keepdims=True)
        u = (x * lax.rsqrt(ms + EPS) * g_ref[...]).astype(BF16)
        return jnp.dot(u, w_scr[...], preferred_element_type=F32).astype(BF16)

    @pl.when((pl.program_id(1) == 0) & (pl.program_id(2) == 0))
    def _():
        w_scr[...] = w_ref[...].astype(BF16)
        o2_ref[...] = project(x2_ref[...])

    half = x_ref.shape[0] // ROW_SPLIT
    for r in range(ROW_SPLIT):
        rows = slice(r * half, (r + 1) * half)
        o_ref[rows, :] = project(x_ref[rows, :])


def _norm_proj(x, x2, g, w_all, layer, jobs=(), *, tt, tn):
    bsz, seq, d = x.shape
    r2 = x2.shape[0]
    n = w_all.shape[2]
    grid = (n // tn, bsz, seq // tt)
    riders = _Riders(jobs, grid, lambda j, b, i: (j * grid[1] + b) * grid[2] + i)
    outs = pl.pallas_call(
        functools.partial(_norm_proj_kernel, riders=riders),
        grid=grid,
        in_specs=[
            pl.BlockSpec((None, tt, d), lambda j, b, i: (b, i, 0)),
            pl.BlockSpec((1, d), lambda j, b, i: (0, 0)),
            pl.BlockSpec((None, d, tn), lambda j, b, i: (layer, 0, j)),
            pl.BlockSpec((r2, d), lambda j, b, i: (0, 0)),
            *riders.in_specs,
        ],
        out_specs=[pl.BlockSpec((None, tt, tn), lambda j, b, i: (b, i, j)),
                   pl.BlockSpec((r2, tn), lambda j, b, i: (0, j)),
                   *riders.out_specs],
        out_shape=[jax.ShapeDtypeStruct((bsz, seq, n), BF16), jax.ShapeDtypeStruct((r2, n), BF16),
                   *riders.out_shapes],
        scratch_shapes=[pltpu.VMEM((d, tn), BF16)],
        compiler_params=_params(("arbitrary", "arbitrary", "arbitrary")),
        name="norm_proj",
    )(x, g.reshape(1, d), w_all, x2, *riders.inputs)
    return outs[0], outs[1], list(outs[2:])


def _s5_disc_kernel(lre_ref, lim_ref, ldt_ref, bre_ref, bim_ref, are_ref, aim_ref, ore_ref, oim_ref):
    lam_re = lre_ref[...]
    lam_im = lim_ref[...]
    dt = jnp.exp(ldt_ref[...])
    mag = jnp.exp(lam_re * dt)
    ang = lam_im * dt
    lbar_re = mag * jnp.cos(ang)
    lbar_im = mag * jnp.sin(ang)
    den = lam_re * lam_re + lam_im * lam_im
    nr = lbar_re - 1.0
    f_re = (nr * lam_re + lbar_im * lam_im) / den
    f_im = (lbar_im * lam_re - nr * lam_im) / den
    b_re = bre_ref[...]
    b_im = bim_ref[...]
    are_ref[...] = lbar_re
    aim_ref[...] = lbar_im
    ore_ref[...] = f_re * b_re - f_im * b_im
    oim_ref[...] = f_re * b_im + f_im * b_re


def _s5_discretize(lam_re, lam_im, log_dt, b_re, b_im):
    g, n = lam_re.shape
    p = b_re.shape[-1]
    rows = g * n
    col = lambda a: a.reshape(rows, 1)
    ldt = jnp.broadcast_to(log_dt[:, None], (g, n))
    outs = pl.pallas_call(
        _s5_disc_kernel,
        out_shape=[jax.ShapeDtypeStruct((rows, 1), F32), jax.ShapeDtypeStruct((rows, 1), F32),
                   jax.ShapeDtypeStruct((rows, p), F32), jax.ShapeDtypeStruct((rows, p), F32)],
        name="s5_discretize",
    )(col(lam_re), col(lam_im), col(ldt), b_re.reshape(rows, p), b_im.reshape(rows, p))
    a_re, a_im, bb_re, bb_im = outs
    return a_re.reshape(g, n), a_im.reshape(g, n), bb_re.reshape(g, n, p), bb_im.reshape(g, n, p)


S5_KCH = 128
S5_NCHUNK = S5_WIDTH // S5_KCH
S5_GPC = S5_KCH // S5_GROUP
S5_SPC = S5_GPC * S5_STATE


def _s5_block_mats(bbar_re, bbar_im, c_re, c_im):
    eye = jnp.eye(S5_GPC, dtype=F32)

    def in_blocks(bb):
        t = bb.reshape(S5_NCHUNK, S5_GPC, S5_STATE, S5_GROUP)
        m = jnp.einsum("cgnp,gh->cgphn", t, eye)
        return m.reshape(S5_NCHUNK, S5_KCH, S5_SPC)

    def out_blocks(cc):
        t = cc.reshape(S5_NCHUNK, S5_GPC, S5_GROUP, S5_STATE)
        m = jnp.einsum("cgpn,gh->cgnhp", t, eye)
        return m.reshape(S5_NCHUNK, S5_SPC, S5_KCH)

    bmat = jnp.concatenate([in_blocks(bbar_re), in_blocks(bbar_im)], axis=-1).astype(BF16)
    cmat = jnp.stack([out_blocks(c_re), -out_blocks(c_im)], axis=1).astype(BF16)
    return bmat, cmat


def _gelu_tanh(x):
    return 0.5 * x * (1.0 + jnp.tanh(math.sqrt(2.0 / math.pi) * (x + 0.044715 * (x * x * x))))


class _Riders:
    def __init__(self, jobs, grid, step_index):
        self.steps = math.prod(grid)
        self.n_axes = len(grid)
        self.step_index = step_index
        self.inputs, self.in_specs, self.out_specs, self.out_shapes, self.src_steps = [], [], [], [], []
        for src, shape, dtype in jobs:
            slab = shape[0] // self.steps
            assert slab * self.steps == shape[0] and slab % 16 == 0
            n_src = 0 if src is None else src.shape[0] // slab
            if src is not None:
                assert n_src * slab == src.shape[0] and src.shape[1] == shape[1]
                self.inputs.append(src)
                self.in_specs.append(pl.BlockSpec(
                    (slab, shape[1]), lambda *idx, n=n_src: (jnp.minimum(step_index(*idx[:self.n_axes]), n - 1), 0)))
            self.out_specs.append(pl.BlockSpec((slab, shape[1]), lambda *idx: (step_index(*idx[:self.n_axes]), 0)))
            self.out_shapes.append(jax.ShapeDtypeStruct(shape, dtype))
            self.src_steps.append(n_src)

    @property
    def n_in(self):
        return len(self.inputs)

    @property
    def n_out(self):
        return len(self.out_shapes)

    def run(self, in_refs, out_refs):
        step = self.step_index(*(pl.program_id(a) for a in range(self.n_axes)))
        srcs = iter(in_refs)
        for dst, n_src in zip(out_refs, self.src_steps):
            if n_src == 0:
                dst[...] = jnp.zeros_like(dst)
                continue
            src = next(srcs)
            if n_src == self.steps:
                dst[...] = src[...].astype(dst.dtype)
                continue

            @pl.when(step < n_src)
            def _(src=src, dst=dst):
                dst[...] = src[...].astype(dst.dtype)

            @pl.when(step >= n_src)
            def _(dst=dst):
                dst[...] = jnp.zeros_like(dst)


def _cast_job(w):
    return (w, w.shape, BF16)


def _s5_kernel(u_ref, h0_ref, are_ref, aim_ref, bmat_ref, cmat_ref, d_ref, gw_ref, gb_ref, *rest,
               nb, tt, lane_chunk, riders):
    rider_in, rest = rest[:riders.n_in], rest[riders.n_in:]
    o_ref, hout_ref = rest[:2]
    rider_out = rest[2:2 + riders.n_out]
    hs_scr, h_scr, io_scr = rest[2 + riders.n_out:]
    riders.run(rider_in, rider_out)

    @pl.when(pl.program_id(0) == 0)
    def _():
        h_scr[...] = h0_ref[...]

    def seq_rows(b):
        return pl.ds(b, tt, stride=nb)

    def lanes(c):
        return slice(c * S5_KCH, (c + 1) * S5_KCH)

    if tt == 1:
        for c in range(S5_NCHUNK):
            io_scr[c] = u_ref[:, lanes(c)].astype(F32)
    else:
        for b in range(nb):
            for c in range(S5_NCHUNK):
                io_scr[c, seq_rows(b), :] = u_ref[b, :, lanes(c)].astype(F32)

    u = jnp.concatenate([io_scr[c] for c in range(S5_NCHUNK)], axis=-1)
    for c in range(S5_NCHUNK):
        bu = jnp.dot(io_scr[c].astype(BF16), bmat_ref[c], preferred_element_type=F32)
        hs_scr[:, c * S5_SPC:(c + 1) * S5_SPC] = bu[:, :S5_SPC]
        hs_scr[:, S5_LANES + c * S5_SPC:S5_LANES + (c + 1) * S5_SPC] = bu[:, S5_SPC:]

    for lc in range(S5_LANES // lane_chunk):
        re_sl = slice(lc * lane_chunk, (lc + 1) * lane_chunk)
        im_sl = slice(S5_LANES + lc * lane_chunk, S5_LANES + (lc + 1) * lane_chunk)
        a_re = jnp.broadcast_to(are_ref[:, re_sl], (nb, lane_chunk))
        a_im = jnp.broadcast_to(aim_ref[:, re_sl], (nb, lane_chunk))

        def step(t, carry):
            h_re, h_im = carry
            r0 = t * nb if isinstance(t, int) else pl.multiple_of(t * nb, nb)
            n_re = a_re * h_re - a_im * h_im + hs_scr[pl.ds(r0, nb), re_sl]
            n_im = a_re * h_im + a_im * h_re + hs_scr[pl.ds(r0, nb), im_sl]
            hs_scr[pl.ds(r0, nb), re_sl] = n_re
            hs_scr[pl.ds(r0, nb), im_sl] = n_im
            return n_re, n_im

        carry = (h_scr[:, re_sl], h_scr[:, im_sl])
        if tt == 1:
            carry = step(0, carry)
        else:
            carry = lax.fori_loop(0, tt, step, carry, unroll=4)
        h_scr[:, re_sl] = carry[0]
        h_scr[:, im_sl] = carry[1]

    hout_ref[...] = h_scr[...]

    ys = []
    for c in range(S5_NCHUNK):
        h_re = hs_scr[:, c * S5_SPC:(c + 1) * S5_SPC].astype(BF16)
        h_im = hs_scr[:, S5_LANES + c * S5_SPC:S5_LANES + (c + 1) * S5_SPC].astype(BF16)
        ys.append(jnp.dot(h_re, cmat_ref[c, 0], preferred_element_type=F32)
                  + jnp.dot(h_im, cmat_ref[c, 1], preferred_element_type=F32))
    y = jnp.concatenate(ys, axis=-1) + d_ref[...] * u
    z = _gelu_tanh(y)
    gate = jnp.dot(z.astype(BF16), gw_ref[...], preferred_element_type=F32) + gb_ref[...]
    out = z * jax.nn.sigmoid(gate)
    if tt == 1:
        o_ref[...] = out.astype(o_ref.dtype)
    else:
        for c in range(S5_NCHUNK):
            io_scr[c] = out[:, lanes(c)]
        for b in range(nb):
            for c in range(S5_NCHUNK):
                o_ref[b, :, lanes(c)] = io_scr[c, seq_rows(b), :].astype(o_ref.dtype)


def _s5_branch(z, h0, a_re, a_im, bmat, cmat, d_skip, glu_w, glu_b, jobs=(), *, single_step, tt):
    bsz, seq, _ = z.shape
    if single_step:
        assert bsz == 1 and tt == 1
        nb = seq
        in_spec = pl.BlockSpec((None, nb, S5_WIDTH), lambda i: (0, 0, ZC_S5))
        out_spec = pl.BlockSpec((None, nb, S5_WIDTH), lambda i: (0, 0, 0))
        grid = (1,)
    else:
        nb = bsz
        in_spec = pl.BlockSpec((nb, tt, S5_WIDTH), lambda i: (0, i, ZC_S5))
        out_spec = pl.BlockSpec((nb, tt, S5_WIDTH), lambda i: (0, i, 0))
        grid = (seq // tt,)
    rblk = tt * nb
    lane_chunk = 1024 if nb <= 8 else 512
    const = lambda shape: pl.BlockSpec(shape, lambda i: (0,) * len(shape))
    riders = _Riders(jobs, grid, lambda i: i)
    outs = pl.pallas_call(
        functools.partial(_s5_kernel, nb=nb, tt=tt, lane_chunk=lane_chunk, riders=riders),
        grid=grid,
        in_specs=[
            in_spec,
            const((nb, 2 * S5_LANES)),
            const((1, S5_LANES)),
            const((1, S5_LANES)),
            const(bmat.shape),
            const(cmat.shape),
            const((1, S5_WIDTH)),
            const((S5_WIDTH, S5_WIDTH)),
            const((1, S5_WIDTH)),
            *riders.in_specs,
        ],
        out_specs=[out_spec, const((nb, 2 * S5_LANES)), *riders.out_specs],
        out_shape=[jax.ShapeDtypeStruct((bsz, seq, S5_WIDTH), BF16),
                   jax.ShapeDtypeStruct((nb, 2 * S5_LANES), F32), *riders.out_shapes],
        scratch_shapes=[pltpu.VMEM((rblk, 2 * S5_LANES), F32), pltpu.VMEM((nb, 2 * S5_LANES), F32),
                        pltpu.VMEM((S5_NCHUNK, rblk, S5_KCH), F32)],
        compiler_params=_params(("arbitrary",)),
        name="s5_branch",
    )(z, h0, a_re.reshape(1, S5_LANES), a_im.reshape(1, S5_LANES), bmat, cmat,
      d_skip.reshape(1, S5_WIDTH), glu_w, glu_b.reshape(1, S5_WIDTH), *riders.inputs)
    return outs[0], outs[1], list(outs[2:])


def _rope_tables(pos):
    half = RET_DK // 2
    freqs = ROPE_BASE ** (-jnp.arange(half, dtype=F32) / half)
    ang = pos[:, None] * freqs[None, :]
    cos = jnp.cos(ang)
    sin = jnp.sin(ang)
    return jnp.concatenate([cos, cos], axis=-1), jnp.concatenate([-sin, sin], axis=-1)


def _rope(x, cos, sin):
    return x * cos + pltpu.roll(x, RET_DK // 2, 1) * sin


def _group_norm(o):
    mu = jnp.mean(o, axis=-1, keepdims=True)
    d = o - mu
    var = jnp.mean(d * d, axis=-1, keepdims=True)
    return d * lax.rsqrt(var + EPS)


def _retention_tables(chunk):
    log_gamma = jnp.log(1.0 - 2.0 ** (-5.0 - jnp.arange(RET_HEADS, dtype=F32)))
    idx = jnp.arange(chunk, dtype=F32)
    diff = idx[:, None] - idx[None, :]
    decay = jnp.where(diff >= 0, jnp.exp(jnp.maximum(diff, 0.0)[None] * log_gamma[:, None, None]), 0.0)
    cross = jnp.exp((idx + 1.0)[None, :] * log_gamma[:, None])[:, :, None]
    kdec = jnp.exp((chunk - 1.0 - idx)[None, :] * log_gamma[:, None])[:, :, None]
    full = jnp.exp(chunk * log_gamma)
    return decay, cross, kdec, full


def _head_cols(refs, h, rows):
    lo = (h % HEADS_PER_ZB) * RET_DV
    return refs[h // HEADS_PER_ZB][rows, lo:lo + RET_DV].astype(F32)


def _retention_kernel(q_ref, k_ref, v0_ref, v1_ref, g0_ref, g1_ref, cos_ref, sin_ref, s0_ref, decay_ref, cross_ref,
                      kdec_ref, full_ref, *rest, n_chunks, riders):
    rider_in, rest = rest[:riders.n_in], rest[riders.n_in:]
    o_ref, sout_ref = rest[:2]
    rider_out, s_scr = rest[2:-1], rest[-1]
    riders.run(rider_in, rider_out)

    @pl.when(pl.program_id(1) == 0)
    def _():
        s_scr[...] = s0_ref[...]

    for c in range(n_chunks):
        rows = slice(c * RET_CHUNK, (c + 1) * RET_CHUNK)
        cos = cos_ref[rows, :]
        sin = sin_ref[rows, :]
        for h in range(RET_HEADS):
            qk_cols = slice(h * RET_DK, (h + 1) * RET_DK)
            v_cols = slice(h * RET_DV, (h + 1) * RET_DV)
            qh = _rope(q_ref[rows, qk_cols].astype(F32), cos, sin)
            kh = _rope(k_ref[rows, qk_cols].astype(F32), cos, sin) * (RET_DK ** -0.5)
            vb = _head_cols((v0_ref, v1_ref), h, rows).astype(BF16)
            qb = qh.astype(BF16)
            state = s_scr[h]
            inner = lax.dot_general(qb, kh.astype(BF16), (((1,), (1,)), ((), ())),
                                    preferred_element_type=F32) * decay_ref[h]
            out = (jnp.dot(inner.astype(BF16), vb, preferred_element_type=F32)
                   + jnp.dot(qb, state.astype(BF16), preferred_element_type=F32) * cross_ref[h])
            kd = (kh * kdec_ref[h]).astype(BF16)
            s_scr[h] = full_ref[h] * state + jnp.dot(kd.T, vb, preferred_element_type=F32)
            gate = _head_cols((g0_ref, g1_ref), h, rows)
            o_ref[rows, v_cols] = (_silu(gate) * _group_norm(out)).astype(o_ref.dtype)

    sout_ref[...] = s_scr[...]


def _retention_prompt(z, state0, pos_offset, jobs=(), *, tt):
    bsz, seq, _ = z.shape
    assert seq % RET_CHUNK == 0 and tt % RET_CHUNK == 0
    cos, sin = _rope_tables(jnp.arange(seq, dtype=F32) + pos_offset)
    decay, cross, kdec, full = _retention_tables(RET_CHUNK)
    full = jnp.broadcast_to(full[:, None, None], (RET_HEADS, 1, RET_DV))
    const = lambda shape: pl.BlockSpec(shape, lambda b, i: (0,) * len(shape))
    zblk = lambda col: pl.BlockSpec((None, tt, ZB), lambda b, i: (b, i, col))
    n_i = seq // tt
    riders = _Riders(jobs, (bsz, n_i), lambda b, i: b * n_i + i)
    outs = pl.pallas_call(
        functools.partial(_retention_kernel, n_chunks=tt // RET_CHUNK, riders=riders),
        grid=(bsz, n_i),
        in_specs=[
            zblk(ZC_Q), zblk(ZC_K), zblk(ZC_V), zblk(ZC_V + 1), zblk(ZC_G), zblk(ZC_G + 1),
            pl.BlockSpec((tt, RET_DK), lambda b, i: (i, 0)),
            pl.BlockSpec((tt, RET_DK), lambda b, i: (i, 0)),
            pl.BlockSpec((None, RET_HEADS, RET_DK, RET_DV), lambda b, i: (b, 0, 0, 0)),
            const((RET_HEADS, RET_CHUNK, RET_CHUNK)),
            const((RET_HEADS, RET_CHUNK, 1)),
            const((RET_HEADS, RET_CHUNK, 1)),
            const((RET_HEADS, 1, RET_DV)),
            *riders.in_specs,
        ],
        out_specs=[
            pl.BlockSpec((None, tt, RET_V), lambda b, i: (b, i, 0)),
            pl.BlockSpec((None, RET_HEADS, RET_DK, RET_DV), lambda b, i: (b, 0, 0, 0)),
            *riders.out_specs,
        ],
        out_shape=[jax.ShapeDtypeStruct((bsz, seq, RET_V), BF16),
                   jax.ShapeDtypeStruct((bsz, RET_HEADS, RET_DK, RET_DV), F32), *riders.out_shapes],
        scratch_shapes=[pltpu.VMEM((RET_HEADS, RET_DK, RET_DV), F32)],
        compiler_params=_params(("arbitrary", "arbitrary")),
        name="retention_prompt",
    )(z, z, z, z, z, z, cos, sin, state0, decay, cross, kdec, full, *riders.inputs)
    return outs[0], outs[1], list(outs[2:])


def _retention_step_kernel(q_ref, k_ref, v0_ref, v1_ref, g0_ref, g1_ref, cos_ref, sin_ref, s_ref, gam_ref, *rest,
                           bb, layer):
    del layer
    o_ref, sout_ref, o_scr = rest[-3:]
    cos = cos_ref[...]
    sin = sin_ref[...]
    for h in range(RET_HEADS):
        qk_cols = slice(h * RET_DK, (h + 1) * RET_DK)
        v_cols = slice(h * RET_DV, (h + 1) * RET_DV)
        qh = _rope(q_ref[:, qk_cols].astype(F32), cos, sin)
        kh = _rope(k_ref[:, qk_cols].astype(F32), cos, sin) * (RET_DK ** -0.5)
        qk = jnp.sum(qh * kh, axis=-1, keepdims=True)
        q_t = qh.T
        k_t = kh.T
        gamma = gam_ref[h]
        v_all = _head_cols((v0_ref, v1_ref), h, slice(None))
        for b in range(bb):
            state = s_ref[b, h]
            vrow = v_all[b:b + 1, :]
            qs = jnp.sum(q_t[:, b:b + 1] * state, axis=0, keepdims=True)
            o_scr[b:b + 1, v_cols] = qk[b:b + 1, :] * vrow + qs * gamma
            sout_ref[b, h] = gamma * state + k_t[:, b:b + 1] * vrow
    for h in range(RET_HEADS):
        v_cols = slice(h * RET_DV, (h + 1) * RET_DV)
        gate = _head_cols((g0_ref, g1_ref), h, slice(None))
        o_ref[:, v_cols] = (_silu(gate) * _group_norm(o_scr[:, v_cols])).astype(o_ref.dtype)


def _retention_step(z, states, layer, stack, pos, *, bb):
    n = z.shape[0]
    cos, sin = _rope_tables(jnp.full((1,), pos, F32))
    log_gamma = jnp.log(1.0 - 2.0 ** (-5.0 - jnp.arange(RET_HEADS, dtype=F32)))
    gam = jnp.broadcast_to(jnp.exp(log_gamma)[:, None, None], (RET_HEADS, 1, RET_DV))
    const = lambda shape: pl.BlockSpec(shape, lambda i: (0,) * len(shape))
    slab = pl.BlockSpec((None, bb, RET_HEADS, RET_DK, RET_DV), lambda i: (layer, i, 0, 0, 0))
    zblk = lambda col: pl.BlockSpec((bb, ZB), lambda i: (i, col))
    in_specs = [
        zblk(ZC_Q), zblk(ZC_K), zblk(ZC_V), zblk(ZC_V + 1), zblk(ZC_G), zblk(ZC_G + 1),
        const((1, RET_DK)),
        const((1, RET_DK)),
        slab,
        const((RET_HEADS, 1, RET_DV)),
    ]
    args = [z, z, z, z, z, z, cos, sin, states, gam]
    if stack is None:
        state_spec = pl.BlockSpec((bb, RET_HEADS, RET_DK, RET_DV), lambda i: (i, 0, 0, 0))
        state_shape = jax.ShapeDtypeStruct(states.shape[1:], F32)
        aliases = {}
    else:
        in_specs.append(pl.BlockSpec(memory_space=pl.ANY))
        args.append(stack)
        state_spec = slab
        state_shape = jax.ShapeDtypeStruct(stack.shape, F32)
        aliases = {len(args) - 1: 1}
    return pl.pallas_call(
        functools.partial(_retention_step_kernel, bb=bb, layer=layer),
        grid=(n // bb,),
        in_specs=in_specs,
        out_specs=[pl.BlockSpec((bb, RET_V), lambda i: (i, 0)), state_spec],
        out_shape=[jax.ShapeDtypeStruct((n, RET_V), BF16), state_shape],
        scratch_shapes=[pltpu.VMEM((bb, RET_V), F32)],
        input_output_aliases=aliases,
        compiler_params=_params(("parallel",)),
        name="retention_step",
    )(*args)


def _layer_norm(y, g, b):
    mu = jnp.mean(y, axis=-1, keepdims=True)
    d = y - mu
    var = jnp.mean(d * d, axis=-1, keepdims=True)
    return d * lax.rsqrt(var + EPS) * g + b


CONV_RB = 128
CONV_PITCH = 2
CONV_LCH = CONV_WIDTH // LANES


def _conv_glu(a_ref, b_ref, pwb_ref):
    a = a_ref[...].astype(F32) + pwb_ref[:, :CONV_WIDTH]
    b = b_ref[...].astype(F32) + pwb_ref[:, CONV_WIDTH:]
    return a * jax.nn.sigmoid(b)


def _conv_kernel(a_ref, b_ref, buf_ref, pwb_ref, dww_ref, dwb_ref, lng_ref, lnb_ref, *rest, tt, riders):
    rider_in, rest = rest[:riders.n_in], rest[riders.n_in:]
    o_ref, hist_ref = rest[:2]
    rider_out, (x_scr, y_scr) = rest[2:-2], rest[-2:]
    riders.run(rider_in, rider_out)
    _conv_body(a_ref, b_ref, buf_ref, pwb_ref, dww_ref, dwb_ref, lng_ref, lnb_ref, o_ref, hist_ref, x_scr, y_scr, tt=tt)


def _conv_body(a_ref, b_ref, buf_ref, pwb_ref, dww_ref, dwb_ref, lng_ref, lnb_ref, o_ref, hist_ref, x_scr, y_scr,
               *, tt):
    i = pl.program_id(1)

    def rows(start, n):
        return pl.ds(CONV_PITCH * start, n, stride=CONV_PITCH)

    def lanes(c):
        return slice(c * LANES, (c + 1) * LANES)

    @pl.when(i == 0)
    def _():
        for c in range(CONV_LCH):
            x_scr[c, rows(0, CONV_HIST), :] = buf_ref[:, lanes(c)]

    @pl.when(i > 0)
    def _():
        for c in range(CONV_LCH):
            x_scr[c, rows(0, CONV_HIST), :] = x_scr[c, rows(tt, CONV_HIST), :]

    glu = _conv_glu(a_ref, b_ref, pwb_ref)
    for c in range(CONV_LCH):
        x_scr[c, rows(CONV_HIST, tt), :] = glu[:, lanes(c)]
    hist_ref[...] = glu[tt - CONV_HIST:, :]

    off = CONV_HIST - (CONV_K - 1)
    for c in range(CONV_LCH):
        def row_block(r, carry, c=c):
            base = r * CONV_RB
            n_grp = CONV_RB // ROW_TILE
            accs = [jnp.broadcast_to(dwb_ref[:, lanes(c)], (ROW_TILE, LANES))] * n_grp
            for m in range(CONV_RB - ROW_TILE + CONV_K):
                win = x_scr[c, rows(base + (off + m), ROW_TILE), :]
                for k in range(m % ROW_TILE, CONV_K, ROW_TILE):
                    j = (m - k) // ROW_TILE
                    if 0 <= j < n_grp:
                        accs[j] = accs[j] + dww_ref[k:k + 1, lanes(c)] * win
            y_scr[pl.ds(pl.multiple_of(base, CONV_RB), CONV_RB), lanes(c)] = jnp.concatenate(accs, axis=0)
            return carry

        lax.fori_loop(0, tt // CONV_RB, row_block, 0)

    o_ref[...] = _silu(_layer_norm(y_scr[...], lng_ref[...], lnb_ref[...])).astype(o_ref.dtype)


def _conv_prompt(z, buf, pw_b, dw_w, dw_b, ln_g, ln_b, jobs=(), *, tt):
    bsz, seq, _ = z.shape
    assert seq >= CONV_HIST and tt >= CONV_HIST
    buf32 = jnp.pad(buf, ((0, 0), (CONV_HIST - (CONV_K - 1), 0), (0, 0)))
    const = lambda shape: pl.BlockSpec(shape, lambda b, i: (0,) * len(shape))
    zblk = lambda col: pl.BlockSpec((None, tt, ZB), lambda b, i: (b, i, col))
    n_i = seq // tt
    riders = _Riders(jobs, (bsz, n_i), lambda b, i: b * n_i + i)
    out, hist, *cast = pl.pallas_call(
        functools.partial(_conv_kernel, tt=tt, riders=riders),
        grid=(bsz, n_i),
        in_specs=[
            zblk(ZC_CONV), zblk(ZC_CONV + 1),
            pl.BlockSpec((None, CONV_HIST, CONV_WIDTH), lambda b, i: (b, 0, 0)),
            const((1, 2 * CONV_WIDTH)),
            const((CONV_K, CONV_WIDTH)),
            const((1, CONV_WIDTH)),
            const((1, CONV_WIDTH)),
            const((1, CONV_WIDTH)),
            *riders.in_specs,
        ],
        out_specs=[
            pl.BlockSpec((None, tt, CONV_WIDTH), lambda b, i: (b, i, 0)),
            pl.BlockSpec((None, CONV_HIST, CONV_WIDTH), lambda b, i: (b, 0, 0)),
            *riders.out_specs,
        ],
        out_shape=[jax.ShapeDtypeStruct((bsz, seq, CONV_WIDTH), BF16),
                   jax.ShapeDtypeStruct((bsz, CONV_HIST, CONV_WIDTH), F32), *riders.out_shapes],
        scratch_shapes=[pltpu.VMEM((CONV_LCH, CONV_PITCH * (CONV_HIST + tt), LANES), F32),
                        pltpu.VMEM((tt, CONV_WIDTH), F32)],
        compiler_params=_params(("arbitrary", "arbitrary")),
        name="conv_prompt",
    )(z, z, buf32, pw_b.reshape(1, -1), dw_w, dw_b.reshape(1, -1), ln_g.reshape(1, -1), ln_b.reshape(1, -1),
      *riders.inputs)
    return out, hist[:, CONV_HIST - (CONV_K - 1):, :], cast


def _conv_step_kernel(a_ref, b_ref, buf_ref, pwb_ref, dww_ref, dwb_ref, lng_ref, lnb_ref, o_ref, hist_ref):
    hist_len = CONV_K - 1
    glu = _conv_glu(a_ref, b_ref, pwb_ref)
    acc = dwb_ref[...] + dww_ref[hist_len:hist_len + 1, :] * glu
    for k in range(hist_len):
        acc = acc + dww_ref[k:k + 1, :] * buf_ref[k]
    o_ref[...] = _silu(_layer_norm(acc, lng_ref[...], lnb_ref[...])).astype(o_ref.dtype)
    for k in range(hist_len - 1):
        hist_ref[k] = buf_ref[k + 1]
    hist_ref[hist_len - 1] = glu


def _conv_step(z, bufs, layer, pw_b, dw_w, dw_b, ln_g, ln_b, *, bb):
    n = z.shape[0]
    hist_len = CONV_K - 1
    const = lambda shape: pl.BlockSpec(shape, lambda i: (0,) * len(shape))
    zblk = lambda col: pl.BlockSpec((bb, ZB), lambda i: (i, col))
    return pl.pallas_call(
        _conv_step_kernel,
        grid=(n // bb,),
        in_specs=[
            zblk(ZC_CONV), zblk(ZC_CONV + 1),
            pl.BlockSpec((None, hist_len, bb, CONV_WIDTH), lambda i: (layer, 0, i, 0)),
            const((1, 2 * CONV_WIDTH)),
            const((CONV_K, CONV_WIDTH)),
            const((1, CONV_WIDTH)),
            const((1, CONV_WIDTH)),
            const((1, CONV_WIDTH)),
        ],
        out_specs=[
            pl.BlockSpec((bb, CONV_WIDTH), lambda i: (i, 0)),
            pl.BlockSpec((hist_len, bb, CONV_WIDTH), lambda i: (0, i, 0)),
        ],
        out_shape=[jax.ShapeDtypeStruct((n, CONV_WIDTH), BF16),
                   jax.ShapeDtypeStruct((hist_len, n, CONV_WIDTH), F32)],
        compiler_params=_params(("parallel",)),
        name="conv_step",
    )(z, z, bufs, pw_b.reshape(1, -1), dw_w, dw_b.reshape(1, -1), ln_g.reshape(1, -1), ln_b.reshape(1, -1))


def _rows_to_tiles(tile_ref, x, rows):
    for c in range(ROW_CHUNKS):
        tile_ref[pl.ds(c, rows, stride=ROW_TILE), :] = x[:, c * LANES:(c + 1) * LANES]


def _tiles_chunk(tile_ref, c, rows):
    return tile_ref[pl.ds(c, rows, stride=ROW_TILE), :]


def _tiles_to_rows(tile_ref, rows):
    return jnp.concatenate([_tiles_chunk(tile_ref, c, rows) for c in range(ROW_CHUNKS)], axis=-1)


def _merge_kernel(x_ref, s5_ref, ret_ref, conv_ref, *rest, tiled_u):
    gate_refs = rest[:N_BRANCH * D_MODEL // ZB]
    ps5_ref, pret_ref, pconv_ref, wout_ref, g_ref, h_ref, u_ref = rest[len(gate_refs):]
    per_branch = D_MODEL // ZB

    def gate(n):
        cols = [gate_refs[n * per_branch + j][...] for j in range(per_branch)]
        return jax.nn.sigmoid(jnp.concatenate(cols, axis=-1).astype(F32))

    merged = (gate(0) * jnp.dot(s5_ref[...], ps5_ref[...], preferred_element_type=F32)
              + gate(1) * jnp.dot(ret_ref[...], pret_ref[...], preferred_element_type=F32)
              + gate(2) * jnp.dot(conv_ref[...], pconv_ref[...], preferred_element_type=F32))
    h = x_ref[...] + _bdot(merged, wout_ref[...])
    h_ref[...] = h
    ms = jnp.mean(h * h, axis=-1, keepdims=True)
    u = h * lax.rsqrt(ms + EPS) * g_ref[...]
    if tiled_u:
        _rows_to_tiles(u_ref, u, u.shape[0])
    else:
        u_ref[...] = u.astype(u_ref.dtype)


def _merge(x, s5_out, ret_out, conv_out, z, s5_proj, ret_proj, conv_proj, w_out, norm_g, *, tm, tiled_u):
    bsz, seq, d = x.shape
    const = lambda shape: pl.BlockSpec(shape, lambda b, i: (0,) * len(shape))
    tok = lambda w, col=0: pl.BlockSpec((None, tm, w), lambda b, i: (b, i, col))
    n_i = seq // tm
    n_gate = N_BRANCH * d // ZB
    if tiled_u:
        u_spec = pl.BlockSpec((tm * ROW_TILE, LANES), lambda b, i: (b * n_i + i, 0))
        u_shape = jax.ShapeDtypeStruct((bsz * seq * ROW_TILE, LANES), F32)
    else:
        u_spec = tok(d)
        u_shape = jax.ShapeDtypeStruct((bsz, seq, d), BF16)
    return pl.pallas_call(
        functools.partial(_merge_kernel, tiled_u=tiled_u),
        grid=(bsz, seq // tm),
        in_specs=[
            tok(d),
            tok(S5_WIDTH),
            tok(RET_V),
            tok(CONV_WIDTH),
            *[tok(ZB, ZC_GATE + j) for j in range(n_gate)],
            const((S5_WIDTH, d)),
            const((RET_V, d)),
            const((CONV_WIDTH, d)),
            const((d, d)),
            const((1, d)),
        ],
        out_specs=[tok(d), u_spec],
        out_shape=[jax.ShapeDtypeStruct((bsz, seq, d), F32), u_shape],
        compiler_params=_params(("parallel", "parallel")),
        name="merge",
    )(x, s5_out, ret_out, conv_out, *([z] * n_gate), s5_proj, ret_proj, conv_proj, w_out, norm_g.reshape(1, d))


def _ffn_kernel(h_ref, u_ref, wg_ref, wu_ref, wd_ref, *rest, riders):
    rider_in, o_ref, rider_out = rest[:riders.n_in], rest[riders.n_in], rest[riders.n_in + 1:]
    f = pl.program_id(1)
    ub = u_ref[...].astype(BF16)
    gate = jnp.dot(ub, wg_ref[...], preferred_element_type=F32)
    up = jnp.dot(ub, wu_ref[...], preferred_element_type=F32)
    part = _bdot(_silu(gate) * up, wd_ref[...])

    @pl.when(f == 0)
    def _():
        o_ref[...] = h_ref[...] + part

    @pl.when(f > 0)
    def _():
        o_ref[...] = o_ref[...] + part

    riders.run(rider_in, rider_out)


def _ffn(h, u, w_gate, w_up, w_down, jobs=(), *, tm, tf):
    rows, d = h.shape
    dff = w_gate.shape[1]
    n_i, n_f = rows // tm, dff // tf
    tok = pl.BlockSpec((tm, d), lambda i, f: (i, 0))
    riders = _Riders(jobs, (n_i, n_f), lambda i, f: i * n_f + f)
    outs = pl.pallas_call(
        functools.partial(_ffn_kernel, riders=riders),
        grid=(n_i, n_f),
        in_specs=[tok, tok,
                  pl.BlockSpec((d, tf), lambda i, f: (0, f)),
                  pl.BlockSpec((d, tf), lambda i, f: (0, f)),
                  pl.BlockSpec((tf, d), lambda i, f: (f, 0)),
                  *riders.in_specs],
        out_specs=[tok, *riders.out_specs],
        out_shape=[jax.ShapeDtypeStruct((rows, d), F32), *riders.out_shapes],
        compiler_params=_params(("arbitrary", "arbitrary")),
        name="ffn_dense",
    )(h, u, w_gate, w_up, w_down, *riders.inputs)
    return outs[0], list(outs[1:])


def _split_bf16(x):
    hi = x.astype(BF16)
    return hi, (x - hi.astype(F32)).astype(BF16)


def _router_kernel(u_ref, rt_ref, tri_ref, idx_ref, wts_ref, rank_ref, cnt_ref, cnt_scr, *, tm):
    @pl.when(pl.program_id(0) == 0)
    def _():
        cnt_scr[...] = jnp.zeros_like(cnt_scr)

    u_hi, u_lo = _split_bf16(_tiles_to_rows(u_ref, tm))
    r_hi, r_lo = _split_bf16(rt_ref[...])
    dn = (((1,), (1,)), ((), ()))
    logits = (lax.dot_general(r_hi, u_hi, dn, preferred_element_type=F32)
              + lax.dot_general(r_lo, u_hi, dn, preferred_element_type=F32)
              + lax.dot_general(r_hi, u_lo, dn, preferred_element_type=F32))
    eidx = lax.broadcasted_iota(jnp.int32, logits.shape, 0)
    m1 = jnp.max(logits, axis=0, keepdims=True)
    i1 = jnp.min(jnp.where(logits == m1, eidx, N_EXPERTS), axis=0, keepdims=True)
    rest = jnp.where(eidx == i1, -jnp.inf, logits)
    m2 = jnp.max(rest, axis=0, keepdims=True)
    i2 = jnp.min(jnp.where(rest == m2, eidx, N_EXPERTS), axis=0, keepdims=True)
    e2 = jnp.exp(m2 - m1)
    w1 = 1.0 / (1.0 + e2)
    idx_ref[...] = jnp.concatenate([i1, i2], axis=0)
    wts_ref[...] = jnp.concatenate([w1, e2 * w1], axis=0)

    hit1 = eidx == i1
    hit2 = eidx == i2
    hits = jnp.where(hit1 | hit2, 1.0, 0.0)
    before = jnp.dot(hits.astype(BF16), tri_ref[...], preferred_element_type=F32) + cnt_scr[...]
    rank_ref[...] = jnp.concatenate(
        [jnp.sum(jnp.where(hit1, before, 0.0), axis=0, keepdims=True),
         jnp.sum(jnp.where(hit2, before, 0.0), axis=0, keepdims=True)], axis=0).astype(jnp.int32)
    cnt_scr[...] = cnt_scr[...] + jnp.sum(hits, axis=1, keepdims=True)
    cnt_ref[...] = cnt_scr[...]


def _router(u8, router, *, tm):
    rows = u8.shape[0] // ROW_TILE
    d = D_MODEL
    tri = (jnp.arange(tm)[:, None] < jnp.arange(tm)[None, :]).astype(BF16)
    const = lambda shape: pl.BlockSpec(shape, lambda i: (0,) * len(shape))
    lane = pl.BlockSpec((TOP_K, tm), lambda i: (0, i))
    return pl.pallas_call(
        functools.partial(_router_kernel, tm=tm),
        grid=(rows // tm,),
        in_specs=[pl.BlockSpec((tm * ROW_TILE, LANES), lambda i: (i, 0)), const((N_EXPERTS, d)), const((tm, tm))],
        out_specs=[lane, lane, lane, const((N_EXPERTS, 1))],
        out_shape=[jax.ShapeDtypeStruct((TOP_K, rows), jnp.int32), jax.ShapeDtypeStruct((TOP_K, rows), F32),
                   jax.ShapeDtypeStruct((TOP_K, rows), jnp.int32), jax.ShapeDtypeStruct((N_EXPERTS, 1), F32)],
        scratch_shapes=[pltpu.VMEM((N_EXPERTS, 1), F32)],
        compiler_params=_params(("arbitrary",)),
        name="moe_router",
    )(u8, router.T, tri)


DMA_UNROLL = 8


def _token_tile(ref, r):
    return ref.at[pl.ds(pl.multiple_of(r * ROW_TILE, ROW_TILE), ROW_TILE)]


def _dispatch_kernel(pos_ref, u_ref, init_ref, xs_ref, sem, *, tb):
    del init_ref

    def start(r, carry):
        for s in range(TOP_K):
            pltpu.make_async_copy(_token_tile(u_ref, r), _token_tile(xs_ref, pos_ref[s, r]),
                                  sem.at[s]).start(priority=s)
        return carry

    lax.fori_loop(0, tb, start, 0, unroll=DMA_UNROLL)
    for s in range(TOP_K):
        pltpu.make_async_copy(u_ref, xs_ref.at[pl.ds(0, tb * ROW_TILE)], sem.at[s]).wait()


def _dispatch(u8, pos, xs8, *, tb):
    rows = u8.shape[0] // ROW_TILE
    return pl.pallas_call(
        functools.partial(_dispatch_kernel, tb=tb),
        grid=(rows // tb,),
        in_specs=[
            pl.BlockSpec((TOP_K, tb), lambda i: (0, i), memory_space=pltpu.SMEM),
            pl.BlockSpec((tb * ROW_TILE, LANES), lambda i: (i, 0)),
            pl.BlockSpec(memory_space=pl.ANY),
        ],
        out_specs=pl.BlockSpec(memory_space=pl.ANY),
        out_shape=jax.ShapeDtypeStruct(xs8.shape, xs8.dtype),
        scratch_shapes=[pltpu.SemaphoreType.DMA((TOP_K,))],
        input_output_aliases={2: 0},
        compiler_params=_params(("arbitrary",)),
        name="moe_dispatch",
    )(pos, u8, xs8)


def _experts_kernel(te_ref, nu_ref, x_ref, wg_ref, wu_ref, wd_ref, o_ref, x_scr, acc_scr, *, tm):
    del te_ref
    i = pl.program_id(0)
    f = pl.program_id(1)

    def swiglu_rows(n):
        xb = x_scr[0:n, :]
        gate = jnp.dot(xb, wg_ref[...], preferred_element_type=F32)
        up = jnp.dot(xb, wu_ref[...], preferred_element_type=F32)
        part = _bdot(_silu(gate) * up, wd_ref[...])

        @pl.when(f == 0)
        def _():
            acc_scr[0:n, :] = part
            if n < tm:
                acc_scr[n:tm, :] = jnp.zeros((tm - n, D_MODEL), F32)

        @pl.when(f > 0)
        def _():
            acc_scr[0:n, :] = acc_scr[0:n, :] + part

    @pl.when(i < nu_ref[0])
    def _():
        @pl.when(f == 0)
        def _():
            x_scr[...] = _tiles_to_rows(x_ref, tm).astype(BF16)

        half = tm // 2
        valid = nu_ref[1 + i]

        @pl.when(valid > half)
        def _():
            swiglu_rows(tm)

        @pl.when(valid <= half)
        def _():
            swiglu_rows(half)

        @pl.when(f == pl.num_programs(1) - 1)
        def _():
            _rows_to_tiles(o_ref, acc_scr[...], tm)

    @pl.when(i >= nu_ref[0])
    def _():
        o_ref[...] = jnp.zeros_like(o_ref)


def _experts(xs8, tile_expert, n_used, w_gate, w_up, w_down, *, tm, tf):
    rows = xs8.shape[0] // ROW_TILE
    d = D_MODEL
    dff = w_gate.shape[-1]
    n_f = dff // tf
    last_f = n_f - 1

    def row_map(i, f, te, nu):
        return (jnp.minimum(i, nu[0] - 1), 0)

    def fsel(i, f, nu):
        return jnp.where(i < nu[0], f, last_f)

    grid_spec = pltpu.PrefetchScalarGridSpec(
        num_scalar_prefetch=2,
        grid=(rows // tm, n_f),
        in_specs=[
            pl.BlockSpec((tm * ROW_TILE, LANES), row_map),
            pl.BlockSpec((None, d, tf), lambda i, f, te, nu: (te[i], 0, fsel(i, f, nu))),
            pl.BlockSpec((None, d, tf), lambda i, f, te, nu: (te[i], 0, fsel(i, f, nu))),
            pl.BlockSpec((None, tf, d), lambda i, f, te, nu: (te[i], fsel(i, f, nu), 0)),
        ],
        out_specs=pl.BlockSpec((tm * ROW_TILE, LANES), lambda i, f, te, nu: (i, 0)),
        scratch_shapes=[pltpu.VMEM((tm, d), BF16), pltpu.VMEM((tm, d), F32)],
    )
    return pl.pallas_call(
        functools.partial(_experts_kernel, tm=tm),
        grid_spec=grid_spec,
        out_shape=jax.ShapeDtypeStruct(xs8.shape, F32),
        compiler_params=_params(("arbitrary", "arbitrary")),
        name="moe_experts",
    )(tile_expert, n_used, xs8, w_gate, w_up, w_down)


def _combine_kernel(pos_ref, h_ref, wts_ref, g_ref, ys_ref, o_ref, y_scr, sem, *, tb, final_norm):
    def start(r, carry):
        for s in range(TOP_K):
            pltpu.make_async_copy(_token_tile(ys_ref, pos_ref[s, r]), _token_tile(y_scr.at[s], r),
                                  sem.at[s]).start(priority=s)
        return carry

    lax.fori_loop(0, tb, start, 0, unroll=DMA_UNROLL)
    for s in range(TOP_K):
        pltpu.make_async_copy(ys_ref.at[pl.ds(0, tb * ROW_TILE)], y_scr.at[s], sem.at[s]).wait()
    w1 = wts_ref[:, 0:1]
    w2 = wts_ref[:, 1:2]
    moe = jnp.concatenate([w1 * _tiles_chunk(y_scr.at[0], c, tb) + w2 * _tiles_chunk(y_scr.at[1], c, tb)
                           for c in range(ROW_CHUNKS)], axis=-1)
    out = h_ref[...] + moe
    if final_norm:
        ms = jnp.mean(out * out, axis=-1, keepdims=True)
        out = out * lax.rsqrt(ms + EPS) * g_ref[...]
    o_ref[...] = out


def _combine(h, pos, wts_t, ys8, final_g, *, tb):
    rows, d = h.shape
    final_norm = final_g is not None
    gain = final_g.reshape(1, d) if final_norm else jnp.ones((1, d), F32)
    return pl.pallas_call(
        functools.partial(_combine_kernel, tb=tb, final_norm=final_norm),
        grid=(rows // tb,),
        in_specs=[
            pl.BlockSpec((TOP_K, tb), lambda i: (0, i), memory_space=pltpu.SMEM),
            pl.BlockSpec((tb, d), lambda i: (i, 0)),
            pl.BlockSpec((tb, TOP_K), lambda i: (i, 0)),
            pl.BlockSpec((1, d), lambda i: (0, 0)),
            pl.BlockSpec(memory_space=pl.ANY),
        ],
        out_specs=pl.BlockSpec((tb, d), lambda i: (i, 0)),
        out_shape=jax.ShapeDtypeStruct((rows, d), F32),
        scratch_shapes=[pltpu.VMEM((TOP_K, tb * ROW_TILE, LANES), F32), pltpu.SemaphoreType.DMA((TOP_K,))],
        compiler_params=_params(("arbitrary",)),
        name="moe_combine",
    )(pos, h, wts_t, gain, ys8)


PROJ_TN = 2560
S5_TT = 128
FFN_TF = 1408
MOE_TM = 512
MOE_TF = 1792
MOE_TB = 1024


def _moe_tiles(n_rows):
    return n_rows // MOE_TM + N_EXPERTS


def _moe_buffer_shape(n_rows):
    return (_moe_tiles(n_rows) * MOE_TM * ROW_TILE, LANES)


def _moe(h_list, u8_list, router, w_gate, w_up, w_down, xs8, final_g):
    routes = []
    for u8 in u8_list:
        routes.append(_router(u8, router, tm=min(512, u8.shape[0] // ROW_TILE)))
    counts = [r[3][:, 0].astype(jnp.int32) for r in routes]
    total = sum(counts)
    padded = ((total + MOE_TM - 1) // MOE_TM) * MOE_TM
    ends = jnp.cumsum(padded)
    starts = ends - padded
    n_rows = sum(h.shape[0] for h in h_list) * TOP_K
    n_tiles = _moe_tiles(n_rows)
    assert xs8.shape == _moe_buffer_shape(n_rows)
    n_used = (ends[-1] // MOE_TM).astype(jnp.int32)
    tile_start = jnp.arange(n_tiles, dtype=jnp.int32) * MOE_TM
    tile_expert = jnp.sum((tile_start[:, None] >= ends[None, :]).astype(jnp.int32), axis=1)
    last_expert = jnp.sum((((n_used - 1) * MOE_TM) >= ends).astype(jnp.int32))
    tile_expert = jnp.where(jnp.arange(n_tiles) < n_used, tile_expert, last_expert).astype(jnp.int32)

    poss = []
    seen = jnp.zeros((N_EXPERTS,), jnp.int32)
    for u8, (idx, _, rank, _), cnt in zip(u8_list, routes, counts):
        base = starts + seen
        pos = rank
        for e in range(N_EXPERTS):
            pos = pos + jnp.where(idx == e, base[e], 0)
        poss.append(pos)
        seen = seen + cnt
        xs8 = _dispatch(u8, pos, xs8, tb=min(MOE_TB, u8.shape[0] // ROW_TILE))
    group_rows = jnp.sum(jnp.where(jnp.arange(N_EXPERTS)[None, :] == tile_expert[:, None],
                                   (starts + total)[None, :], 0), axis=1)
    tile_valid = jnp.clip(group_rows - tile_start, 0, MOE_TM)
    tile_info = jnp.concatenate([n_used.reshape(1), tile_valid]).astype(jnp.int32)
    ys8 = _experts(xs8, tile_expert, tile_info, w_gate, w_up, w_down, tm=MOE_TM, tf=MOE_TF)
    outs = []
    for h, pos, (_, wts, _, _) in zip(h_list, poss, routes):
        outs.append(_combine(h, pos, wts.T, ys8, final_g, tb=min(MOE_TB, h.shape[0])))
    return outs


def _rmsnorm_kernel(x_ref, g_ref, o_ref):
    x = x_ref[...]
    ms = jnp.mean(x * x, axis=-1, keepdims=True)
    o_ref[...] = x * lax.rsqrt(ms + EPS) * g_ref[...]


def _rmsnorm(x, g, *, tm):
    rows, d = x.shape
    return pl.pallas_call(
        _rmsnorm_kernel,
        grid=(rows // tm,),
        in_specs=[pl.BlockSpec((tm, d), lambda i: (i, 0)), pl.BlockSpec((1, d), lambda i: (0, 0))],
        out_specs=pl.BlockSpec((tm, d), lambda i: (i, 0)),
        out_shape=jax.ShapeDtypeStruct((rows, d), F32),
        compiler_params=_params(("parallel",)),
        name="final_norm",
    )(x, g.reshape(1, d))


def _pack_s5_state(re, im):
    n = re.shape[0]
    return jnp.concatenate([re.reshape(n, S5_LANES), im.reshape(n, S5_LANES)], axis=-1)


def _unpack_s5_state(h):
    n = h.shape[0]
    return (h[:, :S5_LANES].reshape(n, S5_GROUPS, S5_STATE), h[:, S5_LANES:].reshape(n, S5_GROUPS, S5_STATE))


def _mixer(x, z, s5_h0, ret_s0, conv_buf, pos_offset, p, *, single_step, tiled_u, layer=0, ret_stack=None, jobs=None):
    jobs = jobs or {}
    done = {}
    bsz, seq, d = x.shape
    s5_out, s5_state, done["s5"] = _s5_branch(z, s5_h0, p["a_re"], p["a_im"], p["bmat"], p["cmat"], p["d_skip"],
                                              p["glu_w"], p["glu_b"], jobs.get("s5", ()), single_step=single_step,
                                              tt=1 if single_step else S5_TT)

    if single_step:
        z2 = z.reshape(seq, N_IN)
        ret_out, ret_state = _retention_step(z2, ret_s0, layer, ret_stack, pos_offset, bb=16)
        conv_out, conv_state = _conv_step(z2, conv_buf, layer, p["pw_b"], p["dw_w"], p["dw_b"], p["ln_g"], p["ln_b"],
                                          bb=32)
        ret_out = ret_out.reshape(bsz, seq, RET_V)
        conv_out = conv_out.reshape(bsz, seq, CONV_WIDTH)
    else:
        ret_out, ret_state, done["ret"] = _retention_prompt(z, ret_s0, pos_offset, jobs.get("ret", ()), tt=256)
        conv_out, conv_state, done["conv"] = _conv_prompt(z, conv_buf, p["pw_b"], p["dw_w"], p["dw_b"], p["ln_g"],
                                                          p["ln_b"], jobs.get("conv", ()), tt=256)

    h, u = _merge(x, s5_out, ret_out, conv_out, z, p["s5_proj"], p["ret_proj"], p["conv_proj"], p["w_out"],
                  p["norm_ffn_g"], tm=min(512, seq), tiled_u=tiled_u)
    return h, u, (s5_state, ret_state, conv_state), done


def kernel(x_prompt, x_sample, state_s5_re, state_s5_im, state_ret, state_conv, norm_mix_g, w_in, s5_lambda_re, s5_lambda_im, s5_log_dt, s5_b_re, s5_b_im, s5_c_re, s5_c_im, s5_d, s5_glu_w, s5_glu_b, s5_proj, ret_proj, conv_pw_b, conv_dw_w, conv_dw_b, conv_ln_g, conv_ln_b, conv_proj, w_out, norm_ffn_g, ffn_w_gate, ffn_w_up, ffn_w_down, moe_router, moe_w_gate, moe_w_up, moe_w_down, norm_final_g):
    depth = w_in.shape[0]
    bp, seq, d = x_prompt.shape
    ns = x_sample.shape[0]
    past_len = 16384
    bf = lambda a: a.astype(BF16)

    hp = x_prompt
    hs = x_sample.reshape(1, ns, d)
    zero_s5 = jnp.zeros((bp, 2 * S5_LANES), F32)
    zero_ret = jnp.zeros((bp, RET_HEADS, RET_DK, RET_DV), F32)
    zero_conv = jnp.zeros((bp, CONV_K - 1, CONV_WIDTH), F32)

    conv_rows = jnp.transpose(state_conv, (0, 2, 1, 3))
    p_states, s_states = [], []
    ret_stack = None
    normed = False
    groups = depth * S5_GROUPS
    disc = _s5_discretize(s5_lambda_re.reshape(groups, S5_STATE), s5_lambda_im.reshape(groups, S5_STATE),
                          s5_log_dt.reshape(groups), s5_b_re.reshape(groups, S5_STATE, S5_GROUP),
                          s5_b_im.reshape(groups, S5_STATE, S5_GROUP))
    disc = [a.reshape((depth, S5_GROUPS) + a.shape[1:]) for a in disc]
    for l in range(depth):
        is_moe = l % 2 == 1
        a_re, a_im, bb_re, bb_im = (a[l] for a in disc)
        bmat, cmat = _s5_block_mats(bb_re, bb_im, s5_c_re[l], s5_c_im[l])
        p = dict(a_re=a_re, a_im=a_im, bmat=bmat, cmat=cmat, d_skip=s5_d[l], glu_w=bf(s5_glu_w[l]),
                 glu_b=s5_glu_b[l], pw_b=conv_pw_b[l], dw_w=conv_dw_w[l], dw_b=conv_dw_b[l], ln_g=conv_ln_g[l],
                 ln_b=conv_ln_b[l], s5_proj=bf(s5_proj[l]), ret_proj=bf(ret_proj[l]), conv_proj=bf(conv_proj[l]),
                 w_out=bf(w_out[l]), norm_ffn_g=norm_ffn_g[l])
        j = l // 2
        flat = lambda w: w.reshape(-1, w.shape[-1])
        if is_moe:
            jobs = {"conv": [_cast_job(flat(moe_w_up[j]))],
                    "proj": [(None, _moe_buffer_shape((bp * seq + ns) * TOP_K), F32)]}
        else:
            jobs = {"s5": [_cast_job(ffn_w_gate[j]), _cast_job(ffn_w_up[j]), _cast_job(ffn_w_down[j])]}
            if l + 1 < depth:
                jm = (l + 1) // 2
                jobs["conv"] = [_cast_job(flat(moe_w_gate[jm]))]
                jobs["ret"] = [_cast_job(flat(moe_w_down[jm]))]
        zp, zs, proj_done = _norm_proj(hp, hs.reshape(ns, d), norm_mix_g[l], w_in, l, jobs.pop("proj", ()),
                                       tt=512, tn=PROJ_TN)
        hp, up, st_p, done = _mixer(hp, zp, zero_s5, zero_ret, zero_conv, 0.0, p, single_step=False, tiled_u=is_moe,
                                    layer=l, jobs=jobs)
        hs, us, st_s, _ = _mixer(hs, zs.reshape(1, ns, N_IN), _pack_s5_state(state_s5_re[l], state_s5_im[l]),
                                 state_ret, conv_rows, float(past_len), p, single_step=True, tiled_u=is_moe, layer=l,
                                 ret_stack=ret_stack)
        p_states.append(st_p)
        s_states.append(st_s)

        hp2, hs2 = hp.reshape(bp * seq, d), hs.reshape(ns, d)
        if is_moe:
            ret_stack = st_s[1]
            final_g = norm_final_g if l == depth - 1 else None
            normed = final_g is not None
            w_gate_b, w_down_b = (c.reshape(w.shape[1:]) for c, w in zip(early, (moe_w_gate, moe_w_down)))
            w_up_b = done["conv"][0].reshape(moe_w_up.shape[1:])
            hp2, hs2 = _moe([hp2, hs2], [up, us], moe_router[j], w_gate_b, w_up_b, w_down_b, proj_done[0], final_g)
        else:
            wg, wu, wd = done["s5"]
            if l + 1 < depth:
                early = [done["conv"][0], done["ret"][0]]
            ffn_jobs = []
            if l == 0 and depth > 1:
                rows = st_s[1].size // RET_DV
                ffn_jobs = [(st_s[1].reshape(rows, RET_DV), (depth * rows, RET_DV), F32)]
            hp2, made = _ffn(hp2, up.reshape(bp * seq, d), wg, wu, wd, ffn_jobs, tm=512, tf=FFN_TF)
            if l == 0:
                ret_stack = made[0].reshape(state_ret.shape) if ffn_jobs else st_s[1][None]
            else:
                ret_stack = st_s[1]
            hs2, _ = _ffn(hs2, us.reshape(ns, d), wg, wu, wd, tm=ns, tf=FFN_TF)
        hp = hp2.reshape(bp, seq, d)
        hs = hs2.reshape(1, ns, d)

    if normed:
        y_prompt, y_sample = hp, hs.reshape(ns, 1, d)
    else:
        y_prompt = _rmsnorm(hp.reshape(bp * seq, d), norm_final_g, tm=512).reshape(bp, seq, d)
        y_sample = _rmsnorm(hs.reshape(ns, d), norm_final_g, tm=ns).reshape(ns, 1, d)

    def stack_s5(states):
        s5 = [_unpack_s5_state(s[0]) for s in states]
        return jnp.stack([a for a, _ in s5]), jnp.stack([b for _, b in s5])

    p_re, p_im = stack_s5(p_states)
    s_re, s_im = stack_s5(s_states)
    p_ret = jnp.stack([s[1] for s in p_states])
    p_conv = jnp.stack([s[2] for s in p_states])
    s_conv = jnp.transpose(jnp.stack([s[2] for s in s_states]), (0, 2, 1, 3))
    return (y_prompt, y_sample, p_re, p_im, p_ret, p_conv, s_re, s_im, ret_stack, s_conv)
```

```python
import functools
import math

import jax
import jax.numpy as jnp
from jax import lax
from jax.experimental import pallas as pl
from jax.experimental.pallas import tpu as pltpu

F32 = jnp.float32
BF16 = jnp.bfloat16

D_MODEL = 1024
S5_WIDTH = 512
S5_GROUP = 16
S5_GROUPS = 32
S5_STATE = 64
S5_LANES = S5_GROUPS * S5_STATE
RET_HEADS = 4
RET_DK = 128
RET_DV = 256
RET_QK = RET_HEADS * RET_DK
RET_V = RET_HEADS * RET_DV
RET_CHUNK = 128
ROPE_BASE = 10000.0
CONV_WIDTH = 512
CONV_K = 31
CONV_HIST = 32
N_EXPERTS = 8
TOP_K = 2
N_BRANCH = 3
EPS = 1e-6
N_IN = S5_WIDTH + 2 * RET_QK + 2 * RET_V + 2 * CONV_WIDTH + N_BRANCH * D_MODEL
ZB = 512
ZC_S5 = 0
ZC_Q = ZC_S5 + S5_WIDTH // ZB
ZC_K = ZC_Q + RET_QK // ZB
ZC_V = ZC_K + RET_QK // ZB
ZC_G = ZC_V + RET_V // ZB
ZC_CONV = ZC_G + RET_V // ZB
ZC_GATE = ZC_CONV + 2 * CONV_WIDTH // ZB
HEADS_PER_ZB = ZB // RET_DV

ROW_TILE = 8
LANES = 128
ROW_CHUNKS = D_MODEL // LANES
VMEM_LIMIT = 48 * 1024 * 1024
ROW_SPLIT = 2


def _params(sem):
    return pltpu.CompilerParams(dimension_semantics=sem, vmem_limit_bytes=VMEM_LIMIT)


def _silu(x):
    return x * jax.nn.sigmoid(x)


def _bdot(a, b):
    return jnp.dot(a.astype(BF16), b.astype(BF16), preferred_element_type=F32)


def _norm_proj_kernel(x_ref, g_ref, w_ref, x2_ref, *rest, riders):
    rider_in, (o_ref, o2_ref) = rest[:riders.n_in], rest[riders.n_in:riders.n_in + 2]
    rider_out, w_scr = rest[riders.n_in + 2:-1], rest[-1]
    riders.run(rider_in, rider_out)

    def project(x):
        ms = jnp.mean(x * x, axis=-1, keepdims=True)
        u = (x * lax.rsqrt(ms + EPS) * g_ref[...]).astype(BF16)
        return jnp.dot(u, w_scr[...], preferred_element_type=F32).astype(BF16)

    @pl.when((pl.program_id(1) == 0) & (pl.program_id(2) == 0))
    def _():
        w_scr[...] = w_ref[...].astype(BF16)
        o2_ref[...] = project(x2_ref[...])

    half = x_ref.shape[0] // ROW_SPLIT
    for r in range(ROW_SPLIT):
        rows = slice(r * half, (r + 1) * half)
        o_ref[rows, :] = project(x_ref[rows, :])


def _norm_proj(x, x2, g, w_all, layer, jobs=(), *, tt, tn):
    bsz, seq, d = x.shape
    r2 = x2.shape[0]
    n = w_all.shape[2]
    grid = (n // tn, bsz, seq // tt)
    riders = _Riders(jobs, grid, lambda j, b, i: (j * grid[1] + b) * grid[2] + i)
    outs = pl.pallas_call(
        functools.partial(_norm_proj_kernel, riders=riders),
        grid=grid,
        in_specs=[
            pl.BlockSpec((None, tt, d), lambda j, b, i: (b, i, 0)),
            pl.BlockSpec((1, d), lambda j, b, i: (0, 0)),
            pl.BlockSpec((None, d, tn), lambda j, b, i: (layer, 0, j)),
            pl.BlockSpec((r2, d), lambda j, b, i: (0, 0)),
            *riders.in_specs,
        ],
        out_specs=[pl.BlockSpec((None, tt, tn), lambda j, b, i: (b, i, j)),
                   pl.BlockSpec((r2, tn), lambda j, b, i: (0, j)),
                   *riders.out_specs],
        out_shape=[jax.ShapeDtypeStruct((bsz, seq, n), BF16), jax.ShapeDtypeStruct((r2, n), BF16),
                   *riders.out_shapes],
        scratch_shapes=[pltpu.VMEM((d, tn), BF16)],
        compiler_params=_params(("arbitrary", "arbitrary", "arbitrary")),
        name="norm_proj",
    )(x, g.reshape(1, d), w_all, x2, *riders.inputs)
    return outs[0], outs[1], list(outs[2:])


def _s5_disc_kernel(lre_ref, lim_ref, ldt_ref, bre_ref, bim_ref, are_ref, aim_ref, ore_ref, oim_ref):
    lam_re = lre_ref[...]
    lam_im = lim_ref[...]
    dt = jnp.exp(ldt_ref[...])
    mag = jnp.exp(lam_re * dt)
    ang = lam_im * dt
    lbar_re = mag * jnp.cos(ang)
    lbar_im = mag * jnp.sin(ang)
    den = lam_re * lam_re + lam_im * lam_im
    nr = lbar_re - 1.0
    f_re = (nr * lam_re + lbar_im * lam_im) / den
    f_im = (lbar_im * lam_re - nr * lam_im) / den
    b_re = bre_ref[...]
    b_im = bim_ref[...]
    are_ref[...] = lbar_re
    aim_ref[...] = lbar_im
    ore_ref[...] = f_re * b_re - f_im * b_im
    oim_ref[...] = f_re * b_im + f_im * b_re


def _s5_discretize(lam_re, lam_im, log_dt, b_re, b_im):
    g, n = lam_re.shape
    p = b_re.shape[-1]
    rows = g * n
    col = lambda a: a.reshape(rows, 1)
    ldt = jnp.broadcast_to(log_dt[:, None], (g, n))
    outs = pl.pallas_call(
        _s5_disc_kernel,
        out_shape=[jax.ShapeDtypeStruct((rows, 1), F32), jax.ShapeDtypeStruct((rows, 1), F32),
                   jax.ShapeDtypeStruct((rows, p), F32), jax.ShapeDtypeStruct((rows, p), F32)],
        name="s5_discretize",
    )(col(lam_re), col(lam_im), col(ldt), b_re.reshape(rows, p), b_im.reshape(rows, p))
    a_re, a_im, bb_re, bb_im = outs
    return a_re.reshape(g, n), a_im.reshape(g, n), bb_re.reshape(g, n, p), bb_im.reshape(g, n, p)


S5_KCH = 128
S5_NCHUNK = S5_WIDTH // S5_KCH
S5_GPC = S5_KCH // S5_GROUP
S5_SPC = S5_GPC * S5_STATE


def _s5_block_mats(bbar_re, bbar_im, c_re, c_im):
    eye = jnp.eye(S5_GPC, dtype=F32)

    def in_blocks(bb):
        t = bb.reshape(S5_NCHUNK, S5_GPC, S5_STATE, S5_GROUP)
        m = jnp.einsum("cgnp,gh->cgphn", t, eye)
        return m.reshape(S5_NCHUNK, S5_KCH, S5_SPC)

    def out_blocks(cc):
        t = cc.reshape(S5_NCHUNK, S5_GPC, S5_GROUP, S5_STATE)
        m = jnp.einsum("cgpn,gh->cgnhp", t, eye)
        return m.reshape(S5_NCHUNK, S5_SPC, S5_KCH)

    bmat = jnp.concatenate([in_blocks(bbar_re), in_blocks(bbar_im)], axis=-1).astype(BF16)
    cmat = jnp.stack([out_blocks(c_re), -out_blocks(c_im)], axis=1).astype(BF16)
    return bmat, cmat


def _gelu_tanh(x):
    return 0.5 * x * (1.0 + jnp.tanh(math.sqrt(2.0 / math.pi) * (x + 0.044715 * (x * x * x))))


class _Riders:
    def __init__(self, jobs, grid, step_index):
        self.steps = math.prod(grid)
        self.n_axes = len(grid)
        self.step_index = step_index
        self.inputs, self.in_specs, self.out_specs, self.out_shapes, self.src_steps = [], [], [], [], []
        for src, shape, dtype in jobs:
            slab = shape[0] // self.steps
            assert slab * self.steps == shape[0] and slab % 16 == 0
            n_src = 0 if src is None else src.shape[0] // slab
            if src is not None:
                assert n_src * slab == src.shape[0] and src.shape[1] == shape[1]
                self.inputs.append(src)
                self.in_specs.append(pl.BlockSpec(
                    (slab, shape[1]), lambda *idx, n=n_src: (jnp.minimum(step_index(*idx[:self.n_axes]), n - 1), 0)))
            self.out_specs.append(pl.BlockSpec((slab, shape[1]), lambda *idx: (step_index(*idx[:self.n_axes]), 0)))
            self.out_shapes.append(jax.ShapeDtypeStruct(shape, dtype))
            self.src_steps.append(n_src)

    @property
    def n_in(self):
        return len(self.inputs)

    @property
    def n_out(self):
        return len(self.out_shapes)

    def run(self, in_refs, out_refs):
        step = self.step_index(*(pl.program_id(a) for a in range(self.n_axes)))
        srcs = iter(in_refs)
        for dst, n_src in zip(out_refs, self.src_steps):
            if n_src == 0:
                dst[...] = jnp.zeros_like(dst)
                continue
            src = next(srcs)
            if n_src == self.steps:
                dst[...] = src[...].astype(dst.dtype)
                continue

            @pl.when(step < n_src)
            def _(src=src, dst=dst):
                dst[...] = src[...].astype(dst.dtype)

            @pl.when(step >= n_src)
            def _(dst=dst):
                dst[...] = jnp.zeros_like(dst)


def _cast_job(w):
    return (w, w.shape, BF16)


def _s5_kernel(u_ref, h0_ref, are_ref, aim_ref, bmat_ref, cmat_ref, d_ref, gw_ref, gb_ref, *rest,
               nb, tt, lane_chunk, riders):
    rider_in, rest = rest[:riders.n_in], rest[riders.n_in:]
    o_ref, hout_ref = rest[:2]
    rider_out = rest[2:2 + riders.n_out]
    hs_scr, h_scr, io_scr = rest[2 + riders.n_out:]
    riders.run(rider_in, rider_out)

    @pl.when(pl.program_id(0) == 0)
    def _():
        h_scr[...] = h0_ref[...]

    def seq_rows(b):
        return pl.ds(b, tt, stride=nb)

    def lanes(c):
        return slice(c * S5_KCH, (c + 1) * S5_KCH)

    if tt == 1:
        for c in range(S5_NCHUNK):
            io_scr[c] = u_ref[:, lanes(c)].astype(F32)
    else:
        for b in range(nb):
            for c in range(S5_NCHUNK):
                io_scr[c, seq_rows(b), :] = u_ref[b, :, lanes(c)].astype(F32)

    u = jnp.concatenate([io_scr[c] for c in range(S5_NCHUNK)], axis=-1)
    for c in range(S5_NCHUNK):
        bu = jnp.dot(io_scr[c].astype(BF16), bmat_ref[c], preferred_element_type=F32)
        hs_scr[:, c * S5_SPC:(c + 1) * S5_SPC] = bu[:, :S5_SPC]
        hs_scr[:, S5_LANES + c * S5_SPC:S5_LANES + (c + 1) * S5_SPC] = bu[:, S5_SPC:]

    for lc in range(S5_LANES // lane_chunk):
        re_sl = slice(lc * lane_chunk, (lc + 1) * lane_chunk)
        im_sl = slice(S5_LANES + lc * lane_chunk, S5_LANES + (lc + 1) * lane_chunk)
        a_re = jnp.broadcast_to(are_ref[:, re_sl], (nb, lane_chunk))
        a_im = jnp.broadcast_to(aim_ref[:, re_sl], (nb, lane_chunk))

        def step(t, carry):
            h_re, h_im = carry
            r0 = t * nb if isinstance(t, int) else pl.multiple_of(t * nb, nb)
            n_re = a_re * h_re - a_im * h_im + hs_scr[pl.ds(r0, nb), re_sl]
            n_im = a_re * h_im + a_im * h_re + hs_scr[pl.ds(r0, nb), im_sl]
            hs_scr[pl.ds(r0, nb), re_sl] = n_re
            hs_scr[pl.ds(r0, nb), im_sl] = n_im
            return n_re, n_im

        carry = (h_scr[:, re_sl], h_scr[:, im_sl])
        if tt == 1:
            carry = step(0, carry)
        else:
            carry = lax.fori_loop(0, tt, step, carry, unroll=4)
        h_scr[:, re_sl] = carry[0]
        h_scr[:, im_sl] = carry[1]

    hout_ref[...] = h_scr[...]

    ys = []
    for c in range(S5_NCHUNK):
        h_re = hs_scr[:, c * S5_SPC:(c + 1) * S5_SPC].astype(BF16)
        h_im = hs_scr[:, S5_LANES + c * S5_SPC:S5_LANES + (c + 1) * S5_SPC].astype(BF16)
        ys.append(jnp.dot(h_re, cmat_ref[c, 0], preferred_element_type=F32)
                  + jnp.dot(h_im, cmat_ref[c, 1], preferred_element_type=F32))
    y = jnp.concatenate(ys, axis=-1) + d_ref[...] * u
    z = _gelu_tanh(y)
    gate = jnp.dot(z.astype(BF16), gw_ref[...], preferred_element_type=F32) + gb_ref[...]
    out = z * jax.nn.sigmoid(gate)
    if tt == 1:
        o_ref[...] = out.astype(o_ref.dtype)
    else:
        for c in range(S5_NCHUNK):
            io_scr[c] = out[:, lanes(c)]
        for b in range(nb):
            for c in range(S5_NCHUNK):
                o_ref[b, :, lanes(c)] = io_scr[c, seq_rows(b), :].astype(o_ref.dtype)


def _s5_branch(z, h0, a_re, a_im, bmat, cmat, d_skip, glu_w, glu_b, jobs=(), *, single_step, tt):
    bsz, seq, _ = z.shape
    if single_step:
        assert bsz == 1 and tt == 1
        nb = seq
        in_spec = pl.BlockSpec((None, nb, S5_WIDTH), lambda i: (0, 0, ZC_S5))
        out_spec = pl.BlockSpec((None, nb, S5_WIDTH), lambda i: (0, 0, 0))
        grid = (1,)
    else:
        nb = bsz
        in_spec = pl.BlockSpec((nb, tt, S5_WIDTH), lambda i: (0, i, ZC_S5))
        out_spec = pl.BlockSpec((nb, tt, S5_WIDTH), lambda i: (0, i, 0))
        grid = (seq // tt,)
    rblk = tt * nb
    lane_chunk = 1024 if nb <= 8 else 512
    const = lambda shape: pl.BlockSpec(shape, lambda i: (0,) * len(shape))
    riders = _Riders(jobs, grid, lambda i: i)
    outs = pl.pallas_call(
        functools.partial(_s5_kernel, nb=nb, tt=tt, lane_chunk=lane_chunk, riders=riders),
        grid=grid,
        in_specs=[
            in_spec,
            const((nb, 2 * S5_LANES)),
            const((1, S5_LANES)),
            const((1, S5_LANES)),
            const(bmat.shape),
            const(cmat.shape),
            const((1, S5_WIDTH)),
            const((S5_WIDTH, S5_WIDTH)),
            const((1, S5_WIDTH)),
            *riders.in_specs,
        ],
        out_specs=[out_spec, const((nb, 2 * S5_LANES)), *riders.out_specs],
        out_shape=[jax.ShapeDtypeStruct((bsz, seq, S5_WIDTH), BF16),
                   jax.ShapeDtypeStruct((nb, 2 * S5_LANES), F32), *riders.out_shapes],
        scratch_shapes=[pltpu.VMEM((rblk, 2 * S5_LANES), F32), pltpu.VMEM((nb, 2 * S5_LANES), F32),
                        pltpu.VMEM((S5_NCHUNK, rblk, S5_KCH), F32)],
        compiler_params=_params(("arbitrary",)),
        name="s5_branch",
    )(z, h0, a_re.reshape(1, S5_LANES), a_im.reshape(1, S5_LANES), bmat, cmat,
      d_skip.reshape(1, S5_WIDTH), glu_w, glu_b.reshape(1, S5_WIDTH), *riders.inputs)
    return outs[0], outs[1], list(outs[2:])


def _rope_tables(pos):
    half = RET_DK // 2
    freqs = ROPE_BASE ** (-jnp.arange(half, dtype=F32) / half)
    ang = pos[:, None] * freqs[None, :]
    cos = jnp.cos(ang)
    sin = jnp.sin(ang)
    return jnp.concatenate([cos, cos], axis=-1), jnp.concatenate([-sin, sin], axis=-1)


def _rope(x, cos, sin):
    return x * cos + pltpu.roll(x, RET_DK // 2, 1) * sin


def _group_norm(o):
    mu = jnp.mean(o, axis=-1, keepdims=True)
    d = o - mu
    var = jnp.mean(d * d, axis=-1, keepdims=True)
    return d * lax.rsqrt(var + EPS)


def _retention_tables(chunk):
    log_gamma = jnp.log(1.0 - 2.0 ** (-5.0 - jnp.arange(RET_HEADS, dtype=F32)))
    idx = jnp.arange(chunk, dtype=F32)
    diff = idx[:, None] - idx[None, :]
    decay = jnp.where(diff >= 0, jnp.exp(jnp.maximum(diff, 0.0)[None] * log_gamma[:, None, None]), 0.0)
    cross = jnp.exp((idx + 1.0)[None, :] * log_gamma[:, None])[:, :, None]
    kdec = jnp.exp((chunk - 1.0 - idx)[None, :] * log_gamma[:, None])[:, :, None]
    full = jnp.exp(chunk * log_gamma)
    return decay, cross, kdec, full


def _head_cols(refs, h, rows):
    lo = (h % HEADS_PER_ZB) * RET_DV
    return refs[h // HEADS_PER_ZB][rows, lo:lo + RET_DV].astype(F32)


def _retention_kernel(q_ref, k_ref, v0_ref, v1_ref, g0_ref, g1_ref, cos_ref, sin_ref, s0_ref, decay_ref, cross_ref,
                      kdec_ref, full_ref, *rest, n_chunks, riders):
    rider_in, rest = rest[:riders.n_in], rest[riders.n_in:]
    o_ref, sout_ref = rest[:2]
    rider_out, s_scr = rest[2:-1], rest[-1]
    riders.run(rider_in, rider_out)

    @pl.when(pl.program_id(1) == 0)
    def _():
        s_scr[...] = s0_ref[...]

    for c in range(n_chunks):
        rows = slice(c * RET_CHUNK, (c + 1) * RET_CHUNK)
        cos = cos_ref[rows, :]
        sin = sin_ref[rows, :]
        for h in range(RET_HEADS):
            qk_cols = slice(h * RET_DK, (h + 1) * RET_DK)
            v_cols = slice(h * RET_DV, (h + 1) * RET_DV)
            qh = _rope(q_ref[rows, qk_cols].astype(F32), cos, sin)
            kh = _rope(k_ref[rows, qk_cols].astype(F32), cos, sin) * (RET_DK ** -0.5)
            vb = _head_cols((v0_ref, v1_ref), h, rows).astype(BF16)
            qb = qh.astype(BF16)
            state = s_scr[h]
            inner = lax.dot_general(qb, kh.astype(BF16), (((1,), (1,)), ((), ())),
                                    preferred_element_type=F32) * decay_ref[h]
            out = (jnp.dot(inner.astype(BF16), vb, preferred_element_type=F32)
                   + jnp.dot(qb, state.astype(BF16), preferred_element_type=F32) * cross_ref[h])
            kd = (kh * kdec_ref[h]).astype(BF16)
            s_scr[h] = full_ref[h] * state + jnp.dot(kd.T, vb, preferred_element_type=F32)
            gate = _head_cols((g0_ref, g1_ref), h, rows)
            o_ref[rows, v_cols] = (_silu(gate) * _group_norm(out)).astype(o_ref.dtype)

    sout_ref[...] = s_scr[...]


def _retention_prompt(z, state0, pos_offset, jobs=(), *, tt):
    bsz, seq, _ = z.shape
    assert seq % RET_CHUNK == 0 and tt % RET_CHUNK == 0
    cos, sin = _rope_tables(jnp.arange(seq, dtype=F32) + pos_offset)
    decay, cross, kdec, full = _retention_tables(RET_CHUNK)
    full = jnp.broadcast_to(full[:, None, None], (RET_HEADS, 1, RET_DV))
    const = lambda shape: pl.BlockSpec(shape, lambda b, i: (0,) * len(shape))
    zblk = lambda col: pl.BlockSpec((None, tt, ZB), lambda b, i: (b, i, col))
    n_i = seq // tt
    riders = _Riders(jobs, (bsz, n_i), lambda b, i: b * n_i + i)
    outs = pl.pallas_call(
        functools.partial(_retention_kernel, n_chunks=tt // RET_CHUNK, riders=riders),
        grid=(bsz, n_i),
        in_specs=[
            zblk(ZC_Q), zblk(ZC_K), zblk(ZC_V), zblk(ZC_V + 1), zblk(ZC_G), zblk(ZC_G + 1),
            pl.BlockSpec((tt, RET_DK), lambda b, i: (i, 0)),
            pl.BlockSpec((tt, RET_DK), lambda b, i: (i, 0)),
            pl.BlockSpec((None, RET_HEADS, RET_DK, RET_DV), lambda b, i: (b, 0, 0, 0)),
            const((RET_HEADS, RET_CHUNK, RET_CHUNK)),
            const((RET_HEADS, RET_CHUNK, 1)),
            const((RET_HEADS, RET_CHUNK, 1)),
            const((RET_HEADS, 1, RET_DV)),
            *riders.in_specs,
        ],
        out_specs=[
            pl.BlockSpec((None, tt, RET_V), lambda b, i: (b, i, 0)),
            pl.BlockSpec((None, RET_HEADS, RET_DK, RET_DV), lambda b, i: (b, 0, 0, 0)),
            *riders.out_specs,
        ],
        out_shape=[jax.ShapeDtypeStruct((bsz, seq, RET_V), BF16),
                   jax.ShapeDtypeStruct((bsz, RET_HEADS, RET_DK, RET_DV), F32), *riders.out_shapes],
        scratch_shapes=[pltpu.VMEM((RET_HEADS, RET_DK, RET_DV), F32)],
        compiler_params=_params(("arbitrary", "arbitrary")),
        name="retention_prompt",
    )(z, z, z, z, z, z, cos, sin, state0, decay, cross, kdec, full, *riders.inputs)
    return outs[0], outs[1], list(outs[2:])


def _retention_step_kernel(q_ref, k_ref, v0_ref, v1_ref, g0_ref, g1_ref, cos_ref, sin_ref, s_ref, gam_ref, *rest,
                           bb, layer):
    del layer
    o_ref, sout_ref, o_scr = rest[-3:]
    cos = cos_ref[...]
    sin = sin_ref[...]
    for h in range(RET_HEADS):
        qk_cols = slice(h * RET_DK, (h + 1) * RET_DK)
        v_cols = slice(h * RET_DV, (h + 1) * RET_DV)
        qh = _rope(q_ref[:, qk_cols].astype(F32), cos, sin)
        kh = _rope(k_ref[:, qk_cols].astype(F32), cos, sin) * (RET_DK ** -0.5)
        qk = jnp.sum(qh * kh, axis=-1, keepdims=True)
        q_t = qh.T
        k_t = kh.T
        gamma = gam_ref[h]
        v_all = _head_cols((v0_ref, v1_ref), h, slice(None))
        for b in range(bb):
            state = s_ref[b, h]
            vrow = v_all[b:b + 1, :]
            qs = jnp.sum(q_t[:, b:b + 1] * state, axis=0, keepdims=True)
            o_scr[b:b + 1, v_cols] = qk[b:b + 1, :] * vrow + qs * gamma
            sout_ref[b, h] = gamma * state + k_t[:, b:b + 1] * vrow
    for h in range(RET_HEADS):
        v_cols = slice(h * RET_DV, (h + 1) * RET_DV)
        gate = _head_cols((g0_ref, g1_ref), h, slice(None))
        o_ref[:, v_cols] = (_silu(gate) * _group_norm(o_scr[:, v_cols])).astype(o_ref.dtype)


def _retention_step(z, states, layer, stack, pos, *, bb):
    n = z.shape[0]
    cos, sin = _rope_tables(jnp.full((1,), pos, F32))
    log_gamma = jnp.log(1.0 - 2.0 ** (-5.0 - jnp.arange(RET_HEADS, dtype=F32)))
    gam = jnp.broadcast_to(jnp.exp(log_gamma)[:, None, None], (RET_HEADS, 1, RET_DV))
    const = lambda shape: pl.BlockSpec(shape, lambda i: (0,) * len(shape))
    slab = pl.BlockSpec((None, bb, RET_HEADS, RET_DK, RET_DV), lambda i: (layer, i, 0, 0, 0))
    zblk = lambda col: pl.BlockSpec((bb, ZB), lambda i: (i, col))
    in_specs = [
        zblk(ZC_Q), zblk(ZC_K), zblk(ZC_V), zblk(ZC_V + 1), zblk(ZC_G), zblk(ZC_G + 1),
        const((1, RET_DK)),
        const((1, RET_DK)),
        slab,
        const((RET_HEADS, 1, RET_DV)),
    ]
    args = [z, z, z, z, z, z, cos, sin, states, gam]
    if stack is None:
        state_spec = pl.BlockSpec((bb, RET_HEADS, RET_DK, RET_DV), lambda i: (i, 0, 0, 0))
        state_shape = jax.ShapeDtypeStruct(states.shape[1:], F32)
        aliases = {}
    else:
        in_specs.append(pl.BlockSpec(memory_space=pl.ANY))
        args.append(stack)
        state_spec = slab
        state_shape = jax.ShapeDtypeStruct(stack.shape, F32)
        aliases = {len(args) - 1: 1}
    return pl.pallas_call(
        functools.partial(_retention_step_kernel, bb=bb, layer=layer),
        grid=(n // bb,),
        in_specs=in_specs,
        out_specs=[pl.BlockSpec((bb, RET_V), lambda i: (i, 0)), state_spec],
        out_shape=[jax.ShapeDtypeStruct((n, RET_V), BF16), state_shape],
        scratch_shapes=[pltpu.VMEM((bb, RET_V), F32)],
        input_output_aliases=aliases,
        compiler_params=_params(("parallel",)),
        name="retention_step",
    )(*args)


def _layer_norm(y, g, b):
    mu = jnp.mean(y, axis=-1, keepdims=True)
    d = y - mu
    var = jnp.mean(d * d, axis=-1, keepdims=True)
    return d * lax.rsqrt(var + EPS) * g + b


CONV_RB = 128
CONV_PITCH = 2
CONV_LCH = CONV_WIDTH // LANES


def _conv_glu(a_ref, b_ref, pwb_ref):
    a = a_ref[...].astype(F32) + pwb_ref[:, :CONV_WIDTH]
    b = b_ref[...].astype(F32) + pwb_ref[:, CONV_WIDTH:]
    return a * jax.nn.sigmoid(b)


def _conv_kernel(a_ref, b_ref, buf_ref, pwb_ref, dww_ref, dwb_ref, lng_ref, lnb_ref, *rest, tt, riders):
    rider_in, rest = rest[:riders.n_in], rest[riders.n_in:]
    o_ref, hist_ref = rest[:2]
    rider_out, (x_scr, y_scr) = rest[2:-2], rest[-2:]
    riders.run(rider_in, rider_out)
    _conv_body(a_ref, b_ref, buf_ref, pwb_ref, dww_ref, dwb_ref, lng_ref, lnb_ref, o_ref, hist_ref, x_scr, y_scr, tt=tt)


def _conv_body(a_ref, b_ref, buf_ref, pwb_ref, dww_ref, dwb_ref, lng_ref, lnb_ref, o_ref, hist_ref, x_scr, y_scr,
               *, tt):
    i = pl.program_id(1)

    def rows(start, n):
        return pl.ds(CONV_PITCH * start, n, stride=CONV_PITCH)

    def lanes(c):
        return slice(c * LANES, (c + 1) * LANES)

    @pl.when(i == 0)
    def _():
        for c in range(CONV_LCH):
            x_scr[c, rows(0, CONV_HIST), :] = buf_ref[:, lanes(c)]

    @pl.when(i > 0)
    def _():
        for c in range(CONV_LCH):
            x_scr[c, rows(0, CONV_HIST), :] = x_scr[c, rows(tt, CONV_HIST), :]

    glu = _conv_glu(a_ref, b_ref, pwb_ref)
    for c in range(CONV_LCH):
        x_scr[c, rows(CONV_HIST, tt), :] = glu[:, lanes(c)]
    hist_ref[...] = glu[tt - CONV_HIST:, :]

    off = CONV_HIST - (CONV_K - 1)
    for c in range(CONV_LCH):
        def row_block(r, carry, c=c):
            base = r * CONV_RB
            n_grp = CONV_RB // ROW_TILE
            accs = [jnp.broadcast_to(dwb_ref[:, lanes(c)], (ROW_TILE, LANES))] * n_grp
            for m in range(CONV_RB - ROW_TILE + CONV_K):
                win = x_scr[c, rows(base + (off + m), ROW_TILE), :]
                for k in range(m % ROW_TILE, CONV_K, ROW_TILE):
                    j = (m - k) // ROW_TILE
                    if 0 <= j < n_grp:
                        accs[j] = accs[j] + dww_ref[k:k + 1, lanes(c)] * win
            y_scr[pl.ds(pl.multiple_of(base, CONV_RB), CONV_RB), lanes(c)] = jnp.concatenate(accs, axis=0)
            return carry

        lax.fori_loop(0, tt // CONV_RB, row_block, 0)

    o_ref[...] = _silu(_layer_norm(y_scr[...], lng_ref[...], lnb_ref[...])).astype(o_ref.dtype)


def _conv_prompt(z, buf, pw_b, dw_w, dw_b, ln_g, ln_b, jobs=(), *, tt):
    bsz, seq, _ = z.shape
    assert seq >= CONV_HIST and tt >= CONV_HIST
    buf32 = jnp.pad(buf, ((0, 0), (CONV_HIST - (CONV_K - 1), 0), (0, 0)))
    const = lambda shape: pl.BlockSpec(shape, lambda b, i: (0,) * len(shape))
    zblk = lambda col: pl.BlockSpec((None, tt, ZB), lambda b, i: (b, i, col))
    n_i = seq // tt
    riders = _Riders(jobs, (bsz, n_i), lambda b, i: b * n_i + i)
    out, hist, *cast = pl.pallas_call(
        functools.partial(_conv_kernel, tt=tt, riders=riders),
        grid=(bsz, n_i),
        in_specs=[
            zblk(ZC_CONV), zblk(ZC_CONV + 1),
            pl.BlockSpec((None, CONV_HIST, CONV_WIDTH), lambda b, i: (b, 0, 0)),
            const((1, 2 * CONV_WIDTH)),
            const((CONV_K, CONV_WIDTH)),
            const((1, CONV_WIDTH)),
            const((1, CONV_WIDTH)),
            const((1, CONV_WIDTH)),
            *riders.in_specs,
        ],
        out_specs=[
            pl.BlockSpec((None, tt, CONV_WIDTH), lambda b, i: (b, i, 0)),
            pl.BlockSpec((None, CONV_HIST, CONV_WIDTH), lambda b, i: (b, 0, 0)),
            *riders.out_specs,
        ],
        out_shape=[jax.ShapeDtypeStruct((bsz, seq, CONV_WIDTH), BF16),
                   jax.ShapeDtypeStruct((bsz, CONV_HIST, CONV_WIDTH), F32), *riders.out_shapes],
        scratch_shapes=[pltpu.VMEM((CONV_LCH, CONV_PITCH * (CONV_HIST + tt), LANES), F32),
                        pltpu.VMEM((tt, CONV_WIDTH), F32)],
        compiler_params=_params(("arbitrary", "arbitrary")),
        name="conv_prompt",
    )(z, z, buf32, pw_b.reshape(1, -1), dw_w, dw_b.reshape(1, -1), ln_g.reshape(1, -1), ln_b.reshape(1, -1),
      *riders.inputs)
    return out, hist[:, CONV_HIST - (CONV_K - 1):, :], cast


def _conv_step_kernel(a_ref, b_ref, buf_ref, pwb_ref, dww_ref, dwb_ref, lng_ref, lnb_ref, o_ref, hist_ref):
    hist_len = CONV_K - 1
    glu = _conv_glu(a_ref, b_ref, pwb_ref)
    acc = dwb_ref[...] + dww_ref[hist_len:hist_len + 1, :] * glu
    for k in range(hist_len):
        acc = acc + dww_ref[k:k + 1, :] * buf_ref[k]
    o_ref[...] = _silu(_layer_norm(acc, lng_ref[...], lnb_ref[...])).astype(o_ref.dtype)
    for k in range(hist_len - 1):
        hist_ref[k] = buf_ref[k + 1]
    hist_ref[hist_len - 1] = glu


def _conv_step(z, bufs, layer, pw_b, dw_w, dw_b, ln_g, ln_b, *, bb):
    n = z.shape[0]
    hist_len = CONV_K - 1
    const = lambda shape: pl.BlockSpec(shape, lambda i: (0,) * len(shape))
    zblk = lambda col: pl.BlockSpec((bb, ZB), lambda i: (i, col))
    return pl.pallas_call(
        _conv_step_kernel,
        grid=(n // bb,),
        in_specs=[
            zblk(ZC_CONV), zblk(ZC_CONV + 1),
            pl.BlockSpec((None, hist_len, bb, CONV_WIDTH), lambda i: (layer, 0, i, 0)),
            const((1, 2 * CONV_WIDTH)),
            const((CONV_K, CONV_WIDTH)),
            const((1, CONV_WIDTH)),
            const((1, CONV_WIDTH)),
            const((1, CONV_WIDTH)),
        ],
        out_specs=[
            pl.BlockSpec((bb, CONV_WIDTH), lambda i: (i, 0)),
            pl.BlockSpec((hist_len, bb, CONV_WIDTH), lambda i: (0, i, 0)),
        ],
        out_shape=[jax.ShapeDtypeStruct((n, CONV_WIDTH), BF16),
                   jax.ShapeDtypeStruct((hist_len, n, CONV_WIDTH), F32)],
        compiler_params=_params(("parallel",)),
        name="conv_step",
    )(z, z, bufs, pw_b.reshape(1, -1), dw_w, dw_b.reshape(1, -1), ln_g.reshape(1, -1), ln_b.reshape(1, -1))


def _rows_to_tiles(tile_ref, x, rows):
    for c in range(ROW_CHUNKS):
        tile_ref[pl.ds(c, rows, stride=ROW_TILE), :] = x[:, c * LANES:(c + 1) * LANES]


def _tiles_chunk(tile_ref, c, rows):
    return tile_ref[pl.ds(c, rows, stride=ROW_TILE), :]


def _tiles_to_rows(tile_ref, rows):
    return jnp.concatenate([_tiles_chunk(tile_ref, c, rows) for c in range(ROW_CHUNKS)], axis=-1)


def _merge_kernel(x_ref, s5_ref, ret_ref, conv_ref, *rest, tiled_u):
    gate_refs = rest[:N_BRANCH * D_MODEL // ZB]
    ps5_ref, pret_ref, pconv_ref, wout_ref, g_ref, h_ref, u_ref = rest[len(gate_refs):]
    per_branch = D_MODEL // ZB

    def gate(n):
        cols = [gate_refs[n * per_branch + j][...] for j in range(per_branch)]
        return jax.nn.sigmoid(jnp.concatenate(cols, axis=-1).astype(F32))

    merged = (gate(0) * jnp.dot(s5_ref[...], ps5_ref[...], preferred_element_type=F32)
              + gate(1) * jnp.dot(ret_ref[...], pret_ref[...], preferred_element_type=F32)
              + gate(2) * jnp.dot(conv_ref[...], pconv_ref[...], preferred_element_type=F32))
    h = x_ref[...] + _bdot(merged, wout_ref[...])
    h_ref[...] = h
    ms = jnp.mean(h * h, axis=-1, keepdims=True)
    u = h * lax.rsqrt(ms + EPS) * g_ref[...]
    if tiled_u:
        _rows_to_tiles(u_ref, u, u.shape[0])
    else:
        u_ref[...] = u.astype(u_ref.dtype)


def _merge(x, s5_out, ret_out, conv_out, z, s5_proj, ret_proj, conv_proj, w_out, norm_g, *, tm, tiled_u):
    bsz, seq, d = x.shape
    const = lambda shape: pl.BlockSpec(shape, lambda b, i: (0,) * len(shape))
    tok = lambda w, col=0: pl.BlockSpec((None, tm, w), lambda b, i: (b, i, col))
    n_i = seq // tm
    n_gate = N_BRANCH * d // ZB
    if tiled_u:
        u_spec = pl.BlockSpec((tm * ROW_TILE, LANES), lambda b, i: (b * n_i + i, 0))
        u_shape = jax.ShapeDtypeStruct((bsz * seq * ROW_TILE, LANES), F32)
    else:
        u_spec = tok(d)
        u_shape = jax.ShapeDtypeStruct((bsz, seq, d), BF16)
    return pl.pallas_call(
        functools.partial(_merge_kernel, tiled_u=tiled_u),
        grid=(bsz, seq // tm),
        in_specs=[
            tok(d),
            tok(S5_WIDTH),
            tok(RET_V),
            tok(CONV_WIDTH),
            *[tok(ZB, ZC_GATE + j) for j in range(n_gate)],
            const((S5_WIDTH, d)),
            const((RET_V, d)),
            const((CONV_WIDTH, d)),
            const((d, d)),
            const((1, d)),
        ],
        out_specs=[tok(d), u_spec],
        out_shape=[jax.ShapeDtypeStruct((bsz, seq, d), F32), u_shape],
        compiler_params=_params(("parallel", "parallel")),
        name="merge",
    )(x, s5_out, ret_out, conv_out, *([z] * n_gate), s5_proj, ret_proj, conv_proj, w_out, norm_g.reshape(1, d))


def _ffn_kernel(h_ref, u_ref, wg_ref, wu_ref, wd_ref, *rest, riders):
    rider_in, o_ref, rider_out = rest[:riders.n_in], rest[riders.n_in], rest[riders.n_in + 1:]
    f = pl.program_id(1)
    ub = u_ref[...].astype(BF16)
    gate = jnp.dot(ub, wg_ref[...], preferred_element_type=F32)
    up = jnp.dot(ub, wu_ref[...], preferred_element_type=F32)
    part = _bdot(_silu(gate) * up, wd_ref[...])

    @pl.when(f == 0)
    def _():
        o_ref[...] = h_ref[...] + part

    @pl.when(f > 0)
    def _():
        o_ref[...] = o_ref[...] + part

    riders.run(rider_in, rider_out)


def _ffn(h, u, w_gate, w_up, w_down, jobs=(), *, tm, tf):
    rows, d = h.shape
    dff = w_gate.shape[1]
    n_i, n_f = rows // tm, dff // tf
    tok = pl.BlockSpec((tm, d), lambda i, f: (i, 0))
    riders = _Riders(jobs, (n_i, n_f), lambda i, f: i * n_f + f)
    outs = pl.pallas_call(
        functools.partial(_ffn_kernel, riders=riders),
        grid=(n_i, n_f),
        in_specs=[tok, tok,
                  pl.BlockSpec((d, tf), lambda i, f: (0, f)),
                  pl.BlockSpec((d, tf), lambda i, f: (0, f)),
                  pl.BlockSpec((tf, d), lambda i, f: (f, 0)),
                  *riders.in_specs],
        out_specs=[tok, *riders.out_specs],
        out_shape=[jax.ShapeDtypeStruct((rows, d), F32), *riders.out_shapes],
        compiler_params=_params(("arbitrary", "arbitrary")),
        name="ffn_dense",
    )(h, u, w_gate, w_up, w_down, *riders.inputs)
    return outs[0], list(outs[1:])


def _split_bf16(x):
    hi = x.astype(BF16)
    return hi, (x - hi.astype(F32)).astype(BF16)


def _router_kernel(u_ref, rt_ref, tri_ref, idx_ref, wts_ref, rank_ref, cnt_ref, cnt_scr, *, tm):
    @pl.when(pl.program_id(0) == 0)
    def _():
        cnt_scr[...] = jnp.zeros_like(cnt_scr)

    u_hi, u_lo = _split_bf16(_tiles_to_rows(u_ref, tm))
    r_hi, r_lo = _split_bf16(rt_ref[...])
    dn = (((1,), (1,)), ((), ()))
    logits = (lax.dot_general(r_hi, u_hi, dn, preferred_element_type=F32)
              + lax.dot_general(r_lo, u_hi, dn, preferred_element_type=F32)
              + lax.dot_general(r_hi, u_lo, dn, preferred_element_type=F32))
    eidx = lax.broadcasted_iota(jnp.int32, logits.shape, 0)
    m1 = jnp.max(logits, axis=0, keepdims=True)
    i1 = jnp.min(jnp.where(logits == m1, eidx, N_EXPERTS), axis=0, keepdims=True)
    rest = jnp.where(eidx == i1, -jnp.inf, logits)
    m2 = jnp.max(rest, axis=0, keepdims=True)
    i2 = jnp.min(jnp.where(rest == m2, eidx, N_EXPERTS), axis=0, keepdims=True)
    e2 = jnp.exp(m2 - m1)
    w1 = 1.0 / (1.0 + e2)
    idx_ref[...] = jnp.concatenate([i1, i2], axis=0)
    wts_ref[...] = jnp.concatenate([w1, e2 * w1], axis=0)

    hit1 = eidx == i1
    hit2 = eidx == i2
    hits = jnp.where(hit1 | hit2, 1.0, 0.0)
    before = jnp.dot(hits.astype(BF16), tri_ref[...], preferred_element_type=F32) + cnt_scr[...]
    rank_ref[...] = jnp.concatenate(
        [jnp.sum(jnp.where(hit1, before, 0.0), axis=0, keepdims=True),
         jnp.sum(jnp.where(hit2, before, 0.0), axis=0, keepdims=True)], axis=0).astype(jnp.int32)
    cnt_scr[...] = cnt_scr[...] + jnp.sum(hits, axis=1, keepdims=True)
    cnt_ref[...] = cnt_scr[...]


def _router(u8, router, *, tm):
    rows = u8.shape[0] // ROW_TILE
    d = D_MODEL
    tri = (jnp.arange(tm)[:, None] < jnp.arange(tm)[None, :]).astype(BF16)
    const = lambda shape: pl.BlockSpec(shape, lambda i: (0,) * len(shape))
    lane = pl.BlockSpec((TOP_K, tm), lambda i: (0, i))
    return pl.pallas_call(
        functools.partial(_router_kernel, tm=tm),
        grid=(rows // tm,),
        in_specs=[pl.BlockSpec((tm * ROW_TILE, LANES), lambda i: (i, 0)), const((N_EXPERTS, d)), const((tm, tm))],
        out_specs=[lane, lane, lane, const((N_EXPERTS, 1))],
        out_shape=[jax.ShapeDtypeStruct((TOP_K, rows), jnp.int32), jax.ShapeDtypeStruct((TOP_K, rows), F32),
                   jax.ShapeDtypeStruct((TOP_K, rows), jnp.int32), jax.ShapeDtypeStruct((N_EXPERTS, 1), F32)],
        scratch_shapes=[pltpu.VMEM((N_EXPERTS, 1), F32)],
        compiler_params=_params(("arbitrary",)),
        name="moe_router",
    )(u8, router.T, tri)


DMA_UNROLL = 8


def _token_tile(ref, r):
    return ref.at[pl.ds(pl.multiple_of(r * ROW_TILE, ROW_TILE), ROW_TILE)]


def _dispatch_kernel(pos_ref, u_ref, init_ref, xs_ref, sem, *, tb):
    del init_ref

    def start(r, carry):
        for s in range(TOP_K):
            pltpu.make_async_copy(_token_tile(u_ref, r), _token_tile(xs_ref, pos_ref[s, r]),
                                  sem.at[s]).start(priority=s)
        return carry

    lax.fori_loop(0, tb, start, 0, unroll=DMA_UNROLL)
    for s in range(TOP_K):
        pltpu.make_async_copy(u_ref, xs_ref.at[pl.ds(0, tb * ROW_TILE)], sem.at[s]).wait()


def _dispatch(u8, pos, xs8, *, tb):
    rows = u8.shape[0] // ROW_TILE
    return pl.pallas_call(
        functools.partial(_dispatch_kernel, tb=tb),
        grid=(rows // tb,),
        in_specs=[
            pl.BlockSpec((TOP_K, tb), lambda i: (0, i), memory_space=pltpu.SMEM),
            pl.BlockSpec((tb * ROW_TILE, LANES), lambda i: (i, 0)),
            pl.BlockSpec(memory_space=pl.ANY),
        ],
        out_specs=pl.BlockSpec(memory_space=pl.ANY),
        out_shape=jax.ShapeDtypeStruct(xs8.shape, xs8.dtype),
        scratch_shapes=[pltpu.SemaphoreType.DMA((TOP_K,))],
        input_output_aliases={2: 0},
        compiler_params=_params(("arbitrary",)),
        name="moe_dispatch",
    )(pos, u8, xs8)


def _experts_kernel(te_ref, nu_ref, x_ref, wg_ref, wu_ref, wd_ref, o_ref, x_scr, acc_scr, *, tm):
    del te_ref
    i = pl.program_id(0)
    f = pl.program_id(1)

    def swiglu_rows(n):
        xb = x_scr[0:n, :]
        gate = jnp.dot(xb, wg_ref[...], preferred_element_type=F32)
        up = jnp.dot(xb, wu_ref[...], preferred_element_type=F32)
        part = _bdot(_silu(gate) * up, wd_ref[...])

        @pl.when(f == 0)
        def _():
            acc_scr[0:n, :] = part
            if n < tm:
                acc_scr[n:tm, :] = jnp.zeros((tm - n, D_MODEL), F32)

        @pl.when(f > 0)
        def _():
            acc_scr[0:n, :] = acc_scr[0:n, :] + part

    @pl.when(i < nu_ref[0])
    def _():
        @pl.when(f == 0)
        def _():
            x_scr[...] = _tiles_to_rows(x_ref, tm).astype(BF16)

        half = tm // 2
        valid = nu_ref[1 + i]

        @pl.when(valid > half)
        def _():
            swiglu_rows(tm)

        @pl.when(valid <= half)
        def _():
            swiglu_rows(half)

        @pl.when(f == pl.num_programs(1) - 1)
        def _():
            _rows_to_tiles(o_ref, acc_scr[...], tm)

    @pl.when(i >= nu_ref[0])
    def _():
        o_ref[...] = jnp.zeros_like(o_ref)


def _experts(xs8, tile_expert, n_used, w_gate, w_up, w_down, *, tm, tf):
    rows = xs8.shape[0] // ROW_TILE
    d = D_MODEL
    dff = w_gate.shape[-1]
    n_f = dff // tf
    last_f = n_f - 1

    def row_map(i, f, te, nu):
        return (jnp.minimum(i, nu[0] - 1), 0)

    def fsel(i, f, nu):
        return jnp.where(i < nu[0], f, last_f)

    grid_spec = pltpu.PrefetchScalarGridSpec(
        num_scalar_prefetch=2,
        grid=(rows // tm, n_f),
        in_specs=[
            pl.BlockSpec((tm * ROW_TILE, LANES), row_map),
            pl.BlockSpec((None, d, tf), lambda i, f, te, nu: (te[i], 0, fsel(i, f, nu))),
            pl.BlockSpec((None, d, tf), lambda i, f, te, nu: (te[i], 0, fsel(i, f, nu))),
            pl.BlockSpec((None, tf, d), lambda i, f, te, nu: (te[i], fsel(i, f, nu), 0)),
        ],
        out_specs=pl.BlockSpec((tm * ROW_TILE, LANES), lambda i, f, te, nu: (i, 0)),
        scratch_shapes=[pltpu.VMEM((tm, d), BF16), pltpu.VMEM((tm, d), F32)],
    )
    return pl.pallas_call(
        functools.partial(_experts_kernel, tm=tm),
        grid_spec=grid_spec,
        out_shape=jax.ShapeDtypeStruct(xs8.shape, F32),
        compiler_params=_params(("arbitrary", "arbitrary")),
        name="moe_experts",
    )(tile_expert, n_used, xs8, w_gate, w_up, w_down)


def _combine_kernel(pos_ref, h_ref, wts_ref, g_ref, ys_ref, o_ref, y_scr, sem, *, tb, final_norm):
    def start(r, carry):
        for s in range(TOP_K):
            pltpu.make_async_copy(_token_tile(ys_ref, pos_ref[s, r]), _token_tile(y_scr.at[s], r),
                                  sem.at[s]).start(priority=s)
        return carry

    lax.fori_loop(0, tb, start, 0, unroll=DMA_UNROLL)
    for s in range(TOP_K):
        pltpu.make_async_copy(ys_ref.at[pl.ds(0, tb * ROW_TILE)], y_scr.at[s], sem.at[s]).wait()
    w1 = wts_ref[:, 0:1]
    w2 = wts_ref[:, 1:2]
    moe = jnp.concatenate([w1 * _tiles_chunk(y_scr.at[0], c, tb) + w2 * _tiles_chunk(y_scr.at[1], c, tb)
                           for c in range(ROW_CHUNKS)], axis=-1)
    out = h_ref[...] + moe
    if final_norm:
        ms = jnp.mean(out * out, axis=-1, keepdims=True)
        out = out * lax.rsqrt(ms + EPS) * g_ref[...]
    o_ref[...] = out


def _combine(h, pos, wts_t, ys8, final_g, *, tb):
    rows, d = h.shape
    final_norm = final_g is not None
    gain = final_g.reshape(1, d) if final_norm else jnp.ones((1, d), F32)
    return pl.pallas_call(
        functools.partial(_combine_kernel, tb=tb, final_norm=final_norm),
        grid=(rows // tb,),
        in_specs=[
            pl.BlockSpec((TOP_K, tb), lambda i: (0, i), memory_space=pltpu.SMEM),
            pl.BlockSpec((tb, d), lambda i: (i, 0)),
            pl.BlockSpec((tb, TOP_K), lambda i: (i, 0)),
            pl.BlockSpec((1, d), lambda i: (0, 0)),
            pl.BlockSpec(memory_space=pl.ANY),
        ],
        out_specs=pl.BlockSpec((tb, d), lambda i: (i, 0)),
        out_shape=jax.ShapeDtypeStruct((rows, d), F32),
        scratch_shapes=[pltpu.VMEM((TOP_K, tb * ROW_TILE, LANES), F32), pltpu.SemaphoreType.DMA((TOP_K,))],
        compiler_params=_params(("arbitrary",)),
        name="moe_combine",
    )(pos, h, wts_t, gain, ys8)


PROJ_TN = 2560
S5_TT = 128
MIX_TT = 512
FFN_TF = 1408
MOE_TM = 512
MOE_TF = 1792
MOE_TB = 1024


def _moe_tiles(n_rows):
    return n_rows // MOE_TM + N_EXPERTS


def _moe_buffer_shape(n_rows):
    return (_moe_tiles(n_rows) * MOE_TM * ROW_TILE, LANES)


def _moe(h_list, u8_list, router, w_gate, w_up, w_down, xs8, final_g):
    routes = []
    for u8 in u8_list:
        routes.append(_router(u8, router, tm=min(512, u8.shape[0] // ROW_TILE)))
    counts = [r[3][:, 0].astype(jnp.int32) for r in routes]
    total = sum(counts)
    padded = ((total + MOE_TM - 1) // MOE_TM) * MOE_TM
    ends = jnp.cumsum(padded)
    starts = ends - padded
    n_rows = sum(h.shape[0] for h in h_list) * TOP_K
    n_tiles = _moe_tiles(n_rows)
    assert xs8.shape == _moe_buffer_shape(n_rows)
    n_used = (ends[-1] // MOE_TM).astype(jnp.int32)
    tile_start = jnp.arange(n_tiles, dtype=jnp.int32) * MOE_TM
    tile_expert = jnp.sum((tile_start[:, None] >= ends[None, :]).astype(jnp.int32), axis=1)
    last_expert = jnp.sum((((n_used - 1) * MOE_TM) >= ends).astype(jnp.int32))
    tile_expert = jnp.where(jnp.arange(n_tiles) < n_used, tile_expert, last_expert).astype(jnp.int32)

    poss = []
    seen = jnp.zeros((N_EXPERTS,), jnp.int32)
    for u8, (idx, _, rank, _), cnt in zip(u8_list, routes, counts):
        base = starts + seen
        pos = rank
        for e in range(N_EXPERTS):
            pos = pos + jnp.where(idx == e, base[e], 0)
        poss.append(pos)
        seen = seen + cnt
        xs8 = _dispatch(u8, pos, xs8, tb=min(2 * MOE_TB, u8.shape[0] // ROW_TILE))
    group_rows = jnp.sum(jnp.where(jnp.arange(N_EXPERTS)[None, :] == tile_expert[:, None],
                                   (starts + total)[None, :], 0), axis=1)
    tile_valid = jnp.clip(group_rows - tile_start, 0, MOE_TM)
    tile_info = jnp.concatenate([n_used.reshape(1), tile_valid]).astype(jnp.int32)
    ys8 = _experts(xs8, tile_expert, tile_info, w_gate, w_up, w_down, tm=MOE_TM, tf=MOE_TF)
    outs = []
    for h, pos, (_, wts, _, _) in zip(h_list, poss, routes):
        outs.append(_combine(h, pos, wts.T, ys8, final_g, tb=min(MOE_TB, h.shape[0])))
    return outs


def _rmsnorm_kernel(x_ref, g_ref, o_ref):
    x = x_ref[...]
    ms = jnp.mean(x * x, axis=-1, keepdims=True)
    o_ref[...] = x * lax.rsqrt(ms + EPS) * g_ref[...]


def _rmsnorm(x, g, *, tm):
    rows, d = x.shape
    return pl.pallas_call(
        _rmsnorm_kernel,
        grid=(rows // tm,),
        in_specs=[pl.BlockSpec((tm, d), lambda i: (i, 0)), pl.BlockSpec((1, d), lambda i: (0, 0))],
        out_specs=pl.BlockSpec((tm, d), lambda i: (i, 0)),
        out_shape=jax.ShapeDtypeStruct((rows, d), F32),
        compiler_params=_params(("parallel",)),
        name="final_norm",
    )(x, g.reshape(1, d))


def _pack_s5_state(re, im):
    n = re.shape[0]
    return jnp.concatenate([re.reshape(n, S5_LANES), im.reshape(n, S5_LANES)], axis=-1)


def _unpack_s5_state(h):
    n = h.shape[0]
    return (h[:, :S5_LANES].reshape(n, S5_GROUPS, S5_STATE), h[:, S5_LANES:].reshape(n, S5_GROUPS, S5_STATE))


def _mixer(x, z, s5_h0, ret_s0, conv_buf, pos_offset, p, *, single_step, tiled_u, layer=0, ret_stack=None, jobs=None):
    jobs = jobs or {}
    done = {}
    bsz, seq, d = x.shape
    s5_out, s5_state, done["s5"] = _s5_branch(z, s5_h0, p["a_re"], p["a_im"], p["bmat"], p["cmat"], p["d_skip"],
                                              p["glu_w"], p["glu_b"], jobs.get("s5", ()), single_step=single_step,
                                              tt=1 if single_step else S5_TT)

    if single_step:
        z2 = z.reshape(seq, N_IN)
        ret_out, ret_state = _retention_step(z2, ret_s0, layer, ret_stack, pos_offset, bb=16)
        conv_out, conv_state = _conv_step(z2, conv_buf, layer, p["pw_b"], p["dw_w"], p["dw_b"], p["ln_g"], p["ln_b"],
                                          bb=32)
        ret_out = ret_out.reshape(bsz, seq, RET_V)
        conv_out = conv_out.reshape(bsz, seq, CONV_WIDTH)
    else:
        ret_out, ret_state, done["ret"] = _retention_prompt(z, ret_s0, pos_offset, jobs.get("ret", ()), tt=MIX_TT)
        conv_out, conv_state, done["conv"] = _conv_prompt(z, conv_buf, p["pw_b"], p["dw_w"], p["dw_b"], p["ln_g"],
                                                          p["ln_b"], jobs.get("conv", ()), tt=MIX_TT)

    h, u = _merge(x, s5_out, ret_out, conv_out, z, p["s5_proj"], p["ret_proj"], p["conv_proj"], p["w_out"],
                  p["norm_ffn_g"], tm=min(512, seq), tiled_u=tiled_u)
    return h, u, (s5_state, ret_state, conv_state), done


def kernel(x_prompt, x_sample, state_s5_re, state_s5_im, state_ret, state_conv, norm_mix_g, w_in, s5_lambda_re, s5_lambda_im, s5_log_dt, s5_b_re, s5_b_im, s5_c_re, s5_c_im, s5_d, s5_glu_w, s5_glu_b, s5_proj, ret_proj, conv_pw_b, conv_dw_w, conv_dw_b, conv_ln_g, conv_ln_b, conv_proj, w_out, norm_ffn_g, ffn_w_gate, ffn_w_up, ffn_w_down, moe_router, moe_w_gate, moe_w_up, moe_w_down, norm_final_g):
    depth = w_in.shape[0]
    bp, seq, d = x_prompt.shape
    ns = x_sample.shape[0]
    past_len = 16384
    bf = lambda a: a.astype(BF16)

    hp = x_prompt
    hs = x_sample.reshape(1, ns, d)
    zero_s5 = jnp.zeros((bp, 2 * S5_LANES), F32)
    zero_ret = jnp.zeros((bp, RET_HEADS, RET_DK, RET_DV), F32)
    zero_conv = jnp.zeros((bp, CONV_K - 1, CONV_WIDTH), F32)

    conv_rows = jnp.transpose(state_conv, (0, 2, 1, 3))
    p_states, s_states = [], []
    ret_stack = None
    normed = False
    groups = depth * S5_GROUPS
    disc = _s5_discretize(s5_lambda_re.reshape(groups, S5_STATE), s5_lambda_im.reshape(groups, S5_STATE),
                          s5_log_dt.reshape(groups), s5_b_re.reshape(groups, S5_STATE, S5_GROUP),
                          s5_b_im.reshape(groups, S5_STATE, S5_GROUP))
    disc = [a.reshape((depth, S5_GROUPS) + a.shape[1:]) for a in disc]
    for l in range(depth):
        is_moe = l % 2 == 1
        a_re, a_im, bb_re, bb_im = (a[l] for a in disc)
        bmat, cmat = _s5_block_mats(bb_re, bb_im, s5_c_re[l], s5_c_im[l])
        p = dict(a_re=a_re, a_im=a_im, bmat=bmat, cmat=cmat, d_skip=s5_d[l], glu_w=bf(s5_glu_w[l]),
                 glu_b=s5_glu_b[l], pw_b=conv_pw_b[l], dw_w=conv_dw_w[l], dw_b=conv_dw_b[l], ln_g=conv_ln_g[l],
                 ln_b=conv_ln_b[l], s5_proj=bf(s5_proj[l]), ret_proj=bf(ret_proj[l]), conv_proj=bf(conv_proj[l]),
                 w_out=bf(w_out[l]), norm_ffn_g=norm_ffn_g[l])
        j = l // 2
        flat = lambda w: w.reshape(-1, w.shape[-1])
        if is_moe:
            jobs = {"conv": [_cast_job(flat(moe_w_up[j]))],
                    "proj": [(None, _moe_buffer_shape((bp * seq + ns) * TOP_K), F32)]}
        else:
            jobs = {"s5": [_cast_job(ffn_w_gate[j]), _cast_job(ffn_w_up[j]), _cast_job(ffn_w_down[j])]}
            if l + 1 < depth:
                jm = (l + 1) // 2
                jobs["conv"] = [_cast_job(flat(moe_w_gate[jm]))]
                jobs["ret"] = [_cast_job(flat(moe_w_down[jm]))]
        zp, zs, proj_done = _norm_proj(hp, hs.reshape(ns, d), norm_mix_g[l], w_in, l, jobs.pop("proj", ()),
                                       tt=512, tn=PROJ_TN)
        hp, up, st_p, done = _mixer(hp, zp, zero_s5, zero_ret, zero_conv, 0.0, p, single_step=False, tiled_u=is_moe,
                                    layer=l, jobs=jobs)
        hs, us, st_s, _ = _mixer(hs, zs.reshape(1, ns, N_IN), _pack_s5_state(state_s5_re[l], state_s5_im[l]),
                                 state_ret, conv_rows, float(past_len), p, single_step=True, tiled_u=is_moe, layer=l,
                                 ret_stack=ret_stack)
        p_states.append(st_p)
        s_states.append(st_s)

        hp2, hs2 = hp.reshape(bp * seq, d), hs.reshape(ns, d)
        if is_moe:
            ret_stack = st_s[1]
            final_g = norm_final_g if l == depth - 1 else None
            normed = final_g is not None
            w_gate_b, w_down_b = (c.reshape(w.shape[1:]) for c, w in zip(early, (moe_w_gate, moe_w_down)))
            w_up_b = done["conv"][0].reshape(moe_w_up.shape[1:])
            hp2, hs2 = _moe([hp2, hs2], [up, us], moe_router[j], w_gate_b, w_up_b, w_down_b, proj_done[0], final_g)
        else:
            wg, wu, wd = done["s5"]
            if l + 1 < depth:
                early = [done["conv"][0], done["ret"][0]]
            ffn_jobs = []
            if l == 0 and depth > 1:
                rows = st_s[1].size // RET_DV
                ffn_jobs = [(st_s[1].reshape(rows, RET_DV), (depth * rows, RET_DV), F32)]
            hp2, made = _ffn(hp2, up.reshape(bp * seq, d), wg, wu, wd, ffn_jobs, tm=512, tf=FFN_TF)
            if l == 0:
                ret_stack = made[0].reshape(state_ret.shape) if ffn_jobs else st_s[1][None]
            else:
                ret_stack = st_s[1]
            hs2, _ = _ffn(hs2, us.reshape(ns, d), wg, wu, wd, tm=ns, tf=FFN_TF)
        hp = hp2.reshape(bp, seq, d)
        hs = hs2.reshape(1, ns, d)

    if normed:
        y_prompt, y_sample = hp, hs.reshape(ns, 1, d)
    else:
        y_prompt = _rmsnorm(hp.reshape(bp * seq, d), norm_final_g, tm=512).reshape(bp, seq, d)
        y_sample = _rmsnorm(hs.reshape(ns, d), norm_final_g, tm=ns).reshape(ns, 1, d)

    def stack_s5(states):
        s5 = [_unpack_s5_state(s[0]) for s in states]
        return jnp.stack([a for a, _ in s5]), jnp.stack([b for _, b in s5])

    p_re, p_im = stack_s5(p_states)
    s_re, s_im = stack_s5(s_states)
    p_ret = jnp.stack([s[1] for s in p_states])
    p_conv = jnp.stack([s[2] for s in p_states])
    s_conv = jnp.transpose(jnp.stack([s[2] for s in s_states]), (0, 2, 1, 3))
    return (y_prompt, y_sample, p_re, p_im, p_ret, p_conv, s_re, s_im, ret_stack, s_conv)
```

```python
import functools
import math

import jax
import jax.numpy as jnp
from jax import lax
from jax.experimental import pallas as pl
from jax.experimental.pallas import tpu as pltpu

F32 = jnp.float32
BF16 = jnp.bfloat16

D_MODEL = 1024
S5_WIDTH = 512
S5_GROUP = 16
S5_GROUPS = 32
S5_STATE = 64
S5_LANES = S5_GROUPS * S5_STATE
RET_HEADS = 4
RET_DK = 128
RET_DV = 256
RET_QK = RET_HEADS * RET_DK
RET_V = RET_HEADS * RET_DV
RET_CHUNK = 128
ROPE_BASE = 10000.0
CONV_WIDTH = 512
CONV_K = 31
CONV_HIST = 32
N_EXPERTS = 8
TOP_K = 2
N_BRANCH = 3
EPS = 1e-6
N_IN = S5_WIDTH + 2 * RET_QK + 2 * RET_V + 2 * CONV_WIDTH + N_BRANCH * D_MODEL
ZB = 512
ZC_S5 = 0
ZC_Q = ZC_S5 + S5_WIDTH // ZB
ZC_K = ZC_Q + RET_QK // ZB
ZC_V = ZC_K + RET_QK // ZB
ZC_G = ZC_V + RET_V // ZB
ZC_CONV = ZC_G + RET_V // ZB
ZC_GATE = ZC_CONV + 2 * CONV_WIDTH // ZB
HEADS_PER_ZB = ZB // RET_DV

ROW_TILE = 8
LANES = 128
ROW_CHUNKS = D_MODEL // LANES
VMEM_LIMIT = 48 * 1024 * 1024
ROW_SPLIT = 2


def _params(sem):
    return pltpu.CompilerParams(dimension_semantics=sem, vmem_limit_bytes=VMEM_LIMIT)


def _silu(x):
    return x * jax.nn.sigmoid(x)


def _bdot(a, b):
    return jnp.dot(a.astype(BF16), b.astype(BF16), preferred_element_type=F32)


def _norm_proj_kernel(x_ref, g_ref, w_ref, x2_ref, *rest, riders):
    rider_in, (o_ref, o2_ref) = rest[:riders.n_in], rest[riders.n_in:riders.n_in + 2]
    rider_out, w_scr = rest[riders.n_in + 2:-1], rest[-1]
    riders.run(rider_in, rider_out)

    def project(x):
        ms = jnp.mean(x * x, axis=-1, keepdims=True)
        u = (x * lax.rsqrt(ms + EPS) * g_ref[...]).astype(BF16)
        return jnp.dot(u, w_scr[...], preferred_element_type=F32).astype(BF16)

    @pl.when((pl.program_id(1) == 0) & (pl.program_id(2) == 0))
    def _():
        w_scr[...] = w_ref[...].astype(BF16)
        o2_ref[...] = project(x2_ref[...])

    half = x_ref.shape[0] // ROW_SPLIT
    for r in range(ROW_SPLIT):
        rows = slice(r * half, (r + 1) * half)
        o_ref[rows, :] = project(x_ref[rows, :])


def _norm_proj(x, x2, g, w_all, layer, jobs=(), *, tt, tn):
    bsz, seq, d = x.shape
    r2 = x2.shape[0]
    n = w_all.shape[2]
    grid = (n // tn, bsz, seq // tt)
    riders = _Riders(jobs, grid, lambda j, b, i: (j * grid[1] + b) * grid[2] + i)
    outs = pl.pallas_call(
        functools.partial(_norm_proj_kernel, riders=riders),
        grid=grid,
        in_specs=[
            pl.BlockSpec((None, tt, d), lambda j, b, i: (b, i, 0)),
            pl.BlockSpec((1, d), lambda j, b, i: (0, 0)),
            pl.BlockSpec((None, d, tn), lambda j, b, i: (layer, 0, j)),
            pl.BlockSpec((r2, d), lambda j, b, i: (0, 0)),
            *riders.in_specs,
        ],
        out_specs=[pl.BlockSpec((None, tt, tn), lambda j, b, i: (b, i, j)),
                   pl.BlockSpec((r2, tn), lambda j, b, i: (0, j)),
                   *riders.out_specs],
        out_shape=[jax.ShapeDtypeStruct((bsz, seq, n), BF16), jax.ShapeDtypeStruct((r2, n), BF16),
                   *riders.out_shapes],
        scratch_shapes=[pltpu.VMEM((d, tn), BF16)],
        compiler_params=_params(("arbitrary", "arbitrary", "arbitrary")),
        name="norm_proj",
    )(x, g.reshape(1, d), w_all, x2, *riders.inputs)
    return outs[0], outs[1], list(outs[2:])


def _s5_disc_kernel(lre_ref, lim_ref, ldt_ref, bre_ref, bim_ref, are_ref, aim_ref, ore_ref, oim_ref):
    lam_re = lre_ref[...]
    lam_im = lim_ref[...]
    dt = jnp.exp(ldt_ref[...])
    mag = jnp.exp(lam_re * dt)
    ang = lam_im * dt
    lbar_re = mag * jnp.cos(ang)
    lbar_im = mag * jnp.sin(ang)
    den = lam_re * lam_re + lam_im * lam_im
    nr = lbar_re - 1.0
    f_re = (nr * lam_re + lbar_im * lam_im) / den
    f_im = (lbar_im * lam_re - nr * lam_im) / den
    b_re = bre_ref[...]
    b_im = bim_ref[...]
    are_ref[...] = lbar_re
    aim_ref[...] = lbar_im
    ore_ref[...] = f_re * b_re - f_im * b_im
    oim_ref[...] = f_re * b_im + f_im * b_re


def _s5_discretize(lam_re, lam_im, log_dt, b_re, b_im):
    g, n = lam_re.shape
    p = b_re.shape[-1]
    rows = g * n
    col = lambda a: a.reshape(rows, 1)
    ldt = jnp.broadcast_to(log_dt[:, None], (g, n))
    outs = pl.pallas_call(
        _s5_disc_kernel,
        out_shape=[jax.ShapeDtypeStruct((rows, 1), F32), jax.ShapeDtypeStruct((rows, 1), F32),
                   jax.ShapeDtypeStruct((rows, p), F32), jax.ShapeDtypeStruct((rows, p), F32)],
        name="s5_discretize",
    )(col(lam_re), col(lam_im), col(ldt), b_re.reshape(rows, p), b_im.reshape(rows, p))
    a_re, a_im, bb_re, bb_im = outs
    return a_re.reshape(g, n), a_im.reshape(g, n), bb_re.reshape(g, n, p), bb_im.reshape(g, n, p)


S5_KCH = 128
S5_NCHUNK = S5_WIDTH // S5_KCH
S5_GPC = S5_KCH // S5_GROUP
S5_SPC = S5_GPC * S5_STATE


def _s5_block_mats(bbar_re, bbar_im, c_re, c_im):
    eye = jnp.eye(S5_GPC, dtype=F32)

    def in_blocks(bb):
        t = bb.reshape(S5_NCHUNK, S5_GPC, S5_STATE, S5_GROUP)
        m = jnp.einsum("cgnp,gh->cgphn", t, eye)
        return m.reshape(S5_NCHUNK, S5_KCH, S5_SPC)

    def out_blocks(cc):
        t = cc.reshape(S5_NCHUNK, S5_GPC, S5_GROUP, S5_STATE)
        m = jnp.einsum("cgpn,gh->cgnhp", t, eye)
        return m.reshape(S5_NCHUNK, S5_SPC, S5_KCH)

    bmat = jnp.concatenate([in_blocks(bbar_re), in_blocks(bbar_im)], axis=-1).astype(BF16)
    cmat = jnp.stack([out_blocks(c_re), -out_blocks(c_im)], axis=1).astype(BF16)
    return bmat, cmat


def _gelu_tanh(x):
    return 0.5 * x * (1.0 + jnp.tanh(math.sqrt(2.0 / math.pi) * (x + 0.044715 * (x * x * x))))


class _Riders:
    def __init__(self, jobs, grid, step_index):
        self.steps = math.prod(grid)
        self.n_axes = len(grid)
        self.step_index = step_index
        self.inputs, self.in_specs, self.out_specs, self.out_shapes, self.src_steps = [], [], [], [], []
        for src, shape, dtype in jobs:
            slab = shape[0] // self.steps
            assert slab * self.steps == shape[0] and slab % 16 == 0
            n_src = 0 if src is None else src.shape[0] // slab
            if src is not None:
                assert n_src * slab == src.shape[0] and src.shape[1] == shape[1]
                self.inputs.append(src)
                self.in_specs.append(pl.BlockSpec(
                    (slab, shape[1]), lambda *idx, n=n_src: (jnp.minimum(step_index(*idx[:self.n_axes]), n - 1), 0)))
            self.out_specs.append(pl.BlockSpec((slab, shape[1]), lambda *idx: (step_index(*idx[:self.n_axes]), 0)))
            self.out_shapes.append(jax.ShapeDtypeStruct(shape, dtype))
            self.src_steps.append(n_src)

    @property
    def n_in(self):
        return len(self.inputs)

    @property
    def n_out(self):
        return len(self.out_shapes)

    def run(self, in_refs, out_refs):
        step = self.step_index(*(pl.program_id(a) for a in range(self.n_axes)))
        srcs = iter(in_refs)
        for dst, n_src in zip(out_refs, self.src_steps):
            if n_src == 0:
                dst[...] = jnp.zeros_like(dst)
                continue
            src = next(srcs)
            if n_src == self.steps:
                dst[...] = src[...].astype(dst.dtype)
                continue

            @pl.when(step < n_src)
            def _(src=src, dst=dst):
                dst[...] = src[...].astype(dst.dtype)

            @pl.when(step >= n_src)
            def _(dst=dst):
                dst[...] = jnp.zeros_like(dst)


def _cast_job(w):
    return (w, w.shape, BF16)


def _s5_kernel(u_ref, h0_ref, are_ref, aim_ref, bmat_ref, cmat_ref, d_ref, gw_ref, gb_ref, *rest,
               nb, tt, lane_chunk, riders):
    rider_in, rest = rest[:riders.n_in], rest[riders.n_in:]
    o_ref, hout_ref = rest[:2]
    rider_out = rest[2:2 + riders.n_out]
    hs_scr, h_scr, io_scr = rest[2 + riders.n_out:]
    riders.run(rider_in, rider_out)

    @pl.when(pl.program_id(0) == 0)
    def _():
        h_scr[...] = h0_ref[...]

    def seq_rows(b):
        return pl.ds(b, tt, stride=nb)

    def lanes(c):
        return slice(c * S5_KCH, (c + 1) * S5_KCH)

    if tt == 1:
        for c in range(S5_NCHUNK):
            io_scr[c] = u_ref[:, lanes(c)].astype(F32)
    else:
        for b in range(nb):
            for c in range(S5_NCHUNK):
                io_scr[c, seq_rows(b), :] = u_ref[b, :, lanes(c)].astype(F32)

    u = jnp.concatenate([io_scr[c] for c in range(S5_NCHUNK)], axis=-1)
    for c in range(S5_NCHUNK):
        bu = jnp.dot(io_scr[c].astype(BF16), bmat_ref[c], preferred_element_type=F32)
        hs_scr[:, c * S5_SPC:(c + 1) * S5_SPC] = bu[:, :S5_SPC]
        hs_scr[:, S5_LANES + c * S5_SPC:S5_LANES + (c + 1) * S5_SPC] = bu[:, S5_SPC:]

    for lc in range(S5_LANES // lane_chunk):
        re_sl = slice(lc * lane_chunk, (lc + 1) * lane_chunk)
        im_sl = slice(S5_LANES + lc * lane_chunk, S5_LANES + (lc + 1) * lane_chunk)
        a_re = jnp.broadcast_to(are_ref[:, re_sl], (nb, lane_chunk))
        a_im = jnp.broadcast_to(aim_ref[:, re_sl], (nb, lane_chunk))

        def step(t, carry):
            h_re, h_im = carry
            r0 = t * nb if isinstance(t, int) else pl.multiple_of(t * nb, nb)
            n_re = a_re * h_re - a_im * h_im + hs_scr[pl.ds(r0, nb), re_sl]
            n_im = a_re * h_im + a_im * h_re + hs_scr[pl.ds(r0, nb), im_sl]
            hs_scr[pl.ds(r0, nb), re_sl] = n_re
            hs_scr[pl.ds(r0, nb), im_sl] = n_im
            return n_re, n_im

        carry = (h_scr[:, re_sl], h_scr[:, im_sl])
        if tt == 1:
            carry = step(0, carry)
        else:
            carry = lax.fori_loop(0, tt, step, carry, unroll=4)
        h_scr[:, re_sl] = carry[0]
        h_scr[:, im_sl] = carry[1]

    hout_ref[...] = h_scr[...]

    ys = []
    for c in range(S5_NCHUNK):
        h_re = hs_scr[:, c * S5_SPC:(c + 1) * S5_SPC].astype(BF16)
        h_im = hs_scr[:, S5_LANES + c * S5_SPC:S5_LANES + (c + 1) * S5_SPC].astype(BF16)
        ys.append(jnp.dot(h_re, cmat_ref[c, 0], preferred_element_type=F32)
                  + jnp.dot(h_im, cmat_ref[c, 1], preferred_element_type=F32))
    y = jnp.concatenate(ys, axis=-1) + d_ref[...] * u
    z = _gelu_tanh(y)
    gate = jnp.dot(z.astype(BF16), gw_ref[...], preferred_element_type=F32) + gb_ref[...]
    out = z * jax.nn.sigmoid(gate)
    if tt == 1:
        o_ref[...] = out.astype(o_ref.dtype)
    else:
        for c in range(S5_NCHUNK):
            io_scr[c] = out[:, lanes(c)]
        for b in range(nb):
            for c in range(S5_NCHUNK):
                o_ref[b, :, lanes(c)] = io_scr[c, seq_rows(b), :].astype(o_ref.dtype)


def _s5_branch(z, h0, a_re, a_im, bmat, cmat, d_skip, glu_w, glu_b, jobs=(), *, single_step, tt):
    bsz, seq, _ = z.shape
    if single_step:
        assert bsz == 1 and tt == 1
        nb = seq
        in_spec = pl.BlockSpec((None, nb, S5_WIDTH), lambda i: (0, 0, ZC_S5))
        out_spec = pl.BlockSpec((None, nb, S5_WIDTH), lambda i: (0, 0, 0))
        grid = (1,)
    else:
        nb = bsz
        in_spec = pl.BlockSpec((nb, tt, S5_WIDTH), lambda i: (0, i, ZC_S5))
        out_spec = pl.BlockSpec((nb, tt, S5_WIDTH), lambda i: (0, i, 0))
        grid = (seq // tt,)
    rblk = tt * nb
    lane_chunk = 1024 if nb <= 8 else 512
    const = lambda shape: pl.BlockSpec(shape, lambda i: (0,) * len(shape))
    riders = _Riders(jobs, grid, lambda i: i)
    outs = pl.pallas_call(
        functools.partial(_s5_kernel, nb=nb, tt=tt, lane_chunk=lane_chunk, riders=riders),
        grid=grid,
        in_specs=[
            in_spec,
            const((nb, 2 * S5_LANES)),
            const((1, S5_LANES)),
            const((1, S5_LANES)),
            const(bmat.shape),
            const(cmat.shape),
            const((1, S5_WIDTH)),
            const((S5_WIDTH, S5_WIDTH)),
            const((1, S5_WIDTH)),
            *riders.in_specs,
        ],
        out_specs=[out_spec, const((nb, 2 * S5_LANES)), *riders.out_specs],
        out_shape=[jax.ShapeDtypeStruct((bsz, seq, S5_WIDTH), BF16),
                   jax.ShapeDtypeStruct((nb, 2 * S5_LANES), F32), *riders.out_shapes],
        scratch_shapes=[pltpu.VMEM((rblk, 2 * S5_LANES), F32), pltpu.VMEM((nb, 2 * S5_LANES), F32),
                        pltpu.VMEM((S5_NCHUNK, rblk, S5_KCH), F32)],
        compiler_params=_params(("arbitrary",)),
        name="s5_branch",
    )(z, h0, a_re.reshape(1, S5_LANES), a_im.reshape(1, S5_LANES), bmat, cmat,
      d_skip.reshape(1, S5_WIDTH), glu_w, glu_b.reshape(1, S5_WIDTH), *riders.inputs)
    return outs[0], outs[1], list(outs[2:])


def _rope_tables(pos):
    half = RET_DK // 2
    freqs = ROPE_BASE ** (-jnp.arange(half, dtype=F32) / half)
    ang = pos[:, None] * freqs[None, :]
    cos = jnp.cos(ang)
    sin = jnp.sin(ang)
    return jnp.concatenate([cos, cos], axis=-1), jnp.concatenate([-sin, sin], axis=-1)


def _rope(x, cos, sin):
    return x * cos + pltpu.roll(x, RET_DK // 2, 1) * sin


def _group_norm(o):
    mu = jnp.mean(o, axis=-1, keepdims=True)
    d = o - mu
    var = jnp.mean(d * d, axis=-1, keepdims=True)
    return d * lax.rsqrt(var + EPS)


def _retention_tables(chunk):
    log_gamma = jnp.log(1.0 - 2.0 ** (-5.0 - jnp.arange(RET_HEADS, dtype=F32)))
    idx = jnp.arange(chunk, dtype=F32)
    diff = idx[:, None] - idx[None, :]
    decay = jnp.where(diff >= 0, jnp.exp(jnp.maximum(diff, 0.0)[None] * log_gamma[:, None, None]), 0.0)
    cross = jnp.exp((idx + 1.0)[None, :] * log_gamma[:, None])[:, :, None]
    kdec = jnp.exp((chunk - 1.0 - idx)[None, :] * log_gamma[:, None])[:, :, None]
    full = jnp.exp(chunk * log_gamma)
    return decay, cross, kdec, full


def _head_cols(refs, h, rows):
    lo = (h % HEADS_PER_ZB) * RET_DV
    return refs[h // HEADS_PER_ZB][rows, lo:lo + RET_DV].astype(F32)


def _retention_kernel(q_ref, k_ref, v0_ref, v1_ref, g0_ref, g1_ref, cos_ref, sin_ref, s0_ref, decay_ref, cross_ref,
                      kdec_ref, full_ref, *rest, n_chunks, riders):
    rider_in, rest = rest[:riders.n_in], rest[riders.n_in:]
    o_ref, sout_ref = rest[:2]
    rider_out, s_scr = rest[2:-1], rest[-1]
    riders.run(rider_in, rider_out)

    @pl.when(pl.program_id(1) == 0)
    def _():
        s_scr[...] = s0_ref[...]

    for c in range(n_chunks):
        rows = slice(c * RET_CHUNK, (c + 1) * RET_CHUNK)
        cos = cos_ref[rows, :]
        sin = sin_ref[rows, :]
        for h in range(RET_HEADS):
            qk_cols = slice(h * RET_DK, (h + 1) * RET_DK)
            v_cols = slice(h * RET_DV, (h + 1) * RET_DV)
            qh = _rope(q_ref[rows, qk_cols].astype(F32), cos, sin)
            kh = _rope(k_ref[rows, qk_cols].astype(F32), cos, sin) * (RET_DK ** -0.5)
            vb = _head_cols((v0_ref, v1_ref), h, rows).astype(BF16)
            qb = qh.astype(BF16)
            state = s_scr[h]
            inner = lax.dot_general(qb, kh.astype(BF16), (((1,), (1,)), ((), ())),
                                    preferred_element_type=F32) * decay_ref[h]
            out = (jnp.dot(inner.astype(BF16), vb, preferred_element_type=F32)
                   + jnp.dot(qb, state.astype(BF16), preferred_element_type=F32) * cross_ref[h])
            kd = (kh * kdec_ref[h]).astype(BF16)
            s_scr[h] = full_ref[h] * state + jnp.dot(kd.T, vb, preferred_element_type=F32)
            gate = _head_cols((g0_ref, g1_ref), h, rows)
            o_ref[rows, v_cols] = (_silu(gate) * _group_norm(out)).astype(o_ref.dtype)

    sout_ref[...] = s_scr[...]


def _retention_prompt(z, state0, pos_offset, jobs=(), *, tt):
    bsz, seq, _ = z.shape
    assert seq % RET_CHUNK == 0 and tt % RET_CHUNK == 0
    cos, sin = _rope_tables(jnp.arange(seq, dtype=F32) + pos_offset)
    decay, cross, kdec, full = _retention_tables(RET_CHUNK)
    full = jnp.broadcast_to(full[:, None, None], (RET_HEADS, 1, RET_DV))
    const = lambda shape: pl.BlockSpec(shape, lambda b, i: (0,) * len(shape))
    zblk = lambda col: pl.BlockSpec((None, tt, ZB), lambda b, i: (b, i, col))
    n_i = seq // tt
    riders = _Riders(jobs, (bsz, n_i), lambda b, i: b * n_i + i)
    outs = pl.pallas_call(
        functools.partial(_retention_kernel, n_chunks=tt // RET_CHUNK, riders=riders),
        grid=(bsz, n_i),
        in_specs=[
            zblk(ZC_Q), zblk(ZC_K), zblk(ZC_V), zblk(ZC_V + 1), zblk(ZC_G), zblk(ZC_G + 1),
            pl.BlockSpec((tt, RET_DK), lambda b, i: (i, 0)),
            pl.BlockSpec((tt, RET_DK), lambda b, i: (i, 0)),
            pl.BlockSpec((None, RET_HEADS, RET_DK, RET_DV), lambda b, i: (b, 0, 0, 0)),
            const((RET_HEADS, RET_CHUNK, RET_CHUNK)),
            const((RET_HEADS, RET_CHUNK, 1)),
            const((RET_HEADS, RET_CHUNK, 1)),
            const((RET_HEADS, 1, RET_DV)),
            *riders.in_specs,
        ],
        out_specs=[
            pl.BlockSpec((None, tt, RET_V), lambda b, i: (b, i, 0)),
            pl.BlockSpec((None, RET_HEADS, RET_DK, RET_DV), lambda b, i: (b, 0, 0, 0)),
            *riders.out_specs,
        ],
        out_shape=[jax.ShapeDtypeStruct((bsz, seq, RET_V), BF16),
                   jax.ShapeDtypeStruct((bsz, RET_HEADS, RET_DK, RET_DV), F32), *riders.out_shapes],
        scratch_shapes=[pltpu.VMEM((RET_HEADS, RET_DK, RET_DV), F32)],
        compiler_params=_params(("arbitrary", "arbitrary")),
        name="retention_prompt",
    )(z, z, z, z, z, z, cos, sin, state0, decay, cross, kdec, full, *riders.inputs)
    return outs[0], outs[1], list(outs[2:])


def _retention_step_kernel(q_ref, k_ref, v0_ref, v1_ref, g0_ref, g1_ref, cos_ref, sin_ref, s_ref, gam_ref, *rest,
                           bb, layer):
    del layer
    o_ref, sout_ref, o_scr = rest[-3:]
    cos = cos_ref[...]
    sin = sin_ref[...]
    for h in range(RET_HEADS):
        qk_cols = slice(h * RET_DK, (h + 1) * RET_DK)
        v_cols = slice(h * RET_DV, (h + 1) * RET_DV)
        qh = _rope(q_ref[:, qk_cols].astype(F32), cos, sin)
        kh = _rope(k_ref[:, qk_cols].astype(F32), cos, sin) * (RET_DK ** -0.5)
        qk = jnp.sum(qh * kh, axis=-1, keepdims=True)
        q_t = qh.T
        k_t = kh.T
        gamma = gam_ref[h]
        v_all = _head_cols((v0_ref, v1_ref), h, slice(None))
        for b in range(bb):
            state = s_ref[b, h]
            vrow = v_all[b:b + 1, :]
            qs = jnp.sum(q_t[:, b:b + 1] * state, axis=0, keepdims=True)
            o_scr[b:b + 1, v_cols] = qk[b:b + 1, :] * vrow + qs * gamma
            sout_ref[b, h] = gamma * state + k_t[:, b:b + 1] * vrow
    for h in range(RET_HEADS):
        v_cols = slice(h * RET_DV, (h + 1) * RET_DV)
        gate = _head_cols((g0_ref, g1_ref), h, slice(None))
        o_ref[:, v_cols] = (_silu(gate) * _group_norm(o_scr[:, v_cols])).astype(o_ref.dtype)


def _retention_step(z, states, layer, stack, pos, *, bb):
    n = z.shape[0]
    cos, sin = _rope_tables(jnp.full((1,), pos, F32))
    log_gamma = jnp.log(1.0 - 2.0 ** (-5.0 - jnp.arange(RET_HEADS, dtype=F32)))
    gam = jnp.broadcast_to(jnp.exp(log_gamma)[:, None, None], (RET_HEADS, 1, RET_DV))
    const = lambda shape: pl.BlockSpec(shape, lambda i: (0,) * len(shape))
    slab = pl.BlockSpec((None, bb, RET_HEADS, RET_DK, RET_DV), lambda i: (layer, i, 0, 0, 0))
    zblk = lambda col: pl.BlockSpec((bb, ZB), lambda i: (i, col))
    in_specs = [
        zblk(ZC_Q), zblk(ZC_K), zblk(ZC_V), zblk(ZC_V + 1), zblk(ZC_G), zblk(ZC_G + 1),
        const((1, RET_DK)),
        const((1, RET_DK)),
        slab,
        const((RET_HEADS, 1, RET_DV)),
    ]
    args = [z, z, z, z, z, z, cos, sin, states, gam]
    if stack is None:
        state_spec = pl.BlockSpec((bb, RET_HEADS, RET_DK, RET_DV), lambda i: (i, 0, 0, 0))
        state_shape = jax.ShapeDtypeStruct(states.shape[1:], F32)
        aliases = {}
    else:
        in_specs.append(pl.BlockSpec(memory_space=pl.ANY))
        args.append(stack)
        state_spec = slab
        state_shape = jax.ShapeDtypeStruct(stack.shape, F32)
        aliases = {len(args) - 1: 1}
    return pl.pallas_call(
        functools.partial(_retention_step_kernel, bb=bb, layer=layer),
        grid=(n // bb,),
        in_specs=in_specs,
        out_specs=[pl.BlockSpec((bb, RET_V), lambda i: (i, 0)), state_spec],
        out_shape=[jax.ShapeDtypeStruct((n, RET_V), BF16), state_shape],
        scratch_shapes=[pltpu.VMEM((bb, RET_V), F32)],
        input_output_aliases=aliases,
        compiler_params=_params(("parallel",)),
        name="retention_step",
    )(*args)


def _layer_norm(y, g, b):
    mu = jnp.mean(y, axis=-1, keepdims=True)
    d = y - mu
    var = jnp.mean(d * d, axis=-1, keepdims=True)
    return d * lax.rsqrt(var + EPS) * g + b


CONV_RB = 128
CONV_PITCH = 2
CONV_LCH = CONV_WIDTH // LANES


def _conv_glu(a_ref, b_ref, pwb_ref):
    a = a_ref[...].astype(F32) + pwb_ref[:, :CONV_WIDTH]
    b = b_ref[...].astype(F32) + pwb_ref[:, CONV_WIDTH:]
    return a * jax.nn.sigmoid(b)


def _conv_kernel(a_ref, b_ref, buf_ref, pwb_ref, dww_ref, dwb_ref, lng_ref, lnb_ref, *rest, tt, riders):
    rider_in, rest = rest[:riders.n_in], rest[riders.n_in:]
    o_ref, hist_ref = rest[:2]
    rider_out, (x_scr, y_scr) = rest[2:-2], rest[-2:]
    riders.run(rider_in, rider_out)
    _conv_body(a_ref, b_ref, buf_ref, pwb_ref, dww_ref, dwb_ref, lng_ref, lnb_ref, o_ref, hist_ref, x_scr, y_scr, tt=tt)


def _conv_body(a_ref, b_ref, buf_ref, pwb_ref, dww_ref, dwb_ref, lng_ref, lnb_ref, o_ref, hist_ref, x_scr, y_scr,
               *, tt):
    i = pl.program_id(1)

    def rows(start, n):
        return pl.ds(CONV_PITCH * start, n, stride=CONV_PITCH)

    def lanes(c):
        return slice(c * LANES, (c + 1) * LANES)

    @pl.when(i == 0)
    def _():
        for c in range(CONV_LCH):
            x_scr[c, rows(0, CONV_HIST), :] = buf_ref[:, lanes(c)]

    @pl.when(i > 0)
    def _():
        for c in range(CONV_LCH):
            x_scr[c, rows(0, CONV_HIST), :] = x_scr[c, rows(tt, CONV_HIST), :]

    glu = _conv_glu(a_ref, b_ref, pwb_ref)
    for c in range(CONV_LCH):
        x_scr[c, rows(CONV_HIST, tt), :] = glu[:, lanes(c)]
    hist_ref[...] = glu[tt - CONV_HIST:, :]

    off = CONV_HIST - (CONV_K - 1)
    for c in range(CONV_LCH):
        def row_block(r, carry, c=c):
            base = r * CONV_RB
            n_grp = CONV_RB // ROW_TILE
            accs = [jnp.broadcast_to(dwb_ref[:, lanes(c)], (ROW_TILE, LANES))] * n_grp
            for m in range(CONV_RB - ROW_TILE + CONV_K):
                win = x_scr[c, rows(base + (off + m), ROW_TILE), :]
                for k in range(m % ROW_TILE, CONV_K, ROW_TILE):
                    j = (m - k) // ROW_TILE
                    if 0 <= j < n_grp:
                        accs[j] = accs[j] + dww_ref[k:k + 1, lanes(c)] * win
            y_scr[pl.ds(pl.multiple_of(base, CONV_RB), CONV_RB), lanes(c)] = jnp.concatenate(accs, axis=0)
            return carry

        lax.fori_loop(0, tt // CONV_RB, row_block, 0)

    o_ref[...] = _silu(_layer_norm(y_scr[...], lng_ref[...], lnb_ref[...])).astype(o_ref.dtype)


def _conv_prompt(z, buf, pw_b, dw_w, dw_b, ln_g, ln_b, jobs=(), *, tt):
    bsz, seq, _ = z.shape
    assert seq >= CONV_HIST and tt >= CONV_HIST
    buf32 = jnp.pad(buf, ((0, 0), (CONV_HIST - (CONV_K - 1), 0), (0, 0)))
    const = lambda shape: pl.BlockSpec(shape, lambda b, i: (0,) * len(shape))
    zblk = lambda col: pl.BlockSpec((None, tt, ZB), lambda b, i: (b, i, col))
    n_i = seq // tt
    riders = _Riders(jobs, (bsz, n_i), lambda b, i: b * n_i + i)
    out, hist, *cast = pl.pallas_call(
        functools.partial(_conv_kernel, tt=tt, riders=riders),
        grid=(bsz, n_i),
        in_specs=[
            zblk(ZC_CONV), zblk(ZC_CONV + 1),
            pl.BlockSpec((None, CONV_HIST, CONV_WIDTH), lambda b, i: (b, 0, 0)),
            const((1, 2 * CONV_WIDTH)),
            const((CONV_K, CONV_WIDTH)),
            const((1, CONV_WIDTH)),
            const((1, CONV_WIDTH)),
            const((1, CONV_WIDTH)),
            *riders.in_specs,
        ],
        out_specs=[
            pl.BlockSpec((None, tt, CONV_WIDTH), lambda b, i: (b, i, 0)),
            pl.BlockSpec((None, CONV_HIST, CONV_WIDTH), lambda b, i: (b, 0, 0)),
            *riders.out_specs,
        ],
        out_shape=[jax.ShapeDtypeStruct((bsz, seq, CONV_WIDTH), BF16),
                   jax.ShapeDtypeStruct((bsz, CONV_HIST, CONV_WIDTH), F32), *riders.out_shapes],
        scratch_shapes=[pltpu.VMEM((CONV_LCH, CONV_PITCH * (CONV_HIST + tt), LANES), F32),
                        pltpu.VMEM((tt, CONV_WIDTH), F32)],
        compiler_params=_params(("arbitrary", "arbitrary")),
        name="conv_prompt",
    )(z, z, buf32, pw_b.reshape(1, -1), dw_w, dw_b.reshape(1, -1), ln_g.reshape(1, -1), ln_b.reshape(1, -1),
      *riders.inputs)
    return out, hist[:, CONV_HIST - (CONV_K - 1):, :], cast


def _conv_step_kernel(a_ref, b_ref, buf_ref, pwb_ref, dww_ref, dwb_ref, lng_ref, lnb_ref, o_ref, hist_ref):
    hist_len = CONV_K - 1
    glu = _conv_glu(a_ref, b_ref, pwb_ref)
    acc = dwb_ref[...] + dww_ref[hist_len:hist_len + 1, :] * glu
    for k in range(hist_len):
        acc = acc + dww_ref[k:k + 1, :] * buf_ref[k]
    o_ref[...] = _silu(_layer_norm(acc, lng_ref[...], lnb_ref[...])).astype(o_ref.dtype)
    for k in range(hist_len - 1):
        hist_ref[k] = buf_ref[k + 1]
    hist_ref[hist_len - 1] = glu


def _conv_step(z, bufs, layer, pw_b, dw_w, dw_b, ln_g, ln_b, *, bb):
    n = z.shape[0]
    hist_len = CONV_K - 1
    const = lambda shape: pl.BlockSpec(shape, lambda i: (0,) * len(shape))
    zblk = lambda col: pl.BlockSpec((bb, ZB), lambda i: (i, col))
    return pl.pallas_call(
        _conv_step_kernel,
        grid=(n // bb,),
        in_specs=[
            zblk(ZC_CONV), zblk(ZC_CONV + 1),
            pl.BlockSpec((None, hist_len, bb, CONV_WIDTH), lambda i: (layer, 0, i, 0)),
            const((1, 2 * CONV_WIDTH)),
            const((CONV_K, CONV_WIDTH)),
            const((1, CONV_WIDTH)),
            const((1, CONV_WIDTH)),
            const((1, CONV_WIDTH)),
        ],
        out_specs=[
            pl.BlockSpec((bb, CONV_WIDTH), lambda i: (i, 0)),
            pl.BlockSpec((hist_len, bb, CONV_WIDTH), lambda i: (0, i, 0)),
        ],
        out_shape=[jax.ShapeDtypeStruct((n, CONV_WIDTH), BF16),
                   jax.ShapeDtypeStruct((hist_len, n, CONV_WIDTH), F32)],
        compiler_params=_params(("parallel",)),
        name="conv_step",
    )(z, z, bufs, pw_b.reshape(1, -1), dw_w, dw_b.reshape(1, -1), ln_g.reshape(1, -1), ln_b.reshape(1, -1))


def _rows_to_tiles(tile_ref, x, rows):
    for c in range(ROW_CHUNKS):
        tile_ref[pl.ds(c, rows, stride=ROW_TILE), :] = x[:, c * LANES:(c + 1) * LANES]


def _tiles_chunk(tile_ref, c, rows):
    return tile_ref[pl.ds(c, rows, stride=ROW_TILE), :]


def _tiles_to_rows(tile_ref, rows):
    return jnp.concatenate([_tiles_chunk(tile_ref, c, rows) for c in range(ROW_CHUNKS)], axis=-1)


def _merge_kernel(x_ref, s5_ref, ret_ref, conv_ref, *rest, tiled_u):
    gate_refs = rest[:N_BRANCH * D_MODEL // ZB]
    ps5_ref, pret_ref, pconv_ref, wout_ref, g_ref, h_ref, u_ref = rest[len(gate_refs):]
    per_branch = D_MODEL // ZB

    def gate(n):
        cols = [gate_refs[n * per_branch + j][...] for j in range(per_branch)]
        return jax.nn.sigmoid(jnp.concatenate(cols, axis=-1).astype(F32))

    merged = (gate(0) * jnp.dot(s5_ref[...], ps5_ref[...], preferred_element_type=F32)
              + gate(1) * jnp.dot(ret_ref[...], pret_ref[...], preferred_element_type=F32)
              + gate(2) * jnp.dot(conv_ref[...], pconv_ref[...], preferred_element_type=F32))
    h = x_ref[...] + _bdot(merged, wout_ref[...])
    h_ref[...] = h
    ms = jnp.mean(h * h, axis=-1, keepdims=True)
    u = h * lax.rsqrt(ms + EPS) * g_ref[...]
    if tiled_u:
        _rows_to_tiles(u_ref, u, u.shape[0])
    else:
        u_ref[...] = u.astype(u_ref.dtype)


def _merge(x, s5_out, ret_out, conv_out, z, s5_proj, ret_proj, conv_proj, w_out, norm_g, *, tm, tiled_u):
    bsz, seq, d = x.shape
    const = lambda shape: pl.BlockSpec(shape, lambda b, i: (0,) * len(shape))
    tok = lambda w, col=0: pl.BlockSpec((None, tm, w), lambda b, i: (b, i, col))
    n_i = seq // tm
    n_gate = N_BRANCH * d // ZB
    if tiled_u:
        u_spec = pl.BlockSpec((tm * ROW_TILE, LANES), lambda b, i: (b * n_i + i, 0))
        u_shape = jax.ShapeDtypeStruct((bsz * seq * ROW_TILE, LANES), F32)
    else:
        u_spec = tok(d)
        u_shape = jax.ShapeDtypeStruct((bsz, seq, d), BF16)
    return pl.pallas_call(
        functools.partial(_merge_kernel, tiled_u=tiled_u),
        grid=(bsz, seq // tm),
        in_specs=[
            tok(d),
            tok(S5_WIDTH),
            tok(RET_V),
            tok(CONV_WIDTH),
            *[tok(ZB, ZC_GATE + j) for j in range(n_gate)],
            const((S5_WIDTH, d)),
            const((RET_V, d)),
            const((CONV_WIDTH, d)),
            const((d, d)),
            const((1, d)),
        ],
        out_specs=[tok(d), u_spec],
        out_shape=[jax.ShapeDtypeStruct((bsz, seq, d), F32), u_shape],
        compiler_params=_params(("parallel", "parallel")),
        name="merge",
    )(x, s5_out, ret_out, conv_out, *([z] * n_gate), s5_proj, ret_proj, conv_proj, w_out, norm_g.reshape(1, d))


def _ffn_kernel(h_ref, u_ref, wg_ref, wu_ref, wd_ref, *rest, riders):
    rider_in, o_ref, rider_out = rest[:riders.n_in], rest[riders.n_in], rest[riders.n_in + 1:]
    f = pl.program_id(1)
    ub = u_ref[...].astype(BF16)
    gate = jnp.dot(ub, wg_ref[...], preferred_element_type=F32)
    up = jnp.dot(ub, wu_ref[...], preferred_element_type=F32)
    part = _bdot(_silu(gate) * up, wd_ref[...])

    @pl.when(f == 0)
    def _():
        o_ref[...] = h_ref[...] + part

    @pl.when(f > 0)
    def _():
        o_ref[...] = o_ref[...] + part

    riders.run(rider_in, rider_out)


def _ffn(h, u, w_gate, w_up, w_down, jobs=(), *, tm, tf):
    rows, d = h.shape
    dff = w_gate.shape[1]
    n_i, n_f = rows // tm, dff // tf
    tok = pl.BlockSpec((tm, d), lambda i, f: (i, 0))
    riders = _Riders(jobs, (n_i, n_f), lambda i, f: i * n_f + f)
    outs = pl.pallas_call(
        functools.partial(_ffn_kernel, riders=riders),
        grid=(n_i, n_f),
        in_specs=[tok, tok,
                  pl.BlockSpec((d, tf), lambda i, f: (0, f)),
                  pl.BlockSpec((d, tf), lambda i, f: (0, f)),
                  pl.BlockSpec((tf, d), lambda i, f: (f, 0)),
                  *riders.in_specs],
        out_specs=[tok, *riders.out_specs],
        out_shape=[jax.ShapeDtypeStruct((rows, d), F32), *riders.out_shapes],
        compiler_params=_params(("arbitrary", "arbitrary")),
        name="ffn_dense",
    )(h, u, w_gate, w_up, w_down, *riders.inputs)
    return outs[0], list(outs[1:])


def _split_bf16(x):
    hi = x.astype(BF16)
    return hi, (x - hi.astype(F32)).astype(BF16)


def _router_kernel(u_ref, rt_ref, tri_ref, idx_ref, wts_ref, rank_ref, cnt_ref, cnt_scr, *, tm):
    @pl.when(pl.program_id(0) == 0)
    def _():
        cnt_scr[...] = jnp.zeros_like(cnt_scr)

    u_hi, u_lo = _split_bf16(_tiles_to_rows(u_ref, tm))
    r_hi, r_lo = _split_bf16(rt_ref[...])
    dn = (((1,), (1,)), ((), ()))
    logits = (lax.dot_general(r_hi, u_hi, dn, preferred_element_type=F32)
              + lax.dot_general(r_lo, u_hi, dn, preferred_element_type=F32)
              + lax.dot_general(r_hi, u_lo, dn, preferred_element_type=F32))
    eidx = lax.broadcasted_iota(jnp.int32, logits.shape, 0)
    m1 = jnp.max(logits, axis=0, keepdims=True)
    i1 = jnp.min(jnp.where(logits == m1, eidx, N_EXPERTS), axis=0, keepdims=True)
    rest = jnp.where(eidx == i1, -jnp.inf, logits)
    m2 = jnp.max(rest, axis=0, keepdims=True)
    i2 = jnp.min(jnp.where(rest == m2, eidx, N_EXPERTS), axis=0, keepdims=True)
    e2 = jnp.exp(m2 - m1)
    w1 = 1.0 / (1.0 + e2)
    idx_ref[...] = jnp.concatenate([i1, i2], axis=0)
    wts_ref[...] = jnp.concatenate([w1, e2 * w1], axis=0)

    hit1 = eidx == i1
    hit2 = eidx == i2
    hits = jnp.where(hit1 | hit2, 1.0, 0.0)
    before = jnp.dot(hits.astype(BF16), tri_ref[...], preferred_element_type=F32) + cnt_scr[...]
    rank_ref[...] = jnp.concatenate(
        [jnp.sum(jnp.where(hit1, before, 0.0), axis=0, keepdims=True),
         jnp.sum(jnp.where(hit2, before, 0.0), axis=0, keepdims=True)], axis=0).astype(jnp.int32)
    cnt_scr[...] = cnt_scr[...] + jnp.sum(hits, axis=1, keepdims=True)
    cnt_ref[...] = cnt_scr[...]


def _router(u8, router, *, tm):
    rows = u8.shape[0] // ROW_TILE
    d = D_MODEL
    tri = (jnp.arange(tm)[:, None] < jnp.arange(tm)[None, :]).astype(BF16)
    const = lambda shape: pl.BlockSpec(shape, lambda i: (0,) * len(shape))
    lane = pl.BlockSpec((TOP_K, tm), lambda i: (0, i))
    return pl.pallas_call(
        functools.partial(_router_kernel, tm=tm),
        grid=(rows // tm,),
        in_specs=[pl.BlockSpec((tm * ROW_TILE, LANES), lambda i: (i, 0)), const((N_EXPERTS, d)), const((tm, tm))],
        out_specs=[lane, lane, lane, const((N_EXPERTS, 1))],
        out_shape=[jax.ShapeDtypeStruct((TOP_K, rows), jnp.int32), jax.ShapeDtypeStruct((TOP_K, rows), F32),
                   jax.ShapeDtypeStruct((TOP_K, rows), jnp.int32), jax.ShapeDtypeStruct((N_EXPERTS, 1), F32)],
        scratch_shapes=[pltpu.VMEM((N_EXPERTS, 1), F32)],
        compiler_params=_params(("arbitrary",)),
        name="moe_router",
    )(u8, router.T, tri)


DMA_UNROLL = 8


def _token_tile(ref, r):
    return ref.at[pl.ds(pl.multiple_of(r * ROW_TILE, ROW_TILE), ROW_TILE)]


def _dispatch_kernel(pos_ref, u_ref, init_ref, xs_ref, sem, *, tb):
    del init_ref

    def start(r, carry):
        for s in range(TOP_K):
            pltpu.make_async_copy(_token_tile(u_ref, r), _token_tile(xs_ref, pos_ref[s, r]),
                                  sem.at[s]).start(priority=s)
        return carry

    lax.fori_loop(0, tb, start, 0, unroll=DMA_UNROLL)
    for s in range(TOP_K):
        pltpu.make_async_copy(u_ref, xs_ref.at[pl.ds(0, tb * ROW_TILE)], sem.at[s]).wait()


def _dispatch(u8, pos, xs8, *, tb):
    rows = u8.shape[0] // ROW_TILE
    return pl.pallas_call(
        functools.partial(_dispatch_kernel, tb=tb),
        grid=(rows // tb,),
        in_specs=[
            pl.BlockSpec((TOP_K, tb), lambda i: (0, i), memory_space=pltpu.SMEM),
            pl.BlockSpec((tb * ROW_TILE, LANES), lambda i: (i, 0)),
            pl.BlockSpec(memory_space=pl.ANY),
        ],
        out_specs=pl.BlockSpec(memory_space=pl.ANY),
        out_shape=jax.ShapeDtypeStruct(xs8.shape, xs8.dtype),
        scratch_shapes=[pltpu.SemaphoreType.DMA((TOP_K,))],
        input_output_aliases={2: 0},
        compiler_params=_params(("arbitrary",)),
        name="moe_dispatch",
    )(pos, u8, xs8)


def _experts_kernel(te_ref, nu_ref, x_ref, wg_ref, wu_ref, wd_ref, o_ref, x_scr, acc_scr, *, tm):
    del te_ref
    i = pl.program_id(0)
    f = pl.program_id(1)

    def swiglu_rows(n):
        xb = x_scr[0:n, :]
        gate = jnp.dot(xb, wg_ref[...], preferred_element_type=F32)
        up = jnp.dot(xb, wu_ref[...], preferred_element_type=F32)
        part = _bdot(_silu(gate) * up, wd_ref[...])

        @pl.when(f == 0)
        def _():
            acc_scr[0:n, :] = part
            if n < tm:
                acc_scr[n:tm, :] = jnp.zeros((tm - n, D_MODEL), F32)

        @pl.when(f > 0)
        def _():
            acc_scr[0:n, :] = acc_scr[0:n, :] + part

    @pl.when(i < nu_ref[0])
    def _():
        @pl.when(f == 0)
        def _():
            x_scr[...] = _tiles_to_rows(x_ref, tm).astype(BF16)

        half = tm // 2
        valid = nu_ref[1 + i]

        @pl.when(valid > half)
        def _():
            swiglu_rows(tm)

        @pl.when(valid <= half)
        def _():
            swiglu_rows(half)

        @pl.when(f == pl.num_programs(1) - 1)
        def _():
            _rows_to_tiles(o_ref, acc_scr[...], tm)

    @pl.when(i >= nu_ref[0])
    def _():
        o_ref[...] = jnp.zeros_like(o_ref)


def _experts(xs8, tile_expert, n_used, w_gate, w_up, w_down, *, tm, tf):
    rows = xs8.shape[0] // ROW_TILE
    d = D_MODEL
    dff = w_gate.shape[-1]
    n_f = dff // tf
    last_f = n_f - 1

    def row_map(i, f, te, nu):
        return (jnp.minimum(i, nu[0] - 1), 0)

    def fsel(i, f, nu):
        return jnp.where(i < nu[0], f, last_f)

    grid_spec = pltpu.PrefetchScalarGridSpec(
        num_scalar_prefetch=2,
        grid=(rows // tm, n_f),
        in_specs=[
            pl.BlockSpec((tm * ROW_TILE, LANES), row_map),
            pl.BlockSpec((None, d, tf), lambda i, f, te, nu: (te[i], 0, fsel(i, f, nu))),
            pl.BlockSpec((None, d, tf), lambda i, f, te, nu: (te[i], 0, fsel(i, f, nu))),
            pl.BlockSpec((None, tf, d), lambda i, f, te, nu: (te[i], fsel(i, f, nu), 0)),
        ],
        out_specs=pl.BlockSpec((tm * ROW_TILE, LANES), lambda i, f, te, nu: (i, 0)),
        scratch_shapes=[pltpu.VMEM((tm, d), BF16), pltpu.VMEM((tm, d), F32)],
    )
    return pl.pallas_call(
        functools.partial(_experts_kernel, tm=tm),
        grid_spec=grid_spec,
        out_shape=jax.ShapeDtypeStruct(xs8.shape, F32),
        compiler_params=_params(("arbitrary", "arbitrary")),
        name="moe_experts",
    )(tile_expert, n_used, xs8, w_gate, w_up, w_down)


def _combine_kernel(pos_ref, next_pos_ref, h_ref, wts_ref, g_ref, ys_ref, o_ref, y_scr, sem, *, tb, final_norm):
    i = pl.program_id(0)
    n = pl.num_programs(0)

    def gather(p_ref, buf):
        def start(r, carry):
            for s in range(TOP_K):
                pltpu.make_async_copy(_token_tile(ys_ref, p_ref[s, r]), _token_tile(y_scr.at[buf, s], r),
                                      sem.at[buf, s]).start(priority=s)
            return carry

        lax.fori_loop(0, tb, start, 0, unroll=DMA_UNROLL)

    @pl.when(i == 0)
    def _():
        gather(pos_ref, 0)

    for buf in range(2):
        @pl.when(i % 2 == buf)
        def _(buf=buf):
            @pl.when(i + 1 < n)
            def _():
                gather(next_pos_ref, 1 - buf)

            for s in range(TOP_K):
                pltpu.make_async_copy(ys_ref.at[pl.ds(0, tb * ROW_TILE)], y_scr.at[buf, s], sem.at[buf, s]).wait()
            w1 = wts_ref[:, 0:1]
            w2 = wts_ref[:, 1:2]
            moe = jnp.concatenate(
                [w1 * _tiles_chunk(y_scr.at[buf, 0], c, tb) + w2 * _tiles_chunk(y_scr.at[buf, 1], c, tb)
                 for c in range(ROW_CHUNKS)], axis=-1)
            out = h_ref[...] + moe
            if final_norm:
                ms = jnp.mean(out * out, axis=-1, keepdims=True)
                out = out * lax.rsqrt(ms + EPS) * g_ref[...]
            o_ref[...] = out


def _combine(h, pos, wts_t, ys8, final_g, *, tb):
    rows, d = h.shape
    final_norm = final_g is not None
    gain = final_g.reshape(1, d) if final_norm else jnp.ones((1, d), F32)
    n_steps = rows // tb
    return pl.pallas_call(
        functools.partial(_combine_kernel, tb=tb, final_norm=final_norm),
        grid=(n_steps,),
        in_specs=[
            pl.BlockSpec((TOP_K, tb), lambda i: (0, i), memory_space=pltpu.SMEM),
            pl.BlockSpec((TOP_K, tb), lambda i: (0, jnp.minimum(i + 1, n_steps - 1)), memory_space=pltpu.SMEM),
            pl.BlockSpec((tb, d), lambda i: (i, 0)),
            pl.BlockSpec((tb, TOP_K), lambda i: (i, 0)),
            pl.BlockSpec((1, d), lambda i: (0, 0)),
            pl.BlockSpec(memory_space=pl.ANY),
        ],
        out_specs=pl.BlockSpec((tb, d), lambda i: (i, 0)),
        out_shape=jax.ShapeDtypeStruct((rows, d), F32),
        scratch_shapes=[pltpu.VMEM((2, TOP_K, tb * ROW_TILE, LANES), F32), pltpu.SemaphoreType.DMA((2, TOP_K))],
        compiler_params=_params(("arbitrary",)),
        name="moe_combine",
    )(pos, pos, h, wts_t, gain, ys8)


PROJ_TN = 2560
S5_TT = 128
MIX_TT = 512
FFN_TF = 1408
MOE_TM = 512
MOE_TF = 1792
MOE_TB = 1024


def _moe_tiles(n_rows):
    return n_rows // MOE_TM + N_EXPERTS


def _moe_buffer_shape(n_rows):
    return (_moe_tiles(n_rows) * MOE_TM * ROW_TILE, LANES)


def _moe(h_list, u8_list, router, w_gate, w_up, w_down, xs8, final_g):
    routes = []
    for u8 in u8_list:
        routes.append(_router(u8, router, tm=min(512, u8.shape[0] // ROW_TILE)))
    counts = [r[3][:, 0].astype(jnp.int32) for r in routes]
    total = sum(counts)
    padded = ((total + MOE_TM - 1) // MOE_TM) * MOE_TM
    ends = jnp.cumsum(padded)
    starts = ends - padded
    n_rows = sum(h.shape[0] for h in h_list) * TOP_K
    n_tiles = _moe_tiles(n_rows)
    assert xs8.shape == _moe_buffer_shape(n_rows)
    n_used = (ends[-1] // MOE_TM).astype(jnp.int32)
    tile_start = jnp.arange(n_tiles, dtype=jnp.int32) * MOE_TM
    tile_expert = jnp.sum((tile_start[:, None] >= ends[None, :]).astype(jnp.int32), axis=1)
    last_expert = jnp.sum((((n_used - 1) * MOE_TM) >= ends).astype(jnp.int32))
    tile_expert = jnp.where(jnp.arange(n_tiles) < n_used, tile_expert, last_expert).astype(jnp.int32)

    poss = []
    seen = jnp.zeros((N_EXPERTS,), jnp.int32)
    for u8, (idx, _, rank, _), cnt in zip(u8_list, routes, counts):
        base = starts + seen
        pos = rank
        for e in range(N_EXPERTS):
            pos = pos + jnp.where(idx == e, base[e], 0)
        poss.append(pos)
        seen = seen + cnt
        xs8 = _dispatch(u8, pos, xs8, tb=min(2 * MOE_TB, u8.shape[0] // ROW_TILE))
    group_rows = jnp.sum(jnp.where(jnp.arange(N_EXPERTS)[None, :] == tile_expert[:, None],
                                   (starts + total)[None, :], 0), axis=1)
    tile_valid = jnp.clip(group_rows - tile_start, 0, MOE_TM)
    tile_info = jnp.concatenate([n_used.reshape(1), tile_valid]).astype(jnp.int32)
    ys8 = _experts(xs8, tile_expert, tile_info, w_gate, w_up, w_down, tm=MOE_TM, tf=MOE_TF)
    outs = []
    for h, pos, (_, wts, _, _) in zip(h_list, poss, routes):
        outs.append(_combine(h, pos, wts.T, ys8, final_g, tb=min(MOE_TB, h.shape[0])))
    return outs


def _rmsnorm_kernel(x_ref, g_ref, o_ref):
    x = x_ref[...]
    ms = jnp.mean(x * x, axis=-1, keepdims=True)
    o_ref[...] = x * lax.rsqrt(ms + EPS) * g_ref[...]


def _rmsnorm(x, g, *, tm):
    rows, d = x.shape
    return pl.pallas_call(
        _rmsnorm_kernel,
        grid=(rows // tm,),
        in_specs=[pl.BlockSpec((tm, d), lambda i: (i, 0)), pl.BlockSpec((1, d), lambda i: (0, 0))],
        out_specs=pl.BlockSpec((tm, d), lambda i: (i, 0)),
        out_shape=jax.ShapeDtypeStruct((rows, d), F32),
        compiler_params=_params(("parallel",)),
        name="final_norm",
    )(x, g.reshape(1, d))


def _pack_s5_state(re, im):
    n = re.shape[0]
    return jnp.concatenate([re.reshape(n, S5_LANES), im.reshape(n, S5_LANES)], axis=-1)


def _unpack_s5_state(h):
    n = h.shape[0]
    return (h[:, :S5_LANES].reshape(n, S5_GROUPS, S5_STATE), h[:, S5_LANES:].reshape(n, S5_GROUPS, S5_STATE))


def _mixer(x, z, s5_h0, ret_s0, conv_buf, pos_offset, p, *, single_step, tiled_u, layer=0, ret_stack=None, jobs=None):
    jobs = jobs or {}
    done = {}
    bsz, seq, d = x.shape
    s5_out, s5_state, done["s5"] = _s5_branch(z, s5_h0, p["a_re"], p["a_im"], p["bmat"], p["cmat"], p["d_skip"],
                                              p["glu_w"], p["glu_b"], jobs.get("s5", ()), single_step=single_step,
                                              tt=1 if single_step else S5_TT)

    if single_step:
        z2 = z.reshape(seq, N_IN)
        ret_out, ret_state = _retention_step(z2, ret_s0, layer, ret_stack, pos_offset, bb=16)
        conv_out, conv_state = _conv_step(z2, conv_buf, layer, p["pw_b"], p["dw_w"], p["dw_b"], p["ln_g"], p["ln_b"],
                                          bb=32)
        ret_out = ret_out.reshape(bsz, seq, RET_V)
        conv_out = conv_out.reshape(bsz, seq, CONV_WIDTH)
    else:
        ret_out, ret_state, done["ret"] = _retention_prompt(z, ret_s0, pos_offset, jobs.get("ret", ()), tt=MIX_TT)
        conv_out, conv_state, done["conv"] = _conv_prompt(z, conv_buf, p["pw_b"], p["dw_w"], p["dw_b"], p["ln_g"],
                                                          p["ln_b"], jobs.get("conv", ()), tt=MIX_TT)

    h, u = _merge(x, s5_out, ret_out, conv_out, z, p["s5_proj"], p["ret_proj"], p["conv_proj"], p["w_out"],
                  p["norm_ffn_g"], tm=min(512, seq), tiled_u=tiled_u)
    return h, u, (s5_state, ret_state, conv_state), done


def kernel(x_prompt, x_sample, state_s5_re, state_s5_im, state_ret, state_conv, norm_mix_g, w_in, s5_lambda_re, s5_lambda_im, s5_log_dt, s5_b_re, s5_b_im, s5_c_re, s5_c_im, s5_d, s5_glu_w, s5_glu_b, s5_proj, ret_proj, conv_pw_b, conv_dw_w, conv_dw_b, conv_ln_g, conv_ln_b, conv_proj, w_out, norm_ffn_g, ffn_w_gate, ffn_w_up, ffn_w_down, moe_router, moe_w_gate, moe_w_up, moe_w_down, norm_final_g):
    depth = w_in.shape[0]
    bp, seq, d = x_prompt.shape
    ns = x_sample.shape[0]
    past_len = 16384
    bf = lambda a: a.astype(BF16)

    hp = x_prompt
    hs = x_sample.reshape(1, ns, d)
    zero_s5 = jnp.zeros((bp, 2 * S5_LANES), F32)
    zero_ret = jnp.zeros((bp, RET_HEADS, RET_DK, RET_DV), F32)
    zero_conv = jnp.zeros((bp, CONV_K - 1, CONV_WIDTH), F32)

    conv_rows = jnp.transpose(state_conv, (0, 2, 1, 3))
    p_states, s_states = [], []
    ret_stack = None
    normed = False
    groups = depth * S5_GROUPS
    disc = _s5_discretize(s5_lambda_re.reshape(groups, S5_STATE), s5_lambda_im.reshape(groups, S5_STATE),
                          s5_log_dt.reshape(groups), s5_b_re.reshape(groups, S5_STATE, S5_GROUP),
                          s5_b_im.reshape(groups, S5_STATE, S5_GROUP))
    disc = [a.reshape((depth, S5_GROUPS) + a.shape[1:]) for a in disc]
    for l in range(depth):
        is_moe = l % 2 == 1
        a_re, a_im, bb_re, bb_im = (a[l] for a in disc)
        bmat, cmat = _s5_block_mats(bb_re, bb_im, s5_c_re[l], s5_c_im[l])
        p = dict(a_re=a_re, a_im=a_im, bmat=bmat, cmat=cmat, d_skip=s5_d[l], glu_w=bf(s5_glu_w[l]),
                 glu_b=s5_glu_b[l], pw_b=conv_pw_b[l], dw_w=conv_dw_w[l], dw_b=conv_dw_b[l], ln_g=conv_ln_g[l],
                 ln_b=conv_ln_b[l], s5_proj=bf(s5_proj[l]), ret_proj=bf(ret_proj[l]), conv_proj=bf(conv_proj[l]),
                 w_out=bf(w_out[l]), norm_ffn_g=norm_ffn_g[l])
        j = l // 2
        flat = lambda w: w.reshape(-1, w.shape[-1])
        if is_moe:
            jobs = {"conv": [_cast_job(flat(moe_w_up[j]))],
                    "proj": [(None, _moe_buffer_shape((bp * seq + ns) * TOP_K), F32)]}
        else:
            jobs = {"s5": [_cast_job(ffn_w_gate[j]), _cast_job(ffn_w_up[j]), _cast_job(ffn_w_down[j])]}
            if l + 1 < depth:
                jm = (l + 1) // 2
                jobs["conv"] = [_cast_job(flat(moe_w_gate[jm]))]
                jobs["ret"] = [_cast_job(flat(moe_w_down[jm]))]
        zp, zs, proj_done = _norm_proj(hp, hs.reshape(ns, d), norm_mix_g[l], w_in, l, jobs.pop("proj", ()),
                                       tt=512, tn=PROJ_TN)
        hp, up, st_p, done = _mixer(hp, zp, zero_s5, zero_ret, zero_conv, 0.0, p, single_step=False, tiled_u=is_moe,
                                    layer=l, jobs=jobs)
        hs, us, st_s, _ = _mixer(hs, zs.reshape(1, ns, N_IN), _pack_s5_state(state_s5_re[l], state_s5_im[l]),
                                 state_ret, conv_rows, float(past_len), p, single_step=True, tiled_u=is_moe, layer=l,
                                 ret_stack=ret_stack)
        p_states.append(st_p)
        s_states.append(st_s)

        hp2, hs2 = hp.reshape(bp * seq, d), hs.reshape(ns, d)
        if is_moe:
            ret_stack = st_s[1]
            final_g = norm_final_g if l == depth - 1 else None
            normed = final_g is not None
            w_gate_b, w_down_b = (c.reshape(w.shape[1:]) for c, w in zip(early, (moe_w_gate, moe_w_down)))
            w_up_b = done["conv"][0].reshape(moe_w_up.shape[1:])
            hp2, hs2 = _moe([hp2, hs2], [up, us], moe_router[j], w_gate_b, w_up_b, w_down_b, proj_done[0], final_g)
        else:
            wg, wu, wd = done["s5"]
            if l + 1 < depth:
                early = [done["conv"][0], done["ret"][0]]
            ffn_jobs = []
            if l == 0 and depth > 1:
                rows = st_s[1].size // RET_DV
                ffn_jobs = [(st_s[1].reshape(rows, RET_DV), (depth * rows, RET_DV), F32)]
            hp2, made = _ffn(hp2, up.reshape(bp * seq, d), wg, wu, wd, ffn_jobs, tm=512, tf=FFN_TF)
            if l == 0:
                ret_stack = made[0].reshape(state_ret.shape) if ffn_jobs else st_s[1][None]
            else:
                ret_stack = st_s[1]
            hs2, _ = _ffn(hs2, us.reshape(ns, d), wg, wu, wd, tm=ns, tf=FFN_TF)
        hp = hp2.reshape(bp, seq, d)
        hs = hs2.reshape(1, ns, d)

    if normed:
        y_prompt, y_sample = hp, hs.reshape(ns, 1, d)
    else:
        y_prompt = _rmsnorm(hp.reshape(bp * seq, d), norm_final_g, tm=512).reshape(bp, seq, d)
        y_sample = _rmsnorm(hs.reshape(ns, d), norm_final_g, tm=ns).reshape(ns, 1, d)

    def stack_s5(states):
        s5 = [_unpack_s5_state(s[0]) for s in states]
        return jnp.stack([a for a, _ in s5]), jnp.stack([b for _, b in s5])

    p_re, p_im = stack_s5(p_states)
    s_re, s_im = stack_s5(s_states)
    p_ret = jnp.stack([s[1] for s in p_states])
    p_conv = jnp.stack([s[2] for s in p_states])
    s_conv = jnp.transpose(jnp.stack([s[2] for s in s_states]), (0, 2, 1, 3))
    return (y_prompt, y_sample, p_re, p_im, p_ret, p_conv, s_re, s_im, ret_stack, s_conv)
```

```python
import functools
import math

import jax
import jax.numpy as jnp
from jax import lax
from jax.experimental import pallas as pl
from jax.experimental.pallas import tpu as pltpu

F32 = jnp.float32
BF16 = jnp.bfloat16

D_MODEL = 1024
S5_WIDTH = 512
S5_GROUP = 16
S5_GROUPS = 32
S5_STATE = 64
S5_LANES = S5_GROUPS * S5_STATE
RET_HEADS = 4
RET_DK = 128
RET_DV = 256
RET_QK = RET_HEADS * RET_DK
RET_V = RET_HEADS * RET_DV
RET_CHUNK = 128
ROPE_BASE = 10000.0
CONV_WIDTH = 512
CONV_K = 31
CONV_HIST = 32
N_EXPERTS = 8
TOP_K = 2
N_BRANCH = 3
EPS = 1e-6
N_IN = S5_WIDTH + 2 * RET_QK + 2 * RET_V + 2 * CONV_WIDTH + N_BRANCH * D_MODEL
ZB = 512
ZC_S5 = 0
ZC_Q = ZC_S5 + S5_WIDTH // ZB
ZC_K = ZC_Q + RET_QK // ZB
ZC_V = ZC_K + RET_QK // ZB
ZC_G = ZC_V + RET_V // ZB
ZC_CONV = ZC_G + RET_V // ZB
ZC_GATE = ZC_CONV + 2 * CONV_WIDTH // ZB
HEADS_PER_ZB = ZB // RET_DV

ROW_TILE = 8
LANES = 128
ROW_CHUNKS = D_MODEL // LANES
VMEM_LIMIT = 48 * 1024 * 1024
ROW_SPLIT = 2


def _params(sem):
    return pltpu.CompilerParams(dimension_semantics=sem, vmem_limit_bytes=VMEM_LIMIT)


def _silu(x):
    return x * jax.nn.sigmoid(x)


def _bdot(a, b):
    return jnp.dot(a.astype(BF16), b.astype(BF16), preferred_element_type=F32)


def _norm_proj_kernel(x_ref, g_ref, w_ref, x2_ref, *rest, riders):
    rider_in, (o_ref, o2_ref) = rest[:riders.n_in], rest[riders.n_in:riders.n_in + 2]
    rider_out, w_scr = rest[riders.n_in + 2:-1], rest[-1]
    riders.run(rider_in, rider_out)

    def project(x):
        ms = jnp.mean(x * x, axis=-1, keepdims=True)
        u = (x * lax.rsqrt(ms + EPS) * g_ref[...]).astype(BF16)
        return jnp.dot(u, w_scr[...], preferred_element_type=F32).astype(BF16)

    @pl.when((pl.program_id(1) == 0) & (pl.program_id(2) == 0))
    def _():
        w_scr[...] = w_ref[...].astype(BF16)
        o2_ref[...] = project(x2_ref[...])

    half = x_ref.shape[0] // ROW_SPLIT
    for r in range(ROW_SPLIT):
        rows = slice(r * half, (r + 1) * half)
        o_ref[rows, :] = project(x_ref[rows, :])


def _norm_proj(x, x2, g, w_all, layer, jobs=(), *, tt, tn):
    bsz, seq, d = x.shape
    r2 = x2.shape[0]
    n = w_all.shape[2]
    grid = (n // tn, bsz, seq // tt)
    riders = _Riders(jobs, grid, lambda j, b, i: (j * grid[1] + b) * grid[2] + i)
    outs = pl.pallas_call(
        functools.partial(_norm_proj_kernel, riders=riders),
        grid=grid,
        in_specs=[
            pl.BlockSpec((None, tt, d), lambda j, b, i: (b, i, 0)),
            pl.BlockSpec((1, d), lambda j, b, i: (0, 0)),
            pl.BlockSpec((None, d, tn), lambda j, b, i: (layer, 0, j)),
            pl.BlockSpec((r2, d), lambda j, b, i: (0, 0)),
            *riders.in_specs,
        ],
        out_specs=[pl.BlockSpec((None, tt, tn), lambda j, b, i: (b, i, j)),
                   pl.BlockSpec((r2, tn), lambda j, b, i: (0, j)),
                   *riders.out_specs],
        out_shape=[jax.ShapeDtypeStruct((bsz, seq, n), BF16), jax.ShapeDtypeStruct((r2, n), BF16),
                   *riders.out_shapes],
        scratch_shapes=[pltpu.VMEM((d, tn), BF16)],
        compiler_params=_params(("arbitrary", "arbitrary", "arbitrary")),
        name="norm_proj",
    )(x, g.reshape(1, d), w_all, x2, *riders.inputs)
    return outs[0], outs[1], list(outs[2:])


def _s5_disc_kernel(lre_ref, lim_ref, ldt_ref, bre_ref, bim_ref, are_ref, aim_ref, ore_ref, oim_ref):
    lam_re = lre_ref[...]
    lam_im = lim_ref[...]
    dt = jnp.exp(ldt_ref[...])
    mag = jnp.exp(lam_re * dt)
    ang = lam_im * dt
    lbar_re = mag * jnp.cos(ang)
    lbar_im = mag * jnp.sin(ang)
    den = lam_re * lam_re + lam_im * lam_im
    nr = lbar_re - 1.0
    f_re = (nr * lam_re + lbar_im * lam_im) / den
    f_im = (lbar_im * lam_re - nr * lam_im) / den
    b_re = bre_ref[...]
    b_im = bim_ref[...]
    are_ref[...] = lbar_re
    aim_ref[...] = lbar_im
    ore_ref[...] = f_re * b_re - f_im * b_im
    oim_ref[...] = f_re * b_im + f_im * b_re


def _s5_discretize(lam_re, lam_im, log_dt, b_re, b_im):
    g, n = lam_re.shape
    p = b_re.shape[-1]
    rows = g * n
    col = lambda a: a.reshape(rows, 1)
    ldt = jnp.broadcast_to(log_dt[:, None], (g, n))
    outs = pl.pallas_call(
        _s5_disc_kernel,
        out_shape=[jax.ShapeDtypeStruct((rows, 1), F32), jax.ShapeDtypeStruct((rows, 1), F32),
                   jax.ShapeDtypeStruct((rows, p), F32), jax.ShapeDtypeStruct((rows, p), F32)],
        name="s5_discretize",
    )(col(lam_re), col(lam_im), col(ldt), b_re.reshape(rows, p), b_im.reshape(rows, p))
    a_re, a_im, bb_re, bb_im = outs
    return a_re.reshape(g, n), a_im.reshape(g, n), bb_re.reshape(g, n, p), bb_im.reshape(g, n, p)


S5_KCH = 128
S5_NCHUNK = S5_WIDTH // S5_KCH
S5_GPC = S5_KCH // S5_GROUP
S5_SPC = S5_GPC * S5_STATE


def _s5_block_mats(bbar_re, bbar_im, c_re, c_im):
    eye = jnp.eye(S5_GPC, dtype=F32)

    def in_blocks(bb):
        t = bb.reshape(S5_NCHUNK, S5_GPC, S5_STATE, S5_GROUP)
        m = jnp.einsum("cgnp,gh->cgphn", t, eye)
        return m.reshape(S5_NCHUNK, S5_KCH, S5_SPC)

    def out_blocks(cc):
        t = cc.reshape(S5_NCHUNK, S5_GPC, S5_GROUP, S5_STATE)
        m = jnp.einsum("cgpn,gh->cgnhp", t, eye)
        return m.reshape(S5_NCHUNK, S5_SPC, S5_KCH)

    bmat = jnp.concatenate([in_blocks(bbar_re), in_blocks(bbar_im)], axis=-1).astype(BF16)
    cmat = jnp.stack([out_blocks(c_re), -out_blocks(c_im)], axis=1).astype(BF16)
    return bmat, cmat


def _gelu_tanh(x):
    return 0.5 * x * (1.0 + jnp.tanh(math.sqrt(2.0 / math.pi) * (x + 0.044715 * (x * x * x))))


class _Riders:
    def __init__(self, jobs, grid, step_index):
        self.steps = math.prod(grid)
        self.n_axes = len(grid)
        self.step_index = step_index
        self.inputs, self.in_specs, self.out_specs, self.out_shapes, self.src_steps = [], [], [], [], []
        for src, shape, dtype in jobs:
            slab = shape[0] // self.steps
            assert slab * self.steps == shape[0] and slab % 16 == 0
            n_src = 0 if src is None else src.shape[0] // slab
            if src is not None:
                assert n_src * slab == src.shape[0] and src.shape[1] == shape[1]
                self.inputs.append(src)
                self.in_specs.append(pl.BlockSpec(
                    (slab, shape[1]), lambda *idx, n=n_src: (jnp.minimum(step_index(*idx[:self.n_axes]), n - 1), 0)))
            self.out_specs.append(pl.BlockSpec((slab, shape[1]), lambda *idx: (step_index(*idx[:self.n_axes]), 0)))
            self.out_shapes.append(jax.ShapeDtypeStruct(shape, dtype))
            self.src_steps.append(n_src)

    @property
    def n_in(self):
        return len(self.inputs)

    @property
    def n_out(self):
        return len(self.out_shapes)

    def run(self, in_refs, out_refs):
        step = self.step_index(*(pl.program_id(a) for a in range(self.n_axes)))
        srcs = iter(in_refs)
        for dst, n_src in zip(out_refs, self.src_steps):
            if n_src == 0:
                dst[...] = jnp.zeros_like(dst)
                continue
            src = next(srcs)
            if n_src == self.steps:
                dst[...] = src[...].astype(dst.dtype)
                continue

            @pl.when(step < n_src)
            def _(src=src, dst=dst):
                dst[...] = src[...].astype(dst.dtype)

            @pl.when(step >= n_src)
            def _(dst=dst):
                dst[...] = jnp.zeros_like(dst)


def _cast_job(w):
    return (w, w.shape, BF16)


def _s5_kernel(u_ref, h0_ref, are_ref, aim_ref, bmat_ref, cmat_ref, d_ref, gw_ref, gb_ref, *rest,
               nb, tt, lane_chunk, riders):
    rider_in, rest = rest[:riders.n_in], rest[riders.n_in:]
    o_ref, hout_ref = rest[:2]
    rider_out = rest[2:2 + riders.n_out]
    hs_scr, h_scr, io_scr = rest[2 + riders.n_out:]
    riders.run(rider_in, rider_out)

    @pl.when(pl.program_id(0) == 0)
    def _():
        h_scr[...] = h0_ref[...]

    def seq_rows(b):
        return pl.ds(b, tt, stride=nb)

    def lanes(c):
        return slice(c * S5_KCH, (c + 1) * S5_KCH)

    if tt == 1:
        for c in range(S5_NCHUNK):
            io_scr[c] = u_ref[:, lanes(c)].astype(F32)
    else:
        for b in range(nb):
            for c in range(S5_NCHUNK):
                io_scr[c, seq_rows(b), :] = u_ref[b, :, lanes(c)].astype(F32)

    u = jnp.concatenate([io_scr[c] for c in range(S5_NCHUNK)], axis=-1)
    for c in range(S5_NCHUNK):
        bu = jnp.dot(io_scr[c].astype(BF16), bmat_ref[c], preferred_element_type=F32)
        hs_scr[:, c * S5_SPC:(c + 1) * S5_SPC] = bu[:, :S5_SPC]
        hs_scr[:, S5_LANES + c * S5_SPC:S5_LANES + (c + 1) * S5_SPC] = bu[:, S5_SPC:]

    for lc in range(S5_LANES // lane_chunk):
        re_sl = slice(lc * lane_chunk, (lc + 1) * lane_chunk)
        im_sl = slice(S5_LANES + lc * lane_chunk, S5_LANES + (lc + 1) * lane_chunk)
        a_re = jnp.broadcast_to(are_ref[:, re_sl], (nb, lane_chunk))
        a_im = jnp.broadcast_to(aim_ref[:, re_sl], (nb, lane_chunk))

        def step(t, carry):
            h_re, h_im = carry
            r0 = t * nb if isinstance(t, int) else pl.multiple_of(t * nb, nb)
            n_re = a_re * h_re - a_im * h_im + hs_scr[pl.ds(r0, nb), re_sl]
            n_im = a_re * h_im + a_im * h_re + hs_scr[pl.ds(r0, nb), im_sl]
            hs_scr[pl.ds(r0, nb), re_sl] = n_re
            hs_scr[pl.ds(r0, nb), im_sl] = n_im
            return n_re, n_im

        carry = (h_scr[:, re_sl], h_scr[:, im_sl])
        if tt == 1:
            carry = step(0, carry)
        else:
            carry = lax.fori_loop(0, tt, step, carry, unroll=4)
        h_scr[:, re_sl] = carry[0]
        h_scr[:, im_sl] = carry[1]

    hout_ref[...] = h_scr[...]

    ys = []
    for c in range(S5_NCHUNK):
        h_re = hs_scr[:, c * S5_SPC:(c + 1) * S5_SPC].astype(BF16)
        h_im = hs_scr[:, S5_LANES + c * S5_SPC:S5_LANES + (c + 1) * S5_SPC].astype(BF16)
        ys.append(jnp.dot(h_re, cmat_ref[c, 0], preferred_element_type=F32)
                  + jnp.dot(h_im, cmat_ref[c, 1], preferred_element_type=F32))
    y = jnp.concatenate(ys, axis=-1) + d_ref[...] * u
    z = _gelu_tanh(y)
    gate = jnp.dot(z.astype(BF16), gw_ref[...], preferred_element_type=F32) + gb_ref[...]
    out = z * jax.nn.sigmoid(gate)
    if tt == 1:
        o_ref[...] = out.astype(o_ref.dtype)
    else:
        for c in range(S5_NCHUNK):
            io_scr[c] = out[:, lanes(c)]
        for b in range(nb):
            for c in range(S5_NCHUNK):
                o_ref[b, :, lanes(c)] = io_scr[c, seq_rows(b), :].astype(o_ref.dtype)


def _s5_branch(z, h0, a_re, a_im, bmat, cmat, d_skip, glu_w, glu_b, jobs=(), *, single_step, tt):
    bsz, seq, _ = z.shape
    if single_step:
        assert bsz == 1 and tt == 1
        nb = seq
        in_spec = pl.BlockSpec((None, nb, S5_WIDTH), lambda i: (0, 0, ZC_S5))
        out_spec = pl.BlockSpec((None, nb, S5_WIDTH), lambda i: (0, 0, 0))
        grid = (1,)
    else:
        nb = bsz
        in_spec = pl.BlockSpec((nb, tt, S5_WIDTH), lambda i: (0, i, ZC_S5))
        out_spec = pl.BlockSpec((nb, tt, S5_WIDTH), lambda i: (0, i, 0))
        grid = (seq // tt,)
    rblk = tt * nb
    lane_chunk = 1024 if nb <= 8 else 512
    const = lambda shape: pl.BlockSpec(shape, lambda i: (0,) * len(shape))
    riders = _Riders(jobs, grid, lambda i: i)
    outs = pl.pallas_call(
        functools.partial(_s5_kernel, nb=nb, tt=tt, lane_chunk=lane_chunk, riders=riders),
        grid=grid,
        in_specs=[
            in_spec,
            const((nb, 2 * S5_LANES)),
            const((1, S5_LANES)),
            const((1, S5_LANES)),
            const(bmat.shape),
            const(cmat.shape),
            const((1, S5_WIDTH)),
            const((S5_WIDTH, S5_WIDTH)),
            const((1, S5_WIDTH)),
            *riders.in_specs,
        ],
        out_specs=[out_spec, const((nb, 2 * S5_LANES)), *riders.out_specs],
        out_shape=[jax.ShapeDtypeStruct((bsz, seq, S5_WIDTH), BF16),
                   jax.ShapeDtypeStruct((nb, 2 * S5_LANES), F32), *riders.out_shapes],
        scratch_shapes=[pltpu.VMEM((rblk, 2 * S5_LANES), F32), pltpu.VMEM((nb, 2 * S5_LANES), F32),
                        pltpu.VMEM((S5_NCHUNK, rblk, S5_KCH), F32)],
        compiler_params=_params(("arbitrary",)),
        name="s5_branch",
    )(z, h0, a_re.reshape(1, S5_LANES), a_im.reshape(1, S5_LANES), bmat, cmat,
      d_skip.reshape(1, S5_WIDTH), glu_w, glu_b.reshape(1, S5_WIDTH), *riders.inputs)
    return outs[0], outs[1], list(outs[2:])


def _rope_tables(pos):
    half = RET_DK // 2
    freqs = ROPE_BASE ** (-jnp.arange(half, dtype=F32) / half)
    ang = pos[:, None] * freqs[None, :]
    cos = jnp.cos(ang)
    sin = jnp.sin(ang)
    return jnp.concatenate([cos, cos], axis=-1), jnp.concatenate([-sin, sin], axis=-1)


def _rope(x, cos, sin):
    return x * cos + pltpu.roll(x, RET_DK // 2, 1) * sin


def _group_norm(o):
    mu = jnp.mean(o, axis=-1, keepdims=True)
    d = o - mu
    var = jnp.mean(d * d, axis=-1, keepdims=True)
    return d * lax.rsqrt(var + EPS)


def _retention_tables(chunk):
    log_gamma = jnp.log(1.0 - 2.0 ** (-5.0 - jnp.arange(RET_HEADS, dtype=F32)))
    idx = jnp.arange(chunk, dtype=F32)
    diff = idx[:, None] - idx[None, :]
    decay = jnp.where(diff >= 0, jnp.exp(jnp.maximum(diff, 0.0)[None] * log_gamma[:, None, None]), 0.0)
    cross = jnp.exp((idx + 1.0)[None, :] * log_gamma[:, None])[:, :, None]
    kdec = jnp.exp((chunk - 1.0 - idx)[None, :] * log_gamma[:, None])[:, :, None]
    full = jnp.exp(chunk * log_gamma)
    return decay, cross, kdec, full


def _head_cols(refs, h, rows):
    lo = (h % HEADS_PER_ZB) * RET_DV
    return refs[h // HEADS_PER_ZB][rows, lo:lo + RET_DV].astype(F32)


def _retention_kernel(q_ref, k_ref, v0_ref, v1_ref, g0_ref, g1_ref, cos_ref, sin_ref, s0_ref, decay_ref, cross_ref,
                      kdec_ref, full_ref, *rest, n_chunks, riders):
    rider_in, rest = rest[:riders.n_in], rest[riders.n_in:]
    o_ref, sout_ref = rest[:2]
    rider_out, s_scr = rest[2:-1], rest[-1]
    riders.run(rider_in, rider_out)

    @pl.when(pl.program_id(1) == 0)
    def _():
        s_scr[...] = s0_ref[...]

    for c in range(n_chunks):
        rows = slice(c * RET_CHUNK, (c + 1) * RET_CHUNK)
        cos = cos_ref[rows, :]
        sin = sin_ref[rows, :]
        for h in range(RET_HEADS):
            qk_cols = slice(h * RET_DK, (h + 1) * RET_DK)
            v_cols = slice(h * RET_DV, (h + 1) * RET_DV)
            qh = _rope(q_ref[rows, qk_cols].astype(F32), cos, sin)
            kh = _rope(k_ref[rows, qk_cols].astype(F32), cos, sin) * (RET_DK ** -0.5)
            vb = _head_cols((v0_ref, v1_ref), h, rows).astype(BF16)
            qb = qh.astype(BF16)
            state = s_scr[h]
            inner = lax.dot_general(qb, kh.astype(BF16), (((1,), (1,)), ((), ())),
                                    preferred_element_type=F32) * decay_ref[h]
            out = (jnp.dot(inner.astype(BF16), vb, preferred_element_type=F32)
                   + jnp.dot(qb, state.astype(BF16), preferred_element_type=F32) * cross_ref[h])
            kd = (kh * kdec_ref[h]).astype(BF16)
            s_scr[h] = full_ref[h] * state + jnp.dot(kd.T, vb, preferred_element_type=F32)
            gate = _head_cols((g0_ref, g1_ref), h, rows)
            o_ref[rows, v_cols] = (_silu(gate) * _group_norm(out)).astype(o_ref.dtype)

    sout_ref[...] = s_scr[...]


def _retention_prompt(z, state0, pos_offset, jobs=(), *, tt):
    bsz, seq, _ = z.shape
    assert seq % RET_CHUNK == 0 and tt % RET_CHUNK == 0
    cos, sin = _rope_tables(jnp.arange(seq, dtype=F32) + pos_offset)
    decay, cross, kdec, full = _retention_tables(RET_CHUNK)
    full = jnp.broadcast_to(full[:, None, None], (RET_HEADS, 1, RET_DV))
    const = lambda shape: pl.BlockSpec(shape, lambda b, i: (0,) * len(shape))
    zblk = lambda col: pl.BlockSpec((None, tt, ZB), lambda b, i: (b, i, col))
    n_i = seq // tt
    riders = _Riders(jobs, (bsz, n_i), lambda b, i: b * n_i + i)
    outs = pl.pallas_call(
        functools.partial(_retention_kernel, n_chunks=tt // RET_CHUNK, riders=riders),
        grid=(bsz, n_i),
        in_specs=[
            zblk(ZC_Q), zblk(ZC_K), zblk(ZC_V), zblk(ZC_V + 1), zblk(ZC_G), zblk(ZC_G + 1),
            pl.BlockSpec((tt, RET_DK), lambda b, i: (i, 0)),
            pl.BlockSpec((tt, RET_DK), lambda b, i: (i, 0)),
            pl.BlockSpec((None, RET_HEADS, RET_DK, RET_DV), lambda b, i: (b, 0, 0, 0)),
            const((RET_HEADS, RET_CHUNK, RET_CHUNK)),
            const((RET_HEADS, RET_CHUNK, 1)),
            const((RET_HEADS, RET_CHUNK, 1)),
            const((RET_HEADS, 1, RET_DV)),
            *riders.in_specs,
        ],
        out_specs=[
            pl.BlockSpec((None, tt, RET_V), lambda b, i: (b, i, 0)),
            pl.BlockSpec((None, RET_HEADS, RET_DK, RET_DV), lambda b, i: (b, 0, 0, 0)),
            *riders.out_specs,
        ],
        out_shape=[jax.ShapeDtypeStruct((bsz, seq, RET_V), BF16),
                   jax.ShapeDtypeStruct((bsz, RET_HEADS, RET_DK, RET_DV), F32), *riders.out_shapes],
        scratch_shapes=[pltpu.VMEM((RET_HEADS, RET_DK, RET_DV), F32)],
        compiler_params=_params(("arbitrary", "arbitrary")),
        name="retention_prompt",
    )(z, z, z, z, z, z, cos, sin, state0, decay, cross, kdec, full, *riders.inputs)
    return outs[0], outs[1], list(outs[2:])


def _retention_step_kernel(q_ref, k_ref, v0_ref, v1_ref, g0_ref, g1_ref, cos_ref, sin_ref, s_ref, gam_ref, *rest,
                           bb, layer):
    del layer
    o_ref, sout_ref, o_scr = rest[-3:]
    cos = cos_ref[...]
    sin = sin_ref[...]
    for h in range(RET_HEADS):
        qk_cols = slice(h * RET_DK, (h + 1) * RET_DK)
        v_cols = slice(h * RET_DV, (h + 1) * RET_DV)
        qh = _rope(q_ref[:, qk_cols].astype(F32), cos, sin)
        kh = _rope(k_ref[:, qk_cols].astype(F32), cos, sin) * (RET_DK ** -0.5)
        qk = jnp.sum(qh * kh, axis=-1, keepdims=True)
        q_t = qh.T
        k_t = kh.T
        gamma = gam_ref[h]
        v_all = _head_cols((v0_ref, v1_ref), h, slice(None))
        for b in range(bb):
            state = s_ref[b, h]
            vrow = v_all[b:b + 1, :]
            qs = jnp.sum(q_t[:, b:b + 1] * state, axis=0, keepdims=True)
            o_scr[b:b + 1, v_cols] = qk[b:b + 1, :] * vrow + qs * gamma
            sout_ref[b, h] = gamma * state + k_t[:, b:b + 1] * vrow
    for h in range(RET_HEADS):
        v_cols = slice(h * RET_DV, (h + 1) * RET_DV)
        gate = _head_cols((g0_ref, g1_ref), h, slice(None))
        o_ref[:, v_cols] = (_silu(gate) * _group_norm(o_scr[:, v_cols])).astype(o_ref.dtype)


def _retention_step(z, states, layer, stack, pos, *, bb):
    n = z.shape[0]
    cos, sin = _rope_tables(jnp.full((1,), pos, F32))
    log_gamma = jnp.log(1.0 - 2.0 ** (-5.0 - jnp.arange(RET_HEADS, dtype=F32)))
    gam = jnp.broadcast_to(jnp.exp(log_gamma)[:, None, None], (RET_HEADS, 1, RET_DV))
    const = lambda shape: pl.BlockSpec(shape, lambda i: (0,) * len(shape))
    slab = pl.BlockSpec((None, bb, RET_HEADS, RET_DK, RET_DV), lambda i: (layer, i, 0, 0, 0))
    zblk = lambda col: pl.BlockSpec((bb, ZB), lambda i: (i, col))
    in_specs = [
        zblk(ZC_Q), zblk(ZC_K), zblk(ZC_V), zblk(ZC_V + 1), zblk(ZC_G), zblk(ZC_G + 1),
        const((1, RET_DK)),
        const((1, RET_DK)),
        slab,
        const((RET_HEADS, 1, RET_DV)),
    ]
    args = [z, z, z, z, z, z, cos, sin, states, gam]
    if stack is None:
        state_spec = pl.BlockSpec((bb, RET_HEADS, RET_DK, RET_DV), lambda i: (i, 0, 0, 0))
        state_shape = jax.ShapeDtypeStruct(states.shape[1:], F32)
        aliases = {}
    else:
        in_specs.append(pl.BlockSpec(memory_space=pl.ANY))
        args.append(stack)
        state_spec = slab
        state_shape = jax.ShapeDtypeStruct(stack.shape, F32)
        aliases = {len(args) - 1: 1}
    return pl.pallas_call(
        functools.partial(_retention_step_kernel, bb=bb, layer=layer),
        grid=(n // bb,),
        in_specs=in_specs,
        out_specs=[pl.BlockSpec((bb, RET_V), lambda i: (i, 0)), state_spec],
        out_shape=[jax.ShapeDtypeStruct((n, RET_V), BF16), state_shape],
        scratch_shapes=[pltpu.VMEM((bb, RET_V), F32)],
        input_output_aliases=aliases,
        compiler_params=_params(("parallel",)),
        name="retention_step",
    )(*args)


def _layer_norm(y, g, b):
    mu = jnp.mean(y, axis=-1, keepdims=True)
    d = y - mu
    var = jnp.mean(d * d, axis=-1, keepdims=True)
    return d * lax.rsqrt(var + EPS) * g + b


CONV_RB = 128
CONV_PITCH = 2
CONV_LCH = CONV_WIDTH // LANES


def _conv_glu(a_ref, b_ref, pwb_ref):
    a = a_ref[...].astype(F32) + pwb_ref[:, :CONV_WIDTH]
    b = b_ref[...].astype(F32) + pwb_ref[:, CONV_WIDTH:]
    return a * jax.nn.sigmoid(b)


def _conv_kernel(a_ref, b_ref, buf_ref, pwb_ref, dww_ref, dwb_ref, lng_ref, lnb_ref, *rest, tt, riders):
    rider_in, rest = rest[:riders.n_in], rest[riders.n_in:]
    o_ref, hist_ref = rest[:2]
    rider_out, (x_scr, y_scr) = rest[2:-2], rest[-2:]
    riders.run(rider_in, rider_out)
    _conv_body(a_ref, b_ref, buf_ref, pwb_ref, dww_ref, dwb_ref, lng_ref, lnb_ref, o_ref, hist_ref, x_scr, y_scr, tt=tt)


def _conv_body(a_ref, b_ref, buf_ref, pwb_ref, dww_ref, dwb_ref, lng_ref, lnb_ref, o_ref, hist_ref, x_scr, y_scr,
               *, tt):
    i = pl.program_id(1)

    def rows(start, n):
        return pl.ds(CONV_PITCH * start, n, stride=CONV_PITCH)

    def lanes(c):
        return slice(c * LANES, (c + 1) * LANES)

    @pl.when(i == 0)
    def _():
        for c in range(CONV_LCH):
            x_scr[c, rows(0, CONV_HIST), :] = buf_ref[:, lanes(c)]

    @pl.when(i > 0)
    def _():
        for c in range(CONV_LCH):
            x_scr[c, rows(0, CONV_HIST), :] = x_scr[c, rows(tt, CONV_HIST), :]

    glu = _conv_glu(a_ref, b_ref, pwb_ref)
    for c in range(CONV_LCH):
        x_scr[c, rows(CONV_HIST, tt), :] = glu[:, lanes(c)]
    hist_ref[...] = glu[tt - CONV_HIST:, :]

    off = CONV_HIST - (CONV_K - 1)
    for c in range(CONV_LCH):
        def row_block(r, carry, c=c):
            base = r * CONV_RB
            n_grp = CONV_RB // ROW_TILE
            accs = [jnp.broadcast_to(dwb_ref[:, lanes(c)], (ROW_TILE, LANES))] * n_grp
            for m in range(CONV_RB - ROW_TILE + CONV_K):
                win = x_scr[c, rows(base + (off + m), ROW_TILE), :]
                for k in range(m % ROW_TILE, CONV_K, ROW_TILE):
                    j = (m - k) // ROW_TILE
                    if 0 <= j < n_grp:
                        accs[j] = accs[j] + dww_ref[k:k + 1, lanes(c)] * win
            y_scr[pl.ds(pl.multiple_of(base, CONV_RB), CONV_RB), lanes(c)] = jnp.concatenate(accs, axis=0)
            return carry

        lax.fori_loop(0, tt // CONV_RB, row_block, 0)

    o_ref[...] = _silu(_layer_norm(y_scr[...], lng_ref[...], lnb_ref[...])).astype(o_ref.dtype)


def _conv_prompt(z, buf, pw_b, dw_w, dw_b, ln_g, ln_b, jobs=(), *, tt):
    bsz, seq, _ = z.shape
    assert seq >= CONV_HIST and tt >= CONV_HIST
    buf32 = jnp.pad(buf, ((0, 0), (CONV_HIST - (CONV_K - 1), 0), (0, 0)))
    const = lambda shape: pl.BlockSpec(shape, lambda b, i: (0,) * len(shape))
    zblk = lambda col: pl.BlockSpec((None, tt, ZB), lambda b, i: (b, i, col))
    n_i = seq // tt
    riders = _Riders(jobs, (bsz, n_i), lambda b, i: b * n_i + i)
    out, hist, *cast = pl.pallas_call(
        functools.partial(_conv_kernel, tt=tt, riders=riders),
        grid=(bsz, n_i),
        in_specs=[
            zblk(ZC_CONV), zblk(ZC_CONV + 1),
            pl.BlockSpec((None, CONV_HIST, CONV_WIDTH), lambda b, i: (b, 0, 0)),
            const((1, 2 * CONV_WIDTH)),
            const((CONV_K, CONV_WIDTH)),
            const((1, CONV_WIDTH)),
            const((1, CONV_WIDTH)),
            const((1, CONV_WIDTH)),
            *riders.in_specs,
        ],
        out_specs=[
            pl.BlockSpec((None, tt, CONV_WIDTH), lambda b, i: (b, i, 0)),
            pl.BlockSpec((None, CONV_HIST, CONV_WIDTH), lambda b, i: (b, 0, 0)),
            *riders.out_specs,
        ],
        out_shape=[jax.ShapeDtypeStruct((bsz, seq, CONV_WIDTH), BF16),
                   jax.ShapeDtypeStruct((bsz, CONV_HIST, CONV_WIDTH), F32), *riders.out_shapes],
        scratch_shapes=[pltpu.VMEM((CONV_LCH, CONV_PITCH * (CONV_HIST + tt), LANES), F32),
                        pltpu.VMEM((tt, CONV_WIDTH), F32)],
        compiler_params=_params(("arbitrary", "arbitrary")),
        name="conv_prompt",
    )(z, z, buf32, pw_b.reshape(1, -1), dw_w, dw_b.reshape(1, -1), ln_g.reshape(1, -1), ln_b.reshape(1, -1),
      *riders.inputs)
    return out, hist[:, CONV_HIST - (CONV_K - 1):, :], cast


def _conv_step_kernel(a_ref, b_ref, buf_ref, pwb_ref, dww_ref, dwb_ref, lng_ref, lnb_ref, o_ref, hist_ref):
    hist_len = CONV_K - 1
    glu = _conv_glu(a_ref, b_ref, pwb_ref)
    acc = dwb_ref[...] + dww_ref[hist_len:hist_len + 1, :] * glu
    for k in range(hist_len):
        acc = acc + dww_ref[k:k + 1, :] * buf_ref[k]
    o_ref[...] = _silu(_layer_norm(acc, lng_ref[...], lnb_ref[...])).astype(o_ref.dtype)
    for k in range(hist_len - 1):
        hist_ref[k] = buf_ref[k + 1]
    hist_ref[hist_len - 1] = glu


def _conv_step(z, bufs, layer, pw_b, dw_w, dw_b, ln_g, ln_b, *, bb):
    n = z.shape[0]
    hist_len = CONV_K - 1
    const = lambda shape: pl.BlockSpec(shape, lambda i: (0,) * len(shape))
    zblk = lambda col: pl.BlockSpec((bb, ZB), lambda i: (i, col))
    return pl.pallas_call(
        _conv_step_kernel,
        grid=(n // bb,),
        in_specs=[
            zblk(ZC_CONV), zblk(ZC_CONV + 1),
            pl.BlockSpec((None, hist_len, bb, CONV_WIDTH), lambda i: (layer, 0, i, 0)),
            const((1, 2 * CONV_WIDTH)),
            const((CONV_K, CONV_WIDTH)),
            const((1, CONV_WIDTH)),
            const((1, CONV_WIDTH)),
            const((1, CONV_WIDTH)),
        ],
        out_specs=[
            pl.BlockSpec((bb, CONV_WIDTH), lambda i: (i, 0)),
            pl.BlockSpec((hist_len, bb, CONV_WIDTH), lambda i: (0, i, 0)),
        ],
        out_shape=[jax.ShapeDtypeStruct((n, CONV_WIDTH), BF16),
                   jax.ShapeDtypeStruct((hist_len, n, CONV_WIDTH), F32)],
        compiler_params=_params(("parallel",)),
        name="conv_step",
    )(z, z, bufs, pw_b.reshape(1, -1), dw_w, dw_b.reshape(1, -1), ln_g.reshape(1, -1), ln_b.reshape(1, -1))


def _rows_to_tiles(tile_ref, x, rows):
    for c in range(ROW_CHUNKS):
        tile_ref[pl.ds(c, rows, stride=ROW_TILE), :] = x[:, c * LANES:(c + 1) * LANES]


def _tiles_chunk(tile_ref, c, rows):
    return tile_ref[pl.ds(c, rows, stride=ROW_TILE), :]


def _tiles_to_rows(tile_ref, rows):
    return jnp.concatenate([_tiles_chunk(tile_ref, c, rows) for c in range(ROW_CHUNKS)], axis=-1)


def _merge_kernel(x_ref, s5_ref, ret_ref, conv_ref, *rest, tiled_u):
    gate_refs = rest[:N_BRANCH * D_MODEL // ZB]
    ps5_ref, pret_ref, pconv_ref, wout_ref, g_ref, h_ref, u_ref = rest[len(gate_refs):]
    per_branch = D_MODEL // ZB

    def gate(n):
        cols = [gate_refs[n * per_branch + j][...] for j in range(per_branch)]
        return jax.nn.sigmoid(jnp.concatenate(cols, axis=-1).astype(F32))

    merged = (gate(0) * jnp.dot(s5_ref[...], ps5_ref[...], preferred_element_type=F32)
              + gate(1) * jnp.dot(ret_ref[...], pret_ref[...], preferred_element_type=F32)
              + gate(2) * jnp.dot(conv_ref[...], pconv_ref[...], preferred_element_type=F32))
    h = x_ref[...] + _bdot(merged, wout_ref[...])
    h_ref[...] = h
    ms = jnp.mean(h * h, axis=-1, keepdims=True)
    u = h * lax.rsqrt(ms + EPS) * g_ref[...]
    if tiled_u:
        _rows_to_tiles(u_ref, u, u.shape[0])
    else:
        u_ref[...] = u.astype(u_ref.dtype)


def _merge(x, s5_out, ret_out, conv_out, z, s5_proj, ret_proj, conv_proj, w_out, norm_g, *, tm, tiled_u):
    bsz, seq, d = x.shape
    const = lambda shape: pl.BlockSpec(shape, lambda b, i: (0,) * len(shape))
    tok = lambda w, col=0: pl.BlockSpec((None, tm, w), lambda b, i: (b, i, col))
    n_i = seq // tm
    n_gate = N_BRANCH * d // ZB
    if tiled_u:
        u_spec = pl.BlockSpec((tm * ROW_TILE, LANES), lambda b, i: (b * n_i + i, 0))
        u_shape = jax.ShapeDtypeStruct((bsz * seq * ROW_TILE, LANES), F32)
    else:
        u_spec = tok(d)
        u_shape = jax.ShapeDtypeStruct((bsz, seq, d), BF16)
    return pl.pallas_call(
        functools.partial(_merge_kernel, tiled_u=tiled_u),
        grid=(bsz, seq // tm),
        in_specs=[
            tok(d),
            tok(S5_WIDTH),
            tok(RET_V),
            tok(CONV_WIDTH),
            *[tok(ZB, ZC_GATE + j) for j in range(n_gate)],
            const((S5_WIDTH, d)),
            const((RET_V, d)),
            const((CONV_WIDTH, d)),
            const((d, d)),
            const((1, d)),
        ],
        out_specs=[tok(d), u_spec],
        out_shape=[jax.ShapeDtypeStruct((bsz, seq, d), F32), u_shape],
        compiler_params=_params(("parallel", "parallel")),
        name="merge",
    )(x, s5_out, ret_out, conv_out, *([z] * n_gate), s5_proj, ret_proj, conv_proj, w_out, norm_g.reshape(1, d))


def _ffn_kernel(h_ref, u_ref, wg_ref, wu_ref, wd_ref, *rest, riders):
    rider_in, o_ref, rider_out = rest[:riders.n_in], rest[riders.n_in], rest[riders.n_in + 1:]
    f = pl.program_id(1)
    ub = u_ref[...].astype(BF16)
    gate = jnp.dot(ub, wg_ref[...], preferred_element_type=F32)
    up = jnp.dot(ub, wu_ref[...], preferred_element_type=F32)
    part = _bdot(_silu(gate) * up, wd_ref[...])

    @pl.when(f == 0)
    def _():
        o_ref[...] = h_ref[...] + part

    @pl.when(f > 0)
    def _():
        o_ref[...] = o_ref[...] + part

    riders.run(rider_in, rider_out)


def _ffn(h, u, w_gate, w_up, w_down, jobs=(), *, tm, tf):
    rows, d = h.shape
    dff = w_gate.shape[1]
    n_i, n_f = rows // tm, dff // tf
    tok = pl.BlockSpec((tm, d), lambda i, f: (i, 0))
    riders = _Riders(jobs, (n_i, n_f), lambda i, f: i * n_f + f)
    outs = pl.pallas_call(
        functools.partial(_ffn_kernel, riders=riders),
        grid=(n_i, n_f),
        in_specs=[tok, tok,
                  pl.BlockSpec((d, tf), lambda i, f: (0, f)),
                  pl.BlockSpec((d, tf), lambda i, f: (0, f)),
                  pl.BlockSpec((tf, d), lambda i, f: (f, 0)),
                  *riders.in_specs],
        out_specs=[tok, *riders.out_specs],
        out_shape=[jax.ShapeDtypeStruct((rows, d), F32), *riders.out_shapes],
        compiler_params=_params(("arbitrary", "arbitrary")),
        name="ffn_dense",
    )(h, u, w_gate, w_up, w_down, *riders.inputs)
    return outs[0], list(outs[1:])


def _router_kernel(u_ref, rt_ref, tri_ref, idx_ref, wts_ref, rank_ref, cnt_ref, cnt_scr, *, tm):
    @pl.when(pl.program_id(0) == 0)
    def _():
        cnt_scr[...] = jnp.zeros_like(cnt_scr)

    logits = lax.dot_general(rt_ref[...].astype(BF16), _tiles_to_rows(u_ref, tm).astype(BF16),
                             (((1,), (1,)), ((), ())), preferred_element_type=F32)
    eidx = lax.broadcasted_iota(jnp.int32, logits.shape, 0)
    m1 = jnp.max(logits, axis=0, keepdims=True)
    i1 = jnp.min(jnp.where(logits == m1, eidx, N_EXPERTS), axis=0, keepdims=True)
    rest = jnp.where(eidx == i1, -jnp.inf, logits)
    m2 = jnp.max(rest, axis=0, keepdims=True)
    i2 = jnp.min(jnp.where(rest == m2, eidx, N_EXPERTS), axis=0, keepdims=True)
    e2 = jnp.exp(m2 - m1)
    w1 = 1.0 / (1.0 + e2)
    idx_ref[...] = jnp.concatenate([i1, i2], axis=0)
    wts_ref[...] = jnp.concatenate([w1, e2 * w1], axis=0)

    hit1 = eidx == i1
    hit2 = eidx == i2
    hits = jnp.where(hit1 | hit2, 1.0, 0.0)
    before = jnp.dot(hits.astype(BF16), tri_ref[...], preferred_element_type=F32) + cnt_scr[...]
    rank_ref[...] = jnp.concatenate(
        [jnp.sum(jnp.where(hit1, before, 0.0), axis=0, keepdims=True),
         jnp.sum(jnp.where(hit2, before, 0.0), axis=0, keepdims=True)], axis=0).astype(jnp.int32)
    cnt_scr[...] = cnt_scr[...] + jnp.sum(hits, axis=1, keepdims=True)
    cnt_ref[...] = cnt_scr[...]


def _router(u8, router, *, tm):
    rows = u8.shape[0] // ROW_TILE
    d = D_MODEL
    tri = (jnp.arange(tm)[:, None] < jnp.arange(tm)[None, :]).astype(BF16)
    const = lambda shape: pl.BlockSpec(shape, lambda i: (0,) * len(shape))
    lane = pl.BlockSpec((TOP_K, tm), lambda i: (0, i))
    return pl.pallas_call(
        functools.partial(_router_kernel, tm=tm),
        grid=(rows // tm,),
        in_specs=[pl.BlockSpec((tm * ROW_TILE, LANES), lambda i: (i, 0)), const((N_EXPERTS, d)), const((tm, tm))],
        out_specs=[lane, lane, lane, const((N_EXPERTS, 1))],
        out_shape=[jax.ShapeDtypeStruct((TOP_K, rows), jnp.int32), jax.ShapeDtypeStruct((TOP_K, rows), F32),
                   jax.ShapeDtypeStruct((TOP_K, rows), jnp.int32), jax.ShapeDtypeStruct((N_EXPERTS, 1), F32)],
        scratch_shapes=[pltpu.VMEM((N_EXPERTS, 1), F32)],
        compiler_params=_params(("arbitrary",)),
        name="moe_router",
    )(u8, router.T, tri)


DMA_UNROLL = 8


def _token_tile(ref, r):
    return ref.at[pl.ds(pl.multiple_of(r * ROW_TILE, ROW_TILE), ROW_TILE)]


def _dispatch_kernel(pos_ref, u_ref, init_ref, xs_ref, sem, *, tb):
    del init_ref

    def start(r, carry):
        for s in range(TOP_K):
            pltpu.make_async_copy(_token_tile(u_ref, r), _token_tile(xs_ref, pos_ref[s, r]),
                                  sem.at[s]).start(priority=s)
        return carry

    lax.fori_loop(0, tb, start, 0, unroll=DMA_UNROLL)
    for s in range(TOP_K):
        pltpu.make_async_copy(u_ref, xs_ref.at[pl.ds(0, tb * ROW_TILE)], sem.at[s]).wait()


def _dispatch(u8, pos, xs8, *, tb):
    rows = u8.shape[0] // ROW_TILE
    return pl.pallas_call(
        functools.partial(_dispatch_kernel, tb=tb),
        grid=(rows // tb,),
        in_specs=[
            pl.BlockSpec((TOP_K, tb), lambda i: (0, i), memory_space=pltpu.SMEM),
            pl.BlockSpec((tb * ROW_TILE, LANES), lambda i: (i, 0)),
            pl.BlockSpec(memory_space=pl.ANY),
        ],
        out_specs=pl.BlockSpec(memory_space=pl.ANY),
        out_shape=jax.ShapeDtypeStruct(xs8.shape, xs8.dtype),
        scratch_shapes=[pltpu.SemaphoreType.DMA((TOP_K,))],
        input_output_aliases={2: 0},
        compiler_params=_params(("arbitrary",)),
        name="moe_dispatch",
    )(pos, u8, xs8)


def _experts_kernel(te_ref, nu_ref, x_ref, wg_ref, wu_ref, wd_ref, o_ref, x_scr, acc_scr, *, tm):
    del te_ref
    i = pl.program_id(0)
    f = pl.program_id(1)

    def swiglu_rows(n):
        xb = x_scr[0:n, :]
        gate = jnp.dot(xb, wg_ref[...], preferred_element_type=F32)
        up = jnp.dot(xb, wu_ref[...], preferred_element_type=F32)
        part = _bdot(_silu(gate) * up, wd_ref[...])

        @pl.when(f == 0)
        def _():
            acc_scr[0:n, :] = part
            if n < tm:
                acc_scr[n:tm, :] = jnp.zeros((tm - n, D_MODEL), F32)

        @pl.when(f > 0)
        def _():
            acc_scr[0:n, :] = acc_scr[0:n, :] + part

    @pl.when(i < nu_ref[0])
    def _():
        @pl.when(f == 0)
        def _():
            x_scr[...] = _tiles_to_rows(x_ref, tm).astype(BF16)

        half = tm // 2
        valid = nu_ref[1 + i]

        @pl.when(valid > half)
        def _():
            swiglu_rows(tm)

        @pl.when(valid <= half)
        def _():
            swiglu_rows(half)

        @pl.when(f == pl.num_programs(1) - 1)
        def _():
            _rows_to_tiles(o_ref, acc_scr[...], tm)

    @pl.when(i >= nu_ref[0])
    def _():
        o_ref[...] = jnp.zeros_like(o_ref)


def _experts(xs8, tile_expert, n_used, w_gate, w_up, w_down, *, tm, tf):
    rows = xs8.shape[0] // ROW_TILE
    d = D_MODEL
    dff = w_gate.shape[-1]
    n_f = dff // tf
    last_f = n_f - 1

    def row_map(i, f, te, nu):
        return (jnp.minimum(i, nu[0] - 1), 0)

    def fsel(i, f, nu):
        return jnp.where(i < nu[0], f, last_f)

    grid_spec = pltpu.PrefetchScalarGridSpec(
        num_scalar_prefetch=2,
        grid=(rows // tm, n_f),
        in_specs=[
            pl.BlockSpec((tm * ROW_TILE, LANES), row_map),
            pl.BlockSpec((None, d, tf), lambda i, f, te, nu: (te[i], 0, fsel(i, f, nu))),
            pl.BlockSpec((None, d, tf), lambda i, f, te, nu: (te[i], 0, fsel(i, f, nu))),
            pl.BlockSpec((None, tf, d), lambda i, f, te, nu: (te[i], fsel(i, f, nu), 0)),
        ],
        out_specs=pl.BlockSpec((tm * ROW_TILE, LANES), lambda i, f, te, nu: (i, 0)),
        scratch_shapes=[pltpu.VMEM((tm, d), BF16), pltpu.VMEM((tm, d), F32)],
    )
    return pl.pallas_call(
        functools.partial(_experts_kernel, tm=tm),
        grid_spec=grid_spec,
        out_shape=jax.ShapeDtypeStruct(xs8.shape, F32),
        compiler_params=_params(("arbitrary", "arbitrary")),
        name="moe_experts",
    )(tile_expert, n_used, xs8, w_gate, w_up, w_down)


def _combine_kernel(pos_ref, next_pos_ref, h_ref, wts_ref, g_ref, ys_ref, o_ref, y_scr, sem, *, tb, final_norm):
    i = pl.program_id(0)
    n = pl.num_programs(0)

    def gather(p_ref, buf):
        def start(r, carry):
            for s in range(TOP_K):
                pltpu.make_async_copy(_token_tile(ys_ref, p_ref[s, r]), _token_tile(y_scr.at[buf, s], r),
                                      sem.at[buf, s]).start(priority=s)
            return carry

        lax.fori_loop(0, tb, start, 0, unroll=DMA_UNROLL)

    @pl.when(i == 0)
    def _():
        gather(pos_ref, 0)

    for buf in range(2):
        @pl.when(i % 2 == buf)
        def _(buf=buf):
            @pl.when(i + 1 < n)
            def _():
                gather(next_pos_ref, 1 - buf)

            for s in range(TOP_K):
                pltpu.make_async_copy(ys_ref.at[pl.ds(0, tb * ROW_TILE)], y_scr.at[buf, s], sem.at[buf, s]).wait()
            w1 = wts_ref[:, 0:1]
            w2 = wts_ref[:, 1:2]
            moe = jnp.concatenate(
                [w1 * _tiles_chunk(y_scr.at[buf, 0], c, tb) + w2 * _tiles_chunk(y_scr.at[buf, 1], c, tb)
                 for c in range(ROW_CHUNKS)], axis=-1)
            out = h_ref[...] + moe
            if final_norm:
                ms = jnp.mean(out * out, axis=-1, keepdims=True)
                out = out * lax.rsqrt(ms + EPS) * g_ref[...]
            o_ref[...] = out


def _combine(h, pos, wts_t, ys8, final_g, *, tb):
    rows, d = h.shape
    final_norm = final_g is not None
    gain = final_g.reshape(1, d) if final_norm else jnp.ones((1, d), F32)
    n_steps = rows // tb
    return pl.pallas_call(
        functools.partial(_combine_kernel, tb=tb, final_norm=final_norm),
        grid=(n_steps,),
        in_specs=[
            pl.BlockSpec((TOP_K, tb), lambda i: (0, i), memory_space=pltpu.SMEM),
            pl.BlockSpec((TOP_K, tb), lambda i: (0, jnp.minimum(i + 1, n_steps - 1)), memory_space=pltpu.SMEM),
            pl.BlockSpec((tb, d), lambda i: (i, 0)),
            pl.BlockSpec((tb, TOP_K), lambda i: (i, 0)),
            pl.BlockSpec((1, d), lambda i: (0, 0)),
            pl.BlockSpec(memory_space=pl.ANY),
        ],
        out_specs=pl.BlockSpec((tb, d), lambda i: (i, 0)),
        out_shape=jax.ShapeDtypeStruct((rows, d), F32),
        scratch_shapes=[pltpu.VMEM((2, TOP_K, tb * ROW_TILE, LANES), F32), pltpu.SemaphoreType.DMA((2, TOP_K))],
        compiler_params=_params(("arbitrary",)),
        name="moe_combine",
    )(pos, pos, h, wts_t, gain, ys8)


PROJ_TN = 2560
S5_TT = 128
MIX_TT = 512
FFN_TF = 1408
MOE_TM = 512
MOE_TF = 1792
MOE_TB = 1024


def _moe_tiles(n_rows):
    return n_rows // MOE_TM + N_EXPERTS


def _moe_buffer_shape(n_rows):
    return (_moe_tiles(n_rows) * MOE_TM * ROW_TILE, LANES)


def _moe(h_list, u8_list, router, w_gate, w_up, w_down, xs8, final_g):
    routes = []
    for u8 in u8_list:
        routes.append(_router(u8, router, tm=min(512, u8.shape[0] // ROW_TILE)))
    counts = [r[3][:, 0].astype(jnp.int32) for r in routes]
    total = sum(counts)
    padded = ((total + MOE_TM - 1) // MOE_TM) * MOE_TM
    ends = jnp.cumsum(padded)
    starts = ends - padded
    n_rows = sum(h.shape[0] for h in h_list) * TOP_K
    n_tiles = _moe_tiles(n_rows)
    assert xs8.shape == _moe_buffer_shape(n_rows)
    n_used = (ends[-1] // MOE_TM).astype(jnp.int32)
    tile_start = jnp.arange(n_tiles, dtype=jnp.int32) * MOE_TM
    tile_expert = jnp.sum((tile_start[:, None] >= ends[None, :]).astype(jnp.int32), axis=1)
    last_expert = jnp.sum((((n_used - 1) * MOE_TM) >= ends).astype(jnp.int32))
    tile_expert = jnp.where(jnp.arange(n_tiles) < n_used, tile_expert, last_expert).astype(jnp.int32)

    poss = []
    seen = jnp.zeros((N_EXPERTS,), jnp.int32)
    for u8, (idx, _, rank, _), cnt in zip(u8_list, routes, counts):
        base = starts + seen
        pos = rank
        for e in range(N_EXPERTS):
            pos = pos + jnp.where(idx == e, base[e], 0)
        poss.append(pos)
        seen = seen + cnt
        xs8 = _dispatch(u8, pos, xs8, tb=min(2 * MOE_TB, u8.shape[0] // ROW_TILE))
    group_rows = jnp.sum(jnp.where(jnp.arange(N_EXPERTS)[None, :] == tile_expert[:, None],
                                   (starts + total)[None, :], 0), axis=1)
    tile_valid = jnp.clip(group_rows - tile_start, 0, MOE_TM)
    tile_info = jnp.concatenate([n_used.reshape(1), tile_valid]).astype(jnp.int32)
    ys8 = _experts(xs8, tile_expert, tile_info, w_gate, w_up, w_down, tm=MOE_TM, tf=MOE_TF)
    outs = []
    for h, pos, (_, wts, _, _) in zip(h_list, poss, routes):
        outs.append(_combine(h, pos, wts.T, ys8, final_g, tb=min(MOE_TB, h.shape[0])))
    return outs


def _rmsnorm_kernel(x_ref, g_ref, o_ref):
    x = x_ref[...]
    ms = jnp.mean(x * x, axis=-1, keepdims=True)
    o_ref[...] = x * lax.rsqrt(ms + EPS) * g_ref[...]


def _rmsnorm(x, g, *, tm):
    rows, d = x.shape
    return pl.pallas_call(
        _rmsnorm_kernel,
        grid=(rows // tm,),
        in_specs=[pl.BlockSpec((tm, d), lambda i: (i, 0)), pl.BlockSpec((1, d), lambda i: (0, 0))],
        out_specs=pl.BlockSpec((tm, d), lambda i: (i, 0)),
        out_shape=jax.ShapeDtypeStruct((rows, d), F32),
        compiler_params=_params(("parallel",)),
        name="final_norm",
    )(x, g.reshape(1, d))


def _pack_s5_state(re, im):
    n = re.shape[0]
    return jnp.concatenate([re.reshape(n, S5_LANES), im.reshape(n, S5_LANES)], axis=-1)


def _unpack_s5_state(h):
    n = h.shape[0]
    return (h[:, :S5_LANES].reshape(n, S5_GROUPS, S5_STATE), h[:, S5_LANES:].reshape(n, S5_GROUPS, S5_STATE))


def _mixer(x, z, s5_h0, ret_s0, conv_buf, pos_offset, p, *, single_step, tiled_u, layer=0, ret_stack=None, jobs=None):
    jobs = jobs or {}
    done = {}
    bsz, seq, d = x.shape
    s5_out, s5_state, done["s5"] = _s5_branch(z, s5_h0, p["a_re"], p["a_im"], p["bmat"], p["cmat"], p["d_skip"],
                                              p["glu_w"], p["glu_b"], jobs.get("s5", ()), single_step=single_step,
                                              tt=1 if single_step else S5_TT)

    if single_step:
        z2 = z.reshape(seq, N_IN)
        ret_out, ret_state = _retention_step(z2, ret_s0, layer, ret_stack, pos_offset, bb=16)
        conv_out, conv_state = _conv_step(z2, conv_buf, layer, p["pw_b"], p["dw_w"], p["dw_b"], p["ln_g"], p["ln_b"],
                                          bb=32)
        ret_out = ret_out.reshape(bsz, seq, RET_V)
        conv_out = conv_out.reshape(bsz, seq, CONV_WIDTH)
    else:
        ret_out, ret_state, done["ret"] = _retention_prompt(z, ret_s0, pos_offset, jobs.get("ret", ()), tt=MIX_TT)
        conv_out, conv_state, done["conv"] = _conv_prompt(z, conv_buf, p["pw_b"], p["dw_w"], p["dw_b"], p["ln_g"],
                                                          p["ln_b"], jobs.get("conv", ()), tt=MIX_TT)

    h, u = _merge(x, s5_out, ret_out, conv_out, z, p["s5_proj"], p["ret_proj"], p["conv_proj"], p["w_out"],
                  p["norm_ffn_g"], tm=min(512, seq), tiled_u=tiled_u)
    return h, u, (s5_state, ret_state, conv_state), done


def kernel(x_prompt, x_sample, state_s5_re, state_s5_im, state_ret, state_conv, norm_mix_g, w_in, s5_lambda_re, s5_lambda_im, s5_log_dt, s5_b_re, s5_b_im, s5_c_re, s5_c_im, s5_d, s5_glu_w, s5_glu_b, s5_proj, ret_proj, conv_pw_b, conv_dw_w, conv_dw_b, conv_ln_g, conv_ln_b, conv_proj, w_out, norm_ffn_g, ffn_w_gate, ffn_w_up, ffn_w_down, moe_router, moe_w_gate, moe_w_up, moe_w_down, norm_final_g):
    depth = w_in.shape[0]
    bp, seq, d = x_prompt.shape
    ns = x_sample.shape[0]
    past_len = 16384
    bf = lambda a: a.astype(BF16)

    hp = x_prompt
    hs = x_sample.reshape(1, ns, d)
    zero_s5 = jnp.zeros((bp, 2 * S5_LANES), F32)
    zero_ret = jnp.zeros((bp, RET_HEADS, RET_DK, RET_DV), F32)
    zero_conv = jnp.zeros((bp, CONV_K - 1, CONV_WIDTH), F32)

    conv_rows = jnp.transpose(state_conv, (0, 2, 1, 3))
    p_states, s_states = [], []
    ret_stack = None
    normed = False
    groups = depth * S5_GROUPS
    disc = _s5_discretize(s5_lambda_re.reshape(groups, S5_STATE), s5_lambda_im.reshape(groups, S5_STATE),
                          s5_log_dt.reshape(groups), s5_b_re.reshape(groups, S5_STATE, S5_GROUP),
                          s5_b_im.reshape(groups, S5_STATE, S5_GROUP))
    disc = [a.reshape((depth, S5_GROUPS) + a.shape[1:]) for a in disc]
    for l in range(depth):
        is_moe = l % 2 == 1
        a_re, a_im, bb_re, bb_im = (a[l] for a in disc)
        bmat, cmat = _s5_block_mats(bb_re, bb_im, s5_c_re[l], s5_c_im[l])
        p = dict(a_re=a_re, a_im=a_im, bmat=bmat, cmat=cmat, d_skip=s5_d[l], glu_w=bf(s5_glu_w[l]),
                 glu_b=s5_glu_b[l], pw_b=conv_pw_b[l], dw_w=conv_dw_w[l], dw_b=conv_dw_b[l], ln_g=conv_ln_g[l],
                 ln_b=conv_ln_b[l], s5_proj=bf(s5_proj[l]), ret_proj=bf(ret_proj[l]), conv_proj=bf(conv_proj[l]),
                 w_out=bf(w_out[l]), norm_ffn_g=norm_ffn_g[l])
        j = l // 2
        flat = lambda w: w.reshape(-1, w.shape[-1])
        if is_moe:
            jobs = {"conv": [_cast_job(flat(moe_w_up[j]))],
                    "proj": [(None, _moe_buffer_shape((bp * seq + ns) * TOP_K), F32)]}
        else:
            jobs = {"s5": [_cast_job(ffn_w_gate[j]), _cast_job(ffn_w_up[j]), _cast_job(ffn_w_down[j])]}
            if l + 1 < depth:
                jm = (l + 1) // 2
                jobs["conv"] = [_cast_job(flat(moe_w_gate[jm])), _cast_job(flat(moe_w_down[jm]))]
        zp, zs, proj_done = _norm_proj(hp, hs.reshape(ns, d), norm_mix_g[l], w_in, l, jobs.pop("proj", ()),
                                       tt=512, tn=PROJ_TN)
        hp, up, st_p, done = _mixer(hp, zp, zero_s5, zero_ret, zero_conv, 0.0, p, single_step=False, tiled_u=is_moe,
                                    layer=l, jobs=jobs)
        hs, us, st_s, _ = _mixer(hs, zs.reshape(1, ns, N_IN), _pack_s5_state(state_s5_re[l], state_s5_im[l]),
                                 state_ret, conv_rows, float(past_len), p, single_step=True, tiled_u=is_moe, layer=l,
                                 ret_stack=ret_stack)
        p_states.append(st_p)
        s_states.append(st_s)

        hp2, hs2 = hp.reshape(bp * seq, d), hs.reshape(ns, d)
        if is_moe:
            ret_stack = st_s[1]
            final_g = norm_final_g if l == depth - 1 else None
            normed = final_g is not None
            w_gate_b, w_down_b = (c.reshape(w.shape[1:]) for c, w in zip(early, (moe_w_gate, moe_w_down)))
            w_up_b = done["conv"][0].reshape(moe_w_up.shape[1:])
            hp2, hs2 = _moe([hp2, hs2], [up, us], moe_router[j], w_gate_b, w_up_b, w_down_b, proj_done[0], final_g)
        else:
            wg, wu, wd = done["s5"]
            if l + 1 < depth:
                early = done["conv"]
            ffn_jobs = []
            if l == 0 and depth > 1:
                rows = st_s[1].size // RET_DV
                ffn_jobs = [(st_s[1].reshape(rows, RET_DV), (depth * rows, RET_DV), F32)]
            hp2, made = _ffn(hp2, up.reshape(bp * seq, d), wg, wu, wd, ffn_jobs, tm=512, tf=FFN_TF)
            if l == 0:
                ret_stack = made[0].reshape(state_ret.shape) if ffn_jobs else st_s[1][None]
            else:
                ret_stack = st_s[1]
            hs2, _ = _ffn(hs2, us.reshape(ns, d), wg, wu, wd, tm=ns, tf=FFN_TF)
        hp = hp2.reshape(bp, seq, d)
        hs = hs2.reshape(1, ns, d)

    if normed:
        y_prompt, y_sample = hp, hs.reshape(ns, 1, d)
    else:
        y_prompt = _rmsnorm(hp.reshape(bp * seq, d), norm_final_g, tm=512).reshape(bp, seq, d)
        y_sample = _rmsnorm(hs.reshape(ns, d), norm_final_g, tm=ns).reshape(ns, 1, d)

    def stack_s5(states):
        s5 = [_unpack_s5_state(s[0]) for s in states]
        return jnp.stack([a for a, _ in s5]), jnp.stack([b for _, b in s5])

    p_re, p_im = stack_s5(p_states)
    s_re, s_im = stack_s5(s_states)
    p_ret = jnp.stack([s[1] for s in p_states])
    p_conv = jnp.stack([s[2] for s in p_states])
    s_conv = jnp.transpose(jnp.stack([s[2] for s in s_states]), (0, 2, 1, 3))
    return (y_prompt, y_sample, p_re, p_im, p_ret, p_conv, s_re, s_im, ret_stack, s_conv)
```

```python
import functools
import math

import jax
import jax.numpy as jnp
from jax import lax
from jax.experimental import pallas as pl
from jax.experimental.pallas import tpu as pltpu

F32 = jnp.float32
BF16 = jnp.bfloat16

D_MODEL = 1024
S5_WIDTH = 512
S5_GROUP = 16
S5_GROUPS = 32
S5_STATE = 64
S5_LANES = S5_GROUPS * S5_STATE
RET_HEADS = 4
RET_DK = 128
RET_DV = 256
RET_QK = RET_HEADS * RET_DK
RET_V = RET_HEADS * RET_DV
RET_CHUNK = 128
ROPE_BASE = 10000.0
CONV_WIDTH = 512
CONV_K = 31
CONV_HIST = 32
N_EXPERTS = 8
TOP_K = 2
N_BRANCH = 3
EPS = 1e-6
N_IN = S5_WIDTH + 2 * RET_QK + 2 * RET_V + 2 * CONV_WIDTH + N_BRANCH * D_MODEL
ZB = 512
ZC_S5 = 0
ZC_Q = ZC_S5 + S5_WIDTH // ZB
ZC_K = ZC_Q + RET_QK // ZB
ZC_V = ZC_K + RET_QK // ZB
ZC_G = ZC_V + RET_V // ZB
ZC_CONV = ZC_G + RET_V // ZB
ZC_GATE = ZC_CONV + 2 * CONV_WIDTH // ZB
HEADS_PER_ZB = ZB // RET_DV

ROW_TILE = 8
LANES = 128
ROW_CHUNKS = D_MODEL // LANES
VMEM_LIMIT = 48 * 1024 * 1024
ROW_SPLIT = 2


def _params(sem):
    return pltpu.CompilerParams(dimension_semantics=sem, vmem_limit_bytes=VMEM_LIMIT)


def _silu(x):
    return x * jax.nn.sigmoid(x)


def _bdot(a, b):
    return jnp.dot(a.astype(BF16), b.astype(BF16), preferred_element_type=F32)


def _norm_proj_kernel(x_ref, g_ref, w_ref, x2_ref, *rest, riders):
    rider_in, (o_ref, o2_ref) = rest[:riders.n_in], rest[riders.n_in:riders.n_in + 2]
    rider_out, w_scr = rest[riders.n_in + 2:-1], rest[-1]
    riders.run(rider_in, rider_out)

    def project(x):
        ms = jnp.mean(x * x, axis=-1, keepdims=True)
        u = (x * lax.rsqrt(ms + EPS) * g_ref[...]).astype(BF16)
        return jnp.dot(u, w_scr[...], preferred_element_type=F32).astype(BF16)

    @pl.when((pl.program_id(1) == 0) & (pl.program_id(2) == 0))
    def _():
        w_scr[...] = w_ref[...].astype(BF16)
        o2_ref[...] = project(x2_ref[...])

    half = x_ref.shape[0] // ROW_SPLIT
    for r in range(ROW_SPLIT):
        rows = slice(r * half, (r + 1) * half)
        o_ref[rows, :] = project(x_ref[rows, :])


def _norm_proj(x, x2, g, w_all, layer, jobs=(), *, tt, tn):
    bsz, seq, d = x.shape
    r2 = x2.shape[0]
    n = w_all.shape[2]
    grid = (n // tn, bsz, seq // tt)
    riders = _Riders(jobs, grid, lambda j, b, i: (j * grid[1] + b) * grid[2] + i)
    outs = pl.pallas_call(
        functools.partial(_norm_proj_kernel, riders=riders),
        grid=grid,
        in_specs=[
            pl.BlockSpec((None, tt, d), lambda j, b, i: (b, i, 0)),
            pl.BlockSpec((1, d), lambda j, b, i: (0, 0)),
            pl.BlockSpec((None, d, tn), lambda j, b, i: (layer, 0, j)),
            pl.BlockSpec((r2, d), lambda j, b, i: (0, 0)),
            *riders.in_specs,
        ],
        out_specs=[pl.BlockSpec((None, tt, tn), lambda j, b, i: (b, i, j)),
                   pl.BlockSpec((r2, tn), lambda j, b, i: (0, j)),
                   *riders.out_specs],
        out_shape=[jax.ShapeDtypeStruct((bsz, seq, n), BF16), jax.ShapeDtypeStruct((r2, n), BF16),
                   *riders.out_shapes],
        scratch_shapes=[pltpu.VMEM((d, tn), BF16)],
        compiler_params=_params(("arbitrary", "arbitrary", "arbitrary")),
        name="norm_proj",
    )(x, g.reshape(1, d), w_all, x2, *riders.inputs)
    return outs[0], outs[1], list(outs[2:])


def _s5_disc_kernel(lre_ref, lim_ref, ldt_ref, bre_ref, bim_ref, are_ref, aim_ref, ore_ref, oim_ref):
    lam_re = lre_ref[...]
    lam_im = lim_ref[...]
    dt = jnp.exp(ldt_ref[...])
    mag = jnp.exp(lam_re * dt)
    ang = lam_im * dt
    lbar_re = mag * jnp.cos(ang)
    lbar_im = mag * jnp.sin(ang)
    den = lam_re * lam_re + lam_im * lam_im
    nr = lbar_re - 1.0
    f_re = (nr * lam_re + lbar_im * lam_im) / den
    f_im = (lbar_im * lam_re - nr * lam_im) / den
    b_re = bre_ref[...]
    b_im = bim_ref[...]
    are_ref[...] = lbar_re
    aim_ref[...] = lbar_im
    ore_ref[...] = f_re * b_re - f_im * b_im
    oim_ref[...] = f_re * b_im + f_im * b_re


def _s5_discretize(lam_re, lam_im, log_dt, b_re, b_im):
    g, n = lam_re.shape
    p = b_re.shape[-1]
    rows = g * n
    col = lambda a: a.reshape(rows, 1)
    ldt = jnp.broadcast_to(log_dt[:, None], (g, n))
    outs = pl.pallas_call(
        _s5_disc_kernel,
        out_shape=[jax.ShapeDtypeStruct((rows, 1), F32), jax.ShapeDtypeStruct((rows, 1), F32),
                   jax.ShapeDtypeStruct((rows, p), F32), jax.ShapeDtypeStruct((rows, p), F32)],
        name="s5_discretize",
    )(col(lam_re), col(lam_im), col(ldt), b_re.reshape(rows, p), b_im.reshape(rows, p))
    a_re, a_im, bb_re, bb_im = outs
    return a_re.reshape(g, n), a_im.reshape(g, n), bb_re.reshape(g, n, p), bb_im.reshape(g, n, p)


S5_KCH = 128
S5_NCHUNK = S5_WIDTH // S5_KCH
S5_GPC = S5_KCH // S5_GROUP
S5_SPC = S5_GPC * S5_STATE


def _s5_block_mats(bbar_re, bbar_im, c_re, c_im):
    eye = jnp.eye(S5_GPC, dtype=F32)

    def in_blocks(bb):
        t = bb.reshape(S5_NCHUNK, S5_GPC, S5_STATE, S5_GROUP)
        m = jnp.einsum("cgnp,gh->cgphn", t, eye)
        return m.reshape(S5_NCHUNK, S5_KCH, S5_SPC)

    def out_blocks(cc):
        t = cc.reshape(S5_NCHUNK, S5_GPC, S5_GROUP, S5_STATE)
        m = jnp.einsum("cgpn,gh->cgnhp", t, eye)
        return m.reshape(S5_NCHUNK, S5_SPC, S5_KCH)

    bmat = jnp.concatenate([in_blocks(bbar_re), in_blocks(bbar_im)], axis=-1).astype(BF16)
    cmat = jnp.stack([out_blocks(c_re), -out_blocks(c_im)], axis=1).astype(BF16)
    return bmat, cmat


def _gelu_tanh(x):
    return 0.5 * x * (1.0 + jnp.tanh(math.sqrt(2.0 / math.pi) * (x + 0.044715 * (x * x * x))))


class _Riders:
    def __init__(self, jobs, grid, step_index):
        self.steps = math.prod(grid)
        self.n_axes = len(grid)
        self.step_index = step_index
        self.inputs, self.in_specs, self.out_specs, self.out_shapes, self.src_steps = [], [], [], [], []
        for src, shape, dtype in jobs:
            slab = shape[0] // self.steps
            assert slab * self.steps == shape[0] and slab % 16 == 0
            n_src = 0 if src is None else src.shape[0] // slab
            if src is not None:
                assert n_src * slab == src.shape[0] and src.shape[1] == shape[1]
                self.inputs.append(src)
                self.in_specs.append(pl.BlockSpec(
                    (slab, shape[1]), lambda *idx, n=n_src: (jnp.minimum(step_index(*idx[:self.n_axes]), n - 1), 0)))
            self.out_specs.append(pl.BlockSpec((slab, shape[1]), lambda *idx: (step_index(*idx[:self.n_axes]), 0)))
            self.out_shapes.append(jax.ShapeDtypeStruct(shape, dtype))
            self.src_steps.append(n_src)

    @property
    def n_in(self):
        return len(self.inputs)

    @property
    def n_out(self):
        return len(self.out_shapes)

    def run(self, in_refs, out_refs):
        step = self.step_index(*(pl.program_id(a) for a in range(self.n_axes)))
        srcs = iter(in_refs)
        for dst, n_src in zip(out_refs, self.src_steps):
            if n_src == 0:
                dst[...] = jnp.zeros_like(dst)
                continue
            src = next(srcs)
            if n_src == self.steps:
                dst[...] = src[...].astype(dst.dtype)
                continue

            @pl.when(step < n_src)
            def _(src=src, dst=dst):
                dst[...] = src[...].astype(dst.dtype)

            @pl.when(step >= n_src)
            def _(dst=dst):
                dst[...] = jnp.zeros_like(dst)


def _cast_job(w):
    return (w, w.shape, BF16)


def _s5_kernel(u_ref, h0_ref, are_ref, aim_ref, bmat_ref, cmat_ref, d_ref, gw_ref, gb_ref, *rest,
               nb, tt, lane_chunk, riders):
    rider_in, rest = rest[:riders.n_in], rest[riders.n_in:]
    o_ref, hout_ref = rest[:2]
    rider_out = rest[2:2 + riders.n_out]
    hs_scr, h_scr, io_scr = rest[2 + riders.n_out:]
    riders.run(rider_in, rider_out)

    @pl.when(pl.program_id(0) == 0)
    def _():
        h_scr[...] = h0_ref[...]

    def seq_rows(b):
        return pl.ds(b, tt, stride=nb)

    def lanes(c):
        return slice(c * S5_KCH, (c + 1) * S5_KCH)

    if tt == 1:
        for c in range(S5_NCHUNK):
            io_scr[c] = u_ref[:, lanes(c)].astype(F32)
    else:
        for b in range(nb):
            for c in range(S5_NCHUNK):
                io_scr[c, seq_rows(b), :] = u_ref[b, :, lanes(c)].astype(F32)

    u = jnp.concatenate([io_scr[c] for c in range(S5_NCHUNK)], axis=-1)
    for c in range(S5_NCHUNK):
        bu = jnp.dot(io_scr[c].astype(BF16), bmat_ref[c], preferred_element_type=F32)
        hs_scr[:, c * S5_SPC:(c + 1) * S5_SPC] = bu[:, :S5_SPC]
        hs_scr[:, S5_LANES + c * S5_SPC:S5_LANES + (c + 1) * S5_SPC] = bu[:, S5_SPC:]

    for lc in range(S5_LANES // lane_chunk):
        re_sl = slice(lc * lane_chunk, (lc + 1) * lane_chunk)
        im_sl = slice(S5_LANES + lc * lane_chunk, S5_LANES + (lc + 1) * lane_chunk)
        a_re = jnp.broadcast_to(are_ref[:, re_sl], (nb, lane_chunk))
        a_im = jnp.broadcast_to(aim_ref[:, re_sl], (nb, lane_chunk))

        def step(t, carry):
            h_re, h_im = carry
            r0 = t * nb if isinstance(t, int) else pl.multiple_of(t * nb, nb)
            n_re = a_re * h_re - a_im * h_im + hs_scr[pl.ds(r0, nb), re_sl]
            n_im = a_re * h_im + a_im * h_re + hs_scr[pl.ds(r0, nb), im_sl]
            hs_scr[pl.ds(r0, nb), re_sl] = n_re
            hs_scr[pl.ds(r0, nb), im_sl] = n_im
            return n_re, n_im

        carry = (h_scr[:, re_sl], h_scr[:, im_sl])
        if tt == 1:
            carry = step(0, carry)
        else:
            carry = lax.fori_loop(0, tt, step, carry, unroll=4)
        h_scr[:, re_sl] = carry[0]
        h_scr[:, im_sl] = carry[1]

    hout_ref[...] = h_scr[...]

    ys = []
    for c in range(S5_NCHUNK):
        h_re = hs_scr[:, c * S5_SPC:(c + 1) * S5_SPC].astype(BF16)
        h_im = hs_scr[:, S5_LANES + c * S5_SPC:S5_LANES + (c + 1) * S5_SPC].astype(BF16)
        ys.append(jnp.dot(h_re, cmat_ref[c, 0], preferred_element_type=F32)
                  + jnp.dot(h_im, cmat_ref[c, 1], preferred_element_type=F32))
    y = jnp.concatenate(ys, axis=-1) + d_ref[...] * u
    z = _gelu_tanh(y)
    gate = jnp.dot(z.astype(BF16), gw_ref[...], preferred_element_type=F32) + gb_ref[...]
    out = z * jax.nn.sigmoid(gate)
    if tt == 1:
        o_ref[...] = out.astype(o_ref.dtype)
    else:
        for c in range(S5_NCHUNK):
            io_scr[c] = out[:, lanes(c)]
        for b in range(nb):
            for c in range(S5_NCHUNK):
                o_ref[b, :, lanes(c)] = io_scr[c, seq_rows(b), :].astype(o_ref.dtype)


def _s5_branch(z, h0, a_re, a_im, bmat, cmat, d_skip, glu_w, glu_b, jobs=(), *, single_step, tt):
    bsz, seq, _ = z.shape
    if single_step:
        assert bsz == 1 and tt == 1
        nb = seq
        in_spec = pl.BlockSpec((None, nb, S5_WIDTH), lambda i: (0, 0, ZC_S5))
        out_spec = pl.BlockSpec((None, nb, S5_WIDTH), lambda i: (0, 0, 0))
        grid = (1,)
    else:
        nb = bsz
        in_spec = pl.BlockSpec((nb, tt, S5_WIDTH), lambda i: (0, i, ZC_S5))
        out_spec = pl.BlockSpec((nb, tt, S5_WIDTH), lambda i: (0, i, 0))
        grid = (seq // tt,)
    rblk = tt * nb
    lane_chunk = 1024 if nb <= 8 else 512
    const = lambda shape: pl.BlockSpec(shape, lambda i: (0,) * len(shape))
    riders = _Riders(jobs, grid, lambda i: i)
    outs = pl.pallas_call(
        functools.partial(_s5_kernel, nb=nb, tt=tt, lane_chunk=lane_chunk, riders=riders),
        grid=grid,
        in_specs=[
            in_spec,
            const((nb, 2 * S5_LANES)),
            const((1, S5_LANES)),
            const((1, S5_LANES)),
            const(bmat.shape),
            const(cmat.shape),
            const((1, S5_WIDTH)),
            const((S5_WIDTH, S5_WIDTH)),
            const((1, S5_WIDTH)),
            *riders.in_specs,
        ],
        out_specs=[out_spec, const((nb, 2 * S5_LANES)), *riders.out_specs],
        out_shape=[jax.ShapeDtypeStruct((bsz, seq, S5_WIDTH), BF16),
                   jax.ShapeDtypeStruct((nb, 2 * S5_LANES), F32), *riders.out_shapes],
        scratch_shapes=[pltpu.VMEM((rblk, 2 * S5_LANES), F32), pltpu.VMEM((nb, 2 * S5_LANES), F32),
                        pltpu.VMEM((S5_NCHUNK, rblk, S5_KCH), F32)],
        compiler_params=_params(("arbitrary",)),
        name="s5_branch",
    )(z, h0, a_re.reshape(1, S5_LANES), a_im.reshape(1, S5_LANES), bmat, cmat,
      d_skip.reshape(1, S5_WIDTH), glu_w, glu_b.reshape(1, S5_WIDTH), *riders.inputs)
    return outs[0], outs[1], list(outs[2:])


def _rope_tables(pos):
    half = RET_DK // 2
    freqs = ROPE_BASE ** (-jnp.arange(half, dtype=F32) / half)
    ang = pos[:, None] * freqs[None, :]
    cos = jnp.cos(ang)
    sin = jnp.sin(ang)
    return jnp.concatenate([cos, cos], axis=-1), jnp.concatenate([-sin, sin], axis=-1)


def _rope(x, cos, sin):
    return x * cos + pltpu.roll(x, RET_DK // 2, 1) * sin


def _group_norm(o):
    mu = jnp.mean(o, axis=-1, keepdims=True)
    d = o - mu
    var = jnp.mean(d * d, axis=-1, keepdims=True)
    return d * lax.rsqrt(var + EPS)


def _retention_tables(chunk):
    log_gamma = jnp.log(1.0 - 2.0 ** (-5.0 - jnp.arange(RET_HEADS, dtype=F32)))
    idx = jnp.arange(chunk, dtype=F32)
    diff = idx[:, None] - idx[None, :]
    decay = jnp.where(diff >= 0, jnp.exp(jnp.maximum(diff, 0.0)[None] * log_gamma[:, None, None]), 0.0)
    cross = jnp.exp((idx + 1.0)[None, :] * log_gamma[:, None])[:, :, None]
    kdec = jnp.exp((chunk - 1.0 - idx)[None, :] * log_gamma[:, None])[:, :, None]
    full = jnp.exp(chunk * log_gamma)
    return decay, cross, kdec, full


def _head_cols(refs, h, rows):
    lo = (h % HEADS_PER_ZB) * RET_DV
    return refs[h // HEADS_PER_ZB][rows, lo:lo + RET_DV].astype(F32)


def _retention_kernel(q_ref, k_ref, v0_ref, v1_ref, g0_ref, g1_ref, cos_ref, sin_ref, s0_ref, decay_ref, cross_ref,
                      kdec_ref, full_ref, *rest, n_chunks, riders):
    rider_in, rest = rest[:riders.n_in], rest[riders.n_in:]
    o_ref, sout_ref = rest[:2]
    rider_out, s_scr = rest[2:-1], rest[-1]
    riders.run(rider_in, rider_out)

    @pl.when(pl.program_id(1) == 0)
    def _():
        s_scr[...] = s0_ref[...]

    for c in range(n_chunks):
        rows = slice(c * RET_CHUNK, (c + 1) * RET_CHUNK)
        cos = cos_ref[rows, :]
        sin = sin_ref[rows, :]
        for h in range(RET_HEADS):
            qk_cols = slice(h * RET_DK, (h + 1) * RET_DK)
            v_cols = slice(h * RET_DV, (h + 1) * RET_DV)
            qh = _rope(q_ref[rows, qk_cols].astype(F32), cos, sin)
            kh = _rope(k_ref[rows, qk_cols].astype(F32), cos, sin) * (RET_DK ** -0.5)
            vb = _head_cols((v0_ref, v1_ref), h, rows).astype(BF16)
            qb = qh.astype(BF16)
            state = s_scr[h]
            inner = lax.dot_general(qb, kh.astype(BF16), (((1,), (1,)), ((), ())),
                                    preferred_element_type=F32) * decay_ref[h]
            out = (jnp.dot(inner.astype(BF16), vb, preferred_element_type=F32)
                   + jnp.dot(qb, state.astype(BF16), preferred_element_type=F32) * cross_ref[h])
            kd = (kh * kdec_ref[h]).astype(BF16)
            s_scr[h] = full_ref[h] * state + jnp.dot(kd.T, vb, preferred_element_type=F32)
            gate = _head_cols((g0_ref, g1_ref), h, rows)
            o_ref[rows, v_cols] = (_silu(gate) * _group_norm(out)).astype(o_ref.dtype)

    sout_ref[...] = s_scr[...]


def _retention_prompt(z, state0, pos_offset, jobs=(), *, tt):
    bsz, seq, _ = z.shape
    assert seq % RET_CHUNK == 0 and tt % RET_CHUNK == 0
    cos, sin = _rope_tables(jnp.arange(seq, dtype=F32) + pos_offset)
    decay, cross, kdec, full = _retention_tables(RET_CHUNK)
    full = jnp.broadcast_to(full[:, None, None], (RET_HEADS, 1, RET_DV))
    const = lambda shape: pl.BlockSpec(shape, lambda b, i: (0,) * len(shape))
    zblk = lambda col: pl.BlockSpec((None, tt, ZB), lambda b, i: (b, i, col))
    n_i = seq // tt
    riders = _Riders(jobs, (bsz, n_i), lambda b, i: b * n_i + i)
    outs = pl.pallas_call(
        functools.partial(_retention_kernel, n_chunks=tt // RET_CHUNK, riders=riders),
        grid=(bsz, n_i),
        in_specs=[
            zblk(ZC_Q), zblk(ZC_K), zblk(ZC_V), zblk(ZC_V + 1), zblk(ZC_G), zblk(ZC_G + 1),
            pl.BlockSpec((tt, RET_DK), lambda b, i: (i, 0)),
            pl.BlockSpec((tt, RET_DK), lambda b, i: (i, 0)),
            pl.BlockSpec((None, RET_HEADS, RET_DK, RET_DV), lambda b, i: (b, 0, 0, 0)),
            const((RET_HEADS, RET_CHUNK, RET_CHUNK)),
            const((RET_HEADS, RET_CHUNK, 1)),
            const((RET_HEADS, RET_CHUNK, 1)),
            const((RET_HEADS, 1, RET_DV)),
            *riders.in_specs,
        ],
        out_specs=[
            pl.BlockSpec((None, tt, RET_V), lambda b, i: (b, i, 0)),
            pl.BlockSpec((None, RET_HEADS, RET_DK, RET_DV), lambda b, i: (b, 0, 0, 0)),
            *riders.out_specs,
        ],
        out_shape=[jax.ShapeDtypeStruct((bsz, seq, RET_V), BF16),
                   jax.ShapeDtypeStruct((bsz, RET_HEADS, RET_DK, RET_DV), F32), *riders.out_shapes],
        scratch_shapes=[pltpu.VMEM((RET_HEADS, RET_DK, RET_DV), F32)],
        compiler_params=_params(("arbitrary", "arbitrary")),
        name="retention_prompt",
    )(z, z, z, z, z, z, cos, sin, state0, decay, cross, kdec, full, *riders.inputs)
    return outs[0], outs[1], list(outs[2:])


def _retention_step_kernel(q_ref, k_ref, v0_ref, v1_ref, g0_ref, g1_ref, cos_ref, sin_ref, s_ref, gam_ref, *rest,
                           bb, layer):
    del layer
    o_ref, sout_ref, o_scr = rest[-3:]
    cos = cos_ref[...]
    sin = sin_ref[...]
    for h in range(RET_HEADS):
        qk_cols = slice(h * RET_DK, (h + 1) * RET_DK)
        v_cols = slice(h * RET_DV, (h + 1) * RET_DV)
        qh = _rope(q_ref[:, qk_cols].astype(F32), cos, sin)
        kh = _rope(k_ref[:, qk_cols].astype(F32), cos, sin) * (RET_DK ** -0.5)
        qk = jnp.sum(qh * kh, axis=-1, keepdims=True)
        q_t = qh.T
        k_t = kh.T
        gamma = gam_ref[h]
        v_all = _head_cols((v0_ref, v1_ref), h, slice(None))
        for b in range(bb):
            state = s_ref[b, h]
            vrow = v_all[b:b + 1, :]
            qs = jnp.sum(q_t[:, b:b + 1] * state, axis=0, keepdims=True)
            o_scr[b:b + 1, v_cols] = qk[b:b + 1, :] * vrow + qs * gamma
            sout_ref[b, h] = gamma * state + k_t[:, b:b + 1] * vrow
    for h in range(RET_HEADS):
        v_cols = slice(h * RET_DV, (h + 1) * RET_DV)
        gate = _head_cols((g0_ref, g1_ref), h, slice(None))
        o_ref[:, v_cols] = (_silu(gate) * _group_norm(o_scr[:, v_cols])).astype(o_ref.dtype)


def _retention_step(z, states, layer, stack, pos, *, bb):
    n = z.shape[0]
    cos, sin = _rope_tables(jnp.full((1,), pos, F32))
    log_gamma = jnp.log(1.0 - 2.0 ** (-5.0 - jnp.arange(RET_HEADS, dtype=F32)))
    gam = jnp.broadcast_to(jnp.exp(log_gamma)[:, None, None], (RET_HEADS, 1, RET_DV))
    const = lambda shape: pl.BlockSpec(shape, lambda i: (0,) * len(shape))
    slab = pl.BlockSpec((None, bb, RET_HEADS, RET_DK, RET_DV), lambda i: (layer, i, 0, 0, 0))
    zblk = lambda col: pl.BlockSpec((bb, ZB), lambda i: (i, col))
    in_specs = [
        zblk(ZC_Q), zblk(ZC_K), zblk(ZC_V), zblk(ZC_V + 1), zblk(ZC_G), zblk(ZC_G + 1),
        const((1, RET_DK)),
        const((1, RET_DK)),
        slab,
        const((RET_HEADS, 1, RET_DV)),
    ]
    args = [z, z, z, z, z, z, cos, sin, states, gam]
    if stack is None:
        state_spec = pl.BlockSpec((bb, RET_HEADS, RET_DK, RET_DV), lambda i: (i, 0, 0, 0))
        state_shape = jax.ShapeDtypeStruct(states.shape[1:], F32)
        aliases = {}
    else:
        in_specs.append(pl.BlockSpec(memory_space=pl.ANY))
        args.append(stack)
        state_spec = slab
        state_shape = jax.ShapeDtypeStruct(stack.shape, F32)
        aliases = {len(args) - 1: 1}
    return pl.pallas_call(
        functools.partial(_retention_step_kernel, bb=bb, layer=layer),
        grid=(n // bb,),
        in_specs=in_specs,
        out_specs=[pl.BlockSpec((bb, RET_V), lambda i: (i, 0)), state_spec],
        out_shape=[jax.ShapeDtypeStruct((n, RET_V), BF16), state_shape],
        scratch_shapes=[pltpu.VMEM((bb, RET_V), F32)],
        input_output_aliases=aliases,
        compiler_params=_params(("parallel",)),
        name="retention_step",
    )(*args)


def _layer_norm(y, g, b):
    mu = jnp.mean(y, axis=-1, keepdims=True)
    d = y - mu
    var = jnp.mean(d * d, axis=-1, keepdims=True)
    return d * lax.rsqrt(var + EPS) * g + b


CONV_RB = 128
CONV_PITCH = 2
CONV_LCH = CONV_WIDTH // LANES


def _conv_glu(a_ref, b_ref, pwb_ref):
    a = a_ref[...].astype(F32) + pwb_ref[:, :CONV_WIDTH]
    b = b_ref[...].astype(F32) + pwb_ref[:, CONV_WIDTH:]
    return a * jax.nn.sigmoid(b)


def _conv_kernel(a_ref, b_ref, buf_ref, pwb_ref, dww_ref, dwb_ref, lng_ref, lnb_ref, *rest, tt, riders):
    rider_in, rest = rest[:riders.n_in], rest[riders.n_in:]
    o_ref, hist_ref = rest[:2]
    rider_out, (x_scr, y_scr) = rest[2:-2], rest[-2:]
    riders.run(rider_in, rider_out)
    _conv_body(a_ref, b_ref, buf_ref, pwb_ref, dww_ref, dwb_ref, lng_ref, lnb_ref, o_ref, hist_ref, x_scr, y_scr, tt=tt)


def _conv_body(a_ref, b_ref, buf_ref, pwb_ref, dww_ref, dwb_ref, lng_ref, lnb_ref, o_ref, hist_ref, x_scr, y_scr,
               *, tt):
    i = pl.program_id(1)

    def rows(start, n):
        return pl.ds(CONV_PITCH * start, n, stride=CONV_PITCH)

    def lanes(c):
        return slice(c * LANES, (c + 1) * LANES)

    @pl.when(i == 0)
    def _():
        for c in range(CONV_LCH):
            x_scr[c, rows(0, CONV_HIST), :] = buf_ref[:, lanes(c)]

    @pl.when(i > 0)
    def _():
        for c in range(CONV_LCH):
            x_scr[c, rows(0, CONV_HIST), :] = x_scr[c, rows(tt, CONV_HIST), :]

    glu = _conv_glu(a_ref, b_ref, pwb_ref)
    for c in range(CONV_LCH):
        x_scr[c, rows(CONV_HIST, tt), :] = glu[:, lanes(c)]
    hist_ref[...] = glu[tt - CONV_HIST:, :]

    off = CONV_HIST - (CONV_K - 1)
    for c in range(CONV_LCH):
        def row_block(r, carry, c=c):
            base = r * CONV_RB
            n_grp = CONV_RB // ROW_TILE
            accs = [jnp.broadcast_to(dwb_ref[:, lanes(c)], (ROW_TILE, LANES))] * n_grp
            for m in range(CONV_RB - ROW_TILE + CONV_K):
                win = x_scr[c, rows(base + (off + m), ROW_TILE), :]
                for k in range(m % ROW_TILE, CONV_K, ROW_TILE):
                    j = (m - k) // ROW_TILE
                    if 0 <= j < n_grp:
                        accs[j] = accs[j] + dww_ref[k:k + 1, lanes(c)] * win
            y_scr[pl.ds(pl.multiple_of(base, CONV_RB), CONV_RB), lanes(c)] = jnp.concatenate(accs, axis=0)
            return carry

        lax.fori_loop(0, tt // CONV_RB, row_block, 0)

    o_ref[...] = _silu(_layer_norm(y_scr[...], lng_ref[...], lnb_ref[...])).astype(o_ref.dtype)


def _conv_prompt(z, buf, pw_b, dw_w, dw_b, ln_g, ln_b, jobs=(), *, tt):
    bsz, seq, _ = z.shape
    assert seq >= CONV_HIST and tt >= CONV_HIST
    buf32 = jnp.pad(buf, ((0, 0), (CONV_HIST - (CONV_K - 1), 0), (0, 0)))
    const = lambda shape: pl.BlockSpec(shape, lambda b, i: (0,) * len(shape))
    zblk = lambda col: pl.BlockSpec((None, tt, ZB), lambda b, i: (b, i, col))
    n_i = seq // tt
    riders = _Riders(jobs, (bsz, n_i), lambda b, i: b * n_i + i)
    out, hist, *cast = pl.pallas_call(
        functools.partial(_conv_kernel, tt=tt, riders=riders),
        grid=(bsz, n_i),
        in_specs=[
            zblk(ZC_CONV), zblk(ZC_CONV + 1),
            pl.BlockSpec((None, CONV_HIST, CONV_WIDTH), lambda b, i: (b, 0, 0)),
            const((1, 2 * CONV_WIDTH)),
            const((CONV_K, CONV_WIDTH)),
            const((1, CONV_WIDTH)),
            const((1, CONV_WIDTH)),
            const((1, CONV_WIDTH)),
            *riders.in_specs,
        ],
        out_specs=[
            pl.BlockSpec((None, tt, CONV_WIDTH), lambda b, i: (b, i, 0)),
            pl.BlockSpec((None, CONV_HIST, CONV_WIDTH), lambda b, i: (b, 0, 0)),
            *riders.out_specs,
        ],
        out_shape=[jax.ShapeDtypeStruct((bsz, seq, CONV_WIDTH), BF16),
                   jax.ShapeDtypeStruct((bsz, CONV_HIST, CONV_WIDTH), F32), *riders.out_shapes],
        scratch_shapes=[pltpu.VMEM((CONV_LCH, CONV_PITCH * (CONV_HIST + tt), LANES), F32),
                        pltpu.VMEM((tt, CONV_WIDTH), F32)],
        compiler_params=_params(("arbitrary", "arbitrary")),
        name="conv_prompt",
    )(z, z, buf32, pw_b.reshape(1, -1), dw_w, dw_b.reshape(1, -1), ln_g.reshape(1, -1), ln_b.reshape(1, -1),
      *riders.inputs)
    return out, hist[:, CONV_HIST - (CONV_K - 1):, :], cast


def _conv_step_kernel(a_ref, b_ref, buf_ref, pwb_ref, dww_ref, dwb_ref, lng_ref, lnb_ref, o_ref, hist_ref):
    hist_len = CONV_K - 1
    glu = _conv_glu(a_ref, b_ref, pwb_ref)
    acc = dwb_ref[...] + dww_ref[hist_len:hist_len + 1, :] * glu
    for k in range(hist_len):
        acc = acc + dww_ref[k:k + 1, :] * buf_ref[k]
    o_ref[...] = _silu(_layer_norm(acc, lng_ref[...], lnb_ref[...])).astype(o_ref.dtype)
    for k in range(hist_len - 1):
        hist_ref[k] = buf_ref[k + 1]
    hist_ref[hist_len - 1] = glu


def _conv_step(z, bufs, layer, pw_b, dw_w, dw_b, ln_g, ln_b, *, bb):
    n = z.shape[0]
    hist_len = CONV_K - 1
    const = lambda shape: pl.BlockSpec(shape, lambda i: (0,) * len(shape))
    zblk = lambda col: pl.BlockSpec((bb, ZB), lambda i: (i, col))
    return pl.pallas_call(
        _conv_step_kernel,
        grid=(n // bb,),
        in_specs=[
            zblk(ZC_CONV), zblk(ZC_CONV + 1),
            pl.BlockSpec((None, hist_len, bb, CONV_WIDTH), lambda i: (layer, 0, i, 0)),
            const((1, 2 * CONV_WIDTH)),
            const((CONV_K, CONV_WIDTH)),
            const((1, CONV_WIDTH)),
            const((1, CONV_WIDTH)),
            const((1, CONV_WIDTH)),
        ],
        out_specs=[
            pl.BlockSpec((bb, CONV_WIDTH), lambda i: (i, 0)),
            pl.BlockSpec((hist_len, bb, CONV_WIDTH), lambda i: (0, i, 0)),
        ],
        out_shape=[jax.ShapeDtypeStruct((n, CONV_WIDTH), BF16),
                   jax.ShapeDtypeStruct((hist_len, n, CONV_WIDTH), F32)],
        compiler_params=_params(("parallel",)),
        name="conv_step",
    )(z, z, bufs, pw_b.reshape(1, -1), dw_w, dw_b.reshape(1, -1), ln_g.reshape(1, -1), ln_b.reshape(1, -1))


def _rows_to_tiles(tile_ref, x, rows):
    for c in range(ROW_CHUNKS):
        tile_ref[pl.ds(c, rows, stride=ROW_TILE), :] = x[:, c * LANES:(c + 1) * LANES]


def _tiles_chunk(tile_ref, c, rows):
    return tile_ref[pl.ds(c, rows, stride=ROW_TILE), :]


def _tiles_to_rows(tile_ref, rows):
    return jnp.concatenate([_tiles_chunk(tile_ref, c, rows) for c in range(ROW_CHUNKS)], axis=-1)


def _merge_kernel(x_ref, s5_ref, ret_ref, conv_ref, *rest, tiled_u):
    gate_refs = rest[:N_BRANCH * D_MODEL // ZB]
    ps5_ref, pret_ref, pconv_ref, wout_ref, g_ref, h_ref, u_ref = rest[len(gate_refs):]
    per_branch = D_MODEL // ZB

    def gate(n):
        cols = [gate_refs[n * per_branch + j][...] for j in range(per_branch)]
        return jax.nn.sigmoid(jnp.concatenate(cols, axis=-1).astype(F32))

    merged = (gate(0) * jnp.dot(s5_ref[...], ps5_ref[...], preferred_element_type=F32)
              + gate(1) * jnp.dot(ret_ref[...], pret_ref[...], preferred_element_type=F32)
              + gate(2) * jnp.dot(conv_ref[...], pconv_ref[...], preferred_element_type=F32))
    h = x_ref[...] + _bdot(merged, wout_ref[...])
    h_ref[...] = h
    ms = jnp.mean(h * h, axis=-1, keepdims=True)
    u = h * lax.rsqrt(ms + EPS) * g_ref[...]
    if tiled_u:
        _rows_to_tiles(u_ref, u, u.shape[0])
    else:
        u_ref[...] = u.astype(u_ref.dtype)


def _merge(x, s5_out, ret_out, conv_out, z, s5_proj, ret_proj, conv_proj, w_out, norm_g, *, tm, tiled_u):
    bsz, seq, d = x.shape
    const = lambda shape: pl.BlockSpec(shape, lambda b, i: (0,) * len(shape))
    tok = lambda w, col=0: pl.BlockSpec((None, tm, w), lambda b, i: (b, i, col))
    n_i = seq // tm
    n_gate = N_BRANCH * d // ZB
    if tiled_u:
        u_spec = pl.BlockSpec((tm * ROW_TILE, LANES), lambda b, i: (b * n_i + i, 0))
        u_shape = jax.ShapeDtypeStruct((bsz * seq * ROW_TILE, LANES), F32)
    else:
        u_spec = tok(d)
        u_shape = jax.ShapeDtypeStruct((bsz, seq, d), BF16)
    return pl.pallas_call(
        functools.partial(_merge_kernel, tiled_u=tiled_u),
        grid=(bsz, seq // tm),
        in_specs=[
            tok(d),
            tok(S5_WIDTH),
            tok(RET_V),
            tok(CONV_WIDTH),
            *[tok(ZB, ZC_GATE + j) for j in range(n_gate)],
            const((S5_WIDTH, d)),
            const((RET_V, d)),
            const((CONV_WIDTH, d)),
            const((d, d)),
            const((1, d)),
        ],
        out_specs=[tok(d), u_spec],
        out_shape=[jax.ShapeDtypeStruct((bsz, seq, d), F32), u_shape],
        compiler_params=_params(("parallel", "parallel")),
        name="merge",
    )(x, s5_out, ret_out, conv_out, *([z] * n_gate), s5_proj, ret_proj, conv_proj, w_out, norm_g.reshape(1, d))


def _ffn_kernel(h_ref, u_ref, wg_ref, wu_ref, wd_ref, *rest, riders):
    rider_in, o_ref, rider_out = rest[:riders.n_in], rest[riders.n_in], rest[riders.n_in + 1:]
    f = pl.program_id(1)
    ub = u_ref[...].astype(BF16)
    gate = jnp.dot(ub, wg_ref[...], preferred_element_type=F32)
    up = jnp.dot(ub, wu_ref[...], preferred_element_type=F32)
    part = _bdot(_silu(gate) * up, wd_ref[...])

    @pl.when(f == 0)
    def _():
        o_ref[...] = h_ref[...] + part

    @pl.when(f > 0)
    def _():
        o_ref[...] = o_ref[...] + part

    riders.run(rider_in, rider_out)


def _ffn(h, u, w_gate, w_up, w_down, jobs=(), *, tm, tf):
    rows, d = h.shape
    dff = w_gate.shape[1]
    n_i, n_f = rows // tm, dff // tf
    tok = pl.BlockSpec((tm, d), lambda i, f: (i, 0))
    riders = _Riders(jobs, (n_i, n_f), lambda i, f: i * n_f + f)
    outs = pl.pallas_call(
        functools.partial(_ffn_kernel, riders=riders),
        grid=(n_i, n_f),
        in_specs=[tok, tok,
                  pl.BlockSpec((d, tf), lambda i, f: (0, f)),
                  pl.BlockSpec((d, tf), lambda i, f: (0, f)),
                  pl.BlockSpec((tf, d), lambda i, f: (f, 0)),
                  *riders.in_specs],
        out_specs=[tok, *riders.out_specs],
        out_shape=[jax.ShapeDtypeStruct((rows, d), F32), *riders.out_shapes],
        compiler_params=_params(("arbitrary", "arbitrary")),
        name="ffn_dense",
    )(h, u, w_gate, w_up, w_down, *riders.inputs)
    return outs[0], list(outs[1:])


def _split_bf16(x):
    hi = x.astype(BF16)
    return hi, (x - hi.astype(F32)).astype(BF16)


def _router_kernel(u_ref, rt_ref, tri_ref, idx_ref, wts_ref, rank_ref, cnt_ref, cnt_scr, *, tm):
    @pl.when(pl.program_id(0) == 0)
    def _():
        cnt_scr[...] = jnp.zeros_like(cnt_scr)

    u_hi, u_lo = _split_bf16(_tiles_to_rows(u_ref, tm))
    r_hi, r_lo = _split_bf16(rt_ref[...])
    dn = (((1,), (1,)), ((), ()))
    logits = (lax.dot_general(r_hi, u_hi, dn, preferred_element_type=F32)
              + lax.dot_general(r_lo, u_hi, dn, preferred_element_type=F32)
              + lax.dot_general(r_hi, u_lo, dn, preferred_element_type=F32))
    eidx = lax.broadcasted_iota(jnp.int32, logits.shape, 0)
    m1 = jnp.max(logits, axis=0, keepdims=True)
    i1 = jnp.min(jnp.where(logits == m1, eidx, N_EXPERTS), axis=0, keepdims=True)
    rest = jnp.where(eidx == i1, -jnp.inf, logits)
    m2 = jnp.max(rest, axis=0, keepdims=True)
    i2 = jnp.min(jnp.where(rest == m2, eidx, N_EXPERTS), axis=0, keepdims=True)
    e2 = jnp.exp(m2 - m1)
    w1 = 1.0 / (1.0 + e2)
    idx_ref[...] = jnp.concatenate([i1, i2], axis=0)
    wts_ref[...] = jnp.concatenate([w1, e2 * w1], axis=0)

    hit1 = eidx == i1
    hit2 = eidx == i2
    hits = jnp.where(hit1 | hit2, 1.0, 0.0)
    before = jnp.dot(hits.astype(BF16), tri_ref[...], preferred_element_type=F32) + cnt_scr[...]
    rank_ref[...] = jnp.concatenate(
        [jnp.sum(jnp.where(hit1, before, 0.0), axis=0, keepdims=True),
         jnp.sum(jnp.where(hit2, before, 0.0), axis=0, keepdims=True)], axis=0).astype(jnp.int32)
    cnt_scr[...] = cnt_scr[...] + jnp.sum(hits, axis=1, keepdims=True)
    cnt_ref[...] = cnt_scr[...]


def _router(u8, router, *, tm):
    rows = u8.shape[0] // ROW_TILE
    d = D_MODEL
    tri = (jnp.arange(tm)[:, None] < jnp.arange(tm)[None, :]).astype(BF16)
    const = lambda shape: pl.BlockSpec(shape, lambda i: (0,) * len(shape))
    lane = pl.BlockSpec((TOP_K, tm), lambda i: (0, i))
    return pl.pallas_call(
        functools.partial(_router_kernel, tm=tm),
        grid=(rows // tm,),
        in_specs=[pl.BlockSpec((tm * ROW_TILE, LANES), lambda i: (i, 0)), const((N_EXPERTS, d)), const((tm, tm))],
        out_specs=[lane, lane, lane, const((N_EXPERTS, 1))],
        out_shape=[jax.ShapeDtypeStruct((TOP_K, rows), jnp.int32), jax.ShapeDtypeStruct((TOP_K, rows), F32),
                   jax.ShapeDtypeStruct((TOP_K, rows), jnp.int32), jax.ShapeDtypeStruct((N_EXPERTS, 1), F32)],
        scratch_shapes=[pltpu.VMEM((N_EXPERTS, 1), F32)],
        compiler_params=_params(("arbitrary",)),
        name="moe_router",
    )(u8, router.T, tri)


DMA_UNROLL = 8


def _token_tile(ref, r):
    return ref.at[pl.ds(pl.multiple_of(r * ROW_TILE, ROW_TILE), ROW_TILE)]


def _dispatch_kernel(pos_ref, u_ref, init_ref, xs_ref, sem, *, tb):
    del init_ref

    def start(r, carry):
        for s in range(TOP_K):
            pltpu.make_async_copy(_token_tile(u_ref, r), _token_tile(xs_ref, pos_ref[s, r]),
                                  sem.at[s]).start(priority=s)
        return carry

    lax.fori_loop(0, tb, start, 0, unroll=DMA_UNROLL)
    for s in range(TOP_K):
        pltpu.make_async_copy(u_ref, xs_ref.at[pl.ds(0, tb * ROW_TILE)], sem.at[s]).wait()


def _dispatch(u8, pos, xs8, *, tb):
    rows = u8.shape[0] // ROW_TILE
    return pl.pallas_call(
        functools.partial(_dispatch_kernel, tb=tb),
        grid=(rows // tb,),
        in_specs=[
            pl.BlockSpec((TOP_K, tb), lambda i: (0, i), memory_space=pltpu.SMEM),
            pl.BlockSpec((tb * ROW_TILE, LANES), lambda i: (i, 0)),
            pl.BlockSpec(memory_space=pl.ANY),
        ],
        out_specs=pl.BlockSpec(memory_space=pl.ANY),
        out_shape=jax.ShapeDtypeStruct(xs8.shape, xs8.dtype),
        scratch_shapes=[pltpu.SemaphoreType.DMA((TOP_K,))],
        input_output_aliases={2: 0},
        compiler_params=_params(("arbitrary",)),
        name="moe_dispatch",
    )(pos, u8, xs8)


def _experts_kernel(te_ref, nu_ref, x_ref, wg_ref, wu_ref, wd_ref, o_ref, x_scr, acc_scr, *, tm):
    del te_ref
    i = pl.program_id(0)
    f = pl.program_id(1)

    def swiglu_rows(n):
        xb = x_scr[0:n, :]
        hidden = []
        for c in range(wg_ref.shape[1] // HIDDEN_CHUNK):
            cols = slice(c * HIDDEN_CHUNK, (c + 1) * HIDDEN_CHUNK)
            gate = jnp.dot(xb, wg_ref[:, cols], preferred_element_type=F32)
            up = jnp.dot(xb, wu_ref[:, cols], preferred_element_type=F32)
            hidden.append((_silu(gate) * up).astype(BF16))
        part = jnp.dot(jnp.concatenate(hidden, axis=-1), wd_ref[...], preferred_element_type=F32)

        @pl.when(f == 0)
        def _():
            acc_scr[0:n, :] = part
            if n < tm:
                acc_scr[n:tm, :] = jnp.zeros((tm - n, D_MODEL), F32)

        @pl.when(f > 0)
        def _():
            acc_scr[0:n, :] = acc_scr[0:n, :] + part

    @pl.when(i < nu_ref[0])
    def _():
        @pl.when(f == 0)
        def _():
            x_scr[...] = _tiles_to_rows(x_ref, tm).astype(BF16)

        half = tm // 2
        valid = nu_ref[1 + i]

        @pl.when(valid > half)
        def _():
            swiglu_rows(tm)

        @pl.when(valid <= half)
        def _():
            swiglu_rows(half)

        @pl.when(f == pl.num_programs(1) - 1)
        def _():
            _rows_to_tiles(o_ref, acc_scr[...], tm)

    @pl.when(i >= nu_ref[0])
    def _():
        o_ref[...] = jnp.zeros_like(o_ref)


def _experts(xs8, tile_expert, n_used, w_gate, w_up, w_down, *, tm, tf):
    rows = xs8.shape[0] // ROW_TILE
    d = D_MODEL
    dff = w_gate.shape[-1]
    n_f = dff // tf
    last_f = n_f - 1

    def row_map(i, f, te, nu):
        return (jnp.minimum(i, nu[0] - 1), 0)

    def fsel(i, f, nu):
        return jnp.where(i < nu[0], f, last_f)

    grid_spec = pltpu.PrefetchScalarGridSpec(
        num_scalar_prefetch=2,
        grid=(rows // tm, n_f),
        in_specs=[
            pl.BlockSpec((tm * ROW_TILE, LANES), row_map),
            pl.BlockSpec((None, d, tf), lambda i, f, te, nu: (te[i], 0, fsel(i, f, nu))),
            pl.BlockSpec((None, d, tf), lambda i, f, te, nu: (te[i], 0, fsel(i, f, nu))),
            pl.BlockSpec((None, tf, d), lambda i, f, te, nu: (te[i], fsel(i, f, nu), 0)),
        ],
        out_specs=pl.BlockSpec((tm * ROW_TILE, LANES), lambda i, f, te, nu: (i, 0)),
        scratch_shapes=[pltpu.VMEM((tm, d), BF16), pltpu.VMEM((tm, d), F32)],
    )
    return pl.pallas_call(
        functools.partial(_experts_kernel, tm=tm),
        grid_spec=grid_spec,
        out_shape=jax.ShapeDtypeStruct(xs8.shape, F32),
        compiler_params=_params(("arbitrary", "arbitrary")),
        name="moe_experts",
    )(tile_expert, n_used, xs8, w_gate, w_up, w_down)


def _combine_kernel(pos_ref, next_pos_ref, h_ref, wts_ref, g_ref, ys_ref, o_ref, y_scr, sem, *, tb, final_norm):
    i = pl.program_id(0)
    n = pl.num_programs(0)

    def gather(p_ref, buf):
        def start(r, carry):
            for s in range(TOP_K):
                pltpu.make_async_copy(_token_tile(ys_ref, p_ref[s, r]), _token_tile(y_scr.at[buf, s], r),
                                      sem.at[buf, s]).start(priority=s)
            return carry

        lax.fori_loop(0, tb, start, 0, unroll=DMA_UNROLL)

    @pl.when(i == 0)
    def _():
        gather(pos_ref, 0)

    for buf in range(2):
        @pl.when(i % 2 == buf)
        def _(buf=buf):
            @pl.when(i + 1 < n)
            def _():
                gather(next_pos_ref, 1 - buf)

            for s in range(TOP_K):
                pltpu.make_async_copy(ys_ref.at[pl.ds(0, tb * ROW_TILE)], y_scr.at[buf, s], sem.at[buf, s]).wait()
            w1 = wts_ref[:, 0:1]
            w2 = wts_ref[:, 1:2]
            moe = jnp.concatenate(
                [w1 * _tiles_chunk(y_scr.at[buf, 0], c, tb) + w2 * _tiles_chunk(y_scr.at[buf, 1], c, tb)
                 for c in range(ROW_CHUNKS)], axis=-1)
            out = h_ref[...] + moe
            if final_norm:
                ms = jnp.mean(out * out, axis=-1, keepdims=True)
                out = out * lax.rsqrt(ms + EPS) * g_ref[...]
            o_ref[...] = out


def _combine(h, pos, wts_t, ys8, final_g, *, tb):
    rows, d = h.shape
    final_norm = final_g is not None
    gain = final_g.reshape(1, d) if final_norm else jnp.ones((1, d), F32)
    n_steps = rows // tb
    return pl.pallas_call(
        functools.partial(_combine_kernel, tb=tb, final_norm=final_norm),
        grid=(n_steps,),
        in_specs=[
            pl.BlockSpec((TOP_K, tb), lambda i: (0, i), memory_space=pltpu.SMEM),
            pl.BlockSpec((TOP_K, tb), lambda i: (0, jnp.minimum(i + 1, n_steps - 1)), memory_space=pltpu.SMEM),
            pl.BlockSpec((tb, d), lambda i: (i, 0)),
            pl.BlockSpec((tb, TOP_K), lambda i: (i, 0)),
            pl.BlockSpec((1, d), lambda i: (0, 0)),
            pl.BlockSpec(memory_space=pl.ANY),
        ],
        out_specs=pl.BlockSpec((tb, d), lambda i: (i, 0)),
        out_shape=jax.ShapeDtypeStruct((rows, d), F32),
        scratch_shapes=[pltpu.VMEM((2, TOP_K, tb * ROW_TILE, LANES), F32), pltpu.SemaphoreType.DMA((2, TOP_K))],
        compiler_params=_params(("arbitrary",)),
        name="moe_combine",
    )(pos, pos, h, wts_t, gain, ys8)


PROJ_TN = 2560
S5_TT = 128
MIX_TT = 512
FFN_TF = 1408
MOE_TM = 512
MOE_TF = 1792
HIDDEN_CHUNK = 256
MOE_TB = 1024


def _moe_tiles(n_rows):
    return n_rows // MOE_TM + N_EXPERTS


def _moe_buffer_shape(n_rows):
    return (_moe_tiles(n_rows) * MOE_TM * ROW_TILE, LANES)


def _moe(h_list, u8_list, router, w_gate, w_up, w_down, xs8, final_g):
    routes = []
    for u8 in u8_list:
        routes.append(_router(u8, router, tm=min(512, u8.shape[0] // ROW_TILE)))
    counts = [r[3][:, 0].astype(jnp.int32) for r in routes]
    total = sum(counts)
    padded = ((total + MOE_TM - 1) // MOE_TM) * MOE_TM
    ends = jnp.cumsum(padded)
    starts = ends - padded
    n_rows = sum(h.shape[0] for h in h_list) * TOP_K
    n_tiles = _moe_tiles(n_rows)
    assert xs8.shape == _moe_buffer_shape(n_rows)
    n_used = (ends[-1] // MOE_TM).astype(jnp.int32)
    tile_start = jnp.arange(n_tiles, dtype=jnp.int32) * MOE_TM
    tile_expert = jnp.sum((tile_start[:, None] >= ends[None, :]).astype(jnp.int32), axis=1)
    last_expert = jnp.sum((((n_used - 1) * MOE_TM) >= ends).astype(jnp.int32))
    tile_expert = jnp.where(jnp.arange(n_tiles) < n_used, tile_expert, last_expert).astype(jnp.int32)

    poss = []
    seen = jnp.zeros((N_EXPERTS,), jnp.int32)
    for u8, (idx, _, rank, _), cnt in zip(u8_list, routes, counts):
        base = starts + seen
        pos = rank
        for e in range(N_EXPERTS):
            pos = pos + jnp.where(idx == e, base[e], 0)
        poss.append(pos)
        seen = seen + cnt
        xs8 = _dispatch(u8, pos, xs8, tb=min(2 * MOE_TB, u8.shape[0] // ROW_TILE))
    group_rows = jnp.sum(jnp.where(jnp.arange(N_EXPERTS)[None, :] == tile_expert[:, None],
                                   (starts + total)[None, :], 0), axis=1)
    tile_valid = jnp.clip(group_rows - tile_start, 0, MOE_TM)
    tile_info = jnp.concatenate([n_used.reshape(1), tile_valid]).astype(jnp.int32)
    ys8 = _experts(xs8, tile_expert, tile_info, w_gate, w_up, w_down, tm=MOE_TM, tf=MOE_TF)
    outs = []
    for h, pos, (_, wts, _, _) in zip(h_list, poss, routes):
        outs.append(_combine(h, pos, wts.T, ys8, final_g, tb=min(MOE_TB, h.shape[0])))
    return outs


def _rmsnorm_kernel(x_ref, g_ref, o_ref):
    x = x_ref[...]
    ms = jnp.mean(x * x, axis=-1, keepdims=True)
    o_ref[...] = x * lax.rsqrt(ms + EPS) * g_ref[...]


def _rmsnorm(x, g, *, tm):
    rows, d = x.shape
    return pl.pallas_call(
        _rmsnorm_kernel,
        grid=(rows // tm,),
        in_specs=[pl.BlockSpec((tm, d), lambda i: (i, 0)), pl.BlockSpec((1, d), lambda i: (0, 0))],
        out_specs=pl.BlockSpec((tm, d), lambda i: (i, 0)),
        out_shape=jax.ShapeDtypeStruct((rows, d), F32),
        compiler_params=_params(("parallel",)),
        name="final_norm",
    )(x, g.reshape(1, d))


def _pack_s5_state(re, im):
    n = re.shape[0]
    return jnp.concatenate([re.reshape(n, S5_LANES), im.reshape(n, S5_LANES)], axis=-1)


def _unpack_s5_state(h):
    n = h.shape[0]
    return (h[:, :S5_LANES].reshape(n, S5_GROUPS, S5_STATE), h[:, S5_LANES:].reshape(n, S5_GROUPS, S5_STATE))


def _mixer(x, z, s5_h0, ret_s0, conv_buf, pos_offset, p, *, single_step, tiled_u, layer=0, ret_stack=None, jobs=None):
    jobs = jobs or {}
    done = {}
    bsz, seq, d = x.shape
    s5_out, s5_state, done["s5"] = _s5_branch(z, s5_h0, p["a_re"], p["a_im"], p["bmat"], p["cmat"], p["d_skip"],
                                              p["glu_w"], p["glu_b"], jobs.get("s5", ()), single_step=single_step,
                                              tt=1 if single_step else S5_TT)

    if single_step:
        z2 = z.reshape(seq, N_IN)
        ret_out, ret_state = _retention_step(z2, ret_s0, layer, ret_stack, pos_offset, bb=16)
        conv_out, conv_state = _conv_step(z2, conv_buf, layer, p["pw_b"], p["dw_w"], p["dw_b"], p["ln_g"], p["ln_b"],
                                          bb=32)
        ret_out = ret_out.reshape(bsz, seq, RET_V)
        conv_out = conv_out.reshape(bsz, seq, CONV_WIDTH)
    else:
        ret_out, ret_state, done["ret"] = _retention_prompt(z, ret_s0, pos_offset, jobs.get("ret", ()), tt=MIX_TT)
        conv_out, conv_state, done["conv"] = _conv_prompt(z, conv_buf, p["pw_b"], p["dw_w"], p["dw_b"], p["ln_g"],
                                                          p["ln_b"], jobs.get("conv", ()), tt=MIX_TT)

    h, u = _merge(x, s5_out, ret_out, conv_out, z, p["s5_proj"], p["ret_proj"], p["conv_proj"], p["w_out"],
                  p["norm_ffn_g"], tm=min(512, seq), tiled_u=tiled_u)
    return h, u, (s5_state, ret_state, conv_state), done


def kernel(x_prompt, x_sample, state_s5_re, state_s5_im, state_ret, state_conv, norm_mix_g, w_in, s5_lambda_re, s5_lambda_im, s5_log_dt, s5_b_re, s5_b_im, s5_c_re, s5_c_im, s5_d, s5_glu_w, s5_glu_b, s5_proj, ret_proj, conv_pw_b, conv_dw_w, conv_dw_b, conv_ln_g, conv_ln_b, conv_proj, w_out, norm_ffn_g, ffn_w_gate, ffn_w_up, ffn_w_down, moe_router, moe_w_gate, moe_w_up, moe_w_down, norm_final_g):
    depth = w_in.shape[0]
    bp, seq, d = x_prompt.shape
    ns = x_sample.shape[0]
    past_len = 16384
    bf = lambda a: a.astype(BF16)

    hp = x_prompt
    hs = x_sample.reshape(1, ns, d)
    zero_s5 = jnp.zeros((bp, 2 * S5_LANES), F32)
    zero_ret = jnp.zeros((bp, RET_HEADS, RET_DK, RET_DV), F32)
    zero_conv = jnp.zeros((bp, CONV_K - 1, CONV_WIDTH), F32)

    conv_rows = jnp.transpose(state_conv, (0, 2, 1, 3))
    p_states, s_states = [], []
    ret_stack = None
    normed = False
    groups = depth * S5_GROUPS
    disc = _s5_discretize(s5_lambda_re.reshape(groups, S5_STATE), s5_lambda_im.reshape(groups, S5_STATE),
                          s5_log_dt.reshape(groups), s5_b_re.reshape(groups, S5_STATE, S5_GROUP),
                          s5_b_im.reshape(groups, S5_STATE, S5_GROUP))
    disc = [a.reshape((depth, S5_GROUPS) + a.shape[1:]) for a in disc]
    for l in range(depth):
        is_moe = l % 2 == 1
        a_re, a_im, bb_re, bb_im = (a[l] for a in disc)
        bmat, cmat = _s5_block_mats(bb_re, bb_im, s5_c_re[l], s5_c_im[l])
        p = dict(a_re=a_re, a_im=a_im, bmat=bmat, cmat=cmat, d_skip=s5_d[l], glu_w=bf(s5_glu_w[l]),
                 glu_b=s5_glu_b[l], pw_b=conv_pw_b[l], dw_w=conv_dw_w[l], dw_b=conv_dw_b[l], ln_g=conv_ln_g[l],
                 ln_b=conv_ln_b[l], s5_proj=bf(s5_proj[l]), ret_proj=bf(ret_proj[l]), conv_proj=bf(conv_proj[l]),
                 w_out=bf(w_out[l]), norm_ffn_g=norm_ffn_g[l])
        j = l // 2
        flat = lambda w: w.reshape(-1, w.shape[-1])
        if is_moe:
            jobs = {"conv": [_cast_job(flat(moe_w_up[j]))],
                    "proj": [(None, _moe_buffer_shape((bp * seq + ns) * TOP_K), F32)]}
        else:
            jobs = {"s5": [_cast_job(ffn_w_gate[j]), _cast_job(ffn_w_up[j]), _cast_job(ffn_w_down[j])]}
            if l + 1 < depth:
                jm = (l + 1) // 2
                jobs["conv"] = [_cast_job(flat(moe_w_gate[jm]))]
                jobs["ret"] = [_cast_job(flat(moe_w_down[jm]))]
        zp, zs, proj_done = _norm_proj(hp, hs.reshape(ns, d), norm_mix_g[l], w_in, l, jobs.pop("proj", ()),
                                       tt=512, tn=PROJ_TN)
        hp, up, st_p, done = _mixer(hp, zp, zero_s5, zero_ret, zero_conv, 0.0, p, single_step=False, tiled_u=is_moe,
                                    layer=l, jobs=jobs)
        hs, us, st_s, _ = _mixer(hs, zs.reshape(1, ns, N_IN), _pack_s5_state(state_s5_re[l], state_s5_im[l]),
                                 state_ret, conv_rows, float(past_len), p, single_step=True, tiled_u=is_moe, layer=l,
                                 ret_stack=ret_stack)
        p_states.append(st_p)
        s_states.append(st_s)

        hp2, hs2 = hp.reshape(bp * seq, d), hs.reshape(ns, d)
        if is_moe:
            ret_stack = st_s[1]
            final_g = norm_final_g if l == depth - 1 else None
            normed = final_g is not None
            w_gate_b, w_down_b = (c.reshape(w.shape[1:]) for c, w in zip(early, (moe_w_gate, moe_w_down)))
            w_up_b = done["conv"][0].reshape(moe_w_up.shape[1:])
            hp2, hs2 = _moe([hp2, hs2], [up, us], moe_router[j], w_gate_b, w_up_b, w_down_b, proj_done[0], final_g)
        else:
            wg, wu, wd = done["s5"]
            if l + 1 < depth:
                early = [done["conv"][0], done["ret"][0]]
            ffn_jobs = []
            if l == 0 and depth > 1:
                rows = st_s[1].size // RET_DV
                ffn_jobs = [(st_s[1].reshape(rows, RET_DV), (depth * rows, RET_DV), F32)]
            hp2, made = _ffn(hp2, up.reshape(bp * seq, d), wg, wu, wd, ffn_jobs, tm=512, tf=FFN_TF)
            if l == 0:
                ret_stack = made[0].reshape(state_ret.shape) if ffn_jobs else st_s[1][None]
            else:
                ret_stack = st_s[1]
            hs2, _ = _ffn(hs2, us.reshape(ns, d), wg, wu, wd, tm=ns, tf=FFN_TF)
        hp = hp2.reshape(bp, seq, d)
        hs = hs2.reshape(1, ns, d)

    if normed:
        y_prompt, y_sample = hp, hs.reshape(ns, 1, d)
    else:
        y_prompt = _rmsnorm(hp.reshape(bp * seq, d), norm_final_g, tm=512).reshape(bp, seq, d)
        y_sample = _rmsnorm(hs.reshape(ns, d), norm_final_g, tm=ns).reshape(ns, 1, d)

    def stack_s5(states):
        s5 = [_unpack_s5_state(s[0]) for s in states]
        return jnp.stack([a for a, _ in s5]), jnp.stack([b for _, b in s5])

    p_re, p_im = stack_s5(p_states)
    s_re, s_im = stack_s5(s_states)
    p_ret = jnp.stack([s[1] for s in p_states])
    p_conv = jnp.stack([s[2] for s in p_states])
    s_conv = jnp.transpose(jnp.stack([s[2] for s in s_states]), (0, 2, 1, 3))
    return (y_prompt, y_sample, p_re, p_im, p_ret, p_conv, s_re, s_im, ret_stack, s_conv)
```
